```python
import math
import jax, jax.numpy as jnp
from jax import lax
import numpy as np

D_MODEL = 1024
BATCH = 2
SEQ = 16384
DEPTH = 1

GRID_W = 64
CTX_LEN = 256
EPS = 1e-6
CHUNK = 128
CONV_K = 5
D_SSD = D_MODEL
SSD_HEAD_DIM = 64
SSD_HEADS = D_SSD // SSD_HEAD_DIM
SSD_STATE = 128
SSD_GROUPS = 2
D_XBC = D_SSD + 2 * SSD_GROUPS * SSD_STATE
D_MLSTM = D_MODEL
MLSTM_DV = 128
MLSTM_HEADS = D_MLSTM // MLSTM_DV
MLSTM_DK = MLSTM_DV // 2
D_QK = 2 * MLSTM_HEADS * MLSTM_DK
N_EXPERTS = 256
TOP_K = 8
N_EXPERT_GROUPS = 8
TOPK_GROUPS = 4
D_EXPERT = D_MODEL // 4
D_SHARED = D_EXPERT
ROUTED_SCALE = 2.5
MOE_BLOCK = 128
IN_PROJ_SIZES = (D_SSD, D_XBC, 2 * SSD_HEADS, D_QK, D_MLSTM, 4 * MLSTM_HEADS, D_MLSTM, 2 * D_MODEL)
D_IN_PROJ = sum(IN_PROJ_SIZES)

kernel_name = 'hybrid_ssd_mlstm_moe_prefix_block'


def rmsnorm(t, g):
    t32 = t.astype(jnp.float32)
    y = t32 * lax.rsqrt(jnp.mean(t32 * t32, axis=-1, keepdims=True) + EPS)
    return (y * g.astype(jnp.float32)).astype(t.dtype)


def modulate(t, g, shift, scale):
    return rmsnorm(t, g) * (1 + scale) + shift


def dwconv_centred(t, w, b, n_seg):
    bsz, length, ch = t.shape
    seg = t.reshape(bsz * n_seg, length // n_seg, ch)
    y = lax.conv_general_dilated(seg, w[:, None, :].astype(t.dtype), window_strides=(1,),
                                 padding=((CONV_K // 2, CONV_K // 2),),
                                 dimension_numbers=('NWC', 'WIO', 'NWC'), feature_group_count=ch)
    return y.reshape(bsz, length, ch) + b


def to_chunks(t):
    bsz, length = t.shape[:2]
    return jnp.moveaxis(t.reshape(bsz, length // CHUNK, CHUNK, *t.shape[2:]), 1, 0)


def from_chunks(t):
    t = jnp.moveaxis(t, 0, 1)
    return t.reshape(t.shape[0], t.shape[1] * t.shape[2], *t.shape[3:])


def ssd_scan(xh, dt, a_neg, bm, cm, s0, with_output):
    bsz, length, n_heads, hd = xh.shape
    n_groups, d_state = bm.shape[2], bm.shape[3]
    hpg = n_heads // n_groups
    f32 = jnp.float32
    xs = (to_chunks(xh.astype(f32).reshape(bsz, length, n_groups, hpg, hd)),
          to_chunks(dt.reshape(bsz, length, n_groups, hpg)),
          to_chunks(bm.astype(f32)), to_chunks(cm.astype(f32)))
    ag = a_neg.reshape(n_groups, hpg)
    causal = jnp.tril(jnp.ones((CHUNK, CHUNK), dtype=bool))

    def step(s, inp):
        xc, dtc, bc, cc = inp
        cum = jnp.cumsum(dtc * ag, axis=1)
        total = cum[:, -1]
        s_new = jnp.exp(total)[..., None, None] * s + jnp.einsum(
            'bkgn,bkgh,bkghp->bghpn', bc, dtc * jnp.exp(total[:, None] - cum), xc)
        if not with_output:
            return s_new, None
        seg = cum[:, :, None] - cum[:, None]
        decay = jnp.exp(jnp.where(causal[None, :, :, None, None], seg, -jnp.inf))
        cb = jnp.einsum('bqgn,bkgn->bqkg', cc, bc)
        y = (jnp.einsum('bqkg,bqkgh,bkgh,bkghp->bqghp', cb, decay, dtc, xc)
             + jnp.einsum('bqgn,bghpn,bqgh->bqghp', cc, s, jnp.exp(cum)))
        return s_new, y

    s_fin, ys = lax.scan(step, s0.reshape(bsz, n_groups, hpg, hd, d_state), xs)
    s_fin = s_fin.reshape(bsz, n_heads, hd, d_state)
    if not with_output:
        return None, s_fin
    return from_chunks(ys).reshape(bsz, length, n_heads, hd), s_fin


def mlstm_scan(q, k, v, ig, lf, state0, with_output):
    f32 = jnp.float32
    xs = (to_chunks(q.astype(f32)), to_chunks(k.astype(f32)), to_chunks(v.astype(f32)), to_chunks(ig), to_chunks(lf))
    causal = jnp.tril(jnp.ones((CHUNK, CHUNK), dtype=bool))

    def step(carry, inp):
        c, n, m = carry
        qc, kc, vc, igc, lfc = inp
        b = jnp.cumsum(lfc, axis=1)
        b_end = b[:, -1]
        g = b_end[:, None] - b + igc
        m_new = jnp.maximum(b_end + m, g.max(axis=1))
        w = jnp.exp(g - m_new[:, None])
        carry_decay = jnp.exp(b_end + m - m_new)
        c_new = carry_decay[..., None, None] * c + jnp.einsum('bkh,bkhv,bkhd->bhvd', w, vc, kc)
        n_new = carry_decay[..., None] * n + jnp.einsum('bkh,bkhd->bhd', w, kc)
        if not with_output:
            return (c_new, n_new, m_new), None
        log_d = jnp.where(causal[None, :, :, None], b[:, :, None] - b[:, None] + igc[:, None], -jnp.inf)
        inter = b + m[:, None]
        m_q = jnp.maximum(log_d.max(axis=2), inter)
        s = jnp.einsum('bqhd,bkhd->bqkh', qc, kc) * jnp.exp(log_d - m_q[:, :, None])
        w_state = jnp.exp(inter - m_q)
        num = jnp.einsum('bqkh,bkhv->bqhv', s, vc) + w_state[..., None] * jnp.einsum('bqhd,bhvd->bqhv', qc, c)
        den = s.sum(axis=2) + w_state * jnp.einsum('bqhd,bhd->bqh', qc, n)
        h = num / jnp.maximum(jnp.abs(den), jnp.exp(-m_q))[..., None]
        return (c_new, n_new, m_new), h

    fin, hs = lax.scan(step, state0, xs)
    if not with_output:
        return None, fin
    return from_chunks(hs), fin


def zero_states(bsz):
    f32 = jnp.float32
    return [(jnp.zeros((bsz, SSD_HEADS, SSD_HEAD_DIM, SSD_STATE), f32),
             (jnp.zeros((bsz, MLSTM_HEADS, MLSTM_DV, MLSTM_DK), f32),
              jnp.zeros((bsz, MLSTM_HEADS, MLSTM_DK), f32),
              jnp.zeros((bsz, MLSTM_HEADS), f32))) for _ in range(2)]


def prepare_mixer_inputs(u, lp, n_seg):
    bsz, length = u.shape[:2]
    f32 = jnp.float32
    proj = u @ lp['w_in']
    split_idx = np.cumsum(IN_PROJ_SIZES)[:-1].tolist()
    z, xbc, dt_raw, qk, v, gates_raw, o_raw, merge_raw = jnp.split(proj, split_idx, axis=-1)
    xbc = jax.nn.silu(dwconv_centred(xbc, lp['conv_xbc_w'], lp['conv_xbc_b'], n_seg))
    xs, bm, cm = jnp.split(xbc, [D_SSD, D_SSD + SSD_GROUPS * SSD_STATE], axis=-1)
    dt = jax.nn.softplus(dt_raw.astype(f32).reshape(bsz, length, 2, SSD_HEADS) + lp['ssd_dt_bias'].astype(f32))
    qk = jax.nn.silu(dwconv_centred(qk, lp['conv_qk_w'], lp['conv_qk_b'], n_seg))
    q, k = jnp.split(qk, 2, axis=-1)
    gates = gates_raw.astype(f32).reshape(bsz, length, 2, 2, MLSTM_HEADS)
    return {
        'z': z,
        'xh': xs.reshape(bsz, length, SSD_HEADS, SSD_HEAD_DIM),
        'bm': bm.reshape(bsz, length, SSD_GROUPS, SSD_STATE),
        'cm': cm.reshape(bsz, length, SSD_GROUPS, SSD_STATE),
        'dt': dt,
        'q': q.reshape(bsz, length, MLSTM_HEADS, MLSTM_DK) * (MLSTM_DK ** -0.5),
        'k': k.reshape(bsz, length, MLSTM_HEADS, MLSTM_DK),
        'v': v.reshape(bsz, length, MLSTM_HEADS, MLSTM_DV),
        'ig': gates[:, :, :, 0] + lp['mlstm_i_bias'].astype(f32),
        'lf': jax.nn.log_sigmoid(gates[:, :, :, 1] + lp['mlstm_f_bias'].astype(f32)),
        'o': jax.nn.sigmoid(o_raw),
        'merge': merge_raw,
    }


def maybe_flip(t, rev):
    return t[:, ::-1] if rev else t


def bidirectional_scans(p, lp, init, with_output):
    a_neg = -jnp.exp(lp['ssd_a_log'].astype(jnp.float32))
    ys_ssd, ys_m, finals = [], [], []
    for d in range(2):
        rev = d == 1
        s0, m0 = init[d]
        y_s, s_fin = ssd_scan(maybe_flip(p['xh'], rev), maybe_flip(p['dt'][:, :, d], rev), a_neg[d],
                              maybe_flip(p['bm'], rev), maybe_flip(p['cm'], rev), s0, with_output)
        h_m, m_fin = mlstm_scan(maybe_flip(p['q'], rev), maybe_flip(p['k'], rev), maybe_flip(p['v'], rev),
                                maybe_flip(p['ig'][:, :, d], rev), maybe_flip(p['lf'][:, :, d], rev), m0, with_output)
        finals.append((s_fin, m_fin))
        if with_output:
            ys_ssd.append(maybe_flip(y_s, rev))
            ys_m.append(maybe_flip(h_m, rev))
    if not with_output:
        return None, None, finals
    return ys_ssd[0] + ys_ssd[1], ys_m[0] + ys_m[1], finals


def merge_branches(p, y_ssd, h_m, lp):
    bsz, length = p['z'].shape[:2]
    dt_in = p['z'].dtype
    y = (y_ssd + lp['ssd_d'].astype(jnp.float32)[:, None] * p['xh'].astype(jnp.float32)).reshape(bsz, length, D_SSD)
    y = rmsnorm(y * jax.nn.silu(p['z'].astype(jnp.float32)), lp['ssd_norm_g']).astype(dt_in)
    h = rmsnorm(h_m, lp['mlstm_norm_g'].reshape(MLSTM_HEADS, MLSTM_DV)).reshape(bsz, length, D_MLSTM).astype(dt_in)
    h = p['o'] * h
    g_ssd, g_m = jnp.split(jax.nn.sigmoid(p['merge']), 2, axis=-1)
    merged = g_ssd * (y @ lp['w_ssd_out']) + g_m * (h @ lp['w_mlstm_out'])
    return merged @ lp['w_out']


def moe_ffn(u, lp):
    bsz, length, dm = u.shape
    t = u.reshape(-1, dm)
    n_tok = t.shape[0]
    scores = jax.nn.sigmoid((t @ lp['router_w']).astype(jnp.float32))
    grouped = (scores + lp['router_bias'].astype(jnp.float32)).reshape(n_tok, N_EXPERT_GROUPS, -1)
    group_score = lax.top_k(grouped, 2)[0].sum(-1)
    _, top_groups = lax.top_k(group_score, TOPK_GROUPS)
    group_mask = jax.nn.one_hot(top_groups, N_EXPERT_GROUPS, dtype=jnp.float32).sum(1) > 0
    sel = jnp.where(group_mask[:, :, None], grouped, -jnp.inf).reshape(n_tok, N_EXPERTS)
    _, top_e = lax.top_k(sel, TOP_K)
    wts = jnp.take_along_axis(scores, top_e, axis=1)
    wts = ROUTED_SCALE * wts / wts.sum(-1, keepdims=True)
    n_assign = n_tok * TOP_K
    flat_e = top_e.reshape(-1)
    order = jnp.argsort(flat_e)
    e_sorted = flat_e[order]
    tok_sorted = (order // TOP_K).astype(jnp.int32)
    w_sorted = wts.reshape(-1)[order]
    counts = jnp.bincount(flat_e, length=N_EXPERTS)
    padded = (counts + MOE_BLOCK - 1) // MOE_BLOCK * MOE_BLOCK
    pad_end = jnp.cumsum(padded)
    pad_start = pad_end - padded
    start = jnp.cumsum(counts) - counts
    dest = pad_start[e_sorted] + jnp.arange(n_assign) - start[e_sorted]
    n_blocks = -(-n_assign // MOE_BLOCK) + N_EXPERTS
    slot_tok = jnp.full((n_blocks * MOE_BLOCK,), n_tok, jnp.int32).at[dest].set(tok_sorted)
    slot_w = jnp.zeros((n_blocks * MOE_BLOCK,), u.dtype).at[dest].set(w_sorted.astype(u.dtype))
    block_expert = jnp.minimum(jnp.searchsorted(pad_end, jnp.arange(n_blocks) * MOE_BLOCK, side='right'), N_EXPERTS - 1)
    t_pad = jnp.concatenate([t, jnp.zeros((1, dm), t.dtype)], axis=0)
    w_gate, w_up, w_down = lp['moe_w_gate'], lp['moe_w_up'], lp['moe_w_down']

    def block(acc, inp):
        tok, wt, e = inp
        xb = t_pad[tok]
        hb = jax.nn.silu(xb @ w_gate[e]) * (xb @ w_up[e])
        return acc.at[tok].add((hb @ w_down[e]) * wt[:, None]), None

    acc, _ = lax.scan(block, jnp.zeros_like(t_pad),
                      (slot_tok.reshape(n_blocks, MOE_BLOCK), slot_w.reshape(n_blocks, MOE_BLOCK), block_expert))
    shared = (jax.nn.silu(t @ lp['shared_w_gate']) * (t @ lp['shared_w_up'])) @ lp['shared_w_down']
    return (acc[:n_tok] + shared).reshape(bsz, length, dm)


def setup_inputs(seed: int = 0) -> dict:
    key = jax.random.key(seed)
    ks = jax.random.split(key, 40)
    nrm = lambda k, shape, s: jax.random.normal(k, shape, jnp.float32) * s
    dt0 = jnp.exp(jax.random.uniform(ks[10], (DEPTH, 2, SSD_HEADS)) * (math.log(0.1) - math.log(0.001)) + math.log(0.001))
    return {
        'x': nrm(ks[0], (BATCH, SEQ, D_MODEL), 1.0),
        'c': nrm(ks[1], (BATCH, D_MODEL), 1.0),
        'ctx': nrm(ks[2], (BATCH, CTX_LEN, D_MODEL), 1.0),
        'c_ctx': nrm(ks[3], (D_MODEL,), 1.0),
        'ada_w': nrm(ks[4], (DEPTH, D_MODEL, 6 * D_MODEL), 0.3 * D_MODEL ** -0.5),
        'ada_b': nrm(ks[5], (DEPTH, 6 * D_MODEL), 0.02),
        'norm_mix_g': 1.0 + nrm(ks[6], (DEPTH, D_MODEL), 0.02),
        'norm_ffn_g': 1.0 + nrm(ks[7], (DEPTH, D_MODEL), 0.02),
        'w_in': nrm(ks[8], (DEPTH, D_MODEL, D_IN_PROJ), D_MODEL ** -0.5),
        'conv_xbc_w': nrm(ks[9], (DEPTH, CONV_K, D_XBC), CONV_K ** -0.5),
        'conv_xbc_b': nrm(ks[11], (DEPTH, D_XBC), 0.02),
        'ssd_dt_bias': dt0 + jnp.log(-jnp.expm1(-dt0)),
        'ssd_a_log': jnp.log(jax.random.uniform(ks[12], (DEPTH, 2, SSD_HEADS), jnp.float32, 1.0, 16.0)),
        'ssd_d': 1.0 + nrm(ks[13], (DEPTH, SSD_HEADS), 0.1),
        'ssd_norm_g': 1.0 + nrm(ks[14], (DEPTH, D_SSD), 0.02),
        'conv_qk_w': nrm(ks[15], (DEPTH, CONV_K, D_QK), CONV_K ** -0.5),
        'conv_qk_b': nrm(ks[16], (DEPTH, D_QK), 0.02),
        'mlstm_i_bias': -1.0 + nrm(ks[17], (DEPTH, 2, MLSTM_HEADS), 0.1),
        'mlstm_f_bias': jnp.linspace(3.0, 6.0, MLSTM_HEADS) + nrm(ks[18], (DEPTH, 2, MLSTM_HEADS), 0.1),
        'mlstm_norm_g': 1.0 + nrm(ks[19], (DEPTH, D_MLSTM), 0.02),
        'w_ssd_out': nrm(ks[20], (DEPTH, D_SSD, D_MODEL), D_SSD ** -0.5),
        'w_mlstm_out': nrm(ks[21], (DEPTH, D_MLSTM, D_MODEL), D_MLSTM ** -0.5),
        'w_out': nrm(ks[22], (DEPTH, D_MODEL, D_MODEL), D_MODEL ** -0.5),
        'router_w': nrm(ks[23], (DEPTH, D_MODEL, N_EXPERTS), D_MODEL ** -0.5),
        'router_bias': nrm(ks[24], (DEPTH, N_EXPERTS), 0.01),
        'moe_w_gate': nrm(ks[25], (DEPTH, N_EXPERTS, D_MODEL, D_EXPERT), D_MODEL ** -0.5),
        'moe_w_up': nrm(ks[26], (DEPTH, N_EXPERTS, D_MODEL, D_EXPERT), D_MODEL ** -0.5),
        'moe_w_down': nrm(ks[27], (DEPTH, N_EXPERTS, D_EXPERT, D_MODEL), D_EXPERT ** -0.5),
        'shared_w_gate': nrm(ks[28], (DEPTH, D_MODEL, D_SHARED), D_MODEL ** -0.5),
        'shared_w_up': nrm(ks[29], (DEPTH, D_MODEL, D_SHARED), D_MODEL ** -0.5),
        'shared_w_down': nrm(ks[30], (DEPTH, D_SHARED, D_MODEL), D_SHARED ** -0.5),
        'norm_final_g': 1.0 + nrm(ks[31], (D_MODEL,), 0.02),
    }


def reference(x, c, ctx, c_ctx, ada_w, ada_b, norm_mix_g, norm_ffn_g, w_in, conv_xbc_w, conv_xbc_b, ssd_dt_bias,
              ssd_a_log, ssd_d, ssd_norm_g, conv_qk_w, conv_qk_b, mlstm_i_bias, mlstm_f_bias, mlstm_norm_g,
              w_ssd_out, w_mlstm_out, w_out, router_w, router_bias, moe_w_gate, moe_w_up, moe_w_down,
              shared_w_gate, shared_w_up, shared_w_down, norm_final_g):
    rows = x.shape[1] // GRID_W
    cond_lat = jax.nn.silu(c)
    cond_ctx = jax.nn.silu(c_ctx)[None]
    for l in range(DEPTH):
        last = l == DEPTH - 1
        lp = {
            'w_in': w_in[l], 'conv_xbc_w': conv_xbc_w[l], 'conv_xbc_b': conv_xbc_b[l],
            'ssd_dt_bias': ssd_dt_bias[l], 'ssd_a_log': ssd_a_log[l], 'ssd_d': ssd_d[l], 'ssd_norm_g': ssd_norm_g[l],
            'conv_qk_w': conv_qk_w[l], 'conv_qk_b': conv_qk_b[l], 'mlstm_i_bias': mlstm_i_bias[l],
            'mlstm_f_bias': mlstm_f_bias[l], 'mlstm_norm_g': mlstm_norm_g[l],
            'w_ssd_out': w_ssd_out[l], 'w_mlstm_out': w_mlstm_out[l], 'w_out': w_out[l],
            'router_w': router_w[l], 'router_bias': router_bias[l],
            'moe_w_gate': moe_w_gate[l], 'moe_w_up': moe_w_up[l], 'moe_w_down': moe_w_down[l],
            'shared_w_gate': shared_w_gate[l], 'shared_w_up': shared_w_up[l], 'shared_w_down': shared_w_down[l],
        }
        mod_lat = jnp.split((cond_lat @ ada_w[l] + ada_b[l])[:, None], 6, axis=-1)
        mod_ctx = jnp.split((cond_ctx @ ada_w[l] + ada_b[l])[:, None], 6, axis=-1)
        p_ctx = prepare_mixer_inputs(modulate(ctx, norm_mix_g[l], mod_ctx[0], mod_ctx[1]), lp, 1)
        y_ssd_c, h_m_c, ctx_states = bidirectional_scans(p_ctx, lp, zero_states(ctx.shape[0]), not last)
        p_lat = prepare_mixer_inputs(modulate(x, norm_mix_g[l], mod_lat[0], mod_lat[1]), lp, rows)
        y_ssd, h_m, _ = bidirectional_scans(p_lat, lp, ctx_states, True)
        x = x + mod_lat[2] * merge_branches(p_lat, y_ssd, h_m, lp)
        x = x + mod_lat[5] * moe_ffn(modulate(x, norm_ffn_g[l], mod_lat[3], mod_lat[4]), lp)
        if not last:
            ctx = ctx + mod_ctx[2] * merge_branches(p_ctx, y_ssd_c, h_m_c, lp)
            ctx = ctx + mod_ctx[5] * moe_ffn(modulate(ctx, norm_ffn_g[l], mod_ctx[3], mod_ctx[4]), lp)
    return rmsnorm(x, norm_final_g)
```

```python
import functools

import numpy as np
import jax
import jax.numpy as jnp
from jax import lax
from jax.experimental import pallas as pl
from jax.experimental.pallas import tpu as pltpu

F32 = jnp.float32
BF16 = jnp.bfloat16
I32 = jnp.int32

EPS = 1e-6
CHUNK = 128
CONV_K = 5
GRID_W = 64
SSD_HEAD_DIM = 64
SSD_STATE = 128
SSD_GROUPS = 2
MLSTM_DV = 128
N_EXPERT_GROUPS = 8
TOPK_GROUPS = 4
TOP_K = 8
ROUTED_SCALE = 2.5

LANES = 128
TOKEN_TILE = 512
COL_CHUNK = 512
MOE_BLOCK = 256
DISPATCH_TILE = 256
VMEM_LIMIT = 56 * 1024 * 1024
NEG_INF = float("-inf")


def _dot(a, b):
    return jnp.dot(a, b, preferred_element_type=F32)


def _dot_nt(a, b):
    return lax.dot_general(a, b, (((1,), (1,)), ((), ())), preferred_element_type=F32)


def _dot_tn(a, b):
    return lax.dot_general(a, b, (((0,), (0,)), ((), ())), preferred_element_type=F32)


def _split_dot(v, e, passes):
    p = v.astype(BF16)
    out = _dot(p, e)
    r = v - p.astype(F32)
    for _ in range(passes - 1):
        p = r.astype(BF16)
        out = out + _dot(p, e)
        r = r - p.astype(F32)
    return out


def _sigmoid(v):
    return 1.0 / (1.0 + jnp.exp(-v))


def _softplus(v):
    return jnp.maximum(v, 0.0) + jnp.log1p(jnp.exp(-jnp.abs(v)))


def _adaln_kernel(c_ref, w_ref, b_ref, o_ref):
    c = c_ref[...]
    s = c * _sigmoid(c)
    w = w_ref[...]
    s_hi = s.astype(BF16)
    s_lo = (s - s_hi.astype(F32)).astype(BF16)
    w_hi = w.astype(BF16)
    w_lo = (w - w_hi.astype(F32)).astype(BF16)
    o_ref[...] = _dot(s_hi, w_hi) + _dot(s_lo, w_hi) + _dot(s_hi, w_lo) + b_ref[...]


def _adaln(cond, w, b):
    rows, d = cond.shape
    n = w.shape[1]
    tn = 1536 if n % 1536 == 0 else n
    return pl.pallas_call(
        _adaln_kernel,
        out_shape=jax.ShapeDtypeStruct((rows, n), F32),
        grid=(n // tn,),
        in_specs=[pl.BlockSpec((rows, d), lambda j: (0, 0)),
                  pl.BlockSpec((d, tn), lambda j: (0, j)),
                  pl.BlockSpec((1, tn), lambda j: (0, j))],
        out_specs=pl.BlockSpec((rows, tn), lambda j: (0, j)),
        compiler_params=pltpu.CompilerParams(dimension_semantics=("arbitrary",), vmem_limit_bytes=VMEM_LIMIT),
        name="adaln",
    )(cond, w, b.reshape(1, n))


CONV_SHIFTS = tuple(j - CONV_K // 2 for j in range(CONV_K) if j != CONV_K // 2)


def _conv_masks(tm, seg_len):
    pos = np.arange(tm) % seg_len
    m = np.zeros((tm, 8), np.float32)
    for i, s in enumerate(CONV_SHIFTS):
        m[:, i] = ((pos + s >= 0) & (pos + s < seg_len)).astype(np.float32)
    return m


def _conv_silu(acc, w5, bias, vm, tm):
    out = acc * w5[CONV_K // 2:CONV_K // 2 + 1] + bias
    for i, s in enumerate(CONV_SHIFTS):
        shifted = pltpu.roll(acc, (-s) % tm, axis=0)
        j = s + CONV_K // 2
        out = out + (shifted * vm[:, i:i + 1]) * w5[j:j + 1]
    return out * _sigmoid(out)


def _inproj_kernel(x_ref, ctx_ref, mod_ref, g_ref, wbig_ref, wsh_ref, wsl_ref, cwx_ref, cbx_ref, cwq_ref, cbq_ref,
                   smb_ref, aneg_ref, tril_ref, triu_ref, cmask_ref,
                   z_ref, xs_ref, bc_ref, qk_ref, v_ref, og_ref, mg_ref, small_ref, smallt_ref,
                   *, n_ctx_tiles, tm, d_model, dk):
    i = pl.program_id(0)
    is_ctx = i < n_ctx_tiles
    xt = jnp.where(is_ctx, ctx_ref[...], x_ref[...])
    m = mod_ref[0]
    ms = jnp.mean(xt * xt, axis=-1, keepdims=True)
    u = xt * lax.rsqrt(ms + EPS) * g_ref[...] * (1.0 + m[1:2]) + m[0:1]
    u_hi = u.astype(BF16)
    u_lo = (u - u_hi.astype(F32)).astype(BF16)
    vm = jnp.where(is_ctx, cmask_ref[1], cmask_ref[0])

    d = d_model
    d_bc = 2 * SSD_GROUPS * SSD_STATE
    pieces = [(z_ref, d, "plain", None), (xs_ref, d, "convx", 0), (bc_ref, d_bc, "convx", d),
              (qk_ref, d, "convq", 0), (v_ref, d, "plain", None), (og_ref, d, "sigmoid", None),
              (mg_ref, 2 * d, "sigmoid", None)]
    col = 0
    for ref, width, kind, coff in pieces:
        for c0 in range(0, width, COL_CHUNK):
            acc = _dot(u_hi, wbig_ref[:, col + c0:col + c0 + COL_CHUNK])
            if kind == "convx":
                cs = coff + c0
                acc = _conv_silu(acc, cwx_ref[:, cs:cs + COL_CHUNK], cbx_ref[:, cs:cs + COL_CHUNK], vm, tm)
            elif kind == "convq":
                acc = _conv_silu(acc, cwq_ref[:, c0:c0 + COL_CHUNK], cbq_ref[:, c0:c0 + COL_CHUNK], vm, tm)
                if c0 < width // 2:
                    acc = acc * (dk ** -0.5)
            elif kind == "sigmoid":
                acc = _sigmoid(acc)
            ref[:, c0:c0 + COL_CHUNK] = acc.astype(ref.dtype)
        col += width

    wsh = wsh_ref[...]
    raw = _dot(u_hi, wsh) + _dot(u_lo, wsh) + _dot(u_hi, wsl_ref[...]) + smb_ref[...]
    p1 = raw[:, :LANES]
    p2 = raw[:, LANES:]
    lane = lax.broadcasted_iota(I32, (tm, LANES), 1)
    n_dt = 2 * (d_model // SSD_HEAD_DIM)
    n_g = 2 * (d_model // MLSTM_DV)
    is_dt = lane < n_dt
    is_gate = jnp.logical_and(lane >= n_dt, lane < n_dt + n_g)
    dt = _softplus(p2)
    pa = jnp.where(is_dt, dt * aneg_ref[...], jnp.where(is_gate, -_softplus(-p1), 0.0))
    pb = jnp.where(is_dt, dt, jnp.where(is_gate, p2, 0.0))
    lane_c = lax.broadcasted_iota(I32, (CHUNK, LANES), 1)
    is_dt_c = lane_c < n_dt
    rev = jnp.logical_or(jnp.logical_and(lane_c >= n_dt // 2, lane_c < n_dt),
                         jnp.logical_and(lane_c >= n_dt + n_g // 2, lane_c < n_dt + n_g))
    tril = tril_ref[...]
    triu = triu_ref[...]
    tq = lax.broadcasted_iota(I32, (CHUNK, LANES), 0)
    for c in range(tm // CHUNK):
        r0 = c * CHUNK
        a_c = pa[r0:r0 + CHUNK]
        hi = a_c.astype(BF16)
        r1 = a_c - hi.astype(F32)
        mid = r1.astype(BF16)
        lo = (r1 - mid.astype(F32)).astype(BF16)
        cs_f = _dot(tril, hi) + _dot(tril, mid) + _dot(tril, lo)
        cs_b = _dot(triu, hi) + _dot(triu, mid) + _dot(triu, lo)
        plane_a = jnp.where(rev, cs_b, cs_f)
        plane_b = jnp.where(is_dt_c, pb[r0:r0 + CHUNK], pb[r0:r0 + CHUNK] - plane_a)
        yf = plane_b
        yb = plane_b
        s = 1
        while s < CHUNK:
            sh = pltpu.roll(yf, s, axis=0)
            yf = jnp.maximum(yf, jnp.where(tq >= s, sh, NEG_INF))
            sh = pltpu.roll(yb, CHUNK - s, axis=0)
            yb = jnp.maximum(yb, jnp.where(tq + s < CHUNK, sh, NEG_INF))
            s *= 2
        plane_c = jnp.where(rev, yb, yf)
        small_ref[r0:r0 + CHUNK, 0:LANES] = plane_a
        small_ref[r0:r0 + CHUNK, LANES:2 * LANES] = plane_b
        small_ref[r0:r0 + CHUNK, 2 * LANES:3 * LANES] = plane_c
        smallt_ref[c, 0] = plane_a.T
        smallt_ref[c, 1] = plane_b.T


def _inproj(x2, ctx2, mods3, g, wbig, wsh, wsl, cwx, cbx, cwq, cbq, smb, aneg, *, batch, seq, ctx_len, dk):
    d = x2.shape[1]
    tm = TOKEN_TILE
    n_ctx_tok = batch * ctx_len
    assert n_ctx_tok % tm == 0 and seq % tm == 0 and tm % ctx_len == 0 and tm % GRID_W == 0
    n_ctx_tiles = n_ctx_tok // tm
    tiles_per_batch = seq // tm
    n_tiles = n_ctx_tiles + batch * tiles_per_batch
    t_all = n_tiles * tm
    n_big = wbig.shape[1]
    tril = jnp.asarray(np.tril(np.ones((CHUNK, CHUNK), np.float32)), BF16)
    triu = jnp.asarray(np.triu(np.ones((CHUNK, CHUNK), np.float32)), BF16)

    def x_map(i):
        return (jnp.maximum(i - n_ctx_tiles, 0), 0)

    def ctx_map(i):
        return (jnp.minimum(i, n_ctx_tiles - 1), 0)

    def mod_map(i):
        return (jnp.where(i < n_ctx_tiles, batch, jnp.maximum(i - n_ctx_tiles, 0) // tiles_per_batch), 0, 0)

    const = lambda i: (0, 0)
    row = lambda i: (i, 0)
    cmask = jnp.asarray(np.stack([_conv_masks(tm, GRID_W), _conv_masks(tm, ctx_len)]))
    kern = functools.partial(_inproj_kernel, n_ctx_tiles=n_ctx_tiles, tm=tm, d_model=d, dk=dk)
    d_bc = 2 * SSD_GROUPS * SSD_STATE
    outs = [jax.ShapeDtypeStruct((t_all, d), BF16), jax.ShapeDtypeStruct((t_all, d), BF16),
            jax.ShapeDtypeStruct((t_all, d_bc), BF16), jax.ShapeDtypeStruct((t_all, d), BF16),
            jax.ShapeDtypeStruct((t_all, d), BF16), jax.ShapeDtypeStruct((t_all, d), BF16),
            jax.ShapeDtypeStruct((t_all, 2 * d), BF16), jax.ShapeDtypeStruct((t_all, 3 * LANES), F32),
            jax.ShapeDtypeStruct((t_all // CHUNK, 2, LANES, CHUNK), F32)]
    out_specs = [pl.BlockSpec((tm, d), row), pl.BlockSpec((tm, d), row), pl.BlockSpec((tm, d_bc), row),
                 pl.BlockSpec((tm, d), row), pl.BlockSpec((tm, d), row), pl.BlockSpec((tm, d), row),
                 pl.BlockSpec((tm, 2 * d), row), pl.BlockSpec((tm, 3 * LANES), row),
                 pl.BlockSpec((tm // CHUNK, 2, LANES, CHUNK), lambda i: (i, 0, 0, 0))]
    in_specs = [pl.BlockSpec((tm, d), x_map), pl.BlockSpec((tm, d), ctx_map),
                pl.BlockSpec((1, 6, d), mod_map), pl.BlockSpec((1, d), const),
                pl.BlockSpec((d, n_big), const, pipeline_mode=pl.Buffered(1)),
                pl.BlockSpec((d, 2 * LANES), const), pl.BlockSpec((d, 2 * LANES), const),
                pl.BlockSpec(cwx.shape, const), pl.BlockSpec(cbx.shape, const),
                pl.BlockSpec(cwq.shape, const), pl.BlockSpec(cbq.shape, const),
                pl.BlockSpec((1, 2 * LANES), const), pl.BlockSpec((1, LANES), const),
                pl.BlockSpec((CHUNK, CHUNK), const), pl.BlockSpec((CHUNK, CHUNK), const),
                pl.BlockSpec((2, tm, 8), lambda i: (0, 0, 0))]
    return pl.pallas_call(
        kern, out_shape=outs, grid=(n_tiles,), in_specs=in_specs, out_specs=out_specs,
        compiler_params=pltpu.CompilerParams(dimension_semantics=("arbitrary",), vmem_limit_bytes=VMEM_LIMIT),
        name="inproj",
    )(x2, ctx2, mods3, g, wbig, wsh, wsl, cwx, cbx, cwq, cbq, smb, aneg, tril, triu, cmask)


def _chunk_block_map(direction, batch, n_ctx_chunks, n_lat_chunks):
    def idx(b, s):
        if direction == 0:
            c_ctx = s
            c_lat = s - n_ctx_chunks
        else:
            c_ctx = n_ctx_chunks - 1 - s
            c_lat = n_lat_chunks - 1 - (s - n_ctx_chunks)
        return jnp.where(s < n_ctx_chunks, b * n_ctx_chunks + c_ctx, batch * n_ctx_chunks + b * n_lat_chunks + c_lat)
    return idx


def _ssd_kernel(xs_ref, bc_ref, small_ref, smallt_ref, e16_ref, ecol_ref, y_ref, st_ref, *, direction, n_heads):
    s = pl.program_id(1)

    @pl.when(s == 0)
    def _():
        st_ref[...] = jnp.zeros_like(st_ref)

    hpg = n_heads // SSD_GROUPS
    lane0 = direction * n_heads
    last = CHUNK - 1 if direction == 0 else 0
    lane = lax.broadcasted_iota(I32, (CHUNK, LANES), 1)
    lm = jnp.logical_and(lane >= lane0, lane < lane0 + n_heads)
    plane_a = jnp.where(lm, small_ref[:, 0:LANES], 0.0)
    plane_b = jnp.where(lm, small_ref[:, LANES:2 * LANES], 0.0)
    cum_t = smallt_ref[0, 0]
    tot = plane_a[last:last + 1]
    e16 = e16_ref[...]
    dtx = _split_dot(plane_b, e16, 2)
    ecx = _split_dot(jnp.where(lm, jnp.exp(plane_a), 0.0), e16, 2)
    wx = _split_dot(plane_b * jnp.exp(tot - plane_a), e16, 2)
    decx = _split_dot(jnp.where(lm[0:8], jnp.exp(jnp.broadcast_to(tot, (8, LANES))), 0.0), e16, 2)[0:1]
    cumcol = _split_dot(plane_a, ecol_ref[...], 3)

    xf = xs_ref[...].astype(F32)
    xdt = xf * dtx
    xw = (xf * wx).astype(BF16)
    plane64 = lax.broadcasted_iota(I32, (CHUNK, 2 * SSD_HEAD_DIM), 1) < SSD_HEAD_DIM
    iq = lax.broadcasted_iota(I32, (CHUNK, CHUNK), 0)
    ik = lax.broadcasted_iota(I32, (CHUNK, CHUNK), 1)
    mask = (iq >= ik) if direction == 0 else (iq <= ik)
    gw = hpg * SSD_HEAD_DIM
    for g in range(SSD_GROUPS):
        bg = bc_ref[:, g * SSD_STATE:(g + 1) * SSD_STATE]
        cg = bc_ref[:, (SSD_GROUPS + g) * SSD_STATE:(SSD_GROUPS + g + 1) * SSD_STATE]
        cb = _dot_nt(cg, bg)
        sg = st_ref[:, g * gw:(g + 1) * gw]
        y_inter = _dot(cg, sg.astype(BF16)) * ecx[:, g * gw:(g + 1) * gw]
        for pair in range(hpg // 2):
            h0 = g * hpg + 2 * pair
            c0 = h0 * SSD_HEAD_DIM
            xpair = xdt[:, c0:c0 + 2 * SSD_HEAD_DIM]
            acc = y_inter[:, 2 * pair * SSD_HEAD_DIM:(2 * pair + 2) * SSD_HEAD_DIM]
            for par in range(2):
                h = h0 + par
                seg = cumcol[:, h * LANES:(h + 1) * LANES] - cum_t[lane0 + h:lane0 + h + 1, :]
                m_h = (cb * jnp.exp(jnp.where(mask, seg, NEG_INF))).astype(BF16)
                keep = plane64 if par == 0 else jnp.logical_not(plane64)
                acc = acc + _dot(m_h, jnp.where(keep, xpair, 0.0).astype(BF16))
            y_ref[:, c0:c0 + 2 * SSD_HEAD_DIM] = acc.astype(y_ref.dtype)
        st_ref[:, g * gw:(g + 1) * gw] = decx[:, g * gw:(g + 1) * gw] * sg + _dot_tn(bg, xw[:, g * gw:(g + 1) * gw])


def _ssd_scan(xs, bc, small, smallt, *, direction, batch, n_ctx_chunks, n_lat_chunks):
    t_all, d = xs.shape
    n_heads = d // SSD_HEAD_DIM
    e16 = np.zeros((LANES, d), np.float32)
    ecol = np.zeros((LANES, n_heads * LANES), np.float32)
    for h in range(n_heads):
        e16[direction * n_heads + h, h * SSD_HEAD_DIM:(h + 1) * SSD_HEAD_DIM] = 1.0
        ecol[direction * n_heads + h, h * LANES:(h + 1) * LANES] = 1.0
    idx = _chunk_block_map(direction, batch, n_ctx_chunks, n_lat_chunks)
    n_steps = n_ctx_chunks + n_lat_chunks
    const = lambda b, s: (0, 0)
    kern = functools.partial(_ssd_kernel, direction=direction, n_heads=n_heads)
    return pl.pallas_call(
        kern,
        out_shape=jax.ShapeDtypeStruct((t_all, d), BF16),
        grid=(batch, n_steps),
        in_specs=[pl.BlockSpec((CHUNK, d), lambda b, s: (idx(b, s), 0)),
                  pl.BlockSpec((CHUNK, bc.shape[1]), lambda b, s: (idx(b, s), 0)),
                  pl.BlockSpec((CHUNK, 3 * LANES), lambda b, s: (idx(b, s), 0)),
                  pl.BlockSpec((1, 2, LANES, CHUNK), lambda b, s: (idx(b, s), 0, 0, 0)),
                  pl.BlockSpec(e16.shape, const), pl.BlockSpec(ecol.shape, const)],
        out_specs=pl.BlockSpec((CHUNK, d), lambda b, s: (idx(b, s), 0)),
        scratch_shapes=[pltpu.VMEM((SSD_STATE, d), F32)],
        compiler_params=pltpu.CompilerParams(dimension_semantics=("arbitrary", "arbitrary"),
                                             vmem_limit_bytes=VMEM_LIMIT),
        name=f"ssd_scan_d{direction}",
    )(xs, bc, small, smallt, jnp.asarray(e16, BF16), jnp.asarray(ecol, BF16))


def _mlstm_kernel(qk_ref, v_ref, small_ref, smallt_ref, em_ref, h_ref, cn_ref, m_ref, *, direction, n_heads, dk):
    s = pl.program_id(1)

    @pl.when(s == 0)
    def _():
        cn_ref[...] = jnp.zeros_like(cn_ref)
        m_ref[...] = jnp.zeros_like(m_ref)

    n_dt = 2 * (n_heads * MLSTM_DV // SSD_HEAD_DIM)
    lane0 = n_dt + direction * n_heads
    last = CHUNK - 1 if direction == 0 else 0
    lane = lax.broadcasted_iota(I32, (CHUNK, LANES), 1)
    lm = jnp.logical_and(lane >= lane0, lane < lane0 + n_heads)
    b_q = jnp.where(lm, small_ref[:, 0:LANES], 0.0)
    r_k = jnp.where(lm, small_ref[:, LANES:2 * LANES], 0.0)
    cmr = jnp.where(lm, small_ref[:, 2 * LANES:3 * LANES], 0.0)
    r_t = smallt_ref[0, 1]
    m_all = m_ref[...]
    m_row = m_all[0:1]
    mm = jnp.maximum(cmr, m_row)
    w_state = jnp.exp(m_row - mm)
    e_mq = jnp.exp(-(b_q + mm))
    em = em_ref[...]
    mmx = _split_dot(mm, em, 3)
    wsx = _split_dot(jnp.where(lm, w_state, 0.0), em, 2)
    emqx = _split_dot(jnp.where(lm, e_mq, 0.0), em, 2)
    m_base8 = jnp.maximum(m_all, jnp.broadcast_to(cmr[last:last + 1], (8, LANES)))
    m_base = m_base8[0:1]
    w_k = jnp.where(lm, jnp.exp(r_k - m_base), 0.0)
    wkx = _split_dot(w_k, em, 2)
    decx = _split_dot(jnp.where(lm[0:8], jnp.exp(m_all - m_base8), 0.0), em, 2)[0:1]
    m_ref[...] = jnp.where(lm[0:8], jnp.broadcast_to(b_q[last:last + 1], (8, LANES)) + m_base8, 0.0)

    iq = lax.broadcasted_iota(I32, (CHUNK, CHUNK), 0)
    ik = lax.broadcasted_iota(I32, (CHUNK, CHUNK), 1)
    mask = (iq >= ik) if direction == 0 else (iq <= ik)
    ones = jnp.ones((CHUNK, MLSTM_DV), BF16)
    d_qk = n_heads * dk
    for h in range(n_heads):
        qh = qk_ref[:, h * dk:(h + 1) * dk]
        kh = qk_ref[:, d_qk + h * dk:d_qk + (h + 1) * dk]
        vh = v_ref[:, h * MLSTM_DV:(h + 1) * MLSTM_DV]
        hs = slice(h * MLSTM_DV, (h + 1) * MLSTM_DV)
        dmat = jnp.exp(jnp.where(mask, r_t[lane0 + h:lane0 + h + 1, :] - mmx[:, hs], NEG_INF))
        smat = (_dot_nt(qh, kh) * dmat).astype(BF16)
        cn = cn_ref[h]
        wsh = wsx[:, hs]
        tot = _dot(smat, jnp.concatenate([vh, ones], axis=1)) + jnp.concatenate([wsh, wsh], axis=1) * _dot(qh, cn.astype(BF16))
        num = tot[:, :MLSTM_DV]
        den = tot[:, MLSTM_DV:]
        h_ref[:, hs] = (num / jnp.maximum(jnp.abs(den), emqx[:, hs])).astype(h_ref.dtype)
        wkh = wkx[:, hs]
        rhs = jnp.concatenate([(vh.astype(F32) * wkh).astype(BF16), wkh.astype(BF16)], axis=1)
        dech = decx[:, hs]
        cn_ref[h] = jnp.concatenate([dech, dech], axis=1) * cn + _dot_tn(kh, rhs)


def _mlstm_scan(qk, v, small, smallt, *, direction, batch, n_ctx_chunks, n_lat_chunks, dk):
    t_all, d = v.shape
    n_heads = d // MLSTM_DV
    n_dt = 2 * (d // SSD_HEAD_DIM)
    em = np.zeros((LANES, d), np.float32)
    for h in range(n_heads):
        em[n_dt + direction * n_heads + h, h * MLSTM_DV:(h + 1) * MLSTM_DV] = 1.0
    idx = _chunk_block_map(direction, batch, n_ctx_chunks, n_lat_chunks)
    n_steps = n_ctx_chunks + n_lat_chunks
    kern = functools.partial(_mlstm_kernel, direction=direction, n_heads=n_heads, dk=dk)
    return pl.pallas_call(
        kern,
        out_shape=jax.ShapeDtypeStruct((t_all, d), BF16),
        grid=(batch, n_steps),
        in_specs=[pl.BlockSpec((CHUNK, qk.shape[1]), lambda b, s: (idx(b, s), 0)),
                  pl.BlockSpec((CHUNK, d), lambda b, s: (idx(b, s), 0)),
                  pl.BlockSpec((CHUNK, 3 * LANES), lambda b, s: (idx(b, s), 0)),
                  pl.BlockSpec((1, 2, LANES, CHUNK), lambda b, s: (idx(b, s), 0, 0, 0)),
                  pl.BlockSpec(em.shape, lambda b, s: (0, 0))],
        out_specs=pl.BlockSpec((CHUNK, d), lambda b, s: (idx(b, s), 0)),
        scratch_shapes=[pltpu.VMEM((n_heads, dk, 2 * MLSTM_DV), F32), pltpu.VMEM((8, LANES), F32)],
        compiler_params=pltpu.CompilerParams(dimension_semantics=("arbitrary", "arbitrary"),
                                             vmem_limit_bytes=VMEM_LIMIT),
        name=f"mlstm_scan_d{direction}",
    )(qk, v, small, smallt, jnp.asarray(em, BF16))


def _merge_kernel(y0_ref, y1_ref, xs_ref, z_ref, h0_ref, h1_ref, og_ref, mg_ref, x_ref, mod_ref, dexp_ref, gs_ref,
                  gm_ref, gf_ref, wso_ref, wmo_ref, wo_ref, x1_ref, u2_ref, *, d_model):
    d = d_model
    m = mod_ref[0]
    y = y0_ref[...].astype(F32) + y1_ref[...].astype(F32) + dexp_ref[...] * xs_ref[...].astype(F32)
    zf = z_ref[...].astype(F32)
    y = y * (zf * _sigmoid(zf))
    y = y * lax.rsqrt(jnp.mean(y * y, axis=-1, keepdims=True) + EPS) * gs_ref[...]
    a = _dot(y.astype(BF16), wso_ref[...])
    hm = h0_ref[...].astype(F32) + h1_ref[...].astype(F32)
    parts = []
    for h in range(d // MLSTM_DV):
        blk = hm[:, h * MLSTM_DV:(h + 1) * MLSTM_DV]
        parts.append(blk * lax.rsqrt(jnp.mean(blk * blk, axis=-1, keepdims=True) + EPS))
    hn = jnp.concatenate(parts, axis=1) * gm_ref[...]
    hh = (og_ref[...].astype(F32) * hn).astype(BF16)
    bm = _dot(hh, wmo_ref[...])
    merged = mg_ref[:, :d].astype(F32) * a + mg_ref[:, d:].astype(F32) * bm
    r = _dot(merged.astype(BF16), wo_ref[...])
    x1 = x_ref[...] + m[2:3] * r
    x1_ref[...] = x1
    u2 = x1 * lax.rsqrt(jnp.mean(x1 * x1, axis=-1, keepdims=True) + EPS) * gf_ref[...] * (1.0 + m[4:5]) + m[3:4]
    u2_ref[...] = u2.astype(u2_ref.dtype)


def _merge(y0, y1, xs, z, h0, h1, og, mg, x2, mods3, dexp, gs, gm, gf, wso, wmo, wo, *, batch, seq, n_ctx_tok):
    t, d = x2.shape
    tm = TOKEN_TILE
    off = n_ctx_tok // tm
    tiles_per_batch = seq // tm
    lat = lambda i: (i + off, 0)
    row = lambda i: (i, 0)
    const = lambda i: (0, 0)
    kern = functools.partial(_merge_kernel, d_model=d)
    wspec = pl.BlockSpec((d, d), const, pipeline_mode=pl.Buffered(1))
    return pl.pallas_call(
        kern,
        out_shape=[jax.ShapeDtypeStruct((t, d), F32), jax.ShapeDtypeStruct((t, d), BF16)],
        grid=(t // tm,),
        in_specs=[pl.BlockSpec((tm, d), lat), pl.BlockSpec((tm, d), lat), pl.BlockSpec((tm, d), lat),
                  pl.BlockSpec((tm, d), lat), pl.BlockSpec((tm, d), lat), pl.BlockSpec((tm, d), lat),
                  pl.BlockSpec((tm, d), lat), pl.BlockSpec((tm, 2 * d), lat), pl.BlockSpec((tm, d), row),
                  pl.BlockSpec((1, 6, d), lambda i: (i // tiles_per_batch, 0, 0)),
                  pl.BlockSpec((1, d), const), pl.BlockSpec((1, d), const), pl.BlockSpec((1, d), const),
                  pl.BlockSpec((1, d), const), wspec, wspec, wspec],
        out_specs=[pl.BlockSpec((tm, d), row), pl.BlockSpec((tm, d), row)],
        compiler_params=pltpu.CompilerParams(dimension_semantics=("arbitrary",), vmem_limit_bytes=VMEM_LIMIT),
        name="merge",
    )(y0, y1, xs, z, h0, h1, og, mg, x2, mods3, dexp, gs, gm, gf, wso, wmo, wo)


def _first_index_of_max(vals, row_iota, n_rows):
    mx = jnp.max(vals, axis=0, keepdims=True)
    idx = jnp.min(jnp.where(vals == mx, row_iota, n_rows), axis=0, keepdims=True)
    return mx, idx


def _router_kernel(u_ref, wt_ref, bias_ref, su_ref, idx_ref, pos_ref, wts_ref, cnt_ref, run_ref, *, n_experts, tr):
    i = pl.program_id(0)

    @pl.when(i == 0)
    def _():
        run_ref[...] = jnp.zeros_like(run_ref)

    scores = _sigmoid(_dot_nt(wt_ref[...], u_ref[...]))
    biased = scores + bias_ref[...]
    gsz = n_experts // N_EXPERT_GROUPS
    gi = lax.broadcasted_iota(I32, (gsz, tr), 0).astype(F32)
    gscores = []
    for g in range(N_EXPERT_GROUPS):
        blk = biased[g * gsz:(g + 1) * gsz]
        m1, i1 = _first_index_of_max(blk, gi, gsz)
        m2 = jnp.max(jnp.where(gi == i1, NEG_INF, blk), axis=0, keepdims=True)
        gscores.append(m1 + m2)
    gs = jnp.concatenate(gscores, axis=0)
    g8 = lax.broadcasted_iota(I32, (N_EXPERT_GROUPS, tr), 0).astype(F32)
    gsel = jnp.zeros((N_EXPERT_GROUPS, tr), F32)
    for _ in range(TOPK_GROUPS):
        _, gidx = _first_index_of_max(gs, g8, N_EXPERT_GROUPS)
        hit = g8 == gidx
        gsel = jnp.where(hit, 1.0, gsel)
        gs = jnp.where(hit, NEG_INF, gs)
    cand = jnp.concatenate(
        [jnp.where(jnp.broadcast_to(gsel[g:g + 1], (gsz, tr)) > 0.5, biased[g * gsz:(g + 1) * gsz], NEG_INF)
         for g in range(N_EXPERT_GROUPS)], axis=0)
    ei = lax.broadcasted_iota(I32, (n_experts, tr), 0).astype(F32)
    sel = jnp.zeros((n_experts, tr), F32)
    idxs, ws = [], []
    for _ in range(TOP_K):
        _, eidx = _first_index_of_max(cand, ei, n_experts)
        hit = ei == eidx
        ws.append(jnp.sum(jnp.where(hit, scores, 0.0), axis=0, keepdims=True))
        idxs.append(eidx)
        sel = jnp.where(hit, 1.0, sel)
        cand = jnp.where(hit, NEG_INF, cand)
    wk = jnp.concatenate(ws, axis=0)
    wts_ref[...] = ROUTED_SCALE * wk / jnp.sum(wk, axis=0, keepdims=True)
    idx_ref[...] = jnp.concatenate(idxs, axis=0).astype(I32)
    selb = sel.astype(BF16)
    posmat = _dot(selb, su_ref[...]) + run_ref[:, 0:1]
    pos_ref[...] = jnp.concatenate(
        [jnp.sum(jnp.where(ei == idxs[k], posmat, 0.0), axis=0, keepdims=True) for k in range(TOP_K)],
        axis=0).astype(I32)
    run = run_ref[...] + _dot(selb, jnp.ones((tr, LANES), BF16))
    run_ref[...] = run
    cnt_ref[...] = run


def _router(u2, router_wt, bias_col):
    t, d = u2.shape
    n_experts = router_wt.shape[0]
    tr = TOKEN_TILE
    su = jnp.asarray(np.triu(np.ones((tr, tr), np.float32), 1), BF16)
    kern = functools.partial(_router_kernel, n_experts=n_experts, tr=tr)
    col = lambda i: (0, i)
    const = lambda i: (0, 0)
    return pl.pallas_call(
        kern,
        out_shape=[jax.ShapeDtypeStruct((TOP_K, t), I32), jax.ShapeDtypeStruct((TOP_K, t), I32),
                   jax.ShapeDtypeStruct((TOP_K, t), F32), jax.ShapeDtypeStruct((n_experts, LANES), F32)],
        grid=(t // tr,),
        in_specs=[pl.BlockSpec((tr, d), lambda i: (i, 0)), pl.BlockSpec((n_experts, d), const),
                  pl.BlockSpec((n_experts, 1), const), pl.BlockSpec((tr, tr), const)],
        out_specs=[pl.BlockSpec((TOP_K, tr), col), pl.BlockSpec((TOP_K, tr), col), pl.BlockSpec((TOP_K, tr), col),
                   pl.BlockSpec((n_experts, LANES), const)],
        scratch_shapes=[pltpu.VMEM((n_experts, LANES), F32)],
        compiler_params=pltpu.CompilerParams(dimension_semantics=("arbitrary",), vmem_limit_bytes=VMEM_LIMIT),
        name="router",
    )(u2, router_wt, bias_col, su)


def _dispatch_kernel(pstart_ref, idx_ref, pos_ref, u_ref, zeros_ref, xs_ref, sem, *, td):
    del zeros_ref

    def copy(t, k):
        dst = pstart_ref[idx_ref[k, t]] + pos_ref[k, t]
        return pltpu.make_async_copy(u_ref.at[pl.ds(t, 1)], xs_ref.at[pl.ds(dst, 1)], sem)

    def start(t, carry):
        for k in range(TOP_K):
            copy(t, k).start()
        return carry

    def wait(t, carry):
        for k in range(TOP_K):
            copy(t, k).wait()
        return carry

    lax.fori_loop(0, td, start, 0)
    lax.fori_loop(0, td, wait, 0)


def _dispatch(pad_start, idx, pos, u2f, n_slots):
    t, d = u2f.shape
    td = DISPATCH_TILE
    zeros = jnp.zeros((n_slots, d), u2f.dtype)
    kern = functools.partial(_dispatch_kernel, td=td)
    smem = lambda: pl.BlockSpec((TOP_K, td), lambda i, ps: (0, i), memory_space=pltpu.SMEM)
    return pl.pallas_call(
        kern,
        out_shape=jax.ShapeDtypeStruct((n_slots, d), u2f.dtype),
        grid_spec=pltpu.PrefetchScalarGridSpec(
            num_scalar_prefetch=1, grid=(t // td,),
            in_specs=[smem(), smem(), pl.BlockSpec((td, d), lambda i, ps: (i, 0)),
                      pl.BlockSpec(memory_space=pl.ANY)],
            out_specs=pl.BlockSpec(memory_space=pl.ANY),
            scratch_shapes=[pltpu.SemaphoreType.DMA]),
        input_output_aliases={4: 0},
        compiler_params=pltpu.CompilerParams(dimension_semantics=("arbitrary",), vmem_limit_bytes=VMEM_LIMIT),
        name="dispatch",
    )(pad_start, idx, pos, u2f, zeros)


def _experts_kernel(be_ref, nu_ref, x_ref, wg_ref, wu_ref, wd_ref, y_ref):
    j = pl.program_id(0)

    @pl.when(j < nu_ref[0])
    def _():
        xb = x_ref[...].astype(BF16)
        hg = _dot(xb, wg_ref[0].astype(BF16))
        hu = _dot(xb, wu_ref[0].astype(BF16))
        hb = (hg * _sigmoid(hg) * hu).astype(BF16)
        y_ref[...] = _dot(hb, wd_ref[0].astype(BF16)).astype(y_ref.dtype)


def _experts(block_expert, n_used, xsorted, wg, wu, wd):
    n_slots, d = xsorted.shape
    de = wg.shape[2]
    nb = n_slots // MOE_BLOCK
    blk = lambda j, be, nu: (jnp.minimum(j, nu[0] - 1), 0)
    wmap = lambda j, be, nu: (be[jnp.minimum(j, nu[0] - 1)], 0, 0)
    return pl.pallas_call(
        _experts_kernel,
        out_shape=jax.ShapeDtypeStruct((n_slots, d), F32),
        grid_spec=pltpu.PrefetchScalarGridSpec(
            num_scalar_prefetch=2, grid=(nb,),
            in_specs=[pl.BlockSpec((MOE_BLOCK, d), blk), pl.BlockSpec((1, d, de), wmap),
                      pl.BlockSpec((1, d, de), wmap), pl.BlockSpec((1, de, d), wmap)],
            out_specs=pl.BlockSpec((MOE_BLOCK, d), blk)),
        compiler_params=pltpu.CompilerParams(dimension_semantics=("arbitrary",), vmem_limit_bytes=VMEM_LIMIT),
        name="experts",
    )(block_expert, n_used, xsorted, wg, wu, wd)


def _combine_kernel(pstart_ref, idx_ref, pos_ref, ys_ref, wt_ref, x1_ref, u_ref, mod_ref, wsg_ref, wsu_ref, wsd_ref,
                    gfin_ref, o_ref, gbuf, sem, *, tc):
    def copy(t, k):
        src = pstart_ref[idx_ref[k, t]] + pos_ref[k, t]
        return pltpu.make_async_copy(ys_ref.at[pl.ds(src, 1)], gbuf.at[k, pl.ds(t, 1)], sem)

    def start(t, carry):
        for k in range(TOP_K):
            copy(t, k).start()
        return carry

    def wait(t, carry):
        for k in range(TOP_K):
            copy(t, k).wait()
        return carry

    lax.fori_loop(0, tc, start, 0)
    ub = u_ref[...]
    hg = _dot(ub, wsg_ref[...])
    hu = _dot(ub, wsu_ref[...])
    acc = _dot((hg * _sigmoid(hg) * hu).astype(BF16), wsd_ref[...])
    lax.fori_loop(0, tc, wait, 0)
    wt = wt_ref[...]
    for k in range(TOP_K):
        acc = acc + gbuf[k] * wt[:, k:k + 1]
    m = mod_ref[0]
    xo = x1_ref[...] + m[5:6] * acc
    o_ref[...] = xo * lax.rsqrt(jnp.mean(xo * xo, axis=-1, keepdims=True) + EPS) * gfin_ref[...]


def _combine(pad_start, idx, pos, ysorted, wts_t, x1, u2, mods3, wsg, wsu, wsd, gfin, *, seq):
    t, d = x1.shape
    tc = DISPATCH_TILE
    tiles_per_batch = seq // tc
    kern = functools.partial(_combine_kernel, tc=tc)
    smem = lambda: pl.BlockSpec((TOP_K, tc), lambda i, ps: (0, i), memory_space=pltpu.SMEM)
    row = lambda i, ps: (i, 0)
    const = lambda i, ps: (0, 0)
    return pl.pallas_call(
        kern,
        out_shape=jax.ShapeDtypeStruct((t, d), F32),
        grid_spec=pltpu.PrefetchScalarGridSpec(
            num_scalar_prefetch=1, grid=(t // tc,),
            in_specs=[smem(), smem(), pl.BlockSpec(memory_space=pl.ANY), pl.BlockSpec((tc, TOP_K), row),
                      pl.BlockSpec((tc, d), row), pl.BlockSpec((tc, d), row),
                      pl.BlockSpec((1, 6, d), lambda i, ps: (i // tiles_per_batch, 0, 0)),
                      pl.BlockSpec(wsg.shape, const), pl.BlockSpec(wsu.shape, const), pl.BlockSpec(wsd.shape, const),
                      pl.BlockSpec((1, d), const)],
            out_specs=pl.BlockSpec((tc, d), row),
            scratch_shapes=[pltpu.VMEM((TOP_K, tc, d), F32), pltpu.SemaphoreType.DMA]),
        compiler_params=pltpu.CompilerParams(dimension_semantics=("arbitrary",), vmem_limit_bytes=VMEM_LIMIT),
        name="combine",
    )(pad_start, idx, pos, ysorted, wts_t, x1, u2, mods3, wsg, wsu, wsd, gfin)


def kernel(x, c, ctx, c_ctx, ada_w, ada_b, norm_mix_g, norm_ffn_g, w_in, conv_xbc_w, conv_xbc_b, ssd_dt_bias, ssd_a_log, ssd_d, ssd_norm_g, conv_qk_w, conv_qk_b, mlstm_i_bias, mlstm_f_bias, mlstm_norm_g, w_ssd_out, w_mlstm_out, w_out, router_w, router_bias, moe_w_gate, moe_w_up, moe_w_down, shared_w_gate, shared_w_up, shared_w_down, norm_final_g):
    batch, seq, d = x.shape
    ctx_len = ctx.shape[1]
    depth = ada_w.shape[0]
    assert depth == 1, "only the single-layer configuration is implemented"
    assert seq % CHUNK == 0 and ctx_len % CHUNK == 0 and seq % GRID_W == 0
    l = 0
    n_sh = d // SSD_HEAD_DIM
    n_mh = d // MLSTM_DV
    dk = MLSTM_DV // 2
    d_xbc = d + 2 * SSD_GROUPS * SSD_STATE
    d_qk = 2 * n_mh * dk
    sizes = (d, d_xbc, 2 * n_sh, d_qk, d, 4 * n_mh, d, 2 * d)
    offs = np.concatenate([[0], np.cumsum(sizes)])
    assert offs[-1] == w_in.shape[2] and 2 * n_sh + 2 * n_mh <= LANES

    cond = jnp.concatenate([c, c_ctx[None], jnp.zeros((8 - (batch + 1) % 8, d), F32)], axis=0)
    mods = _adaln(cond, ada_w[l], ada_b[l])
    mods3 = mods.reshape(mods.shape[0], 6, d)

    w = w_in[l]
    seg = lambda k: w[:, offs[k]:offs[k + 1]]
    wbig = jnp.concatenate([seg(0), seg(1), seg(3), seg(4), seg(6), seg(7)], axis=1).astype(BF16)
    w_dt = seg(2)
    w_g = seg(5).reshape(d, 2, 2, n_mh)
    w_i = w_g[:, :, 0].reshape(d, 2 * n_mh)
    w_f = w_g[:, :, 1].reshape(d, 2 * n_mh)
    pad = jnp.zeros((d, LANES - 2 * n_sh - 2 * n_mh), F32)
    wsm = jnp.concatenate([w_dt, w_f, pad, w_dt, w_i, pad], axis=1)
    wsh = wsm.astype(BF16)
    wsl = (wsm - wsh.astype(F32)).astype(BF16)
    padb = jnp.zeros((LANES - 2 * n_sh - 2 * n_mh,), F32)
    dtb = ssd_dt_bias[l].reshape(-1).astype(F32)
    smb = jnp.concatenate([dtb, mlstm_f_bias[l].reshape(-1).astype(F32), padb,
                           dtb, mlstm_i_bias[l].reshape(-1).astype(F32), padb]).reshape(1, 2 * LANES)
    aneg = jnp.concatenate([-jnp.exp(ssd_a_log[l].astype(F32)).reshape(-1),
                            jnp.zeros((LANES - 2 * n_sh,), F32)]).reshape(1, LANES)

    x2 = x.reshape(batch * seq, d)
    ctx2 = ctx.reshape(batch * ctx_len, d)
    z, xs, bc, qk, v, og, mg, small, smallt = _inproj(
        x2, ctx2, mods3, norm_mix_g[l].reshape(1, d), wbig, wsh, wsl,
        conv_xbc_w[l], conv_xbc_b[l].reshape(1, d_xbc), conv_qk_w[l], conv_qk_b[l].reshape(1, d_qk), smb, aneg,
        batch=batch, seq=seq, ctx_len=ctx_len, dk=dk)

    ncc = ctx_len // CHUNK
    ncl = seq // CHUNK
    scan_args = dict(batch=batch, n_ctx_chunks=ncc, n_lat_chunks=ncl)
    y0 = _ssd_scan(xs, bc, small, smallt, direction=0, **scan_args)
    y1 = _ssd_scan(xs, bc, small, smallt, direction=1, **scan_args)
    h0 = _mlstm_scan(qk, v, small, smallt, direction=0, dk=dk, **scan_args)
    h1 = _mlstm_scan(qk, v, small, smallt, direction=1, dk=dk, **scan_args)

    dexp = jnp.repeat(ssd_d[l].astype(F32), SSD_HEAD_DIM).reshape(1, d)
    x1, u2 = _merge(y0, y1, xs, z, h0, h1, og, mg, x2, mods3, dexp, ssd_norm_g[l].reshape(1, d),
                    mlstm_norm_g[l].reshape(1, d), norm_ffn_g[l].reshape(1, d),
                    w_ssd_out[l].astype(BF16), w_mlstm_out[l].astype(BF16), w_out[l].astype(BF16),
                    batch=batch, seq=seq, n_ctx_tok=batch * ctx_len)

    n_experts = router_w.shape[2]
    idx, pos, wts, cnt = _router(u2, router_w[l].T.astype(BF16), router_bias[l].astype(F32).reshape(n_experts, 1))
    counts = cnt[:, 0].astype(I32)
    padded = (counts + MOE_BLOCK - 1) // MOE_BLOCK * MOE_BLOCK
    pad_end = jnp.cumsum(padded)
    pad_start = (pad_end - padded).astype(I32)
    t = batch * seq
    nb = t * TOP_K // MOE_BLOCK + n_experts
    n_used = (pad_end[-1] // MOE_BLOCK).astype(I32).reshape(1)
    block_expert = jnp.minimum(jnp.searchsorted(pad_end, jnp.arange(nb) * MOE_BLOCK, side="right"),
                               n_experts - 1).astype(I32)

    xsorted = _dispatch(pad_start, idx, pos, u2.astype(F32), nb * MOE_BLOCK)
    ysorted = _experts(block_expert, n_used, xsorted, moe_w_gate[l], moe_w_up[l], moe_w_down[l])
    out = _combine(pad_start, idx, pos, ysorted, wts.T, x1, u2, mods3,
                   shared_w_gate[l].astype(BF16), shared_w_up[l].astype(BF16), shared_w_down[l].astype(BF16),
                   norm_final_g.reshape(1, d), seq=seq)
    return out.reshape(batch, seq, d)
```

```python
import functools

import numpy as np
import jax
import jax.numpy as jnp
from jax import lax
from jax.experimental import pallas as pl
from jax.experimental.pallas import tpu as pltpu

F32 = jnp.float32
BF16 = jnp.bfloat16
I32 = jnp.int32
U32 = jnp.uint32

EPS = 1e-6
CHUNK = 128
CONV_K = 5
GRID_W = 64
SSD_HEAD_DIM = 64
SSD_STATE = 128
SSD_GROUPS = 2
MLSTM_DV = 128
N_EXPERT_GROUPS = 8
TOPK_GROUPS = 4
TOP_K = 8
ROUTED_SCALE = 2.5

LANES = 128
SUBLANES = 8
TOKEN_TILE = 512
COL_CHUNK = 512
MOE_BLOCK = 256
DISPATCH_TILE = 256
VMEM_LIMIT = 56 * 1024 * 1024
NEG_INF = float("-inf")


def _dot(a, b):
    return jnp.dot(a, b, preferred_element_type=F32)


def _dot_nt(a, b):
    return lax.dot_general(a, b, (((1,), (1,)), ((), ())), preferred_element_type=F32)


def _dot_tn(a, b):
    return lax.dot_general(a, b, (((0,), (0,)), ((), ())), preferred_element_type=F32)


def _split_dot(v, e, passes):
    p = v.astype(BF16)
    out = _dot(p, e)
    r = v - p.astype(F32)
    for _ in range(passes - 1):
        p = r.astype(BF16)
        out = out + _dot(p, e)
        r = r - p.astype(F32)
    return out


def _sigmoid(v):
    return 1.0 / (1.0 + jnp.exp(-v))


def _pack_halves(v):
    n = v.shape[1] // 2
    hi = lax.bitcast_convert_type(v[:, :n].astype(BF16).astype(F32), U32)
    lo = lax.bitcast_convert_type(v[:, n:].astype(BF16).astype(F32), U32)
    return hi | (lo >> 16)


def _unpack_halves(p):
    left = lax.bitcast_convert_type(p & jnp.uint32(0xFFFF0000), F32)
    right = lax.bitcast_convert_type(p << 16, F32)
    return left, right


def _softplus(v):
    return jnp.maximum(v, 0.0) + jnp.log1p(jnp.exp(-jnp.abs(v)))


def _adaln_kernel(c_ref, w_ref, b_ref, o_ref):
    c = c_ref[...]
    s = c * _sigmoid(c)
    w = w_ref[...]
    s_hi = s.astype(BF16)
    s_lo = (s - s_hi.astype(F32)).astype(BF16)
    w_hi = w.astype(BF16)
    w_lo = (w - w_hi.astype(F32)).astype(BF16)
    o_ref[...] = _dot(s_hi, w_hi) + _dot(s_lo, w_hi) + _dot(s_hi, w_lo) + b_ref[...]


def _adaln(cond, w, b):
    rows, d = cond.shape
    n = w.shape[1]
    tn = 1536 if n % 1536 == 0 else n
    return pl.pallas_call(
        _adaln_kernel,
        out_shape=jax.ShapeDtypeStruct((rows, n), F32),
        grid=(n // tn,),
        in_specs=[pl.BlockSpec((rows, d), lambda j: (0, 0)),
                  pl.BlockSpec((d, tn), lambda j: (0, j)),
                  pl.BlockSpec((1, tn), lambda j: (0, j))],
        out_specs=pl.BlockSpec((rows, tn), lambda j: (0, j)),
        compiler_params=pltpu.CompilerParams(dimension_semantics=("arbitrary",), vmem_limit_bytes=VMEM_LIMIT),
        name="adaln",
    )(cond, w, b.reshape(1, n))


CONV_SHIFTS = tuple(j - CONV_K // 2 for j in range(CONV_K) if j != CONV_K // 2)


def _conv_masks(tm, seg_len):
    pos = np.arange(tm) % seg_len
    m = np.zeros((tm, 8), np.float32)
    for i, s in enumerate(CONV_SHIFTS):
        m[:, i] = ((pos + s >= 0) & (pos + s < seg_len)).astype(np.float32)
    return m


def _conv_silu(acc, w5, bias, vm, tm):
    out = acc * w5[CONV_K // 2:CONV_K // 2 + 1] + bias
    for i, s in enumerate(CONV_SHIFTS):
        shifted = pltpu.roll(acc, (-s) % tm, axis=0)
        j = s + CONV_K // 2
        out = out + (shifted * vm[:, i:i + 1]) * w5[j:j + 1]
    return out * _sigmoid(out)


def _inproj_kernel(x_ref, ctx_ref, mod_ref, g_ref, wbig_ref, wsh_ref, wsl_ref, cwx_ref, cbx_ref, cwq_ref, cbq_ref,
                   smb_ref, aneg_ref, tril_ref, triu_ref, cmask_ref,
                   z_ref, xs_ref, bc_ref, qk_ref, v_ref, og_ref, mg_ref, small_ref, smallt_ref,
                   *, n_ctx_tiles, tm, d_model, dk):
    i = pl.program_id(0)
    is_ctx = i < n_ctx_tiles
    xt = jnp.where(is_ctx, ctx_ref[...], x_ref[...])
    m = mod_ref[0]
    ms = jnp.mean(xt * xt, axis=-1, keepdims=True)
    u = xt * lax.rsqrt(ms + EPS) * g_ref[...] * (1.0 + m[1:2]) + m[0:1]
    u_hi = u.astype(BF16)
    u_lo = (u - u_hi.astype(F32)).astype(BF16)
    vm = jnp.where(is_ctx, cmask_ref[1], cmask_ref[0])

    d = d_model
    d_bc = 2 * SSD_GROUPS * SSD_STATE
    pieces = [(z_ref, d, "plain", None), (xs_ref, d, "convx", 0), (bc_ref, d_bc, "convx", d),
              (qk_ref, d, "convq", 0), (v_ref, d, "plain", None), (og_ref, d, "sigmoid", None),
              (mg_ref, 2 * d, "sigmoid", None)]
    col = 0
    for ref, width, kind, coff in pieces:
        for c0 in range(0, width, COL_CHUNK):
            acc = _dot(u_hi, wbig_ref[:, col + c0:col + c0 + COL_CHUNK])
            if kind == "convx":
                cs = coff + c0
                acc = _conv_silu(acc, cwx_ref[:, cs:cs + COL_CHUNK], cbx_ref[:, cs:cs + COL_CHUNK], vm, tm)
            elif kind == "convq":
                acc = _conv_silu(acc, cwq_ref[:, c0:c0 + COL_CHUNK], cbq_ref[:, c0:c0 + COL_CHUNK], vm, tm)
                if c0 < width // 2:
                    acc = acc * (dk ** -0.5)
            elif kind == "sigmoid":
                acc = _sigmoid(acc)
            ref[:, c0:c0 + COL_CHUNK] = acc.astype(ref.dtype)
        col += width

    wsh = wsh_ref[...]
    raw = _dot(u_hi, wsh) + _dot(u_lo, wsh) + _dot(u_hi, wsl_ref[...]) + smb_ref[...]
    p1 = raw[:, :LANES]
    p2 = raw[:, LANES:]
    lane = lax.broadcasted_iota(I32, (tm, LANES), 1)
    n_dt = 2 * (d_model // SSD_HEAD_DIM)
    n_g = 2 * (d_model // MLSTM_DV)
    is_dt = lane < n_dt
    is_gate = jnp.logical_and(lane >= n_dt, lane < n_dt + n_g)
    dt = _softplus(p2)
    pa = jnp.where(is_dt, dt * aneg_ref[...], jnp.where(is_gate, -_softplus(-p1), 0.0))
    pb = jnp.where(is_dt, dt, jnp.where(is_gate, p2, 0.0))
    lane_c = lax.broadcasted_iota(I32, (CHUNK, LANES), 1)
    is_dt_c = lane_c < n_dt
    rev = jnp.logical_or(jnp.logical_and(lane_c >= n_dt // 2, lane_c < n_dt),
                         jnp.logical_and(lane_c >= n_dt + n_g // 2, lane_c < n_dt + n_g))
    tril = tril_ref[...]
    triu = triu_ref[...]
    tq = lax.broadcasted_iota(I32, (CHUNK, LANES), 0)
    for c in range(tm // CHUNK):
        r0 = c * CHUNK
        a_c = pa[r0:r0 + CHUNK]
        hi = a_c.astype(BF16)
        r1 = a_c - hi.astype(F32)
        mid = r1.astype(BF16)
        lo = (r1 - mid.astype(F32)).astype(BF16)
        cs_f = _dot(tril, hi) + _dot(tril, mid) + _dot(tril, lo)
        cs_b = _dot(triu, hi) + _dot(triu, mid) + _dot(triu, lo)
        plane_a = jnp.where(rev, cs_b, cs_f)
        plane_b = jnp.where(is_dt_c, pb[r0:r0 + CHUNK], pb[r0:r0 + CHUNK] - plane_a)
        yf = plane_b
        yb = plane_b
        s = 1
        while s < CHUNK:
            sh = pltpu.roll(yf, s, axis=0)
            yf = jnp.maximum(yf, jnp.where(tq >= s, sh, NEG_INF))
            sh = pltpu.roll(yb, CHUNK - s, axis=0)
            yb = jnp.maximum(yb, jnp.where(tq + s < CHUNK, sh, NEG_INF))
            s *= 2
        plane_c = jnp.where(rev, yb, yf)
        small_ref[r0:r0 + CHUNK, 0:LANES] = plane_a
        small_ref[r0:r0 + CHUNK, LANES:2 * LANES] = plane_b
        small_ref[r0:r0 + CHUNK, 2 * LANES:3 * LANES] = plane_c
        smallt_ref[c, 0] = plane_a.T
        smallt_ref[c, 1] = plane_b.T


def _inproj(x2, ctx2, mods3, g, wbig, wsh, wsl, cwx, cbx, cwq, cbq, smb, aneg, *, batch, seq, ctx_len, dk):
    d = x2.shape[1]
    tm = TOKEN_TILE
    n_ctx_tok = batch * ctx_len
    assert n_ctx_tok % tm == 0 and seq % tm == 0 and tm % ctx_len == 0 and tm % GRID_W == 0
    n_ctx_tiles = n_ctx_tok // tm
    tiles_per_batch = seq // tm
    n_tiles = n_ctx_tiles + batch * tiles_per_batch
    t_all = n_tiles * tm
    n_big = wbig.shape[1]
    tril = jnp.asarray(np.tril(np.ones((CHUNK, CHUNK), np.float32)), BF16)
    triu = jnp.asarray(np.triu(np.ones((CHUNK, CHUNK), np.float32)), BF16)

    def x_map(i):
        return (jnp.maximum(i - n_ctx_tiles, 0), 0)

    def ctx_map(i):
        return (jnp.minimum(i, n_ctx_tiles - 1), 0)

    def mod_map(i):
        return (jnp.where(i < n_ctx_tiles, batch, jnp.maximum(i - n_ctx_tiles, 0) // tiles_per_batch), 0, 0)

    const = lambda i: (0, 0)
    row = lambda i: (i, 0)
    cmask = jnp.asarray(np.stack([_conv_masks(tm, GRID_W), _conv_masks(tm, ctx_len)]))
    kern = functools.partial(_inproj_kernel, n_ctx_tiles=n_ctx_tiles, tm=tm, d_model=d, dk=dk)
    d_bc = 2 * SSD_GROUPS * SSD_STATE
    outs = [jax.ShapeDtypeStruct((t_all, d), BF16), jax.ShapeDtypeStruct((t_all, d), BF16),
            jax.ShapeDtypeStruct((t_all, d_bc), BF16), jax.ShapeDtypeStruct((t_all, d), BF16),
            jax.ShapeDtypeStruct((t_all, d), BF16), jax.ShapeDtypeStruct((t_all, d), BF16),
            jax.ShapeDtypeStruct((t_all, 2 * d), BF16), jax.ShapeDtypeStruct((t_all, 3 * LANES), F32),
            jax.ShapeDtypeStruct((t_all // CHUNK, 2, LANES, CHUNK), F32)]
    out_specs = [pl.BlockSpec((tm, d), row), pl.BlockSpec((tm, d), row), pl.BlockSpec((tm, d_bc), row),
                 pl.BlockSpec((tm, d), row), pl.BlockSpec((tm, d), row), pl.BlockSpec((tm, d), row),
                 pl.BlockSpec((tm, 2 * d), row), pl.BlockSpec((tm, 3 * LANES), row),
                 pl.BlockSpec((tm // CHUNK, 2, LANES, CHUNK), lambda i: (i, 0, 0, 0))]
    in_specs = [pl.BlockSpec((tm, d), x_map), pl.BlockSpec((tm, d), ctx_map),
                pl.BlockSpec((1, 6, d), mod_map), pl.BlockSpec((1, d), const),
                pl.BlockSpec((d, n_big), const, pipeline_mode=pl.Buffered(1)),
                pl.BlockSpec((d, 2 * LANES), const), pl.BlockSpec((d, 2 * LANES), const),
                pl.BlockSpec(cwx.shape, const), pl.BlockSpec(cbx.shape, const),
                pl.BlockSpec(cwq.shape, const), pl.BlockSpec(cbq.shape, const),
                pl.BlockSpec((1, 2 * LANES), const), pl.BlockSpec((1, LANES), const),
                pl.BlockSpec((CHUNK, CHUNK), const), pl.BlockSpec((CHUNK, CHUNK), const),
                pl.BlockSpec((2, tm, 8), lambda i: (0, 0, 0))]
    return pl.pallas_call(
        kern, out_shape=outs, grid=(n_tiles,), in_specs=in_specs, out_specs=out_specs,
        compiler_params=pltpu.CompilerParams(dimension_semantics=("arbitrary",), vmem_limit_bytes=VMEM_LIMIT),
        name="inproj",
    )(x2, ctx2, mods3, g, wbig, wsh, wsl, cwx, cbx, cwq, cbq, smb, aneg, tril, triu, cmask)


def _chunk_block_map(direction, batch, n_ctx_chunks, n_lat_chunks):
    def idx(b, s):
        if direction == 0:
            c_ctx = s
            c_lat = s - n_ctx_chunks
        else:
            c_ctx = n_ctx_chunks - 1 - s
            c_lat = n_lat_chunks - 1 - (s - n_ctx_chunks)
        return jnp.where(s < n_ctx_chunks, b * n_ctx_chunks + c_ctx, batch * n_ctx_chunks + b * n_lat_chunks + c_lat)
    return idx


def _ssd_kernel(xs_ref, bc_ref, small_ref, smallt_ref, e16_ref, ecol_ref, y_ref, st_ref, *, direction, n_heads):
    s = pl.program_id(1)

    @pl.when(s == 0)
    def _():
        st_ref[...] = jnp.zeros_like(st_ref)

    hpg = n_heads // SSD_GROUPS
    lane0 = direction * n_heads
    last = CHUNK - 1 if direction == 0 else 0
    lane = lax.broadcasted_iota(I32, (CHUNK, LANES), 1)
    lm = jnp.logical_and(lane >= lane0, lane < lane0 + n_heads)
    plane_a = jnp.where(lm, small_ref[:, 0:LANES], 0.0)
    plane_b = jnp.where(lm, small_ref[:, LANES:2 * LANES], 0.0)
    cum_t = smallt_ref[0, 0]
    tot = plane_a[last:last + 1]
    e16 = e16_ref[...]
    dtx = _split_dot(plane_b, e16, 2)
    ecx = _split_dot(jnp.where(lm, jnp.exp(plane_a), 0.0), e16, 2)
    wx = _split_dot(plane_b * jnp.exp(tot - plane_a), e16, 2)
    decx = _split_dot(jnp.where(lm[0:8], jnp.exp(jnp.broadcast_to(tot, (8, LANES))), 0.0), e16, 2)[0:1]
    cumcol = _split_dot(plane_a, ecol_ref[...], 3)

    xf = xs_ref[...].astype(F32)
    xdt = xf * dtx
    xw = (xf * wx).astype(BF16)
    plane64 = lax.broadcasted_iota(I32, (CHUNK, 2 * SSD_HEAD_DIM), 1) < SSD_HEAD_DIM
    iq = lax.broadcasted_iota(I32, (CHUNK, CHUNK), 0)
    ik = lax.broadcasted_iota(I32, (CHUNK, CHUNK), 1)
    mask = (iq >= ik) if direction == 0 else (iq <= ik)
    gw = hpg * SSD_HEAD_DIM
    for g in range(SSD_GROUPS):
        bg = bc_ref[:, g * SSD_STATE:(g + 1) * SSD_STATE]
        cg = bc_ref[:, (SSD_GROUPS + g) * SSD_STATE:(SSD_GROUPS + g + 1) * SSD_STATE]
        cb = _dot_nt(cg, bg)
        sg = st_ref[:, g * gw:(g + 1) * gw]
        y_inter = _dot(cg, sg.astype(BF16)) * ecx[:, g * gw:(g + 1) * gw]
        for pair in range(hpg // 2):
            h0 = g * hpg + 2 * pair
            c0 = h0 * SSD_HEAD_DIM
            xpair = xdt[:, c0:c0 + 2 * SSD_HEAD_DIM]
            acc = y_inter[:, 2 * pair * SSD_HEAD_DIM:(2 * pair + 2) * SSD_HEAD_DIM]
            for par in range(2):
                h = h0 + par
                seg = cumcol[:, h * LANES:(h + 1) * LANES] - cum_t[lane0 + h:lane0 + h + 1, :]
                m_h = (cb * jnp.exp(jnp.where(mask, seg, NEG_INF))).astype(BF16)
                keep = plane64 if par == 0 else jnp.logical_not(plane64)
                acc = acc + _dot(m_h, jnp.where(keep, xpair, 0.0).astype(BF16))
            y_ref[:, c0:c0 + 2 * SSD_HEAD_DIM] = acc.astype(y_ref.dtype)
        st_ref[:, g * gw:(g + 1) * gw] = decx[:, g * gw:(g + 1) * gw] * sg + _dot_tn(bg, xw[:, g * gw:(g + 1) * gw])


def _ssd_scan(xs, bc, small, smallt, *, direction, batch, n_ctx_chunks, n_lat_chunks):
    t_all, d = xs.shape
    n_heads = d // SSD_HEAD_DIM
    e16 = np.zeros((LANES, d), np.float32)
    ecol = np.zeros((LANES, n_heads * LANES), np.float32)
    for h in range(n_heads):
        e16[direction * n_heads + h, h * SSD_HEAD_DIM:(h + 1) * SSD_HEAD_DIM] = 1.0
        ecol[direction * n_heads + h, h * LANES:(h + 1) * LANES] = 1.0
    idx = _chunk_block_map(direction, batch, n_ctx_chunks, n_lat_chunks)
    n_steps = n_ctx_chunks + n_lat_chunks
    const = lambda b, s: (0, 0)
    kern = functools.partial(_ssd_kernel, direction=direction, n_heads=n_heads)
    return pl.pallas_call(
        kern,
        out_shape=jax.ShapeDtypeStruct((t_all, d), BF16),
        grid=(batch, n_steps),
        in_specs=[pl.BlockSpec((CHUNK, d), lambda b, s: (idx(b, s), 0)),
                  pl.BlockSpec((CHUNK, bc.shape[1]), lambda b, s: (idx(b, s), 0)),
                  pl.BlockSpec((CHUNK, 3 * LANES), lambda b, s: (idx(b, s), 0)),
                  pl.BlockSpec((1, 2, LANES, CHUNK), lambda b, s: (idx(b, s), 0, 0, 0)),
                  pl.BlockSpec(e16.shape, const), pl.BlockSpec(ecol.shape, const)],
        out_specs=pl.BlockSpec((CHUNK, d), lambda b, s: (idx(b, s), 0)),
        scratch_shapes=[pltpu.VMEM((SSD_STATE, d), F32)],
        compiler_params=pltpu.CompilerParams(dimension_semantics=("arbitrary", "arbitrary"),
                                             vmem_limit_bytes=VMEM_LIMIT),
        name=f"ssd_scan_d{direction}",
    )(xs, bc, small, smallt, jnp.asarray(e16, BF16), jnp.asarray(ecol, BF16))


def _mlstm_kernel(qk_ref, v_ref, small_ref, smallt_ref, em_ref, h_ref, cn_ref, m_ref, *, direction, n_heads, dk):
    s = pl.program_id(1)

    @pl.when(s == 0)
    def _():
        cn_ref[...] = jnp.zeros_like(cn_ref)
        m_ref[...] = jnp.zeros_like(m_ref)

    n_dt = 2 * (n_heads * MLSTM_DV // SSD_HEAD_DIM)
    lane0 = n_dt + direction * n_heads
    last = CHUNK - 1 if direction == 0 else 0
    lane = lax.broadcasted_iota(I32, (CHUNK, LANES), 1)
    lm = jnp.logical_and(lane >= lane0, lane < lane0 + n_heads)
    b_q = jnp.where(lm, small_ref[:, 0:LANES], 0.0)
    r_k = jnp.where(lm, small_ref[:, LANES:2 * LANES], 0.0)
    cmr = jnp.where(lm, small_ref[:, 2 * LANES:3 * LANES], 0.0)
    r_t = smallt_ref[0, 1]
    m_all = m_ref[...]
    m_row = m_all[0:1]
    mm = jnp.maximum(cmr, m_row)
    w_state = jnp.exp(m_row - mm)
    e_mq = jnp.exp(-(b_q + mm))
    em = em_ref[...]
    mmx = _split_dot(mm, em, 3)
    wsx = _split_dot(jnp.where(lm, w_state, 0.0), em, 2)
    emqx = _split_dot(jnp.where(lm, e_mq, 0.0), em, 2)
    m_base8 = jnp.maximum(m_all, jnp.broadcast_to(cmr[last:last + 1], (8, LANES)))
    m_base = m_base8[0:1]
    w_k = jnp.where(lm, jnp.exp(r_k - m_base), 0.0)
    wkx = _split_dot(w_k, em, 2)
    decx = _split_dot(jnp.where(lm[0:8], jnp.exp(m_all - m_base8), 0.0), em, 2)[0:1]
    m_ref[...] = jnp.where(lm[0:8], jnp.broadcast_to(b_q[last:last + 1], (8, LANES)) + m_base8, 0.0)

    iq = lax.broadcasted_iota(I32, (CHUNK, CHUNK), 0)
    ik = lax.broadcasted_iota(I32, (CHUNK, CHUNK), 1)
    mask = (iq >= ik) if direction == 0 else (iq <= ik)
    ones = jnp.ones((CHUNK, MLSTM_DV), BF16)
    d_qk = n_heads * dk
    for h in range(n_heads):
        qh = qk_ref[:, h * dk:(h + 1) * dk]
        kh = qk_ref[:, d_qk + h * dk:d_qk + (h + 1) * dk]
        vh = v_ref[:, h * MLSTM_DV:(h + 1) * MLSTM_DV]
        hs = slice(h * MLSTM_DV, (h + 1) * MLSTM_DV)
        dmat = jnp.exp(jnp.where(mask, r_t[lane0 + h:lane0 + h + 1, :] - mmx[:, hs], NEG_INF))
        smat = (_dot_nt(qh, kh) * dmat).astype(BF16)
        cn = cn_ref[h]
        wsh = wsx[:, hs]
        tot = _dot(smat, jnp.concatenate([vh, ones], axis=1)) + jnp.concatenate([wsh, wsh], axis=1) * _dot(qh, cn.astype(BF16))
        num = tot[:, :MLSTM_DV]
        den = tot[:, MLSTM_DV:]
        h_ref[:, hs] = (num / jnp.maximum(jnp.abs(den), emqx[:, hs])).astype(h_ref.dtype)
        wkh = wkx[:, hs]
        rhs = jnp.concatenate([(vh.astype(F32) * wkh).astype(BF16), wkh.astype(BF16)], axis=1)
        dech = decx[:, hs]
        cn_ref[h] = jnp.concatenate([dech, dech], axis=1) * cn + _dot_tn(kh, rhs)


def _mlstm_scan(qk, v, small, smallt, *, direction, batch, n_ctx_chunks, n_lat_chunks, dk):
    t_all, d = v.shape
    n_heads = d // MLSTM_DV
    n_dt = 2 * (d // SSD_HEAD_DIM)
    em = np.zeros((LANES, d), np.float32)
    for h in range(n_heads):
        em[n_dt + direction * n_heads + h, h * MLSTM_DV:(h + 1) * MLSTM_DV] = 1.0
    idx = _chunk_block_map(direction, batch, n_ctx_chunks, n_lat_chunks)
    n_steps = n_ctx_chunks + n_lat_chunks
    kern = functools.partial(_mlstm_kernel, direction=direction, n_heads=n_heads, dk=dk)
    return pl.pallas_call(
        kern,
        out_shape=jax.ShapeDtypeStruct((t_all, d), BF16),
        grid=(batch, n_steps),
        in_specs=[pl.BlockSpec((CHUNK, qk.shape[1]), lambda b, s: (idx(b, s), 0)),
                  pl.BlockSpec((CHUNK, d), lambda b, s: (idx(b, s), 0)),
                  pl.BlockSpec((CHUNK, 3 * LANES), lambda b, s: (idx(b, s), 0)),
                  pl.BlockSpec((1, 2, LANES, CHUNK), lambda b, s: (idx(b, s), 0, 0, 0)),
                  pl.BlockSpec(em.shape, lambda b, s: (0, 0))],
        out_specs=pl.BlockSpec((CHUNK, d), lambda b, s: (idx(b, s), 0)),
        scratch_shapes=[pltpu.VMEM((n_heads, dk, 2 * MLSTM_DV), F32), pltpu.VMEM((8, LANES), F32)],
        compiler_params=pltpu.CompilerParams(dimension_semantics=("arbitrary", "arbitrary"),
                                             vmem_limit_bytes=VMEM_LIMIT),
        name=f"mlstm_scan_d{direction}",
    )(qk, v, small, smallt, jnp.asarray(em, BF16))


def _merge_kernel(y0_ref, y1_ref, xs_ref, z_ref, h0_ref, h1_ref, og_ref, mg_ref, x_ref, mod_ref, dexp_ref, gs_ref,
                  gm_ref, gf_ref, wso_ref, wmo_ref, wo_ref, x1_ref, u2_ref, *, d_model):
    d = d_model
    m = mod_ref[0]
    y = y0_ref[...].astype(F32) + y1_ref[...].astype(F32) + dexp_ref[...] * xs_ref[...].astype(F32)
    zf = z_ref[...].astype(F32)
    y = y * (zf * _sigmoid(zf))
    y = y * lax.rsqrt(jnp.mean(y * y, axis=-1, keepdims=True) + EPS) * gs_ref[...]
    a = _dot(y.astype(BF16), wso_ref[...])
    hm = h0_ref[...].astype(F32) + h1_ref[...].astype(F32)
    parts = []
    for h in range(d // MLSTM_DV):
        blk = hm[:, h * MLSTM_DV:(h + 1) * MLSTM_DV]
        parts.append(blk * lax.rsqrt(jnp.mean(blk * blk, axis=-1, keepdims=True) + EPS))
    hn = jnp.concatenate(parts, axis=1) * gm_ref[...]
    hh = (og_ref[...].astype(F32) * hn).astype(BF16)
    bm = _dot(hh, wmo_ref[...])
    merged = mg_ref[:, :d].astype(F32) * a + mg_ref[:, d:].astype(F32) * bm
    r = _dot(merged.astype(BF16), wo_ref[...])
    x1 = x_ref[...] + m[2:3] * r
    x1_ref[...] = x1
    u2 = x1 * lax.rsqrt(jnp.mean(x1 * x1, axis=-1, keepdims=True) + EPS) * gf_ref[...] * (1.0 + m[4:5]) + m[3:4]
    u2_ref[...] = _pack_halves(u2)


def _merge(y0, y1, xs, z, h0, h1, og, mg, x2, mods3, dexp, gs, gm, gf, wso, wmo, wo, *, batch, seq, n_ctx_tok):
    t, d = x2.shape
    tm = TOKEN_TILE
    off = n_ctx_tok // tm
    tiles_per_batch = seq // tm
    lat = lambda i: (i + off, 0)
    row = lambda i: (i, 0)
    const = lambda i: (0, 0)
    kern = functools.partial(_merge_kernel, d_model=d)
    wspec = pl.BlockSpec((d, d), const, pipeline_mode=pl.Buffered(1))
    return pl.pallas_call(
        kern,
        out_shape=[jax.ShapeDtypeStruct((t, d), F32), jax.ShapeDtypeStruct((t, d // 2), U32)],
        grid=(t // tm,),
        in_specs=[pl.BlockSpec((tm, d), lat), pl.BlockSpec((tm, d), lat), pl.BlockSpec((tm, d), lat),
                  pl.BlockSpec((tm, d), lat), pl.BlockSpec((tm, d), lat), pl.BlockSpec((tm, d), lat),
                  pl.BlockSpec((tm, d), lat), pl.BlockSpec((tm, 2 * d), lat), pl.BlockSpec((tm, d), row),
                  pl.BlockSpec((1, 6, d), lambda i: (i // tiles_per_batch, 0, 0)),
                  pl.BlockSpec((1, d), const), pl.BlockSpec((1, d), const), pl.BlockSpec((1, d), const),
                  pl.BlockSpec((1, d), const), wspec, wspec, wspec],
        out_specs=[pl.BlockSpec((tm, d), row), pl.BlockSpec((tm, d // 2), row)],
        compiler_params=pltpu.CompilerParams(dimension_semantics=("arbitrary",), vmem_limit_bytes=VMEM_LIMIT),
        name="merge",
    )(y0, y1, xs, z, h0, h1, og, mg, x2, mods3, dexp, gs, gm, gf, wso, wmo, wo)


def _first_index_of_max(vals, row_iota, n_rows):
    mx = jnp.max(vals, axis=0, keepdims=True)
    idx = jnp.min(jnp.where(vals == mx, row_iota, n_rows), axis=0, keepdims=True)
    return mx, idx


def _router_kernel(u_ref, wt_ref, bias_ref, su_ref, idx_ref, pos_ref, wts_ref, cnt_ref, run_ref, *, n_experts, tr):
    i = pl.program_id(0)

    @pl.when(i == 0)
    def _():
        run_ref[...] = jnp.zeros_like(run_ref)

    ua, ub = _unpack_halves(u_ref[...])
    half = ua.shape[1]
    scores = _sigmoid(_dot_nt(wt_ref[:, :half], ua.astype(BF16)) + _dot_nt(wt_ref[:, half:], ub.astype(BF16)))
    biased = scores + bias_ref[...]
    gsz = n_experts // N_EXPERT_GROUPS
    gi = lax.broadcasted_iota(I32, (gsz, tr), 0).astype(F32)
    gscores = []
    for g in range(N_EXPERT_GROUPS):
        blk = biased[g * gsz:(g + 1) * gsz]
        m1, i1 = _first_index_of_max(blk, gi, gsz)
        m2 = jnp.max(jnp.where(gi == i1, NEG_INF, blk), axis=0, keepdims=True)
        gscores.append(m1 + m2)
    gs = jnp.concatenate(gscores, axis=0)
    g8 = lax.broadcasted_iota(I32, (N_EXPERT_GROUPS, tr), 0).astype(F32)
    gsel = jnp.zeros((N_EXPERT_GROUPS, tr), F32)
    for _ in range(TOPK_GROUPS):
        _, gidx = _first_index_of_max(gs, g8, N_EXPERT_GROUPS)
        hit = g8 == gidx
        gsel = jnp.where(hit, 1.0, gsel)
        gs = jnp.where(hit, NEG_INF, gs)
    cand = jnp.concatenate(
        [jnp.where(jnp.broadcast_to(gsel[g:g + 1], (gsz, tr)) > 0.5, biased[g * gsz:(g + 1) * gsz], NEG_INF)
         for g in range(N_EXPERT_GROUPS)], axis=0)
    ei = lax.broadcasted_iota(I32, (n_experts, tr), 0).astype(F32)
    sel = jnp.zeros((n_experts, tr), F32)
    idxs, ws = [], []
    for _ in range(TOP_K):
        _, eidx = _first_index_of_max(cand, ei, n_experts)
        hit = ei == eidx
        ws.append(jnp.sum(jnp.where(hit, scores, 0.0), axis=0, keepdims=True))
        idxs.append(eidx)
        sel = jnp.where(hit, 1.0, sel)
        cand = jnp.where(hit, NEG_INF, cand)
    wk = jnp.concatenate(ws, axis=0)
    wts_ref[...] = ROUTED_SCALE * wk / jnp.sum(wk, axis=0, keepdims=True)
    idx_ref[...] = jnp.concatenate(idxs, axis=0).astype(I32)
    selb = sel.astype(BF16)
    posmat = _dot(selb, su_ref[...]) + run_ref[:, 0:1]
    pos_ref[...] = jnp.concatenate(
        [jnp.sum(jnp.where(ei == idxs[k], posmat, 0.0), axis=0, keepdims=True) for k in range(TOP_K)],
        axis=0).astype(I32)
    run = run_ref[...] + _dot(selb, jnp.ones((tr, LANES), BF16))
    run_ref[...] = run
    cnt_ref[...] = run


def _router(u2p, router_wt, bias_col):
    t = u2p.shape[0]
    d = router_wt.shape[1]
    n_experts = router_wt.shape[0]
    tr = TOKEN_TILE
    su = jnp.asarray(np.triu(np.ones((tr, tr), np.float32), 1), BF16)
    kern = functools.partial(_router_kernel, n_experts=n_experts, tr=tr)
    col = lambda i: (0, i)
    const = lambda i: (0, 0)
    return pl.pallas_call(
        kern,
        out_shape=[jax.ShapeDtypeStruct((TOP_K, t), I32), jax.ShapeDtypeStruct((TOP_K, t), I32),
                   jax.ShapeDtypeStruct((TOP_K, t), F32), jax.ShapeDtypeStruct((n_experts, LANES), F32)],
        grid=(t // tr,),
        in_specs=[pl.BlockSpec((tr, d // 2), lambda i: (i, 0)), pl.BlockSpec((n_experts, d), const),
                  pl.BlockSpec((n_experts, 1), const), pl.BlockSpec((tr, tr), const)],
        out_specs=[pl.BlockSpec((TOP_K, tr), col), pl.BlockSpec((TOP_K, tr), col), pl.BlockSpec((TOP_K, tr), col),
                   pl.BlockSpec((n_experts, LANES), const)],
        scratch_shapes=[pltpu.VMEM((n_experts, LANES), F32)],
        compiler_params=pltpu.CompilerParams(dimension_semantics=("arbitrary",), vmem_limit_bytes=VMEM_LIMIT),
        name="router",
    )(u2p, router_wt, bias_col, su)


def _slots_kernel(idx_ref, pos_ref, pstart_ref, dest_ref, *, n_experts, tr):
    ei = lax.broadcasted_iota(I32, (n_experts, tr), 0).astype(F32)
    pstart = pstart_ref[...]
    idx = idx_ref[...].astype(F32)
    rows = [jnp.sum(jnp.where(ei == idx[k:k + 1], pstart, 0.0), axis=0, keepdims=True) for k in range(TOP_K)]
    dest_ref[...] = jnp.concatenate(rows, axis=0).astype(I32) + pos_ref[...]


def _slots(idx, pos, pstart_col):
    t = idx.shape[1]
    n_experts = pstart_col.shape[0]
    tr = TOKEN_TILE
    col = lambda i: (0, i)
    return pl.pallas_call(
        functools.partial(_slots_kernel, n_experts=n_experts, tr=tr),
        out_shape=jax.ShapeDtypeStruct((TOP_K, t), I32),
        grid=(t // tr,),
        in_specs=[pl.BlockSpec((TOP_K, tr), col), pl.BlockSpec((TOP_K, tr), col),
                  pl.BlockSpec((n_experts, 1), lambda i: (0, 0))],
        out_specs=pl.BlockSpec((TOP_K, tr), col),
        compiler_params=pltpu.CompilerParams(dimension_semantics=("arbitrary",), vmem_limit_bytes=VMEM_LIMIT),
        name="slots",
    )(idx, pos, pstart_col)


PAD_BITS = tuple(1 << b for b in reversed(range((MOE_BLOCK - 1).bit_length())))


def _dispatch_kernel(pstart_ref, cnt_ref, dest_ref, u_ref, xs_ref, zbuf, sem, psem, *, td, n_experts):
    i = pl.program_id(0)

    @pl.when(i == 0)
    def _():
        zbuf[...] = jnp.zeros_like(zbuf)

        def pads(e, wait):
            cnt = cnt_ref[e]
            n_pad = (MOE_BLOCK - (cnt & (MOE_BLOCK - 1))) & (MOE_BLOCK - 1)
            base = pstart_ref[e] + cnt
            n_single = n_pad & (SUBLANES - 1)

            def go(cp, cond):
                @pl.when(cond)
                def _():
                    if wait:
                        cp.wait()
                    else:
                        cp.start()

            for q in range(SUBLANES - 1):
                go(pltpu.make_async_copy(zbuf.at[pl.ds(0, 1)], xs_ref.at[pl.ds(base + q, 1)], psem), q < n_single)
            base = base + n_single
            for bit in PAD_BITS:
                if bit < SUBLANES:
                    continue
                has = (n_pad & bit) != 0
                go(pltpu.make_async_copy(zbuf.at[pl.ds(0, bit)], xs_ref.at[pl.ds(pl.multiple_of(base, SUBLANES), bit)],
                                         psem), has)
                base = base + jnp.where(has, bit, 0)

        def start_pads(e, carry):
            pads(e, False)
            return carry

        def wait_pads(e, carry):
            pads(e, True)
            return carry

        lax.fori_loop(0, n_experts, start_pads, 0)
        lax.fori_loop(0, n_experts, wait_pads, 0)

    def start(t, carry):
        for k in range(TOP_K):
            pltpu.make_async_copy(u_ref.at[pl.ds(t, 1)], xs_ref.at[pl.ds(dest_ref[k, t], 1)], sem).start()
        return carry

    def wait(t, carry):
        for k in range(TOP_K):
            pltpu.make_async_copy(u_ref.at[pl.ds(0, 1)], xs_ref.at[pl.ds(0, 1)], sem).wait()
        return carry

    lax.fori_loop(0, td, start, 0, unroll=4)
    lax.fori_loop(0, td, wait, 0, unroll=8)


def _dispatch(pad_start, counts, dest, u2p, n_slots):
    t, dh = u2p.shape
    td = DISPATCH_TILE
    n_experts = pad_start.shape[0]
    assert MOE_BLOCK & (MOE_BLOCK - 1) == 0
    kern = functools.partial(_dispatch_kernel, td=td, n_experts=n_experts)
    return pl.pallas_call(
        kern,
        out_shape=jax.ShapeDtypeStruct((n_slots, dh), U32),
        grid_spec=pltpu.PrefetchScalarGridSpec(
            num_scalar_prefetch=2, grid=(t // td,),
            in_specs=[pl.BlockSpec((TOP_K, td), lambda i, ps, cn: (0, i), memory_space=pltpu.SMEM),
                      pl.BlockSpec((td, dh), lambda i, ps, cn: (i, 0))],
            out_specs=pl.BlockSpec(memory_space=pl.ANY),
            scratch_shapes=[pltpu.VMEM((MOE_BLOCK // 2, dh), U32), pltpu.SemaphoreType.DMA,
                            pltpu.SemaphoreType.DMA]),
        compiler_params=pltpu.CompilerParams(dimension_semantics=("arbitrary",), vmem_limit_bytes=VMEM_LIMIT),
        name="dispatch",
    )(pad_start, counts, dest, u2p)


def _experts_kernel(be_ref, nu_ref, x_ref, wg_ref, wu_ref, wd_ref, y_ref, wgb, wub, wdb):
    j = pl.program_id(0)
    used = j < nu_ref[0]
    jj = jnp.minimum(j, nu_ref[0] - 1)
    new_expert = jnp.logical_or(j == 0, be_ref[jj] != be_ref[jnp.maximum(jj - 1, 0)])

    @pl.when(jnp.logical_and(used, new_expert))
    def _():
        wgb[...] = wg_ref[0].astype(BF16)
        wub[...] = wu_ref[0].astype(BF16)
        wdb[...] = wd_ref[0].astype(BF16)

    @pl.when(used)
    def _():
        xa, xb = _unpack_halves(x_ref[...])
        half = xa.shape[1]
        xa = xa.astype(BF16)
        xb = xb.astype(BF16)
        hg = _dot(xa, wgb[:half]) + _dot(xb, wgb[half:])
        hu = _dot(xa, wub[:half]) + _dot(xb, wub[half:])
        hb = (hg * _sigmoid(hg) * hu).astype(BF16)
        y_ref[...] = _pack_halves(_dot(hb, wdb[...]))


def _experts(block_expert, n_used, xsorted, wg, wu, wd):
    n_slots, dh = xsorted.shape
    d, de = wg.shape[1], wg.shape[2]
    nb = n_slots // MOE_BLOCK
    blk = lambda j, be, nu: (jnp.minimum(j, nu[0] - 1), 0)
    wmap = lambda j, be, nu: (be[jnp.minimum(j, nu[0] - 1)], 0, 0)
    return pl.pallas_call(
        _experts_kernel,
        out_shape=jax.ShapeDtypeStruct((n_slots, dh), U32),
        grid_spec=pltpu.PrefetchScalarGridSpec(
            num_scalar_prefetch=2, grid=(nb,),
            in_specs=[pl.BlockSpec((MOE_BLOCK, dh), blk), pl.BlockSpec((1, d, de), wmap),
                      pl.BlockSpec((1, d, de), wmap), pl.BlockSpec((1, de, d), wmap)],
            out_specs=pl.BlockSpec((MOE_BLOCK, dh), blk),
            scratch_shapes=[pltpu.VMEM((d, de), BF16), pltpu.VMEM((d, de), BF16), pltpu.VMEM((de, d), BF16)]),
        compiler_params=pltpu.CompilerParams(dimension_semantics=("arbitrary",), vmem_limit_bytes=VMEM_LIMIT),
        name="experts",
    )(block_expert, n_used, xsorted, wg, wu, wd)


def _combine_kernel(dest_ref, dnext_ref, ys_ref, wt_ref, x1_ref, u_ref, mod_ref, wsg_ref, wsu_ref, wsd_ref,
                    gfin_ref, o_ref, gbuf, sem, *, tc, n_steps):
    i = pl.program_id(0)
    slot = i % 2

    def issue(dref, s):
        def body(t, carry):
            for k in range(TOP_K):
                pltpu.make_async_copy(ys_ref.at[pl.ds(dref[k, t], 1)], gbuf.at[s, k, pl.ds(t, 1)], sem.at[s]).start()
            return carry
        lax.fori_loop(0, tc, body, 0, unroll=4)

    @pl.when(i == 0)
    def _():
        issue(dest_ref, 0)

    @pl.when(i + 1 < n_steps)
    def _():
        issue(dnext_ref, 1 - slot)

    ua, ub = _unpack_halves(u_ref[...])
    half = ua.shape[1]
    ua = ua.astype(BF16)
    ub = ub.astype(BF16)
    hg = _dot(ua, wsg_ref[:half]) + _dot(ub, wsg_ref[half:])
    hu = _dot(ua, wsu_ref[:half]) + _dot(ub, wsu_ref[half:])
    shared = _dot((hg * _sigmoid(hg) * hu).astype(BF16), wsd_ref[...])
    acc_l = shared[:, :half]
    acc_r = shared[:, half:]

    def wait(t, carry):
        for k in range(TOP_K):
            pltpu.make_async_copy(ys_ref.at[pl.ds(0, 1)], gbuf.at[slot, 0, pl.ds(0, 1)], sem.at[slot]).wait()
        return carry

    lax.fori_loop(0, tc, wait, 0, unroll=8)
    wt = wt_ref[...]
    for k in range(TOP_K):
        ga, gb = _unpack_halves(gbuf[slot, k])
        wk = wt[:, k:k + 1]
        acc_l = acc_l + ga * wk
        acc_r = acc_r + gb * wk
    m = mod_ref[0]
    xo_l = x1_ref[:, :half] + m[5:6, :half] * acc_l
    xo_r = x1_ref[:, half:] + m[5:6, half:] * acc_r
    ms = (jnp.sum(xo_l * xo_l, axis=-1, keepdims=True) + jnp.sum(xo_r * xo_r, axis=-1, keepdims=True)) / (2 * half)
    inv = lax.rsqrt(ms + EPS)
    o_ref[:, :half] = xo_l * inv * gfin_ref[:, :half]
    o_ref[:, half:] = xo_r * inv * gfin_ref[:, half:]


def _combine(dest, ysorted, wts_t, x1, u2p, mods3, wsg, wsu, wsd, gfin, *, seq):
    t, d = x1.shape
    dh = d // 2
    tc = DISPATCH_TILE
    tiles_per_batch = seq // tc
    n_steps = t // tc
    kern = functools.partial(_combine_kernel, tc=tc, n_steps=n_steps)
    row = lambda i: (i, 0)
    const = lambda i: (0, 0)
    return pl.pallas_call(
        kern,
        out_shape=jax.ShapeDtypeStruct((t, d), F32),
        grid=(n_steps,),
        in_specs=[pl.BlockSpec((TOP_K, tc), lambda i: (0, i), memory_space=pltpu.SMEM),
                  pl.BlockSpec((TOP_K, tc), lambda i: (0, jnp.minimum(i + 1, n_steps - 1)), memory_space=pltpu.SMEM),
                  pl.BlockSpec(memory_space=pl.ANY), pl.BlockSpec((tc, TOP_K), row),
                  pl.BlockSpec((tc, d), row), pl.BlockSpec((tc, dh), row),
                  pl.BlockSpec((1, 6, d), lambda i: (i // tiles_per_batch, 0, 0)),
                  pl.BlockSpec(wsg.shape, const), pl.BlockSpec(wsu.shape, const), pl.BlockSpec(wsd.shape, const),
                  pl.BlockSpec((1, d), const)],
        out_specs=pl.BlockSpec((tc, d), row),
        scratch_shapes=[pltpu.VMEM((2, TOP_K, tc, dh), U32), pltpu.SemaphoreType.DMA((2,))],
        compiler_params=pltpu.CompilerParams(dimension_semantics=("arbitrary",), vmem_limit_bytes=VMEM_LIMIT),
        name="combine",
    )(dest, dest, ysorted, wts_t, x1, u2p, mods3, wsg, wsu, wsd, gfin)


def kernel(x, c, ctx, c_ctx, ada_w, ada_b, norm_mix_g, norm_ffn_g, w_in, conv_xbc_w, conv_xbc_b, ssd_dt_bias, ssd_a_log, ssd_d, ssd_norm_g, conv_qk_w, conv_qk_b, mlstm_i_bias, mlstm_f_bias, mlstm_norm_g, w_ssd_out, w_mlstm_out, w_out, router_w, router_bias, moe_w_gate, moe_w_up, moe_w_down, shared_w_gate, shared_w_up, shared_w_down, norm_final_g):
    batch, seq, d = x.shape
    ctx_len = ctx.shape[1]
    depth = ada_w.shape[0]
    assert depth == 1, "only the single-layer configuration is implemented"
    assert seq % CHUNK == 0 and ctx_len % CHUNK == 0 and seq % GRID_W == 0
    l = 0
    n_sh = d // SSD_HEAD_DIM
    n_mh = d // MLSTM_DV
    dk = MLSTM_DV // 2
    d_xbc = d + 2 * SSD_GROUPS * SSD_STATE
    d_qk = 2 * n_mh * dk
    sizes = (d, d_xbc, 2 * n_sh, d_qk, d, 4 * n_mh, d, 2 * d)
    offs = np.concatenate([[0], np.cumsum(sizes)])
    assert offs[-1] == w_in.shape[2] and 2 * n_sh + 2 * n_mh <= LANES

    cond = jnp.concatenate([c, c_ctx[None], jnp.zeros((8 - (batch + 1) % 8, d), F32)], axis=0)
    mods = _adaln(cond, ada_w[l], ada_b[l])
    mods3 = mods.reshape(mods.shape[0], 6, d)

    w = w_in[l]
    seg = lambda k: w[:, offs[k]:offs[k + 1]]
    wbig = jnp.concatenate([seg(0), seg(1), seg(3), seg(4), seg(6), seg(7)], axis=1).astype(BF16)
    w_dt = seg(2)
    w_g = seg(5).reshape(d, 2, 2, n_mh)
    w_i = w_g[:, :, 0].reshape(d, 2 * n_mh)
    w_f = w_g[:, :, 1].reshape(d, 2 * n_mh)
    pad = jnp.zeros((d, LANES - 2 * n_sh - 2 * n_mh), F32)
    wsm = jnp.concatenate([w_dt, w_f, pad, w_dt, w_i, pad], axis=1)
    wsh = wsm.astype(BF16)
    wsl = (wsm - wsh.astype(F32)).astype(BF16)
    padb = jnp.zeros((LANES - 2 * n_sh - 2 * n_mh,), F32)
    dtb = ssd_dt_bias[l].reshape(-1).astype(F32)
    smb = jnp.concatenate([dtb, mlstm_f_bias[l].reshape(-1).astype(F32), padb,
                           dtb, mlstm_i_bias[l].reshape(-1).astype(F32), padb]).reshape(1, 2 * LANES)
    aneg = jnp.concatenate([-jnp.exp(ssd_a_log[l].astype(F32)).reshape(-1),
                            jnp.zeros((LANES - 2 * n_sh,), F32)]).reshape(1, LANES)

    x2 = x.reshape(batch * seq, d)
    ctx2 = ctx.reshape(batch * ctx_len, d)
    z, xs, bc, qk, v, og, mg, small, smallt = _inproj(
        x2, ctx2, mods3, norm_mix_g[l].reshape(1, d), wbig, wsh, wsl,
        conv_xbc_w[l], conv_xbc_b[l].reshape(1, d_xbc), conv_qk_w[l], conv_qk_b[l].reshape(1, d_qk), smb, aneg,
        batch=batch, seq=seq, ctx_len=ctx_len, dk=dk)

    ncc = ctx_len // CHUNK
    ncl = seq // CHUNK
    scan_args = dict(batch=batch, n_ctx_chunks=ncc, n_lat_chunks=ncl)
    y0 = _ssd_scan(xs, bc, small, smallt, direction=0, **scan_args)
    y1 = _ssd_scan(xs, bc, small, smallt, direction=1, **scan_args)
    h0 = _mlstm_scan(qk, v, small, smallt, direction=0, dk=dk, **scan_args)
    h1 = _mlstm_scan(qk, v, small, smallt, direction=1, dk=dk, **scan_args)

    dexp = jnp.repeat(ssd_d[l].astype(F32), SSD_HEAD_DIM).reshape(1, d)
    x1, u2 = _merge(y0, y1, xs, z, h0, h1, og, mg, x2, mods3, dexp, ssd_norm_g[l].reshape(1, d),
                    mlstm_norm_g[l].reshape(1, d), norm_ffn_g[l].reshape(1, d),
                    w_ssd_out[l].astype(BF16), w_mlstm_out[l].astype(BF16), w_out[l].astype(BF16),
                    batch=batch, seq=seq, n_ctx_tok=batch * ctx_len)

    n_experts = router_w.shape[2]
    idx, pos, wts, cnt = _router(u2, router_w[l].T.astype(BF16), router_bias[l].astype(F32).reshape(n_experts, 1))
    counts = cnt[:, 0].astype(I32)
    padded = (counts + MOE_BLOCK - 1) // MOE_BLOCK * MOE_BLOCK
    pad_end = jnp.cumsum(padded)
    pad_start = (pad_end - padded).astype(I32)
    t = batch * seq
    nb = t * TOP_K // MOE_BLOCK + n_experts
    n_used = (pad_end[-1] // MOE_BLOCK).astype(I32).reshape(1)
    block_first = jnp.arange(nb, dtype=I32) * MOE_BLOCK
    block_expert = jnp.minimum(jnp.sum((pad_end[None, :] <= block_first[:, None]).astype(I32), axis=1), n_experts - 1)

    dest = _slots(idx, pos, pad_start.astype(F32).reshape(n_experts, 1))
    xsorted = _dispatch(pad_start, counts, dest, u2, nb * MOE_BLOCK)
    ysorted = _experts(block_expert, n_used, xsorted, moe_w_gate[l], moe_w_up[l], moe_w_down[l])
    out = _combine(dest, ysorted, wts.T, x1, u2, mods3,
                   shared_w_gate[l].astype(BF16), shared_w_up[l].astype(BF16), shared_w_down[l].astype(BF16),
                   norm_final_g.reshape(1, d), seq=seq)
    return out.reshape(batch, seq, d)
```

```python
import functools

import numpy as np
import jax
import jax.numpy as jnp
from jax import lax
from jax.experimental import pallas as pl
from jax.experimental.pallas import tpu as pltpu

F32 = jnp.float32
BF16 = jnp.bfloat16
I32 = jnp.int32
U32 = jnp.uint32

EPS = 1e-6
CHUNK = 128
CONV_K = 5
GRID_W = 64
SSD_HEAD_DIM = 64
SSD_STATE = 128
SSD_GROUPS = 2
MLSTM_DV = 128
N_EXPERT_GROUPS = 8
TOPK_GROUPS = 4
TOP_K = 8
ROUTED_SCALE = 2.5

LANES = 128
SUBLANES = 8
TOKEN_TILE = 512
COL_CHUNK = 512
MOE_BLOCK = 256
DISPATCH_TILE = 256
VMEM_LIMIT = 56 * 1024 * 1024
NEG_INF = float("-inf")


def _dot(a, b):
    return jnp.dot(a, b, preferred_element_type=F32)


def _dot_nt(a, b):
    return lax.dot_general(a, b, (((1,), (1,)), ((), ())), preferred_element_type=F32)


def _dot_tn(a, b):
    return lax.dot_general(a, b, (((0,), (0,)), ((), ())), preferred_element_type=F32)


def _split_dot(v, e, passes):
    p = v.astype(BF16)
    out = _dot(p, e)
    r = v - p.astype(F32)
    for _ in range(passes - 1):
        p = r.astype(BF16)
        out = out + _dot(p, e)
        r = r - p.astype(F32)
    return out


def _sigmoid(v):
    return 1.0 / (1.0 + jnp.exp(-v))


def _pack_halves(v):
    n = v.shape[1] // 2
    hi = lax.bitcast_convert_type(v[:, :n].astype(BF16).astype(F32), U32)
    lo = lax.bitcast_convert_type(v[:, n:].astype(BF16).astype(F32), U32)
    return hi | (lo >> 16)


def _unpack_halves(p):
    left = lax.bitcast_convert_type(p & jnp.uint32(0xFFFF0000), F32)
    right = lax.bitcast_convert_type(p << 16, F32)
    return left, right


def _softplus(v):
    return jnp.maximum(v, 0.0) + jnp.log1p(jnp.exp(-jnp.abs(v)))


def _adaln_kernel(c_ref, w_ref, b_ref, o_ref):
    c = c_ref[...]
    s = c * _sigmoid(c)
    w = w_ref[...]
    s_hi = s.astype(BF16)
    s_lo = (s - s_hi.astype(F32)).astype(BF16)
    w_hi = w.astype(BF16)
    w_lo = (w - w_hi.astype(F32)).astype(BF16)
    o_ref[...] = _dot(s_hi, w_hi) + _dot(s_lo, w_hi) + _dot(s_hi, w_lo) + b_ref[...]


def _adaln(cond, w, b):
    rows, d = cond.shape
    n = w.shape[1]
    tn = 1536 if n % 1536 == 0 else n
    return pl.pallas_call(
        _adaln_kernel,
        out_shape=jax.ShapeDtypeStruct((rows, n), F32),
        grid=(n // tn,),
        in_specs=[pl.BlockSpec((rows, d), lambda j: (0, 0)),
                  pl.BlockSpec((d, tn), lambda j: (0, j)),
                  pl.BlockSpec((1, tn), lambda j: (0, j))],
        out_specs=pl.BlockSpec((rows, tn), lambda j: (0, j)),
        compiler_params=pltpu.CompilerParams(dimension_semantics=("arbitrary",), vmem_limit_bytes=VMEM_LIMIT),
        name="adaln",
    )(cond, w, b.reshape(1, n))


CONV_SHIFTS = tuple(j - CONV_K // 2 for j in range(CONV_K) if j != CONV_K // 2)


def _conv_masks(tm, seg_len):
    pos = np.arange(tm) % seg_len
    m = np.zeros((tm, 8), np.float32)
    for i, s in enumerate(CONV_SHIFTS):
        m[:, i] = ((pos + s >= 0) & (pos + s < seg_len)).astype(np.float32)
    return m


def _conv_silu(acc, w5, bias, vm, tm):
    out = acc * w5[CONV_K // 2:CONV_K // 2 + 1] + bias
    for i, s in enumerate(CONV_SHIFTS):
        shifted = pltpu.roll(acc, (-s) % tm, axis=0)
        j = s + CONV_K // 2
        out = out + (shifted * vm[:, i:i + 1]) * w5[j:j + 1]
    return out * _sigmoid(out)


def _inproj_kernel(x_ref, ctx_ref, mod_ref, g_ref, wbig_ref, wsh_ref, wsl_ref, cwx_ref, cbx_ref, cwq_ref, cbq_ref,
                   smb_ref, aneg_ref, tril_ref, triu_ref, cmask_ref,
                   z_ref, xs_ref, bc_ref, qk_ref, v_ref, og_ref, mg_ref, small_ref, smallt_ref,
                   *, n_ctx_tiles, tm, d_model, dk):
    i = pl.program_id(0)
    is_ctx = i < n_ctx_tiles
    xt = jnp.where(is_ctx, ctx_ref[...], x_ref[...])
    m = mod_ref[0]
    ms = jnp.mean(xt * xt, axis=-1, keepdims=True)
    u = xt * lax.rsqrt(ms + EPS) * g_ref[...] * (1.0 + m[1:2]) + m[0:1]
    u_hi = u.astype(BF16)
    u_lo = (u - u_hi.astype(F32)).astype(BF16)
    vm = jnp.where(is_ctx, cmask_ref[1], cmask_ref[0])

    d = d_model
    d_bc = 2 * SSD_GROUPS * SSD_STATE
    pieces = [(z_ref, d, "plain", None), (xs_ref, d, "convx", 0), (bc_ref, d_bc, "convx", d),
              (qk_ref, d, "convq", 0), (v_ref, d, "plain", None), (og_ref, d, "sigmoid", None),
              (mg_ref, 2 * d, "sigmoid", None)]
    col = 0
    for ref, width, kind, coff in pieces:
        for c0 in range(0, width, COL_CHUNK):
            acc = _dot(u_hi, wbig_ref[:, col + c0:col + c0 + COL_CHUNK])
            if kind == "convx":
                cs = coff + c0
                acc = _conv_silu(acc, cwx_ref[:, cs:cs + COL_CHUNK], cbx_ref[:, cs:cs + COL_CHUNK], vm, tm)
            elif kind == "convq":
                acc = _conv_silu(acc, cwq_ref[:, c0:c0 + COL_CHUNK], cbq_ref[:, c0:c0 + COL_CHUNK], vm, tm)
                if c0 < width // 2:
                    acc = acc * (dk ** -0.5)
            elif kind == "sigmoid":
                acc = _sigmoid(acc)
            ref[:, c0:c0 + COL_CHUNK] = acc.astype(ref.dtype)
        col += width

    wsh = wsh_ref[...]
    raw = _dot(u_hi, wsh) + _dot(u_lo, wsh) + _dot(u_hi, wsl_ref[...]) + smb_ref[...]
    p1 = raw[:, :LANES]
    p2 = raw[:, LANES:]
    lane = lax.broadcasted_iota(I32, (tm, LANES), 1)
    n_dt = 2 * (d_model // SSD_HEAD_DIM)
    n_g = 2 * (d_model // MLSTM_DV)
    is_dt = lane < n_dt
    is_gate = jnp.logical_and(lane >= n_dt, lane < n_dt + n_g)
    dt = _softplus(p2)
    pa = jnp.where(is_dt, dt * aneg_ref[...], jnp.where(is_gate, -_softplus(-p1), 0.0))
    pb = jnp.where(is_dt, dt, jnp.where(is_gate, p2, 0.0))
    lane_c = lax.broadcasted_iota(I32, (CHUNK, LANES), 1)
    is_dt_c = lane_c < n_dt
    rev = jnp.logical_or(jnp.logical_and(lane_c >= n_dt // 2, lane_c < n_dt),
                         jnp.logical_and(lane_c >= n_dt + n_g // 2, lane_c < n_dt + n_g))
    tril = tril_ref[...]
    triu = triu_ref[...]
    tq = lax.broadcasted_iota(I32, (CHUNK, LANES), 0)
    for c in range(tm // CHUNK):
        r0 = c * CHUNK
        a_c = pa[r0:r0 + CHUNK]
        hi = a_c.astype(BF16)
        r1 = a_c - hi.astype(F32)
        mid = r1.astype(BF16)
        lo = (r1 - mid.astype(F32)).astype(BF16)
        cs_f = _dot(tril, hi) + _dot(tril, mid) + _dot(tril, lo)
        cs_b = _dot(triu, hi) + _dot(triu, mid) + _dot(triu, lo)
        plane_a = jnp.where(rev, cs_b, cs_f)
        plane_b = jnp.where(is_dt_c, pb[r0:r0 + CHUNK], pb[r0:r0 + CHUNK] - plane_a)
        yf = plane_b
        yb = plane_b
        s = 1
        while s < CHUNK:
            sh = pltpu.roll(yf, s, axis=0)
            yf = jnp.maximum(yf, jnp.where(tq >= s, sh, NEG_INF))
            sh = pltpu.roll(yb, CHUNK - s, axis=0)
            yb = jnp.maximum(yb, jnp.where(tq + s < CHUNK, sh, NEG_INF))
            s *= 2
        plane_c = jnp.where(rev, yb, yf)
        small_ref[r0:r0 + CHUNK, 0:LANES] = plane_a
        small_ref[r0:r0 + CHUNK, LANES:2 * LANES] = plane_b
        small_ref[r0:r0 + CHUNK, 2 * LANES:3 * LANES] = plane_c
        smallt_ref[c, 0] = plane_a.T
        smallt_ref[c, 1] = plane_b.T


def _inproj(x2, ctx2, mods3, g, wbig, wsh, wsl, cwx, cbx, cwq, cbq, smb, aneg, *, batch, seq, ctx_len, dk):
    d = x2.shape[1]
    tm = TOKEN_TILE
    n_ctx_tok = batch * ctx_len
    assert n_ctx_tok % tm == 0 and seq % tm == 0 and tm % ctx_len == 0 and tm % GRID_W == 0
    n_ctx_tiles = n_ctx_tok // tm
    tiles_per_batch = seq // tm
    n_tiles = n_ctx_tiles + batch * tiles_per_batch
    t_all = n_tiles * tm
    n_big = wbig.shape[1]
    tril = jnp.asarray(np.tril(np.ones((CHUNK, CHUNK), np.float32)), BF16)
    triu = jnp.asarray(np.triu(np.ones((CHUNK, CHUNK), np.float32)), BF16)

    def x_map(i):
        return (jnp.maximum(i - n_ctx_tiles, 0), 0)

    def ctx_map(i):
        return (jnp.minimum(i, n_ctx_tiles - 1), 0)

    def mod_map(i):
        return (jnp.where(i < n_ctx_tiles, batch, jnp.maximum(i - n_ctx_tiles, 0) // tiles_per_batch), 0, 0)

    const = lambda i: (0, 0)
    row = lambda i: (i, 0)
    cmask = jnp.asarray(np.stack([_conv_masks(tm, GRID_W), _conv_masks(tm, ctx_len)]))
    kern = functools.partial(_inproj_kernel, n_ctx_tiles=n_ctx_tiles, tm=tm, d_model=d, dk=dk)
    d_bc = 2 * SSD_GROUPS * SSD_STATE
    outs = [jax.ShapeDtypeStruct((t_all, d), BF16), jax.ShapeDtypeStruct((t_all, d), BF16),
            jax.ShapeDtypeStruct((t_all, d_bc), BF16), jax.ShapeDtypeStruct((t_all, d), BF16),
            jax.ShapeDtypeStruct((t_all, d), BF16), jax.ShapeDtypeStruct((t_all, d), BF16),
            jax.ShapeDtypeStruct((t_all, 2 * d), BF16), jax.ShapeDtypeStruct((t_all, 3 * LANES), F32),
            jax.ShapeDtypeStruct((t_all // CHUNK, 2, LANES, CHUNK), F32)]
    out_specs = [pl.BlockSpec((tm, d), row), pl.BlockSpec((tm, d), row), pl.BlockSpec((tm, d_bc), row),
                 pl.BlockSpec((tm, d), row), pl.BlockSpec((tm, d), row), pl.BlockSpec((tm, d), row),
                 pl.BlockSpec((tm, 2 * d), row), pl.BlockSpec((tm, 3 * LANES), row),
                 pl.BlockSpec((tm // CHUNK, 2, LANES, CHUNK), lambda i: (i, 0, 0, 0))]
    in_specs = [pl.BlockSpec((tm, d), x_map), pl.BlockSpec((tm, d), ctx_map),
                pl.BlockSpec((1, 6, d), mod_map), pl.BlockSpec((1, d), const),
                pl.BlockSpec((d, n_big), const, pipeline_mode=pl.Buffered(1)),
                pl.BlockSpec((d, 2 * LANES), const), pl.BlockSpec((d, 2 * LANES), const),
                pl.BlockSpec(cwx.shape, const), pl.BlockSpec(cbx.shape, const),
                pl.BlockSpec(cwq.shape, const), pl.BlockSpec(cbq.shape, const),
                pl.BlockSpec((1, 2 * LANES), const), pl.BlockSpec((1, LANES), const),
                pl.BlockSpec((CHUNK, CHUNK), const), pl.BlockSpec((CHUNK, CHUNK), const),
                pl.BlockSpec((2, tm, 8), lambda i: (0, 0, 0))]
    return pl.pallas_call(
        kern, out_shape=outs, grid=(n_tiles,), in_specs=in_specs, out_specs=out_specs,
        compiler_params=pltpu.CompilerParams(dimension_semantics=("arbitrary",), vmem_limit_bytes=VMEM_LIMIT),
        name="inproj",
    )(x2, ctx2, mods3, g, wbig, wsh, wsl, cwx, cbx, cwq, cbq, smb, aneg, tril, triu, cmask)


def _chunk_block_map(direction, batch, n_ctx_chunks, n_lat_chunks):
    def idx(b, s):
        if direction == 0:
            c_ctx = s
            c_lat = s - n_ctx_chunks
        else:
            c_ctx = n_ctx_chunks - 1 - s
            c_lat = n_lat_chunks - 1 - (s - n_ctx_chunks)
        return jnp.where(s < n_ctx_chunks, b * n_ctx_chunks + c_ctx, batch * n_ctx_chunks + b * n_lat_chunks + c_lat)
    return idx


def _ssd_kernel(xs_ref, bc_ref, small_ref, smallt_ref, e16_ref, ecol_ref, y_ref, st_ref, *, direction, n_heads):
    s = pl.program_id(1)

    @pl.when(s == 0)
    def _():
        st_ref[...] = jnp.zeros_like(st_ref)

    hpg = n_heads // SSD_GROUPS
    lane0 = direction * n_heads
    last = CHUNK - 1 if direction == 0 else 0
    lane = lax.broadcasted_iota(I32, (CHUNK, LANES), 1)
    lm = jnp.logical_and(lane >= lane0, lane < lane0 + n_heads)
    plane_a = jnp.where(lm, small_ref[:, 0:LANES], 0.0)
    plane_b = jnp.where(lm, small_ref[:, LANES:2 * LANES], 0.0)
    cum_t = smallt_ref[0, 0]
    tot = plane_a[last:last + 1]
    e16 = e16_ref[...]
    dtx = _split_dot(plane_b, e16, 2)
    ecx = _split_dot(jnp.where(lm, jnp.exp(plane_a), 0.0), e16, 2)
    wx = _split_dot(plane_b * jnp.exp(tot - plane_a), e16, 2)
    decx = _split_dot(jnp.where(lm[0:8], jnp.exp(jnp.broadcast_to(tot, (8, LANES))), 0.0), e16, 2)[0:1]
    cumcol = _split_dot(plane_a, ecol_ref[...], 3)

    xf = xs_ref[...].astype(F32)
    xdt = xf * dtx
    xw = (xf * wx).astype(BF16)
    plane64 = lax.broadcasted_iota(I32, (CHUNK, 2 * SSD_HEAD_DIM), 1) < SSD_HEAD_DIM
    iq = lax.broadcasted_iota(I32, (CHUNK, CHUNK), 0)
    ik = lax.broadcasted_iota(I32, (CHUNK, CHUNK), 1)
    mask = (iq >= ik) if direction == 0 else (iq <= ik)
    gw = hpg * SSD_HEAD_DIM
    for g in range(SSD_GROUPS):
        bg = bc_ref[:, g * SSD_STATE:(g + 1) * SSD_STATE]
        cg = bc_ref[:, (SSD_GROUPS + g) * SSD_STATE:(SSD_GROUPS + g + 1) * SSD_STATE]
        cb = _dot_nt(cg, bg)
        sg = st_ref[:, g * gw:(g + 1) * gw]
        y_inter = _dot(cg, sg.astype(BF16)) * ecx[:, g * gw:(g + 1) * gw]
        for pair in range(hpg // 2):
            h0 = g * hpg + 2 * pair
            c0 = h0 * SSD_HEAD_DIM
            xpair = xdt[:, c0:c0 + 2 * SSD_HEAD_DIM]
            acc = y_inter[:, 2 * pair * SSD_HEAD_DIM:(2 * pair + 2) * SSD_HEAD_DIM]
            for par in range(2):
                h = h0 + par
                seg = cumcol[:, h * LANES:(h + 1) * LANES] - cum_t[lane0 + h:lane0 + h + 1, :]
                m_h = (cb * jnp.exp(jnp.where(mask, seg, NEG_INF))).astype(BF16)
                keep = plane64 if par == 0 else jnp.logical_not(plane64)
                acc = acc + _dot(m_h, jnp.where(keep, xpair, 0.0).astype(BF16))
            y_ref[:, c0:c0 + 2 * SSD_HEAD_DIM] = acc.astype(y_ref.dtype)
        st_ref[:, g * gw:(g + 1) * gw] = decx[:, g * gw:(g + 1) * gw] * sg + _dot_tn(bg, xw[:, g * gw:(g + 1) * gw])


def _selectors(direction, d):
    n_sh = d // SSD_HEAD_DIM
    n_mh = d // MLSTM_DV
    e16 = np.zeros((LANES, d), np.float32)
    ecol = np.zeros((LANES, n_sh * LANES), np.float32)
    em = np.zeros((LANES, d), np.float32)
    for h in range(n_sh):
        e16[direction * n_sh + h, h * SSD_HEAD_DIM:(h + 1) * SSD_HEAD_DIM] = 1.0
        ecol[direction * n_sh + h, h * LANES:(h + 1) * LANES] = 1.0
    for h in range(n_mh):
        em[2 * n_sh + direction * n_mh + h, h * MLSTM_DV:(h + 1) * MLSTM_DV] = 1.0
    return jnp.asarray(e16, BF16), jnp.asarray(ecol, BF16), jnp.asarray(em, BF16)


def _mlstm_kernel(qk_ref, v_ref, small_ref, smallt_ref, em_ref, h_ref, cn_ref, m_ref, *, direction, n_heads, dk):
    s = pl.program_id(1)

    @pl.when(s == 0)
    def _():
        cn_ref[...] = jnp.zeros_like(cn_ref)
        m_ref[...] = jnp.zeros_like(m_ref)

    n_dt = 2 * (n_heads * MLSTM_DV // SSD_HEAD_DIM)
    lane0 = n_dt + direction * n_heads
    last = CHUNK - 1 if direction == 0 else 0
    lane = lax.broadcasted_iota(I32, (CHUNK, LANES), 1)
    lm = jnp.logical_and(lane >= lane0, lane < lane0 + n_heads)
    b_q = jnp.where(lm, small_ref[:, 0:LANES], 0.0)
    r_k = jnp.where(lm, small_ref[:, LANES:2 * LANES], 0.0)
    cmr = jnp.where(lm, small_ref[:, 2 * LANES:3 * LANES], 0.0)
    r_t = smallt_ref[0, 1]
    m_all = m_ref[...]
    m_row = m_all[0:1]
    mm = jnp.maximum(cmr, m_row)
    w_state = jnp.exp(m_row - mm)
    e_mq = jnp.exp(-(b_q + mm))
    em = em_ref[...]
    mmx = _split_dot(mm, em, 3)
    wsx = _split_dot(jnp.where(lm, w_state, 0.0), em, 2)
    emqx = _split_dot(jnp.where(lm, e_mq, 0.0), em, 2)
    m_base8 = jnp.maximum(m_all, jnp.broadcast_to(cmr[last:last + 1], (8, LANES)))
    m_base = m_base8[0:1]
    w_k = jnp.where(lm, jnp.exp(r_k - m_base), 0.0)
    wkx = _split_dot(w_k, em, 2)
    decx = _split_dot(jnp.where(lm[0:8], jnp.exp(m_all - m_base8), 0.0), em, 2)[0:1]
    m_ref[...] = jnp.where(lm[0:8], jnp.broadcast_to(b_q[last:last + 1], (8, LANES)) + m_base8, 0.0)

    iq = lax.broadcasted_iota(I32, (CHUNK, CHUNK), 0)
    ik = lax.broadcasted_iota(I32, (CHUNK, CHUNK), 1)
    mask = (iq >= ik) if direction == 0 else (iq <= ik)
    ones = jnp.ones((CHUNK, MLSTM_DV), BF16)
    d_qk = n_heads * dk
    for h in range(n_heads):
        qh = qk_ref[:, h * dk:(h + 1) * dk]
        kh = qk_ref[:, d_qk + h * dk:d_qk + (h + 1) * dk]
        vh = v_ref[:, h * MLSTM_DV:(h + 1) * MLSTM_DV]
        hs = slice(h * MLSTM_DV, (h + 1) * MLSTM_DV)
        dmat = jnp.exp(jnp.where(mask, r_t[lane0 + h:lane0 + h + 1, :] - mmx[:, hs], NEG_INF))
        smat = (_dot_nt(qh, kh) * dmat).astype(BF16)
        cn = cn_ref[h]
        wsh = wsx[:, hs]
        tot = _dot(smat, jnp.concatenate([vh, ones], axis=1)) + jnp.concatenate([wsh, wsh], axis=1) * _dot(qh, cn.astype(BF16))
        num = tot[:, :MLSTM_DV]
        den = tot[:, MLSTM_DV:]
        h_ref[:, hs] = (num / jnp.maximum(jnp.abs(den), emqx[:, hs])).astype(h_ref.dtype)
        wkh = wkx[:, hs]
        rhs = jnp.concatenate([(vh.astype(F32) * wkh).astype(BF16), wkh.astype(BF16)], axis=1)
        dech = decx[:, hs]
        cn_ref[h] = jnp.concatenate([dech, dech], axis=1) * cn + _dot_tn(kh, rhs)


def _scans_kernel(*refs, n_sh, n_mh, dk):
    (xs0, bc0, qk0, v0, sm0, smt0, xs1, bc1, qk1, v1, sm1, smt1, e16_0, ecol_0, em_0, e16_1, ecol_1, em_1,
     y0, y1, h0, h1, st0, st1, cn0, cn1, m0, m1) = refs
    _ssd_kernel(xs0, bc0, sm0, smt0, e16_0, ecol_0, y0, st0, direction=0, n_heads=n_sh)
    _ssd_kernel(xs1, bc1, sm1, smt1, e16_1, ecol_1, y1, st1, direction=1, n_heads=n_sh)
    _mlstm_kernel(qk0, v0, sm0, smt0, em_0, h0, cn0, m0, direction=0, n_heads=n_mh, dk=dk)
    _mlstm_kernel(qk1, v1, sm1, smt1, em_1, h1, cn1, m1, direction=1, n_heads=n_mh, dk=dk)


def _scans(xs, bc, qk, v, small, smallt, *, batch, n_ctx_chunks, n_lat_chunks, dk):
    t_all, d = xs.shape
    n_sh = d // SSD_HEAD_DIM
    n_mh = d // MLSTM_DV
    n_steps = n_ctx_chunks + n_lat_chunks
    in_specs, args = [], []
    for direction in range(2):
        idx = _chunk_block_map(direction, batch, n_ctx_chunks, n_lat_chunks)
        rows = lambda b, s, idx=idx: (idx(b, s), 0)
        in_specs += [pl.BlockSpec((CHUNK, d), rows), pl.BlockSpec((CHUNK, bc.shape[1]), rows),
                     pl.BlockSpec((CHUNK, qk.shape[1]), rows), pl.BlockSpec((CHUNK, d), rows),
                     pl.BlockSpec((CHUNK, 3 * LANES), rows),
                     pl.BlockSpec((1, 2, LANES, CHUNK), lambda b, s, idx=idx: (idx(b, s), 0, 0, 0))]
        args += [xs, bc, qk, v, small, smallt]
    out_specs = []
    for direction in (0, 1, 0, 1):
        idx = _chunk_block_map(direction, batch, n_ctx_chunks, n_lat_chunks)
        out_specs.append(pl.BlockSpec((CHUNK, d), lambda b, s, idx=idx: (idx(b, s), 0)))
    for direction in range(2):
        sel = _selectors(direction, d)
        in_specs += [pl.BlockSpec(a.shape, lambda b, s: (0, 0)) for a in sel]
        args += list(sel)
    state = [pltpu.VMEM((SSD_STATE, d), F32)] * 2 + [pltpu.VMEM((n_mh, dk, 2 * MLSTM_DV), F32)] * 2 \
        + [pltpu.VMEM((SUBLANES, LANES), F32)] * 2
    return pl.pallas_call(
        functools.partial(_scans_kernel, n_sh=n_sh, n_mh=n_mh, dk=dk),
        out_shape=[jax.ShapeDtypeStruct((t_all, d), BF16)] * 4,
        grid=(batch, n_steps),
        in_specs=in_specs, out_specs=out_specs, scratch_shapes=state,
        compiler_params=pltpu.CompilerParams(dimension_semantics=("arbitrary", "arbitrary"),
                                             vmem_limit_bytes=VMEM_LIMIT),
        name="scans",
    )(*args)


def _merge_kernel(y0_ref, y1_ref, xs_ref, z_ref, h0_ref, h1_ref, og_ref, mg_ref, x_ref, mod_ref, dexp_ref, gs_ref,
                  gm_ref, gf_ref, wso_ref, wmo_ref, wo_ref, x1_ref, u2_ref, *, d_model):
    d = d_model
    m = mod_ref[0]
    y = y0_ref[...].astype(F32) + y1_ref[...].astype(F32) + dexp_ref[...] * xs_ref[...].astype(F32)
    zf = z_ref[...].astype(F32)
    y = y * (zf * _sigmoid(zf))
    y = y * lax.rsqrt(jnp.mean(y * y, axis=-1, keepdims=True) + EPS) * gs_ref[...]
    a = _dot(y.astype(BF16), wso_ref[...])
    hm = h0_ref[...].astype(F32) + h1_ref[...].astype(F32)
    parts = []
    for h in range(d // MLSTM_DV):
        blk = hm[:, h * MLSTM_DV:(h + 1) * MLSTM_DV]
        parts.append(blk * lax.rsqrt(jnp.mean(blk * blk, axis=-1, keepdims=True) + EPS))
    hn = jnp.concatenate(parts, axis=1) * gm_ref[...]
    hh = (og_ref[...].astype(F32) * hn).astype(BF16)
    bm = _dot(hh, wmo_ref[...])
    merged = mg_ref[:, :d].astype(F32) * a + mg_ref[:, d:].astype(F32) * bm
    r = _dot(merged.astype(BF16), wo_ref[...])
    x1 = x_ref[...] + m[2:3] * r
    x1_ref[...] = x1
    u2 = x1 * lax.rsqrt(jnp.mean(x1 * x1, axis=-1, keepdims=True) + EPS) * gf_ref[...] * (1.0 + m[4:5]) + m[3:4]
    u2_ref[...] = _pack_halves(u2)


def _merge(y0, y1, xs, z, h0, h1, og, mg, x2, mods3, dexp, gs, gm, gf, wso, wmo, wo, *, batch, seq, n_ctx_tok):
    t, d = x2.shape
    tm = TOKEN_TILE
    off = n_ctx_tok // tm
    tiles_per_batch = seq // tm
    lat = lambda i: (i + off, 0)
    row = lambda i: (i, 0)
    const = lambda i: (0, 0)
    kern = functools.partial(_merge_kernel, d_model=d)
    wspec = pl.BlockSpec((d, d), const, pipeline_mode=pl.Buffered(1))
    return pl.pallas_call(
        kern,
        out_shape=[jax.ShapeDtypeStruct((t, d), F32), jax.ShapeDtypeStruct((t, d // 2), U32)],
        grid=(t // tm,),
        in_specs=[pl.BlockSpec((tm, d), lat), pl.BlockSpec((tm, d), lat), pl.BlockSpec((tm, d), lat),
                  pl.BlockSpec((tm, d), lat), pl.BlockSpec((tm, d), lat), pl.BlockSpec((tm, d), lat),
                  pl.BlockSpec((tm, d), lat), pl.BlockSpec((tm, 2 * d), lat), pl.BlockSpec((tm, d), row),
                  pl.BlockSpec((1, 6, d), lambda i: (i // tiles_per_batch, 0, 0)),
                  pl.BlockSpec((1, d), const), pl.BlockSpec((1, d), const), pl.BlockSpec((1, d), const),
                  pl.BlockSpec((1, d), const), wspec, wspec, wspec],
        out_specs=[pl.BlockSpec((tm, d), row), pl.BlockSpec((tm, d // 2), row)],
        compiler_params=pltpu.CompilerParams(dimension_semantics=("arbitrary",), vmem_limit_bytes=VMEM_LIMIT),
        name="merge",
    )(y0, y1, xs, z, h0, h1, og, mg, x2, mods3, dexp, gs, gm, gf, wso, wmo, wo)


def _first_index_of_max(vals, row_iota, n_rows):
    mx = jnp.max(vals, axis=0, keepdims=True)
    idx = jnp.min(jnp.where(vals == mx, row_iota, n_rows), axis=0, keepdims=True)
    return mx, idx


def _router_kernel(u_ref, wt_ref, bias_ref, su_ref, idx_ref, pos_ref, wts_ref, cnt_ref, run_ref, *, n_experts, tr):
    i = pl.program_id(0)

    @pl.when(i == 0)
    def _():
        run_ref[...] = jnp.zeros_like(run_ref)

    ua, ub = _unpack_halves(u_ref[...])
    half = ua.shape[1]
    scores = _sigmoid(_dot_nt(wt_ref[:, :half], ua.astype(BF16)) + _dot_nt(wt_ref[:, half:], ub.astype(BF16)))
    biased = scores + bias_ref[...]
    gsz = n_experts // N_EXPERT_GROUPS
    gi = lax.broadcasted_iota(I32, (gsz, tr), 0).astype(F32)
    gscores = []
    for g in range(N_EXPERT_GROUPS):
        blk = biased[g * gsz:(g + 1) * gsz]
        m1, i1 = _first_index_of_max(blk, gi, gsz)
        m2 = jnp.max(jnp.where(gi == i1, NEG_INF, blk), axis=0, keepdims=True)
        gscores.append(m1 + m2)
    gs = jnp.concatenate(gscores, axis=0)
    g8 = lax.broadcasted_iota(I32, (N_EXPERT_GROUPS, tr), 0).astype(F32)
    gsel = jnp.zeros((N_EXPERT_GROUPS, tr), F32)
    for _ in range(TOPK_GROUPS):
        _, gidx = _first_index_of_max(gs, g8, N_EXPERT_GROUPS)
        hit = g8 == gidx
        gsel = jnp.where(hit, 1.0, gsel)
        gs = jnp.where(hit, NEG_INF, gs)
    cand = jnp.concatenate(
        [jnp.where(jnp.broadcast_to(gsel[g:g + 1], (gsz, tr)) > 0.5, biased[g * gsz:(g + 1) * gsz], NEG_INF)
         for g in range(N_EXPERT_GROUPS)], axis=0)
    ei = lax.broadcasted_iota(I32, (n_experts, tr), 0).astype(F32)
    sel = jnp.zeros((n_experts, tr), F32)
    idxs, ws = [], []
    for _ in range(TOP_K):
        _, eidx = _first_index_of_max(cand, ei, n_experts)
        hit = ei == eidx
        ws.append(jnp.sum(jnp.where(hit, scores, 0.0), axis=0, keepdims=True))
        idxs.append(eidx)
        sel = jnp.where(hit, 1.0, sel)
        cand = jnp.where(hit, NEG_INF, cand)
    wk = jnp.concatenate(ws, axis=0)
    wts_ref[...] = ROUTED_SCALE * wk / jnp.sum(wk, axis=0, keepdims=True)
    idx_ref[...] = jnp.concatenate(idxs, axis=0).astype(I32)
    selb = sel.astype(BF16)
    posmat = _dot(selb, su_ref[...]) + run_ref[:, 0:1]
    pos_ref[...] = jnp.concatenate(
        [jnp.sum(jnp.where(ei == idxs[k], posmat, 0.0), axis=0, keepdims=True) for k in range(TOP_K)],
        axis=0).astype(I32)
    run = run_ref[...] + _dot(selb, jnp.ones((tr, LANES), BF16))
    run_ref[...] = run
    cnt_ref[...] = run


def _router(u2p, router_wt, bias_col):
    t = u2p.shape[0]
    d = router_wt.shape[1]
    n_experts = router_wt.shape[0]
    tr = TOKEN_TILE
    su = jnp.asarray(np.triu(np.ones((tr, tr), np.float32), 1), BF16)
    kern = functools.partial(_router_kernel, n_experts=n_experts, tr=tr)
    col = lambda i: (0, i)
    const = lambda i: (0, 0)
    return pl.pallas_call(
        kern,
        out_shape=[jax.ShapeDtypeStruct((TOP_K, t), I32), jax.ShapeDtypeStruct((TOP_K, t), I32),
                   jax.ShapeDtypeStruct((TOP_K, t), F32), jax.ShapeDtypeStruct((n_experts, LANES), F32)],
        grid=(t // tr,),
        in_specs=[pl.BlockSpec((tr, d // 2), lambda i: (i, 0)), pl.BlockSpec((n_experts, d), const),
                  pl.BlockSpec((n_experts, 1), const), pl.BlockSpec((tr, tr), const)],
        out_specs=[pl.BlockSpec((TOP_K, tr), col), pl.BlockSpec((TOP_K, tr), col), pl.BlockSpec((TOP_K, tr), col),
                   pl.BlockSpec((n_experts, LANES), const)],
        scratch_shapes=[pltpu.VMEM((n_experts, LANES), F32)],
        compiler_params=pltpu.CompilerParams(dimension_semantics=("arbitrary",), vmem_limit_bytes=VMEM_LIMIT),
        name="router",
    )(u2p, router_wt, bias_col, su)


def _slots_kernel(idx_ref, pos_ref, pstart_ref, dest_ref, *, n_experts, tr):
    ei = lax.broadcasted_iota(I32, (n_experts, tr), 0).astype(F32)
    pstart = pstart_ref[...]
    idx = idx_ref[...].astype(F32)
    rows = [jnp.sum(jnp.where(ei == idx[k:k + 1], pstart, 0.0), axis=0, keepdims=True) for k in range(TOP_K)]
    dest_ref[...] = jnp.concatenate(rows, axis=0).astype(I32) + pos_ref[...]


def _slots(idx, pos, pstart_col):
    t = idx.shape[1]
    n_experts = pstart_col.shape[0]
    tr = TOKEN_TILE
    col = lambda i: (0, i)
    return pl.pallas_call(
        functools.partial(_slots_kernel, n_experts=n_experts, tr=tr),
        out_shape=jax.ShapeDtypeStruct((TOP_K, t), I32),
        grid=(t // tr,),
        in_specs=[pl.BlockSpec((TOP_K, tr), col), pl.BlockSpec((TOP_K, tr), col),
                  pl.BlockSpec((n_experts, 1), lambda i: (0, 0))],
        out_specs=pl.BlockSpec((TOP_K, tr), col),
        compiler_params=pltpu.CompilerParams(dimension_semantics=("arbitrary",), vmem_limit_bytes=VMEM_LIMIT),
        name="slots",
    )(idx, pos, pstart_col)


PAD_BITS = tuple(1 << b for b in reversed(range((MOE_BLOCK - 1).bit_length())))


def _dispatch_kernel(pstart_ref, cnt_ref, dest_ref, u_ref, xs_ref, zbuf, sem, psem, *, td, n_experts):
    i = pl.program_id(0)

    @pl.when(i == 0)
    def _():
        zbuf[...] = jnp.zeros_like(zbuf)

        def pads(e, wait):
            cnt = cnt_ref[e]
            n_pad = (MOE_BLOCK - (cnt & (MOE_BLOCK - 1))) & (MOE_BLOCK - 1)
            base = pstart_ref[e] + cnt
            n_single = n_pad & (SUBLANES - 1)

            def go(cp, cond):
                @pl.when(cond)
                def _():
                    if wait:
                        cp.wait()
                    else:
                        cp.start()

            for q in range(SUBLANES - 1):
                go(pltpu.make_async_copy(zbuf.at[pl.ds(0, 1)], xs_ref.at[pl.ds(base + q, 1)], psem), q < n_single)
            base = base + n_single
            for bit in PAD_BITS:
                if bit < SUBLANES:
                    continue
                has = (n_pad & bit) != 0
                go(pltpu.make_async_copy(zbuf.at[pl.ds(0, bit)], xs_ref.at[pl.ds(pl.multiple_of(base, SUBLANES), bit)],
                                         psem), has)
                base = base + jnp.where(has, bit, 0)

        def start_pads(e, carry):
            pads(e, False)
            return carry

        def wait_pads(e, carry):
            pads(e, True)
            return carry

        lax.fori_loop(0, n_experts, start_pads, 0)
        lax.fori_loop(0, n_experts, wait_pads, 0)

    def start(t, carry):
        for k in range(TOP_K):
            pltpu.make_async_copy(u_ref.at[pl.ds(t, 1)], xs_ref.at[pl.ds(dest_ref[k, t], 1)], sem).start(
                priority=k % 2)
        return carry

    def wait(t, carry):
        for k in range(TOP_K):
            pltpu.make_async_copy(u_ref.at[pl.ds(0, 1)], xs_ref.at[pl.ds(0, 1)], sem).wait()
        return carry

    lax.fori_loop(0, td, start, 0, unroll=4)
    lax.fori_loop(0, td, wait, 0, unroll=8)


def _dispatch(pad_start, counts, dest, u2p, n_slots):
    t, dh = u2p.shape
    td = DISPATCH_TILE
    n_experts = pad_start.shape[0]
    assert MOE_BLOCK & (MOE_BLOCK - 1) == 0
    kern = functools.partial(_dispatch_kernel, td=td, n_experts=n_experts)
    return pl.pallas_call(
        kern,
        out_shape=jax.ShapeDtypeStruct((n_slots, dh), U32),
        grid_spec=pltpu.PrefetchScalarGridSpec(
            num_scalar_prefetch=2, grid=(t // td,),
            in_specs=[pl.BlockSpec((TOP_K, td), lambda i, ps, cn: (0, i), memory_space=pltpu.SMEM),
                      pl.BlockSpec((td, dh), lambda i, ps, cn: (i, 0))],
            out_specs=pl.BlockSpec(memory_space=pl.ANY),
            scratch_shapes=[pltpu.VMEM((MOE_BLOCK // 2, dh), U32), pltpu.SemaphoreType.DMA,
                            pltpu.SemaphoreType.DMA]),
        compiler_params=pltpu.CompilerParams(dimension_semantics=("arbitrary",), vmem_limit_bytes=VMEM_LIMIT),
        name="dispatch",
    )(pad_start, counts, dest, u2p)


def _experts_kernel(bs_ref, nblk_ref, nu_ref, xs_ref, wg_ref, wu_ref, wd_ref, ys_ref, wgb, wub, wdb, xbuf, ybuf,
                    semx, semy, *, n_experts):
    e = pl.program_id(0)
    n_used = nu_ref[0]

    def x_copy(g, slot):
        rows = pl.ds(pl.multiple_of(g * MOE_BLOCK, MOE_BLOCK), MOE_BLOCK)
        return pltpu.make_async_copy(xs_ref.at[rows], xbuf.at[slot], semx.at[slot])

    def y_copy(g, slot):
        rows = pl.ds(pl.multiple_of(g * MOE_BLOCK, MOE_BLOCK), MOE_BLOCK)
        return pltpu.make_async_copy(ybuf.at[slot], ys_ref.at[rows], semy.at[slot])

    @pl.when(e == 0)
    def _():
        x_copy(0, 0).start()

    g0 = bs_ref[e]
    nb = nblk_ref[e]

    @pl.when(nb > 0)
    def _():
        wgb[...] = wg_ref[0].astype(BF16)
        wub[...] = wu_ref[0].astype(BF16)
        wdb[...] = wd_ref[0].astype(BF16)

    def block(g, carry):
        slot = g % 2

        @pl.when(g + 1 < n_used)
        def _():
            x_copy(g + 1, 1 - slot).start()

        x_copy(g, slot).wait()

        @pl.when(g >= 2)
        def _():
            y_copy(g - 2, slot).wait()

        xa, xb = _unpack_halves(xbuf[slot])
        half = xa.shape[1]
        xa = xa.astype(BF16)
        xb = xb.astype(BF16)
        hg = _dot(xa, wgb[:half]) + _dot(xb, wgb[half:])
        hu = _dot(xa, wub[:half]) + _dot(xb, wub[half:])
        hb = (hg * _sigmoid(hg) * hu).astype(BF16)
        ybuf[slot] = _pack_halves(_dot(hb, wdb[...]))
        y_copy(g, slot).start()
        return carry

    lax.fori_loop(g0, g0 + nb, block, 0)

    @pl.when(e == n_experts - 1)
    def _():
        @pl.when(n_used >= 2)
        def _():
            y_copy(n_used - 2, n_used % 2).wait()

        @pl.when(n_used >= 1)
        def _():
            y_copy(n_used - 1, (n_used - 1) % 2).wait()


def _experts(block_start, n_blocks, n_used, xsorted, wg, wu, wd):
    n_slots, dh = xsorted.shape
    n_experts, d, de = wg.shape
    wmap = lambda e, bs, nb, nu: (e, 0, 0)
    return pl.pallas_call(
        functools.partial(_experts_kernel, n_experts=n_experts),
        out_shape=jax.ShapeDtypeStruct((n_slots, dh), U32),
        grid_spec=pltpu.PrefetchScalarGridSpec(
            num_scalar_prefetch=3, grid=(n_experts,),
            in_specs=[pl.BlockSpec(memory_space=pl.ANY), pl.BlockSpec((1, d, de), wmap),
                      pl.BlockSpec((1, d, de), wmap), pl.BlockSpec((1, de, d), wmap)],
            out_specs=pl.BlockSpec(memory_space=pl.ANY),
            scratch_shapes=[pltpu.VMEM((d, de), BF16), pltpu.VMEM((d, de), BF16), pltpu.VMEM((de, d), BF16),
                            pltpu.VMEM((2, MOE_BLOCK, dh), U32), pltpu.VMEM((2, MOE_BLOCK, dh), U32),
                            pltpu.SemaphoreType.DMA((2,)), pltpu.SemaphoreType.DMA((2,))]),
        compiler_params=pltpu.CompilerParams(dimension_semantics=("arbitrary",), vmem_limit_bytes=VMEM_LIMIT),
        name="experts",
    )(block_start, n_blocks, n_used, xsorted, wg, wu, wd)


def _combine_kernel(dest_ref, dnext_ref, ys_ref, wt_ref, x1_ref, u_ref, mod_ref, wsg_ref, wsu_ref, wsd_ref,
                    gfin_ref, o_ref, gbuf, sem, *, tc, n_steps):
    i = pl.program_id(0)
    slot = i % 2

    def issue(dref, s):
        def body(t, carry):
            for k in range(TOP_K):
                pltpu.make_async_copy(ys_ref.at[pl.ds(dref[k, t], 1)], gbuf.at[s, k, pl.ds(t, 1)], sem.at[s]).start(
                    priority=k % 2)
            return carry
        lax.fori_loop(0, tc, body, 0, unroll=4)

    @pl.when(i == 0)
    def _():
        issue(dest_ref, 0)

    @pl.when(i + 1 < n_steps)
    def _():
        issue(dnext_ref, 1 - slot)

    ua, ub = _unpack_halves(u_ref[...])
    half = ua.shape[1]
    ua = ua.astype(BF16)
    ub = ub.astype(BF16)
    hg = _dot(ua, wsg_ref[:half]) + _dot(ub, wsg_ref[half:])
    hu = _dot(ua, wsu_ref[:half]) + _dot(ub, wsu_ref[half:])
    shared = _dot((hg * _sigmoid(hg) * hu).astype(BF16), wsd_ref[...])
    acc_l = shared[:, :half]
    acc_r = shared[:, half:]

    def wait(t, carry):
        for k in range(TOP_K):
            pltpu.make_async_copy(ys_ref.at[pl.ds(0, 1)], gbuf.at[slot, 0, pl.ds(0, 1)], sem.at[slot]).wait()
        return carry

    lax.fori_loop(0, tc, wait, 0, unroll=8)
    wt = wt_ref[...]
    for k in range(TOP_K):
        ga, gb = _unpack_halves(gbuf[slot, k])
        wk = wt[:, k:k + 1]
        acc_l = acc_l + ga * wk
        acc_r = acc_r + gb * wk
    m = mod_ref[0]
    xo_l = x1_ref[:, :half] + m[5:6, :half] * acc_l
    xo_r = x1_ref[:, half:] + m[5:6, half:] * acc_r
    ms = (jnp.sum(xo_l * xo_l, axis=-1, keepdims=True) + jnp.sum(xo_r * xo_r, axis=-1, keepdims=True)) / (2 * half)
    inv = lax.rsqrt(ms + EPS)
    o_ref[:, :half] = xo_l * inv * gfin_ref[:, :half]
    o_ref[:, half:] = xo_r * inv * gfin_ref[:, half:]


def _combine(dest, ysorted, wts_t, x1, u2p, mods3, wsg, wsu, wsd, gfin, *, seq):
    t, d = x1.shape
    dh = d // 2
    tc = DISPATCH_TILE
    tiles_per_batch = seq // tc
    n_steps = t // tc
    kern = functools.partial(_combine_kernel, tc=tc, n_steps=n_steps)
    row = lambda i: (i, 0)
    const = lambda i: (0, 0)
    return pl.pallas_call(
        kern,
        out_shape=jax.ShapeDtypeStruct((t, d), F32),
        grid=(n_steps,),
        in_specs=[pl.BlockSpec((TOP_K, tc), lambda i: (0, i), memory_space=pltpu.SMEM),
                  pl.BlockSpec((TOP_K, tc), lambda i: (0, jnp.minimum(i + 1, n_steps - 1)), memory_space=pltpu.SMEM),
                  pl.BlockSpec(memory_space=pl.ANY), pl.BlockSpec((tc, TOP_K), row),
                  pl.BlockSpec((tc, d), row), pl.BlockSpec((tc, dh), row),
                  pl.BlockSpec((1, 6, d), lambda i: (i // tiles_per_batch, 0, 0)),
                  pl.BlockSpec(wsg.shape, const), pl.BlockSpec(wsu.shape, const), pl.BlockSpec(wsd.shape, const),
                  pl.BlockSpec((1, d), const)],
        out_specs=pl.BlockSpec((tc, d), row),
        scratch_shapes=[pltpu.VMEM((2, TOP_K, tc, dh), U32), pltpu.SemaphoreType.DMA((2,))],
        compiler_params=pltpu.CompilerParams(dimension_semantics=("arbitrary",), vmem_limit_bytes=VMEM_LIMIT),
        name="combine",
    )(dest, dest, ysorted, wts_t, x1, u2p, mods3, wsg, wsu, wsd, gfin)


def kernel(x, c, ctx, c_ctx, ada_w, ada_b, norm_mix_g, norm_ffn_g, w_in, conv_xbc_w, conv_xbc_b, ssd_dt_bias, ssd_a_log, ssd_d, ssd_norm_g, conv_qk_w, conv_qk_b, mlstm_i_bias, mlstm_f_bias, mlstm_norm_g, w_ssd_out, w_mlstm_out, w_out, router_w, router_bias, moe_w_gate, moe_w_up, moe_w_down, shared_w_gate, shared_w_up, shared_w_down, norm_final_g):
    batch, seq, d = x.shape
    ctx_len = ctx.shape[1]
    depth = ada_w.shape[0]
    assert depth == 1, "only the single-layer configuration is implemented"
    assert seq % CHUNK == 0 and ctx_len % CHUNK == 0 and seq % GRID_W == 0
    l = 0
    n_sh = d // SSD_HEAD_DIM
    n_mh = d // MLSTM_DV
    dk = MLSTM_DV // 2
    d_xbc = d + 2 * SSD_GROUPS * SSD_STATE
    d_qk = 2 * n_mh * dk
    sizes = (d, d_xbc, 2 * n_sh, d_qk, d, 4 * n_mh, d, 2 * d)
    offs = np.concatenate([[0], np.cumsum(sizes)])
    assert offs[-1] == w_in.shape[2] and 2 * n_sh + 2 * n_mh <= LANES

    cond = jnp.concatenate([c, c_ctx[None], jnp.zeros((8 - (batch + 1) % 8, d), F32)], axis=0)
    mods = _adaln(cond, ada_w[l], ada_b[l])
    mods3 = mods.reshape(mods.shape[0], 6, d)

    w = w_in[l]
    seg = lambda k: w[:, offs[k]:offs[k + 1]]
    wbig = jnp.concatenate([seg(0), seg(1), seg(3), seg(4), seg(6), seg(7)], axis=1).astype(BF16)
    w_dt = seg(2)
    w_g = seg(5).reshape(d, 2, 2, n_mh)
    w_i = w_g[:, :, 0].reshape(d, 2 * n_mh)
    w_f = w_g[:, :, 1].reshape(d, 2 * n_mh)
    pad = jnp.zeros((d, LANES - 2 * n_sh - 2 * n_mh), F32)
    wsm = jnp.concatenate([w_dt, w_f, pad, w_dt, w_i, pad], axis=1)
    wsh = wsm.astype(BF16)
    wsl = (wsm - wsh.astype(F32)).astype(BF16)
    padb = jnp.zeros((LANES - 2 * n_sh - 2 * n_mh,), F32)
    dtb = ssd_dt_bias[l].reshape(-1).astype(F32)
    smb = jnp.concatenate([dtb, mlstm_f_bias[l].reshape(-1).astype(F32), padb,
                           dtb, mlstm_i_bias[l].reshape(-1).astype(F32), padb]).reshape(1, 2 * LANES)
    aneg = jnp.concatenate([-jnp.exp(ssd_a_log[l].astype(F32)).reshape(-1),
                            jnp.zeros((LANES - 2 * n_sh,), F32)]).reshape(1, LANES)

    x2 = x.reshape(batch * seq, d)
    ctx2 = ctx.reshape(batch * ctx_len, d)
    z, xs, bc, qk, v, og, mg, small, smallt = _inproj(
        x2, ctx2, mods3, norm_mix_g[l].reshape(1, d), wbig, wsh, wsl,
        conv_xbc_w[l], conv_xbc_b[l].reshape(1, d_xbc), conv_qk_w[l], conv_qk_b[l].reshape(1, d_qk), smb, aneg,
        batch=batch, seq=seq, ctx_len=ctx_len, dk=dk)

    ncc = ctx_len // CHUNK
    ncl = seq // CHUNK
    y0, y1, h0, h1 = _scans(xs, bc, qk, v, small, smallt, batch=batch, n_ctx_chunks=ncc, n_lat_chunks=ncl, dk=dk)

    dexp = jnp.repeat(ssd_d[l].astype(F32), SSD_HEAD_DIM).reshape(1, d)
    x1, u2 = _merge(y0, y1, xs, z, h0, h1, og, mg, x2, mods3, dexp, ssd_norm_g[l].reshape(1, d),
                    mlstm_norm_g[l].reshape(1, d), norm_ffn_g[l].reshape(1, d),
                    w_ssd_out[l].astype(BF16), w_mlstm_out[l].astype(BF16), w_out[l].astype(BF16),
                    batch=batch, seq=seq, n_ctx_tok=batch * ctx_len)

    n_experts = router_w.shape[2]
    idx, pos, wts, cnt = _router(u2, router_w[l].T.astype(BF16), router_bias[l].astype(F32).reshape(n_experts, 1))
    counts = cnt[:, 0].astype(I32)
    padded = (counts + MOE_BLOCK - 1) // MOE_BLOCK * MOE_BLOCK
    pad_end = jnp.cumsum(padded)
    pad_start = (pad_end - padded).astype(I32)
    t = batch * seq
    nb = t * TOP_K // MOE_BLOCK + n_experts
    n_used = (pad_end[-1] // MOE_BLOCK).astype(I32).reshape(1)

    dest = _slots(idx, pos, pad_start.astype(F32).reshape(n_experts, 1))
    xsorted = _dispatch(pad_start, counts, dest, u2, nb * MOE_BLOCK)
    ysorted = _experts(pad_start // MOE_BLOCK, (padded // MOE_BLOCK).astype(I32), n_used, xsorted,
                       moe_w_gate[l], moe_w_up[l], moe_w_down[l])
    out = _combine(dest, ysorted, wts.T, x1, u2, mods3,
                   shared_w_gate[l].astype(BF16), shared_w_up[l].astype(BF16), shared_w_down[l].astype(BF16),
                   norm_final_g.reshape(1, d), seq=seq)
    return out.reshape(batch, seq, d)
```

```python
import functools

import numpy as np
import jax
import jax.numpy as jnp
from jax import lax
from jax.experimental import pallas as pl
from jax.experimental.pallas import tpu as pltpu

F32 = jnp.float32
BF16 = jnp.bfloat16
I32 = jnp.int32
U32 = jnp.uint32

EPS = 1e-6
CHUNK = 128
CONV_K = 5
GRID_W = 64
SSD_HEAD_DIM = 64
SSD_STATE = 128
SSD_GROUPS = 2
MLSTM_DV = 128
N_EXPERT_GROUPS = 8
TOPK_GROUPS = 4
TOP_K = 8
ROUTED_SCALE = 2.5

LANES = 128
SUBLANES = 8
PACK_ROWS = 16
TOKEN_TILE = 512
COL_CHUNK = 512
MOE_BLOCK = 256
DISPATCH_TILE = 256
X_BUFFERS = 4
ROW_SUBLANES = 4
VMEM_LIMIT = 56 * 1024 * 1024
NEG_INF = float("-inf")


def _dot(a, b):
    return jnp.dot(a, b, preferred_element_type=F32)


def _dot_nt(a, b):
    return lax.dot_general(a, b, (((1,), (1,)), ((), ())), preferred_element_type=F32)


def _dot_tn(a, b):
    return lax.dot_general(a, b, (((0,), (0,)), ((), ())), preferred_element_type=F32)


def _spread(parts, e):
    stack = jnp.concatenate(parts, axis=0)
    n = stack.shape[0]
    hi = stack.astype(BF16)
    lo = (stack - hi.astype(F32)).astype(BF16)
    res = _dot(jnp.concatenate([hi, lo], axis=0), e)
    res = res[:n] + res[n:]
    out, r0 = [], 0
    for p in parts:
        out.append(res[r0:r0 + p.shape[0]])
        r0 += p.shape[0]
    return out


def _rows16(row):
    r8 = jnp.broadcast_to(row, (SUBLANES, row.shape[1]))
    return jnp.concatenate([r8, r8], axis=0)


def _sigmoid(v):
    return 1.0 / (1.0 + jnp.exp(-v))


def _pack_halves(v):
    n = v.shape[1] // 2
    hi = lax.bitcast_convert_type(v[:, :n].astype(BF16).astype(F32), U32)
    lo = lax.bitcast_convert_type(v[:, n:].astype(BF16).astype(F32), U32)
    return hi | (lo >> 16)


def _unpack_halves(p):
    left = lax.bitcast_convert_type(p & jnp.uint32(0xFFFF0000), F32)
    right = lax.bitcast_convert_type(p << 16, F32)
    return left, right


def _softplus(v):
    return jnp.maximum(v, 0.0) + jnp.log1p(jnp.exp(-jnp.abs(v)))


def _adaln_kernel(c_ref, w_ref, b_ref, o_ref):
    c = c_ref[...]
    s = c * _sigmoid(c)
    w = w_ref[...]
    s_hi = s.astype(BF16)
    s_lo = (s - s_hi.astype(F32)).astype(BF16)
    w_hi = w.astype(BF16)
    w_lo = (w - w_hi.astype(F32)).astype(BF16)
    o_ref[...] = _dot(s_hi, w_hi) + _dot(s_lo, w_hi) + _dot(s_hi, w_lo) + b_ref[...]


def _adaln(cond, w, b):
    rows, d = cond.shape
    n = w.shape[1]
    tn = 1536 if n % 1536 == 0 else n
    return pl.pallas_call(
        _adaln_kernel,
        out_shape=jax.ShapeDtypeStruct((rows, n), F32),
        grid=(n // tn,),
        in_specs=[pl.BlockSpec((rows, d), lambda j: (0, 0)),
                  pl.BlockSpec((d, tn), lambda j: (0, j)),
                  pl.BlockSpec((1, tn), lambda j: (0, j))],
        out_specs=pl.BlockSpec((rows, tn), lambda j: (0, j)),
        compiler_params=pltpu.CompilerParams(dimension_semantics=("arbitrary",), vmem_limit_bytes=VMEM_LIMIT),
        name="adaln",
    )(cond, w, b.reshape(1, n))


CONV_SHIFTS = tuple(j - CONV_K // 2 for j in range(CONV_K) if j != CONV_K // 2)


def _conv_masks(tm, seg_len):
    pos = np.arange(tm) % seg_len
    m = np.zeros((tm, 8), np.float32)
    for i, s in enumerate(CONV_SHIFTS):
        m[:, i] = ((pos + s >= 0) & (pos + s < seg_len)).astype(np.float32)
    return m


def _conv_silu(acc, w5, bias, vm, tm):
    out = acc * w5[CONV_K // 2:CONV_K // 2 + 1] + bias
    for i, s in enumerate(CONV_SHIFTS):
        shifted = pltpu.roll(acc, (-s) % tm, axis=0)
        j = s + CONV_K // 2
        out = out + (shifted * vm[:, i:i + 1]) * w5[j:j + 1]
    return out * _sigmoid(out)


def _inproj_kernel(x_ref, ctx_ref, mod_ref, g_ref, wbig_ref, wsh_ref, wsl_ref, cwx_ref, cbx_ref, cwq_ref, cbq_ref,
                   smb_ref, aneg_ref, tril_ref, triu_ref, cmask_ref,
                   z_ref, xs_ref, bc_ref, qk_ref, v_ref, og_ref, mg_ref, small_ref, smallt_ref,
                   *, n_ctx_tiles, tm, d_model, dk):
    i = pl.program_id(0)
    is_ctx = i < n_ctx_tiles
    xt = jnp.where(is_ctx, ctx_ref[...], x_ref[...])
    m = mod_ref[0]
    ms = jnp.mean(xt * xt, axis=-1, keepdims=True)
    u = xt * lax.rsqrt(ms + EPS) * g_ref[...] * (1.0 + m[1:2]) + m[0:1]
    u_hi = u.astype(BF16)
    u_lo = (u - u_hi.astype(F32)).astype(BF16)
    vm = jnp.where(is_ctx, cmask_ref[1], cmask_ref[0])

    d = d_model
    d_bc = 2 * SSD_GROUPS * SSD_STATE
    pieces = [(z_ref, d, "plain", None), (xs_ref, d, "convx", 0), (bc_ref, d_bc, "convx", d),
              (qk_ref, d, "convq", 0), (v_ref, d, "plain", None), (og_ref, d, "sigmoid", None),
              (mg_ref, 2 * d, "sigmoid", None)]
    col = 0
    for ref, width, kind, coff in pieces:
        for c0 in range(0, width, COL_CHUNK):
            acc = _dot(u_hi, wbig_ref[:, col + c0:col + c0 + COL_CHUNK])
            if kind == "convx":
                cs = coff + c0
                acc = _conv_silu(acc, cwx_ref[:, cs:cs + COL_CHUNK], cbx_ref[:, cs:cs + COL_CHUNK], vm, tm)
            elif kind == "convq":
                acc = _conv_silu(acc, cwq_ref[:, c0:c0 + COL_CHUNK], cbq_ref[:, c0:c0 + COL_CHUNK], vm, tm)
                if c0 < width // 2:
                    acc = acc * (dk ** -0.5)
            elif kind == "sigmoid":
                acc = _sigmoid(acc)
            ref[:, c0:c0 + COL_CHUNK] = acc.astype(ref.dtype)
        col += width

    wsh = wsh_ref[...]
    raw = _dot(u_hi, wsh) + _dot(u_lo, wsh) + _dot(u_hi, wsl_ref[...]) + smb_ref[...]
    p1 = raw[:, :LANES]
    p2 = raw[:, LANES:]
    lane = lax.broadcasted_iota(I32, (tm, LANES), 1)
    n_dt = 2 * (d_model // SSD_HEAD_DIM)
    n_g = 2 * (d_model // MLSTM_DV)
    is_dt = lane < n_dt
    is_gate = jnp.logical_and(lane >= n_dt, lane < n_dt + n_g)
    dt = _softplus(p2)
    pa = jnp.where(is_dt, dt * aneg_ref[...], jnp.where(is_gate, -_softplus(-p1), 0.0))
    pb = jnp.where(is_dt, dt, jnp.where(is_gate, p2, 0.0))
    lane_c = lax.broadcasted_iota(I32, (CHUNK, LANES), 1)
    is_dt_c = lane_c < n_dt
    rev = jnp.logical_or(jnp.logical_and(lane_c >= n_dt // 2, lane_c < n_dt),
                         jnp.logical_and(lane_c >= n_dt + n_g // 2, lane_c < n_dt + n_g))
    tril = tril_ref[...]
    triu = triu_ref[...]
    tq = lax.broadcasted_iota(I32, (CHUNK, LANES), 0)
    for c in range(tm // CHUNK):
        r0 = c * CHUNK
        a_c = pa[r0:r0 + CHUNK]
        hi = a_c.astype(BF16)
        r1 = a_c - hi.astype(F32)
        mid = r1.astype(BF16)
        lo = (r1 - mid.astype(F32)).astype(BF16)
        cs_f = _dot(tril, hi) + _dot(tril, mid) + _dot(tril, lo)
        cs_b = _dot(triu, hi) + _dot(triu, mid) + _dot(triu, lo)
        plane_a = jnp.where(rev, cs_b, cs_f)
        plane_b = jnp.where(is_dt_c, pb[r0:r0 + CHUNK], pb[r0:r0 + CHUNK] - plane_a)
        yf = plane_b
        yb = plane_b
        s = 1
        while s < CHUNK:
            sh = pltpu.roll(yf, s, axis=0)
            yf = jnp.maximum(yf, jnp.where(tq >= s, sh, NEG_INF))
            sh = pltpu.roll(yb, CHUNK - s, axis=0)
            yb = jnp.maximum(yb, jnp.where(tq + s < CHUNK, sh, NEG_INF))
            s *= 2
        plane_c = jnp.where(rev, yb, yf)
        small_ref[r0:r0 + CHUNK, 0:LANES] = plane_a
        small_ref[r0:r0 + CHUNK, LANES:2 * LANES] = plane_b
        small_ref[r0:r0 + CHUNK, 2 * LANES:3 * LANES] = plane_c
        smallt_ref[c, 0] = plane_a.T
        smallt_ref[c, 1] = plane_b.T


def _inproj(x2, ctx2, mods3, g, wbig, wsh, wsl, cwx, cbx, cwq, cbq, smb, aneg, *, batch, seq, ctx_len, dk):
    d = x2.shape[1]
    tm = TOKEN_TILE
    n_ctx_tok = batch * ctx_len
    assert n_ctx_tok % tm == 0 and seq % tm == 0 and tm % ctx_len == 0 and tm % GRID_W == 0
    n_ctx_tiles = n_ctx_tok // tm
    tiles_per_batch = seq // tm
    n_tiles = n_ctx_tiles + batch * tiles_per_batch
    t_all = n_tiles * tm
    n_big = wbig.shape[1]
    tril = jnp.asarray(np.tril(np.ones((CHUNK, CHUNK), np.float32)), BF16)
    triu = jnp.asarray(np.triu(np.ones((CHUNK, CHUNK), np.float32)), BF16)

    def x_map(i):
        return (jnp.maximum(i - n_ctx_tiles, 0), 0)

    def ctx_map(i):
        return (jnp.minimum(i, n_ctx_tiles - 1), 0)

    def mod_map(i):
        return (jnp.where(i < n_ctx_tiles, batch, jnp.maximum(i - n_ctx_tiles, 0) // tiles_per_batch), 0, 0)

    const = lambda i: (0, 0)
    row = lambda i: (i, 0)
    cmask = jnp.asarray(np.stack([_conv_masks(tm, GRID_W), _conv_masks(tm, ctx_len)]))
    kern = functools.partial(_inproj_kernel, n_ctx_tiles=n_ctx_tiles, tm=tm, d_model=d, dk=dk)
    d_bc = 2 * SSD_GROUPS * SSD_STATE
    outs = [jax.ShapeDtypeStruct((t_all, d), BF16), jax.ShapeDtypeStruct((t_all, d), BF16),
            jax.ShapeDtypeStruct((t_all, d_bc), BF16), jax.ShapeDtypeStruct((t_all, d), BF16),
            jax.ShapeDtypeStruct((t_all, d), BF16), jax.ShapeDtypeStruct((t_all, d), BF16),
            jax.ShapeDtypeStruct((t_all, 2 * d), BF16), jax.ShapeDtypeStruct((t_all, 3 * LANES), F32),
            jax.ShapeDtypeStruct((t_all // CHUNK, 2, LANES, CHUNK), F32)]
    out_specs = [pl.BlockSpec((tm, d), row), pl.BlockSpec((tm, d), row), pl.BlockSpec((tm, d_bc), row),
                 pl.BlockSpec((tm, d), row), pl.BlockSpec((tm, d), row), pl.BlockSpec((tm, d), row),
                 pl.BlockSpec((tm, 2 * d), row), pl.BlockSpec((tm, 3 * LANES), row),
                 pl.BlockSpec((tm // CHUNK, 2, LANES, CHUNK), lambda i: (i, 0, 0, 0))]
    in_specs = [pl.BlockSpec((tm, d), x_map), pl.BlockSpec((tm, d), ctx_map),
                pl.BlockSpec((1, 6, d), mod_map), pl.BlockSpec((1, d), const),
                pl.BlockSpec((d, n_big), const, pipeline_mode=pl.Buffered(1)),
                pl.BlockSpec((d, 2 * LANES), const), pl.BlockSpec((d, 2 * LANES), const),
                pl.BlockSpec(cwx.shape, const), pl.BlockSpec(cbx.shape, const),
                pl.BlockSpec(cwq.shape, const), pl.BlockSpec(cbq.shape, const),
                pl.BlockSpec((1, 2 * LANES), const), pl.BlockSpec((1, LANES), const),
                pl.BlockSpec((CHUNK, CHUNK), const), pl.BlockSpec((CHUNK, CHUNK), const),
                pl.BlockSpec((2, tm, 8), lambda i: (0, 0, 0))]
    return pl.pallas_call(
        kern, out_shape=outs, grid=(n_tiles,), in_specs=in_specs, out_specs=out_specs,
        compiler_params=pltpu.CompilerParams(dimension_semantics=("arbitrary",), vmem_limit_bytes=VMEM_LIMIT),
        name="inproj",
    )(x2, ctx2, mods3, g, wbig, wsh, wsl, cwx, cbx, cwq, cbq, smb, aneg, tril, triu, cmask)


def _chunk_block_map(direction, batch, n_ctx_chunks, n_lat_chunks):
    def idx(b, s):
        if direction == 0:
            c_ctx = s
            c_lat = s - n_ctx_chunks
        else:
            c_ctx = n_ctx_chunks - 1 - s
            c_lat = n_lat_chunks - 1 - (s - n_ctx_chunks)
        return jnp.where(s < n_ctx_chunks, b * n_ctx_chunks + c_ctx, batch * n_ctx_chunks + b * n_lat_chunks + c_lat)
    return idx


def _ssd_kernel(xs_ref, bc_ref, small_ref, smallt_ref, e16_ref, ecol_ref, y_ref, st_ref, *, direction, n_heads):
    s = pl.program_id(1)

    @pl.when(s == 0)
    def _():
        st_ref[...] = jnp.zeros_like(st_ref)

    hpg = n_heads // SSD_GROUPS
    lane0 = direction * n_heads
    last = CHUNK - 1 if direction == 0 else 0
    lane = lax.broadcasted_iota(I32, (CHUNK, LANES), 1)
    lm = jnp.logical_and(lane >= lane0, lane < lane0 + n_heads)
    plane_a = jnp.where(lm, small_ref[:, 0:LANES], 0.0)
    plane_b = jnp.where(lm, small_ref[:, LANES:2 * LANES], 0.0)
    cum_t = smallt_ref[0, 0]
    tot = plane_a[last:last + 1]
    lane16 = lax.broadcasted_iota(I32, (PACK_ROWS, LANES), 1)
    lm16 = jnp.logical_and(lane16 >= lane0, lane16 < lane0 + n_heads)
    dec_rows = jnp.where(lm16, jnp.exp(_rows16(tot)), 0.0)
    dtx, ecx, wx, decx = _spread(
        [plane_b, jnp.where(lm, jnp.exp(plane_a), 0.0), plane_b * jnp.exp(tot - plane_a), dec_rows], e16_ref[...])
    decx = decx[0:1]
    cumcol, = _spread([plane_a], ecol_ref[...])

    xf = xs_ref[...].astype(F32)
    xdt = xf * dtx
    xw = (xf * wx).astype(BF16)
    plane64 = lax.broadcasted_iota(I32, (CHUNK, 2 * SSD_HEAD_DIM), 1) < SSD_HEAD_DIM
    iq = lax.broadcasted_iota(I32, (CHUNK, CHUNK), 0)
    ik = lax.broadcasted_iota(I32, (CHUNK, CHUNK), 1)
    mask = (iq >= ik) if direction == 0 else (iq <= ik)
    gw = hpg * SSD_HEAD_DIM
    for g in range(SSD_GROUPS):
        bg = bc_ref[:, g * SSD_STATE:(g + 1) * SSD_STATE]
        cg = bc_ref[:, (SSD_GROUPS + g) * SSD_STATE:(SSD_GROUPS + g + 1) * SSD_STATE]
        cb = _dot_nt(cg, bg)
        sg = st_ref[:, g * gw:(g + 1) * gw]
        y_inter = _dot(cg, sg.astype(BF16)) * ecx[:, g * gw:(g + 1) * gw]
        for pair in range(hpg // 2):
            h0 = g * hpg + 2 * pair
            c0 = h0 * SSD_HEAD_DIM
            xpair = xdt[:, c0:c0 + 2 * SSD_HEAD_DIM]
            acc = y_inter[:, 2 * pair * SSD_HEAD_DIM:(2 * pair + 2) * SSD_HEAD_DIM]
            for par in range(2):
                h = h0 + par
                seg = cumcol[:, h * LANES:(h + 1) * LANES] - cum_t[lane0 + h:lane0 + h + 1, :]
                m_h = (cb * jnp.exp(jnp.where(mask, seg, NEG_INF))).astype(BF16)
                keep = plane64 if par == 0 else jnp.logical_not(plane64)
                acc = acc + _dot(m_h, jnp.where(keep, xpair, 0.0).astype(BF16))
            y_ref[:, c0:c0 + 2 * SSD_HEAD_DIM] = acc.astype(y_ref.dtype)
        st_ref[:, g * gw:(g + 1) * gw] = decx[:, g * gw:(g + 1) * gw] * sg + _dot_tn(bg, xw[:, g * gw:(g + 1) * gw])


def _selectors(direction, d):
    n_sh = d // SSD_HEAD_DIM
    n_mh = d // MLSTM_DV
    e16 = np.zeros((LANES, d), np.float32)
    ecol = np.zeros((LANES, n_sh * LANES), np.float32)
    em = np.zeros((LANES, d), np.float32)
    for h in range(n_sh):
        e16[direction * n_sh + h, h * SSD_HEAD_DIM:(h + 1) * SSD_HEAD_DIM] = 1.0
        ecol[direction * n_sh + h, h * LANES:(h + 1) * LANES] = 1.0
    for h in range(n_mh):
        em[2 * n_sh + direction * n_mh + h, h * MLSTM_DV:(h + 1) * MLSTM_DV] = 1.0
    return jnp.asarray(e16, BF16), jnp.asarray(ecol, BF16), jnp.asarray(em, BF16)


def _mlstm_kernel(qk_ref, v_ref, small_ref, smallt_ref, em_ref, h_ref, cn_ref, m_ref, *, direction, n_heads, dk):
    s = pl.program_id(1)

    @pl.when(s == 0)
    def _():
        cn_ref[...] = jnp.zeros_like(cn_ref)
        m_ref[...] = jnp.zeros_like(m_ref)

    n_dt = 2 * (n_heads * MLSTM_DV // SSD_HEAD_DIM)
    lane0 = n_dt + direction * n_heads
    last = CHUNK - 1 if direction == 0 else 0
    lane = lax.broadcasted_iota(I32, (CHUNK, LANES), 1)
    lm = jnp.logical_and(lane >= lane0, lane < lane0 + n_heads)
    b_q = jnp.where(lm, small_ref[:, 0:LANES], 0.0)
    r_k = jnp.where(lm, small_ref[:, LANES:2 * LANES], 0.0)
    cmr = jnp.where(lm, small_ref[:, 2 * LANES:3 * LANES], 0.0)
    r_t = smallt_ref[0, 1]
    m_all = m_ref[...]
    m_row = m_all[0:1]
    mm = jnp.maximum(cmr, m_row)
    w_state = jnp.exp(m_row - mm)
    e_mq = jnp.exp(-(b_q + mm))
    m_base8 = jnp.maximum(m_all, _rows16(cmr[last:last + 1]))
    m_base = m_base8[0:1]
    w_k = jnp.where(lm, jnp.exp(r_k - m_base), 0.0)
    lane16 = lax.broadcasted_iota(I32, (PACK_ROWS, LANES), 1)
    lm16 = jnp.logical_and(lane16 >= lane0, lane16 < lane0 + n_heads)
    dec_rows = jnp.where(lm16, jnp.exp(m_all - m_base8), 0.0)
    mmx, wsx, emqx, wkx, decx = _spread(
        [mm, jnp.where(lm, w_state, 0.0), jnp.where(lm, e_mq, 0.0), w_k, dec_rows], em_ref[...])
    decx = decx[0:1]
    m_ref[...] = jnp.where(lm16, _rows16(b_q[last:last + 1]) + m_base8, 0.0)

    iq = lax.broadcasted_iota(I32, (CHUNK, CHUNK), 0)
    ik = lax.broadcasted_iota(I32, (CHUNK, CHUNK), 1)
    mask = (iq >= ik) if direction == 0 else (iq <= ik)
    ones = jnp.ones((CHUNK, MLSTM_DV), BF16)
    d_qk = n_heads * dk
    for h in range(n_heads):
        qh = qk_ref[:, h * dk:(h + 1) * dk]
        kh = qk_ref[:, d_qk + h * dk:d_qk + (h + 1) * dk]
        vh = v_ref[:, h * MLSTM_DV:(h + 1) * MLSTM_DV]
        hs = slice(h * MLSTM_DV, (h + 1) * MLSTM_DV)
        dmat = jnp.exp(jnp.where(mask, r_t[lane0 + h:lane0 + h + 1, :] - mmx[:, hs], NEG_INF))
        smat = (_dot_nt(qh, kh) * dmat).astype(BF16)
        cn = cn_ref[h]
        wsh = wsx[:, hs]
        tot = _dot(smat, jnp.concatenate([vh, ones], axis=1)) + jnp.concatenate([wsh, wsh], axis=1) * _dot(qh, cn.astype(BF16))
        num = tot[:, :MLSTM_DV]
        den = tot[:, MLSTM_DV:]
        h_ref[:, hs] = (num / jnp.maximum(jnp.abs(den), emqx[:, hs])).astype(h_ref.dtype)
        wkh = wkx[:, hs]
        rhs = jnp.concatenate([(vh.astype(F32) * wkh).astype(BF16), wkh.astype(BF16)], axis=1)
        dech = decx[:, hs]
        cn_ref[h] = jnp.concatenate([dech, dech], axis=1) * cn + _dot_tn(kh, rhs)


def _scans_kernel(*refs, n_sh, n_mh, dk):
    (xs0, bc0, qk0, v0, sm0, smt0, xs1, bc1, qk1, v1, sm1, smt1, e16_0, ecol_0, em_0, e16_1, ecol_1, em_1,
     y0, y1, h0, h1, st0, st1, cn0, cn1, m0, m1) = refs
    _ssd_kernel(xs0, bc0, sm0, smt0, e16_0, ecol_0, y0, st0, direction=0, n_heads=n_sh)
    _ssd_kernel(xs1, bc1, sm1, smt1, e16_1, ecol_1, y1, st1, direction=1, n_heads=n_sh)
    _mlstm_kernel(qk0, v0, sm0, smt0, em_0, h0, cn0, m0, direction=0, n_heads=n_mh, dk=dk)
    _mlstm_kernel(qk1, v1, sm1, smt1, em_1, h1, cn1, m1, direction=1, n_heads=n_mh, dk=dk)


def _scans(xs, bc, qk, v, small, smallt, *, batch, n_ctx_chunks, n_lat_chunks, dk):
    t_all, d = xs.shape
    n_sh = d // SSD_HEAD_DIM
    n_mh = d // MLSTM_DV
    n_steps = n_ctx_chunks + n_lat_chunks
    in_specs, args = [], []
    for direction in range(2):
        idx = _chunk_block_map(direction, batch, n_ctx_chunks, n_lat_chunks)
        rows = lambda b, s, idx=idx: (idx(b, s), 0)
        in_specs += [pl.BlockSpec((CHUNK, d), rows), pl.BlockSpec((CHUNK, bc.shape[1]), rows),
                     pl.BlockSpec((CHUNK, qk.shape[1]), rows), pl.BlockSpec((CHUNK, d), rows),
                     pl.BlockSpec((CHUNK, 3 * LANES), rows),
                     pl.BlockSpec((1, 2, LANES, CHUNK), lambda b, s, idx=idx: (idx(b, s), 0, 0, 0))]
        args += [xs, bc, qk, v, small, smallt]
    out_specs = []
    for direction in (0, 1, 0, 1):
        idx = _chunk_block_map(direction, batch, n_ctx_chunks, n_lat_chunks)
        out_specs.append(pl.BlockSpec((CHUNK, d), lambda b, s, idx=idx: (idx(b, s), 0)))
    for direction in range(2):
        sel = _selectors(direction, d)
        in_specs += [pl.BlockSpec(a.shape, lambda b, s: (0, 0)) for a in sel]
        args += list(sel)
    state = [pltpu.VMEM((SSD_STATE, d), F32)] * 2 + [pltpu.VMEM((n_mh, dk, 2 * MLSTM_DV), F32)] * 2 \
        + [pltpu.VMEM((PACK_ROWS, LANES), F32)] * 2
    return pl.pallas_call(
        functools.partial(_scans_kernel, n_sh=n_sh, n_mh=n_mh, dk=dk),
        out_shape=[jax.ShapeDtypeStruct((t_all, d), BF16)] * 4,
        grid=(batch, n_steps),
        in_specs=in_specs, out_specs=out_specs, scratch_shapes=state,
        compiler_params=pltpu.CompilerParams(dimension_semantics=("arbitrary", "arbitrary"),
                                             vmem_limit_bytes=VMEM_LIMIT),
        name="scans",
    )(*args)


def _merge_kernel(y0_ref, y1_ref, xs_ref, z_ref, h0_ref, h1_ref, og_ref, mg_ref, x_ref, mod_ref, dexp_ref, gs_ref,
                  gm_ref, gf_ref, wso_ref, wmo_ref, wo_ref, x1_ref, u2_ref, *, d_model):
    d = d_model
    m = mod_ref[0]
    y = y0_ref[...].astype(F32) + y1_ref[...].astype(F32) + dexp_ref[...] * xs_ref[...].astype(F32)
    zf = z_ref[...].astype(F32)
    y = y * (zf * _sigmoid(zf))
    y = y * lax.rsqrt(jnp.mean(y * y, axis=-1, keepdims=True) + EPS) * gs_ref[...]
    a = _dot(y.astype(BF16), wso_ref[...])
    hm = h0_ref[...].astype(F32) + h1_ref[...].astype(F32)
    parts = []
    for h in range(d // MLSTM_DV):
        blk = hm[:, h * MLSTM_DV:(h + 1) * MLSTM_DV]
        parts.append(blk * lax.rsqrt(jnp.mean(blk * blk, axis=-1, keepdims=True) + EPS))
    hn = jnp.concatenate(parts, axis=1) * gm_ref[...]
    hh = (og_ref[...].astype(F32) * hn).astype(BF16)
    bm = _dot(hh, wmo_ref[...])
    merged = mg_ref[:, :d].astype(F32) * a + mg_ref[:, d:].astype(F32) * bm
    r = _dot(merged.astype(BF16), wo_ref[...])
    x1 = x_ref[...] + m[2:3] * r
    x1_ref[...] = x1
    u2 = x1 * lax.rsqrt(jnp.mean(x1 * x1, axis=-1, keepdims=True) + EPS) * gf_ref[...] * (1.0 + m[4:5]) + m[3:4]
    u2_ref[...] = _pack_halves(u2)


def _merge(y0, y1, xs, z, h0, h1, og, mg, x2, mods3, dexp, gs, gm, gf, wso, wmo, wo, *, batch, seq, n_ctx_tok):
    t, d = x2.shape
    tm = TOKEN_TILE
    off = n_ctx_tok // tm
    tiles_per_batch = seq // tm
    lat = lambda i: (i + off, 0)
    row = lambda i: (i, 0)
    const = lambda i: (0, 0)
    kern = functools.partial(_merge_kernel, d_model=d)
    wspec = pl.BlockSpec((d, d), const, pipeline_mode=pl.Buffered(1))
    return pl.pallas_call(
        kern,
        out_shape=[jax.ShapeDtypeStruct((t, d), F32), jax.ShapeDtypeStruct((t, d // 2), U32)],
        grid=(t // tm,),
        in_specs=[pl.BlockSpec((tm, d), lat), pl.BlockSpec((tm, d), lat), pl.BlockSpec((tm, d), lat),
                  pl.BlockSpec((tm, d), lat), pl.BlockSpec((tm, d), lat), pl.BlockSpec((tm, d), lat),
                  pl.BlockSpec((tm, d), lat), pl.BlockSpec((tm, 2 * d), lat), pl.BlockSpec((tm, d), row),
                  pl.BlockSpec((1, 6, d), lambda i: (i // tiles_per_batch, 0, 0)),
                  pl.BlockSpec((1, d), const), pl.BlockSpec((1, d), const), pl.BlockSpec((1, d), const),
                  pl.BlockSpec((1, d), const), wspec, wspec, wspec],
        out_specs=[pl.BlockSpec((tm, d), row), pl.BlockSpec((tm, d // 2), row)],
        compiler_params=pltpu.CompilerParams(dimension_semantics=("arbitrary",), vmem_limit_bytes=VMEM_LIMIT),
        name="merge",
    )(y0, y1, xs, z, h0, h1, og, mg, x2, mods3, dexp, gs, gm, gf, wso, wmo, wo)


def _first_index_of_max(vals, row_iota, n_rows):
    mx = jnp.max(vals, axis=0, keepdims=True)
    idx = jnp.min(jnp.where(vals == mx, row_iota, n_rows), axis=0, keepdims=True)
    return mx, idx


def _router_kernel(u_ref, wt_ref, bias_ref, su_ref, idx_ref, pos_ref, wts_ref, cnt_ref, run_ref, *, n_experts, tr):
    i = pl.program_id(0)

    @pl.when(i == 0)
    def _():
        run_ref[...] = jnp.zeros_like(run_ref)

    ua, ub = _unpack_halves(u_ref[...])
    half = ua.shape[1]
    scores = _sigmoid(_dot_nt(wt_ref[:, :half], ua.astype(BF16)) + _dot_nt(wt_ref[:, half:], ub.astype(BF16)))
    biased = scores + bias_ref[...]
    gsz = n_experts // N_EXPERT_GROUPS
    gi = lax.broadcasted_iota(I32, (gsz, tr), 0).astype(F32)
    gscores = []
    for g in range(N_EXPERT_GROUPS):
        blk = biased[g * gsz:(g + 1) * gsz]
        m1, i1 = _first_index_of_max(blk, gi, gsz)
        m2 = jnp.max(jnp.where(gi == i1, NEG_INF, blk), axis=0, keepdims=True)
        gscores.append(m1 + m2)
    gs = jnp.concatenate(gscores, axis=0)
    g8 = lax.broadcasted_iota(I32, (N_EXPERT_GROUPS, tr), 0).astype(F32)
    gsel = jnp.zeros((N_EXPERT_GROUPS, tr), F32)
    for _ in range(TOPK_GROUPS):
        _, gidx = _first_index_of_max(gs, g8, N_EXPERT_GROUPS)
        hit = g8 == gidx
        gsel = jnp.where(hit, 1.0, gsel)
        gs = jnp.where(hit, NEG_INF, gs)
    cand = jnp.concatenate(
        [jnp.where(jnp.broadcast_to(gsel[g:g + 1], (gsz, tr)) > 0.5, biased[g * gsz:(g + 1) * gsz], NEG_INF)
         for g in range(N_EXPERT_GROUPS)], axis=0)
    ei = lax.broadcasted_iota(I32, (n_experts, tr), 0).astype(F32)
    sel = jnp.zeros((n_experts, tr), F32)
    idxs, ws = [], []
    for _ in range(TOP_K):
        _, eidx = _first_index_of_max(cand, ei, n_experts)
        hit = ei == eidx
        ws.append(jnp.sum(jnp.where(hit, scores, 0.0), axis=0, keepdims=True))
        idxs.append(eidx)
        sel = jnp.where(hit, 1.0, sel)
        cand = jnp.where(hit, NEG_INF, cand)
    wk = jnp.concatenate(ws, axis=0)
    wts_ref[...] = ROUTED_SCALE * wk / jnp.sum(wk, axis=0, keepdims=True)
    idx_ref[...] = jnp.concatenate(idxs, axis=0).astype(I32)
    selb = sel.astype(BF16)
    posmat = _dot(selb, su_ref[...]) + run_ref[:, 0:1]
    pos_ref[...] = jnp.concatenate(
        [jnp.sum(jnp.where(ei == idxs[k], posmat, 0.0), axis=0, keepdims=True) for k in range(TOP_K)],
        axis=0).astype(I32)
    run = run_ref[...] + _dot(selb, jnp.ones((tr, LANES), BF16))
    run_ref[...] = run
    cnt_ref[...] = run


def _router(u2p, router_wt, bias_col):
    t = u2p.shape[0]
    d = router_wt.shape[1]
    n_experts = router_wt.shape[0]
    tr = TOKEN_TILE
    su = jnp.asarray(np.triu(np.ones((tr, tr), np.float32), 1), BF16)
    kern = functools.partial(_router_kernel, n_experts=n_experts, tr=tr)
    col = lambda i: (0, i)
    const = lambda i: (0, 0)
    return pl.pallas_call(
        kern,
        out_shape=[jax.ShapeDtypeStruct((TOP_K, t), I32), jax.ShapeDtypeStruct((TOP_K, t), I32),
                   jax.ShapeDtypeStruct((TOP_K, t), F32), jax.ShapeDtypeStruct((n_experts, LANES), F32)],
        grid=(t // tr,),
        in_specs=[pl.BlockSpec((tr, d // 2), lambda i: (i, 0)), pl.BlockSpec((n_experts, d), const),
                  pl.BlockSpec((n_experts, 1), const), pl.BlockSpec((tr, tr), const)],
        out_specs=[pl.BlockSpec((TOP_K, tr), col), pl.BlockSpec((TOP_K, tr), col), pl.BlockSpec((TOP_K, tr), col),
                   pl.BlockSpec((n_experts, LANES), const)],
        scratch_shapes=[pltpu.VMEM((n_experts, LANES), F32)],
        compiler_params=pltpu.CompilerParams(dimension_semantics=("arbitrary",), vmem_limit_bytes=VMEM_LIMIT),
        name="router",
    )(u2p, router_wt, bias_col, su)


def _slots_kernel(idx_ref, pos_ref, pstart_ref, dest_ref, *, n_experts, tr):
    ei = lax.broadcasted_iota(I32, (n_experts, tr), 0).astype(F32)
    pstart = pstart_ref[...]
    idx = idx_ref[...].astype(F32)
    rows = [jnp.sum(jnp.where(ei == idx[k:k + 1], pstart, 0.0), axis=0, keepdims=True) for k in range(TOP_K)]
    dest_ref[...] = jnp.concatenate(rows, axis=0).astype(I32) + pos_ref[...]


def _slots(idx, pos, pstart_col):
    t = idx.shape[1]
    n_experts = pstart_col.shape[0]
    tr = TOKEN_TILE
    col = lambda i: (0, i)
    return pl.pallas_call(
        functools.partial(_slots_kernel, n_experts=n_experts, tr=tr),
        out_shape=jax.ShapeDtypeStruct((TOP_K, t), I32),
        grid=(t // tr,),
        in_specs=[pl.BlockSpec((TOP_K, tr), col), pl.BlockSpec((TOP_K, tr), col),
                  pl.BlockSpec((n_experts, 1), lambda i: (0, 0))],
        out_specs=pl.BlockSpec((TOP_K, tr), col),
        compiler_params=pltpu.CompilerParams(dimension_semantics=("arbitrary",), vmem_limit_bytes=VMEM_LIMIT),
        name="slots",
    )(idx, pos, pstart_col)


PAD_BITS = tuple(1 << b for b in reversed(range((MOE_BLOCK - 1).bit_length())))


def _dispatch_kernel(pstart_ref, cnt_ref, dest_ref, u_ref, xs_ref, zbuf, sem, psem, *, td, n_experts):
    i = pl.program_id(0)

    @pl.when(i == 0)
    def _():
        zbuf[...] = jnp.zeros_like(zbuf)

        def pads(e, wait):
            cnt = cnt_ref[e]
            n_pad = (MOE_BLOCK - (cnt & (MOE_BLOCK - 1))) & (MOE_BLOCK - 1)
            base = pstart_ref[e] + cnt
            n_single = n_pad & (SUBLANES - 1)

            def go(cp, cond):
                @pl.when(cond)
                def _():
                    if wait:
                        cp.wait()
                    else:
                        cp.start()

            for q in range(SUBLANES - 1):
                go(pltpu.make_async_copy(zbuf.at[pl.ds(0, 1)], xs_ref.at[pl.ds(base + q, 1)], psem), q < n_single)
            base = base + n_single
            for bit in PAD_BITS:
                if bit < SUBLANES:
                    continue
                has = (n_pad & bit) != 0
                go(pltpu.make_async_copy(zbuf.at[pl.ds(0, bit)], xs_ref.at[pl.ds(pl.multiple_of(base, SUBLANES), bit)],
                                         psem), has)
                base = base + jnp.where(has, bit, 0)

        def start_pads(e, carry):
            pads(e, False)
            return carry

        def wait_pads(e, carry):
            pads(e, True)
            return carry

        lax.fori_loop(0, n_experts, start_pads, 0)
        lax.fori_loop(0, n_experts, wait_pads, 0)

    def start(t, carry):
        for k in range(TOP_K):
            pltpu.make_async_copy(u_ref.at[pl.ds(t, 1)], xs_ref.at[pl.ds(dest_ref[k, t], 1)], sem).start(
                priority=k % 2)
        return carry

    def wait(t, carry):
        for k in range(TOP_K):
            pltpu.make_async_copy(u_ref.at[pl.ds(0, 1)], xs_ref.at[pl.ds(0, 1)], sem).wait()
        return carry

    lax.fori_loop(0, td, start, 0, unroll=4)
    lax.fori_loop(0, td, wait, 0, unroll=8)


def _dispatch(pad_start, counts, dest, u2p, n_slots):
    t, dh = u2p.shape
    td = DISPATCH_TILE
    n_experts = pad_start.shape[0]
    assert MOE_BLOCK & (MOE_BLOCK - 1) == 0
    kern = functools.partial(_dispatch_kernel, td=td, n_experts=n_experts)
    return pl.pallas_call(
        kern,
        out_shape=jax.ShapeDtypeStruct((n_slots, dh), U32),
        grid_spec=pltpu.PrefetchScalarGridSpec(
            num_scalar_prefetch=2, grid=(t // td,),
            in_specs=[pl.BlockSpec((TOP_K, td), lambda i, ps, cn: (0, i), memory_space=pltpu.SMEM),
                      pl.BlockSpec((td, dh), lambda i, ps, cn: (i, 0))],
            out_specs=pl.BlockSpec(memory_space=pl.ANY),
            scratch_shapes=[pltpu.VMEM((MOE_BLOCK // 2, dh), U32), pltpu.SemaphoreType.DMA,
                            pltpu.SemaphoreType.DMA]),
        compiler_params=pltpu.CompilerParams(dimension_semantics=("arbitrary",), vmem_limit_bytes=VMEM_LIMIT),
        name="dispatch",
    )(pad_start, counts, dest, u2p)


def _experts_kernel(bs_ref, nblk_ref, nu_ref, xs_ref, wg_ref, wu_ref, wd_ref, ys_ref, wgb, wub, wdb, xbuf, ybuf,
                    semx, semy, *, n_experts):
    e = pl.program_id(0)
    n_used = nu_ref[0]

    def x_copy(g):
        rows = pl.ds(pl.multiple_of(g * MOE_BLOCK, MOE_BLOCK), MOE_BLOCK)
        return pltpu.make_async_copy(xs_ref.at[rows], xbuf.at[g % X_BUFFERS], semx.at[g % X_BUFFERS])

    def y_copies(g, slot):
        rows = pl.ds(pl.multiple_of(g * MOE_BLOCK, MOE_BLOCK), MOE_BLOCK)
        return [pltpu.make_async_copy(ybuf.at[slot, :, pl.ds(j * LANES, LANES)], ys_ref.at[rows, j, :], semy.at[slot])
                for j in range(ROW_SUBLANES)]

    def y_start(g, slot):
        for cp in y_copies(g, slot):
            cp.start()

    def y_wait(g, slot):
        for cp in y_copies(g, slot):
            cp.wait()

    @pl.when(e == 0)
    def _():
        for g in range(X_BUFFERS - 1):
            @pl.when(g < n_used)
            def _():
                x_copy(jnp.int32(g)).start()

    g0 = bs_ref[e]
    nb = nblk_ref[e]

    @pl.when(nb > 0)
    def _():
        wgb[...] = wg_ref[0].astype(BF16)
        wub[...] = wu_ref[0].astype(BF16)
        wdb[...] = wd_ref[0].astype(BF16)

    def block(g, carry):
        slot = g % 2

        @pl.when(g + X_BUFFERS - 1 < n_used)
        def _():
            x_copy(g + X_BUFFERS - 1).start()

        x_copy(g).wait()

        @pl.when(g >= 2)
        def _():
            y_wait(g - 2, slot)

        xa, xb = _unpack_halves(xbuf[g % X_BUFFERS])
        half = xa.shape[1]
        xa = xa.astype(BF16)
        xb = xb.astype(BF16)
        hg = _dot(xa, wgb[:half]) + _dot(xb, wgb[half:])
        hu = _dot(xa, wub[:half]) + _dot(xb, wub[half:])
        hb = (hg * _sigmoid(hg) * hu).astype(BF16)
        ybuf[slot] = _pack_halves(_dot(hb, wdb[...]))
        y_start(g, slot)
        return carry

    lax.fori_loop(g0, g0 + nb, block, 0)

    @pl.when(e == n_experts - 1)
    def _():
        @pl.when(n_used >= 2)
        def _():
            y_wait(n_used - 2, n_used % 2)

        @pl.when(n_used >= 1)
        def _():
            y_wait(n_used - 1, (n_used - 1) % 2)


def _experts(block_start, n_blocks, n_used, xsorted, wg, wu, wd):
    n_slots, dh = xsorted.shape
    n_experts, d, de = wg.shape
    wmap = lambda e, bs, nb, nu: (e, 0, 0)
    assert dh == ROW_SUBLANES * LANES
    return pl.pallas_call(
        functools.partial(_experts_kernel, n_experts=n_experts),
        out_shape=jax.ShapeDtypeStruct((n_slots, ROW_SUBLANES, LANES), U32),
        grid_spec=pltpu.PrefetchScalarGridSpec(
            num_scalar_prefetch=3, grid=(n_experts,),
            in_specs=[pl.BlockSpec(memory_space=pl.ANY), pl.BlockSpec((1, d, de), wmap),
                      pl.BlockSpec((1, d, de), wmap), pl.BlockSpec((1, de, d), wmap)],
            out_specs=pl.BlockSpec(memory_space=pl.ANY),
            scratch_shapes=[pltpu.VMEM((d, de), BF16), pltpu.VMEM((d, de), BF16), pltpu.VMEM((de, d), BF16),
                            pltpu.VMEM((X_BUFFERS, MOE_BLOCK, dh), U32),
                            pltpu.VMEM((2, MOE_BLOCK, dh), U32),
                            pltpu.SemaphoreType.DMA((X_BUFFERS,)), pltpu.SemaphoreType.DMA((2,))]),
        compiler_params=pltpu.CompilerParams(dimension_semantics=("arbitrary",), vmem_limit_bytes=VMEM_LIMIT),
        name="experts",
    )(block_start, n_blocks, n_used, xsorted, wg, wu, wd)


def _combine_kernel(dest_ref, dnext_ref, ys_ref, wt_ref, x1_ref, u_ref, mod_ref, wsg_ref, wsu_ref, wsd_ref,
                    gfin_ref, o_ref, gbuf, accbuf, sem, *, tc, n_steps):
    i = pl.program_id(0)
    slot = i % 2

    def issue(dref, s):
        def body(t, carry):
            for k in range(TOP_K):
                pltpu.make_async_copy(ys_ref.at[dref[k, t]], gbuf.at[s, k, t], sem.at[s]).start(priority=k % 2)
            return carry
        lax.fori_loop(0, tc, body, 0, unroll=4)

    @pl.when(i == 0)
    def _():
        issue(dest_ref, 0)

    @pl.when(i + 1 < n_steps)
    def _():
        issue(dnext_ref, 1 - slot)

    ua, ub = _unpack_halves(u_ref[...])
    half = ua.shape[1]
    ua = ua.astype(BF16)
    ub = ub.astype(BF16)
    hg = _dot(ua, wsg_ref[:half]) + _dot(ub, wsg_ref[half:])
    hu = _dot(ua, wsu_ref[:half]) + _dot(ub, wsu_ref[half:])
    shared = _dot((hg * _sigmoid(hg) * hu).astype(BF16), wsd_ref[...])
    acc_l = shared[:, :half]
    acc_r = shared[:, half:]

    def wait(t, carry):
        for k in range(TOP_K):
            pltpu.make_async_copy(ys_ref.at[0], gbuf.at[slot, 0, 0], sem.at[slot]).wait()
        return carry

    lax.fori_loop(0, tc, wait, 0, unroll=8)
    sum_l = jnp.zeros((tc, ROW_SUBLANES, LANES), F32)
    sum_r = jnp.zeros((tc, ROW_SUBLANES, LANES), F32)
    for k in range(TOP_K):
        ga, gb = _unpack_halves(gbuf[slot, k])
        wk = jnp.broadcast_to(wt_ref[:, k:k + 1, :], (tc, ROW_SUBLANES, LANES))
        sum_l = sum_l + ga * wk
        sum_r = sum_r + gb * wk
    accbuf[0] = sum_l
    accbuf[1] = sum_r
    acc_l = jnp.concatenate([acc_l[:, j * LANES:(j + 1) * LANES] + accbuf[0, :, j, :] for j in range(ROW_SUBLANES)],
                            axis=1)
    acc_r = jnp.concatenate([acc_r[:, j * LANES:(j + 1) * LANES] + accbuf[1, :, j, :] for j in range(ROW_SUBLANES)],
                            axis=1)
    m = mod_ref[0]
    xo_l = x1_ref[:, :half] + m[5:6, :half] * acc_l
    xo_r = x1_ref[:, half:] + m[5:6, half:] * acc_r
    ms = (jnp.sum(xo_l * xo_l, axis=-1, keepdims=True) + jnp.sum(xo_r * xo_r, axis=-1, keepdims=True)) / (2 * half)
    inv = lax.rsqrt(ms + EPS)
    o_ref[:, :half] = xo_l * inv * gfin_ref[:, :half]
    o_ref[:, half:] = xo_r * inv * gfin_ref[:, half:]


def _combine(dest, ysorted, wts_t, x1, u2p, mods3, wsg, wsu, wsd, gfin, *, seq):
    t, d = x1.shape
    dh = d // 2
    tc = DISPATCH_TILE
    tiles_per_batch = seq // tc
    n_steps = t // tc
    kern = functools.partial(_combine_kernel, tc=tc, n_steps=n_steps)
    row = lambda i: (i, 0)
    const = lambda i: (0, 0)
    return pl.pallas_call(
        kern,
        out_shape=jax.ShapeDtypeStruct((t, d), F32),
        grid=(n_steps,),
        in_specs=[pl.BlockSpec((TOP_K, tc), lambda i: (0, i), memory_space=pltpu.SMEM),
                  pl.BlockSpec((TOP_K, tc), lambda i: (0, jnp.minimum(i + 1, n_steps - 1)), memory_space=pltpu.SMEM),
                  pl.BlockSpec(memory_space=pl.ANY), pl.BlockSpec((tc, TOP_K, LANES), lambda i: (i, 0, 0)),
                  pl.BlockSpec((tc, d), row), pl.BlockSpec((tc, dh), row),
                  pl.BlockSpec((1, 6, d), lambda i: (i // tiles_per_batch, 0, 0)),
                  pl.BlockSpec(wsg.shape, const), pl.BlockSpec(wsu.shape, const), pl.BlockSpec(wsd.shape, const),
                  pl.BlockSpec((1, d), const)],
        out_specs=pl.BlockSpec((tc, d), row),
        scratch_shapes=[pltpu.VMEM((2, TOP_K, tc, ROW_SUBLANES, LANES), U32),
                        pltpu.VMEM((2, tc, ROW_SUBLANES, LANES), F32), pltpu.SemaphoreType.DMA((2,))],
        compiler_params=pltpu.CompilerParams(dimension_semantics=("arbitrary",), vmem_limit_bytes=VMEM_LIMIT),
        name="combine",
    )(dest, dest, ysorted, wts_t, x1, u2p, mods3, wsg, wsu, wsd, gfin)


def kernel(x, c, ctx, c_ctx, ada_w, ada_b, norm_mix_g, norm_ffn_g, w_in, conv_xbc_w, conv_xbc_b, ssd_dt_bias, ssd_a_log, ssd_d, ssd_norm_g, conv_qk_w, conv_qk_b, mlstm_i_bias, mlstm_f_bias, mlstm_norm_g, w_ssd_out, w_mlstm_out, w_out, router_w, router_bias, moe_w_gate, moe_w_up, moe_w_down, shared_w_gate, shared_w_up, shared_w_down, norm_final_g):
    batch, seq, d = x.shape
    ctx_len = ctx.shape[1]
    depth = ada_w.shape[0]
    assert depth == 1, "only the single-layer configuration is implemented"
    assert seq % CHUNK == 0 and ctx_len % CHUNK == 0 and seq % GRID_W == 0
    l = 0
    n_sh = d // SSD_HEAD_DIM
    n_mh = d // MLSTM_DV
    dk = MLSTM_DV // 2
    d_xbc = d + 2 * SSD_GROUPS * SSD_STATE
    d_qk = 2 * n_mh * dk
    sizes = (d, d_xbc, 2 * n_sh, d_qk, d, 4 * n_mh, d, 2 * d)
    offs = np.concatenate([[0], np.cumsum(sizes)])
    assert offs[-1] == w_in.shape[2] and 2 * n_sh + 2 * n_mh <= LANES

    cond = jnp.concatenate([c, c_ctx[None], jnp.zeros((8 - (batch + 1) % 8, d), F32)], axis=0)
    mods = _adaln(cond, ada_w[l], ada_b[l])
    mods3 = mods.reshape(mods.shape[0], 6, d)

    w = w_in[l]
    seg = lambda k: w[:, offs[k]:offs[k + 1]]
    wbig = jnp.concatenate([seg(0), seg(1), seg(3), seg(4), seg(6), seg(7)], axis=1).astype(BF16)
    w_dt = seg(2)
    w_g = seg(5).reshape(d, 2, 2, n_mh)
    w_i = w_g[:, :, 0].reshape(d, 2 * n_mh)
    w_f = w_g[:, :, 1].reshape(d, 2 * n_mh)
    pad = jnp.zeros((d, LANES - 2 * n_sh - 2 * n_mh), F32)
    wsm = jnp.concatenate([w_dt, w_f, pad, w_dt, w_i, pad], axis=1)
    wsh = wsm.astype(BF16)
    wsl = (wsm - wsh.astype(F32)).astype(BF16)
    padb = jnp.zeros((LANES - 2 * n_sh - 2 * n_mh,), F32)
    dtb = ssd_dt_bias[l].reshape(-1).astype(F32)
    smb = jnp.concatenate([dtb, mlstm_f_bias[l].reshape(-1).astype(F32), padb,
                           dtb, mlstm_i_bias[l].reshape(-1).astype(F32), padb]).reshape(1, 2 * LANES)
    aneg = jnp.concatenate([-jnp.exp(ssd_a_log[l].astype(F32)).reshape(-1),
                            jnp.zeros((LANES - 2 * n_sh,), F32)]).reshape(1, LANES)

    x2 = x.reshape(batch * seq, d)
    ctx2 = ctx.reshape(batch * ctx_len, d)
    z, xs, bc, qk, v, og, mg, small, smallt = _inproj(
        x2, ctx2, mods3, norm_mix_g[l].reshape(1, d), wbig, wsh, wsl,
        conv_xbc_w[l], conv_xbc_b[l].reshape(1, d_xbc), conv_qk_w[l], conv_qk_b[l].reshape(1, d_qk), smb, aneg,
        batch=batch, seq=seq, ctx_len=ctx_len, dk=dk)

    ncc = ctx_len // CHUNK
    ncl = seq // CHUNK
    y0, y1, h0, h1 = _scans(xs, bc, qk, v, small, smallt, batch=batch, n_ctx_chunks=ncc, n_lat_chunks=ncl, dk=dk)

    dexp = jnp.repeat(ssd_d[l].astype(F32), SSD_HEAD_DIM).reshape(1, d)
    x1, u2 = _merge(y0, y1, xs, z, h0, h1, og, mg, x2, mods3, dexp, ssd_norm_g[l].reshape(1, d),
                    mlstm_norm_g[l].reshape(1, d), norm_ffn_g[l].reshape(1, d),
                    w_ssd_out[l].astype(BF16), w_mlstm_out[l].astype(BF16), w_out[l].astype(BF16),
                    batch=batch, seq=seq, n_ctx_tok=batch * ctx_len)

    n_experts = router_w.shape[2]
    idx, pos, wts, cnt = _router(u2, router_w[l].T.astype(BF16), router_bias[l].astype(F32).reshape(n_experts, 1))
    counts = cnt[:, 0].astype(I32)
    padded = (counts + MOE_BLOCK - 1) // MOE_BLOCK * MOE_BLOCK
    pad_end = jnp.cumsum(padded)
    pad_start = (pad_end - padded).astype(I32)
    t = batch * seq
    nb = t * TOP_K // MOE_BLOCK + n_experts
    n_used = (pad_end[-1] // MOE_BLOCK).astype(I32).reshape(1)

    dest = _slots(idx, pos, pad_start.astype(F32).reshape(n_experts, 1))
    xsorted = _dispatch(pad_start, counts, dest, u2, nb * MOE_BLOCK)
    ysorted = _experts(pad_start // MOE_BLOCK, (padded // MOE_BLOCK).astype(I32), n_used, xsorted,
                       moe_w_gate[l], moe_w_up[l], moe_w_down[l])
    wts_lanes = jnp.broadcast_to(wts.T[:, :, None], (t, TOP_K, LANES))
    out = _combine(dest, ysorted, wts_lanes, x1, u2, mods3,
                   shared_w_gate[l].astype(BF16), shared_w_up[l].astype(BF16), shared_w_down[l].astype(BF16),
                   norm_final_g.reshape(1, d), seq=seq)
    return out.reshape(batch, seq, d)
```

```python
import functools

import numpy as np
import jax
import jax.numpy as jnp
from jax import lax
from jax.experimental import pallas as pl
from jax.experimental.pallas import tpu as pltpu

F32 = jnp.float32
BF16 = jnp.bfloat16
I32 = jnp.int32
U32 = jnp.uint32

EPS = 1e-6
CHUNK = 128
CONV_K = 5
GRID_W = 64
SSD_HEAD_DIM = 64
SSD_STATE = 128
SSD_GROUPS = 2
MLSTM_DV = 128
N_EXPERT_GROUPS = 8
TOPK_GROUPS = 4
TOP_K = 8
ROUTED_SCALE = 2.5

LANES = 128
SUBLANES = 8
PACK_ROWS = 16
TOKEN_TILE = 512
COL_CHUNK = 512
MOE_BLOCK = 256
DISPATCH_TILE = 256
X_BUFFERS = 4
ROW_SUBLANES = 4
VMEM_LIMIT = 56 * 1024 * 1024
NEG_INF = float("-inf")


def _dot(a, b):
    return jnp.dot(a, b, preferred_element_type=F32)


def _dot_nt(a, b):
    return lax.dot_general(a, b, (((1,), (1,)), ((), ())), preferred_element_type=F32)


def _dot_tn(a, b):
    return lax.dot_general(a, b, (((0,), (0,)), ((), ())), preferred_element_type=F32)


def _spread(parts, e):
    res = _dot(jnp.concatenate(parts, axis=0).astype(BF16), e)
    out, r0 = [], 0
    for p in parts:
        out.append(res[r0:r0 + p.shape[0]])
        r0 += p.shape[0]
    return out


def _rows16(row):
    r8 = jnp.broadcast_to(row, (SUBLANES, row.shape[1]))
    return jnp.concatenate([r8, r8], axis=0)


def _sigmoid(v):
    return 1.0 / (1.0 + jnp.exp(-v))


def _pack_halves(v):
    n = v.shape[1] // 2
    hi = lax.bitcast_convert_type(v[:, :n].astype(BF16).astype(F32), U32)
    lo = lax.bitcast_convert_type(v[:, n:].astype(BF16).astype(F32), U32)
    return hi | (lo >> 16)


def _unpack_halves(p):
    left = lax.bitcast_convert_type(p & jnp.uint32(0xFFFF0000), F32)
    right = lax.bitcast_convert_type(p << 16, F32)
    return left, right


def _softplus(v):
    return jnp.maximum(v, 0.0) + jnp.log1p(jnp.exp(-jnp.abs(v)))


def _adaln_kernel(c_ref, w_ref, b_ref, o_ref):
    c = c_ref[...]
    s = c * _sigmoid(c)
    w = w_ref[...]
    s_hi = s.astype(BF16)
    s_lo = (s - s_hi.astype(F32)).astype(BF16)
    w_hi = w.astype(BF16)
    w_lo = (w - w_hi.astype(F32)).astype(BF16)
    o_ref[...] = _dot(s_hi, w_hi) + _dot(s_lo, w_hi) + _dot(s_hi, w_lo) + b_ref[...]


def _adaln(cond, w, b):
    rows, d = cond.shape
    n = w.shape[1]
    tn = 1536 if n % 1536 == 0 else n
    return pl.pallas_call(
        _adaln_kernel,
        out_shape=jax.ShapeDtypeStruct((rows, n), F32),
        grid=(n // tn,),
        in_specs=[pl.BlockSpec((rows, d), lambda j: (0, 0)),
                  pl.BlockSpec((d, tn), lambda j: (0, j)),
                  pl.BlockSpec((1, tn), lambda j: (0, j))],
        out_specs=pl.BlockSpec((rows, tn), lambda j: (0, j)),
        compiler_params=pltpu.CompilerParams(dimension_semantics=("arbitrary",), vmem_limit_bytes=VMEM_LIMIT),
        name="adaln",
    )(cond, w, b.reshape(1, n))


CONV_SHIFTS = tuple(j - CONV_K // 2 for j in range(CONV_K) if j != CONV_K // 2)


def _conv_masks(tm, seg_len):
    pos = np.arange(tm) % seg_len
    m = np.zeros((tm, 8), np.float32)
    for i, s in enumerate(CONV_SHIFTS):
        m[:, i] = ((pos + s >= 0) & (pos + s < seg_len)).astype(np.float32)
    return m


def _conv_silu(acc, w5, bias, vm, tm):
    out = acc * w5[CONV_K // 2:CONV_K // 2 + 1] + bias
    for i, s in enumerate(CONV_SHIFTS):
        shifted = pltpu.roll(acc, (-s) % tm, axis=0)
        j = s + CONV_K // 2
        out = out + (shifted * vm[:, i:i + 1]) * w5[j:j + 1]
    return out * _sigmoid(out)


def _inproj_kernel(x_ref, ctx_ref, mod_ref, g_ref, wbig_ref, wsh_ref, wsl_ref, cwx_ref, cbx_ref, cwq_ref, cbq_ref,
                   smb_ref, aneg_ref, tril_ref, triu_ref, cmask_ref,
                   z_ref, xs_ref, bc_ref, qk_ref, v_ref, og_ref, mg_ref, small_ref, smallt_ref,
                   *, n_ctx_tiles, tm, d_model, dk):
    i = pl.program_id(0)
    is_ctx = i < n_ctx_tiles
    xt = jnp.where(is_ctx, ctx_ref[...], x_ref[...])
    m = mod_ref[0]
    ms = jnp.mean(xt * xt, axis=-1, keepdims=True)
    u = xt * lax.rsqrt(ms + EPS) * g_ref[...] * (1.0 + m[1:2]) + m[0:1]
    u_hi = u.astype(BF16)
    u_lo = (u - u_hi.astype(F32)).astype(BF16)
    vm = jnp.where(is_ctx, cmask_ref[1], cmask_ref[0])

    d = d_model
    d_bc = 2 * SSD_GROUPS * SSD_STATE
    pieces = [(z_ref, d, "plain", None), (xs_ref, d, "convx", 0), (bc_ref, d_bc, "convx", d),
              (qk_ref, d, "convq", 0), (v_ref, d, "plain", None), (og_ref, d, "sigmoid", None),
              (mg_ref, 2 * d, "sigmoid", None)]
    col = 0
    for ref, width, kind, coff in pieces:
        for c0 in range(0, width, COL_CHUNK):
            acc = _dot(u_hi, wbig_ref[:, col + c0:col + c0 + COL_CHUNK])
            if kind == "convx":
                cs = coff + c0
                acc = _conv_silu(acc, cwx_ref[:, cs:cs + COL_CHUNK], cbx_ref[:, cs:cs + COL_CHUNK], vm, tm)
            elif kind == "convq":
                acc = _conv_silu(acc, cwq_ref[:, c0:c0 + COL_CHUNK], cbq_ref[:, c0:c0 + COL_CHUNK], vm, tm)
                if c0 < width // 2:
                    acc = acc * (dk ** -0.5)
            elif kind == "sigmoid":
                acc = _sigmoid(acc)
            ref[:, c0:c0 + COL_CHUNK] = acc.astype(ref.dtype)
        col += width

    wsh = wsh_ref[...]
    raw = _dot(u_hi, wsh) + _dot(u_lo, wsh) + _dot(u_hi, wsl_ref[...]) + smb_ref[...]
    p1 = raw[:, :LANES]
    p2 = raw[:, LANES:]
    lane = lax.broadcasted_iota(I32, (tm, LANES), 1)
    n_dt = 2 * (d_model // SSD_HEAD_DIM)
    n_g = 2 * (d_model // MLSTM_DV)
    is_dt = lane < n_dt
    is_gate = jnp.logical_and(lane >= n_dt, lane < n_dt + n_g)
    dt = _softplus(p2)
    pa = jnp.where(is_dt, dt * aneg_ref[...], jnp.where(is_gate, -_softplus(-p1), 0.0))
    pb = jnp.where(is_dt, dt, jnp.where(is_gate, p2, 0.0))
    lane_c = lax.broadcasted_iota(I32, (CHUNK, LANES), 1)
    is_dt_c = lane_c < n_dt
    rev = jnp.logical_or(jnp.logical_and(lane_c >= n_dt // 2, lane_c < n_dt),
                         jnp.logical_and(lane_c >= n_dt + n_g // 2, lane_c < n_dt + n_g))
    tril = tril_ref[...]
    triu = triu_ref[...]
    tq = lax.broadcasted_iota(I32, (CHUNK, LANES), 0)
    for c in range(tm // CHUNK):
        r0 = c * CHUNK
        a_c = pa[r0:r0 + CHUNK]
        hi = a_c.astype(BF16)
        r1 = a_c - hi.astype(F32)
        mid = r1.astype(BF16)
        lo = (r1 - mid.astype(F32)).astype(BF16)
        cs_f = _dot(tril, hi) + _dot(tril, mid) + _dot(tril, lo)
        cs_b = _dot(triu, hi) + _dot(triu, mid) + _dot(triu, lo)
        plane_a = jnp.where(rev, cs_b, cs_f)
        plane_b = jnp.where(is_dt_c, pb[r0:r0 + CHUNK], pb[r0:r0 + CHUNK] - plane_a)
        yf = plane_b
        yb = plane_b
        s = 1
        while s < CHUNK:
            sh = pltpu.roll(yf, s, axis=0)
            yf = jnp.maximum(yf, jnp.where(tq >= s, sh, NEG_INF))
            sh = pltpu.roll(yb, CHUNK - s, axis=0)
            yb = jnp.maximum(yb, jnp.where(tq + s < CHUNK, sh, NEG_INF))
            s *= 2
        plane_c = jnp.where(rev, yb, yf)
        small_ref[r0:r0 + CHUNK, 0:LANES] = plane_a
        small_ref[r0:r0 + CHUNK, LANES:2 * LANES] = plane_b
        small_ref[r0:r0 + CHUNK, 2 * LANES:3 * LANES] = plane_c
        smallt_ref[c, 0] = plane_a.T
        smallt_ref[c, 1] = plane_b.T


def _inproj(x2, ctx2, mods3, g, wbig, wsh, wsl, cwx, cbx, cwq, cbq, smb, aneg, *, batch, seq, ctx_len, dk):
    d = x2.shape[1]
    tm = TOKEN_TILE
    n_ctx_tok = batch * ctx_len
    assert n_ctx_tok % tm == 0 and seq % tm == 0 and tm % ctx_len == 0 and tm % GRID_W == 0
    n_ctx_tiles = n_ctx_tok // tm
    tiles_per_batch = seq // tm
    n_tiles = n_ctx_tiles + batch * tiles_per_batch
    t_all = n_tiles * tm
    n_big = wbig.shape[1]
    tril = jnp.asarray(np.tril(np.ones((CHUNK, CHUNK), np.float32)), BF16)
    triu = jnp.asarray(np.triu(np.ones((CHUNK, CHUNK), np.float32)), BF16)

    def x_map(i):
        return (jnp.maximum(i - n_ctx_tiles, 0), 0)

    def ctx_map(i):
        return (jnp.minimum(i, n_ctx_tiles - 1), 0)

    def mod_map(i):
        return (jnp.where(i < n_ctx_tiles, batch, jnp.maximum(i - n_ctx_tiles, 0) // tiles_per_batch), 0, 0)

    const = lambda i: (0, 0)
    row = lambda i: (i, 0)
    cmask = jnp.asarray(np.stack([_conv_masks(tm, GRID_W), _conv_masks(tm, ctx_len)]))
    kern = functools.partial(_inproj_kernel, n_ctx_tiles=n_ctx_tiles, tm=tm, d_model=d, dk=dk)
    d_bc = 2 * SSD_GROUPS * SSD_STATE
    outs = [jax.ShapeDtypeStruct((t_all, d), BF16), jax.ShapeDtypeStruct((t_all, d), BF16),
            jax.ShapeDtypeStruct((t_all, d_bc), BF16), jax.ShapeDtypeStruct((t_all, d), BF16),
            jax.ShapeDtypeStruct((t_all, d), BF16), jax.ShapeDtypeStruct((t_all, d), BF16),
            jax.ShapeDtypeStruct((t_all, 2 * d), BF16), jax.ShapeDtypeStruct((t_all, 3 * LANES), F32),
            jax.ShapeDtypeStruct((t_all // CHUNK, 2, LANES, CHUNK), F32)]
    out_specs = [pl.BlockSpec((tm, d), row), pl.BlockSpec((tm, d), row), pl.BlockSpec((tm, d_bc), row),
                 pl.BlockSpec((tm, d), row), pl.BlockSpec((tm, d), row), pl.BlockSpec((tm, d), row),
                 pl.BlockSpec((tm, 2 * d), row), pl.BlockSpec((tm, 3 * LANES), row),
                 pl.BlockSpec((tm // CHUNK, 2, LANES, CHUNK), lambda i: (i, 0, 0, 0))]
    in_specs = [pl.BlockSpec((tm, d), x_map), pl.BlockSpec((tm, d), ctx_map),
                pl.BlockSpec((1, 6, d), mod_map), pl.BlockSpec((1, d), const),
                pl.BlockSpec((d, n_big), const, pipeline_mode=pl.Buffered(1)),
                pl.BlockSpec((d, 2 * LANES), const), pl.BlockSpec((d, 2 * LANES), const),
                pl.BlockSpec(cwx.shape, const), pl.BlockSpec(cbx.shape, const),
                pl.BlockSpec(cwq.shape, const), pl.BlockSpec(cbq.shape, const),
                pl.BlockSpec((1, 2 * LANES), const), pl.BlockSpec((1, LANES), const),
                pl.BlockSpec((CHUNK, CHUNK), const), pl.BlockSpec((CHUNK, CHUNK), const),
                pl.BlockSpec((2, tm, 8), lambda i: (0, 0, 0))]
    return pl.pallas_call(
        kern, out_shape=outs, grid=(n_tiles,), in_specs=in_specs, out_specs=out_specs,
        compiler_params=pltpu.CompilerParams(dimension_semantics=("arbitrary",), vmem_limit_bytes=VMEM_LIMIT),
        name="inproj",
    )(x2, ctx2, mods3, g, wbig, wsh, wsl, cwx, cbx, cwq, cbq, smb, aneg, tril, triu, cmask)


def _chunk_block_map(direction, batch, n_ctx_chunks, n_lat_chunks):
    def idx(b, s):
        if direction == 0:
            c_ctx = s
            c_lat = s - n_ctx_chunks
        else:
            c_ctx = n_ctx_chunks - 1 - s
            c_lat = n_lat_chunks - 1 - (s - n_ctx_chunks)
        return jnp.where(s < n_ctx_chunks, b * n_ctx_chunks + c_ctx, batch * n_ctx_chunks + b * n_lat_chunks + c_lat)
    return idx


def _ssd_kernel(xs_ref, bc_ref, small_ref, smallt_ref, e16_ref, y_ref, st_ref, *, direction, n_heads):
    s = pl.program_id(1)

    @pl.when(s == 0)
    def _():
        st_ref[...] = jnp.zeros_like(st_ref)

    hpg = n_heads // SSD_GROUPS
    lane0 = direction * n_heads
    last = CHUNK - 1 if direction == 0 else 0
    lane = lax.broadcasted_iota(I32, (CHUNK, LANES), 1)
    lm = jnp.logical_and(lane >= lane0, lane < lane0 + n_heads)
    plane_a = jnp.where(lm, small_ref[:, 0:LANES], 0.0)
    plane_b = jnp.where(lm, small_ref[:, LANES:2 * LANES], 0.0)
    cum_t = smallt_ref[0, 0]
    tot = plane_a[last:last + 1]
    lane16 = lax.broadcasted_iota(I32, (PACK_ROWS, LANES), 1)
    lm16 = jnp.logical_and(lane16 >= lane0, lane16 < lane0 + n_heads)
    dec_rows = jnp.where(lm16, jnp.exp(_rows16(tot)), 0.0)
    dtx, ecx, wx, decx = _spread(
        [plane_b, jnp.where(lm, jnp.exp(plane_a), 0.0), plane_b * jnp.exp(tot - plane_a), dec_rows], e16_ref[...])
    decx = decx[0:1]

    xf = xs_ref[...].astype(F32)
    xdt = xf * dtx
    xw = (xf * wx).astype(BF16)
    plane64 = lax.broadcasted_iota(I32, (CHUNK, 2 * SSD_HEAD_DIM), 1) < SSD_HEAD_DIM
    iq = lax.broadcasted_iota(I32, (CHUNK, CHUNK), 0)
    ik = lax.broadcasted_iota(I32, (CHUNK, CHUNK), 1)
    mask = (iq >= ik) if direction == 0 else (iq <= ik)
    gw = hpg * SSD_HEAD_DIM
    for g in range(SSD_GROUPS):
        bg = bc_ref[:, g * SSD_STATE:(g + 1) * SSD_STATE]
        cg = bc_ref[:, (SSD_GROUPS + g) * SSD_STATE:(SSD_GROUPS + g + 1) * SSD_STATE]
        cb = _dot_nt(cg, bg)
        sg = st_ref[:, g * gw:(g + 1) * gw]
        y_inter = _dot(cg, sg.astype(BF16)) * ecx[:, g * gw:(g + 1) * gw]
        for pair in range(hpg // 2):
            h0 = g * hpg + 2 * pair
            c0 = h0 * SSD_HEAD_DIM
            xpair = xdt[:, c0:c0 + 2 * SSD_HEAD_DIM]
            acc = y_inter[:, 2 * pair * SSD_HEAD_DIM:(2 * pair + 2) * SSD_HEAD_DIM]
            for par in range(2):
                h = h0 + par
                cum_q = jnp.broadcast_to(plane_a[:, lane0 + h:lane0 + h + 1], (CHUNK, CHUNK))
                seg = cum_q - cum_t[lane0 + h:lane0 + h + 1, :]
                m_h = (cb * jnp.exp(jnp.where(mask, seg, NEG_INF))).astype(BF16)
                keep = plane64 if par == 0 else jnp.logical_not(plane64)
                acc = acc + _dot(m_h, jnp.where(keep, xpair, 0.0).astype(BF16))
            y_ref[:, c0:c0 + 2 * SSD_HEAD_DIM] = acc.astype(y_ref.dtype)
        st_ref[:, g * gw:(g + 1) * gw] = decx[:, g * gw:(g + 1) * gw] * sg + _dot_tn(bg, xw[:, g * gw:(g + 1) * gw])


def _selectors(direction, d):
    n_sh = d // SSD_HEAD_DIM
    n_mh = d // MLSTM_DV
    e16 = np.zeros((LANES, d), np.float32)
    em = np.zeros((LANES, d), np.float32)
    for h in range(n_sh):
        e16[direction * n_sh + h, h * SSD_HEAD_DIM:(h + 1) * SSD_HEAD_DIM] = 1.0
    for h in range(n_mh):
        em[2 * n_sh + direction * n_mh + h, h * MLSTM_DV:(h + 1) * MLSTM_DV] = 1.0
    return jnp.asarray(e16, BF16), jnp.asarray(em, BF16)


def _mlstm_kernel(qk_ref, v_ref, small_ref, smallt_ref, em_ref, h_ref, cn_ref, m_ref, *, direction, n_heads, dk):
    s = pl.program_id(1)

    @pl.when(s == 0)
    def _():
        cn_ref[...] = jnp.zeros_like(cn_ref)
        m_ref[...] = jnp.zeros_like(m_ref)

    n_dt = 2 * (n_heads * MLSTM_DV // SSD_HEAD_DIM)
    lane0 = n_dt + direction * n_heads
    last = CHUNK - 1 if direction == 0 else 0
    lane = lax.broadcasted_iota(I32, (CHUNK, LANES), 1)
    lm = jnp.logical_and(lane >= lane0, lane < lane0 + n_heads)
    b_q = jnp.where(lm, small_ref[:, 0:LANES], 0.0)
    r_k = jnp.where(lm, small_ref[:, LANES:2 * LANES], 0.0)
    cmr = jnp.where(lm, small_ref[:, 2 * LANES:3 * LANES], 0.0)
    r_t = smallt_ref[0, 1]
    m_all = m_ref[...]
    m_row = m_all[0:1]
    mm = jnp.maximum(cmr, m_row)
    w_state = jnp.exp(m_row - mm)
    e_mq = jnp.exp(-(b_q + mm))
    m_base8 = jnp.maximum(m_all, _rows16(cmr[last:last + 1]))
    m_base = m_base8[0:1]
    w_k = jnp.where(lm, jnp.exp(r_k - m_base), 0.0)
    lane16 = lax.broadcasted_iota(I32, (PACK_ROWS, LANES), 1)
    lm16 = jnp.logical_and(lane16 >= lane0, lane16 < lane0 + n_heads)
    dec_rows = jnp.where(lm16, jnp.exp(m_all - m_base8), 0.0)
    wsx, emqx, wkx, decx = _spread(
        [jnp.where(lm, w_state, 0.0), jnp.where(lm, e_mq, 0.0), w_k, dec_rows], em_ref[...])
    decx = decx[0:1]
    m_ref[...] = jnp.where(lm16, _rows16(b_q[last:last + 1]) + m_base8, 0.0)

    iq = lax.broadcasted_iota(I32, (CHUNK, CHUNK), 0)
    ik = lax.broadcasted_iota(I32, (CHUNK, CHUNK), 1)
    mask = (iq >= ik) if direction == 0 else (iq <= ik)
    ones = jnp.ones((CHUNK, MLSTM_DV), BF16)
    d_qk = n_heads * dk
    for h in range(n_heads):
        qh = qk_ref[:, h * dk:(h + 1) * dk]
        kh = qk_ref[:, d_qk + h * dk:d_qk + (h + 1) * dk]
        vh = v_ref[:, h * MLSTM_DV:(h + 1) * MLSTM_DV]
        hs = slice(h * MLSTM_DV, (h + 1) * MLSTM_DV)
        mm_q = jnp.broadcast_to(mm[:, lane0 + h:lane0 + h + 1], (CHUNK, CHUNK))
        dmat = jnp.exp(jnp.where(mask, r_t[lane0 + h:lane0 + h + 1, :] - mm_q, NEG_INF))
        smat = (_dot_nt(qh, kh) * dmat).astype(BF16)
        cn = cn_ref[h]
        wsh = wsx[:, hs]
        tot = _dot(smat, jnp.concatenate([vh, ones], axis=1)) + jnp.concatenate([wsh, wsh], axis=1) * _dot(qh, cn.astype(BF16))
        num = tot[:, :MLSTM_DV]
        den = tot[:, MLSTM_DV:]
        h_ref[:, hs] = (num / jnp.maximum(jnp.abs(den), emqx[:, hs])).astype(h_ref.dtype)
        wkh = wkx[:, hs]
        rhs = jnp.concatenate([(vh.astype(F32) * wkh).astype(BF16), wkh.astype(BF16)], axis=1)
        dech = decx[:, hs]
        cn_ref[h] = jnp.concatenate([dech, dech], axis=1) * cn + _dot_tn(kh, rhs)


def _scans_kernel(*refs, n_sh, n_mh, dk):
    (xs0, bc0, qk0, v0, sm0, smt0, xs1, bc1, qk1, v1, sm1, smt1, e16_0, em_0, e16_1, em_1,
     y0, y1, h0, h1, st0, st1, cn0, cn1, m0, m1) = refs
    _ssd_kernel(xs0, bc0, sm0, smt0, e16_0, y0, st0, direction=0, n_heads=n_sh)
    _ssd_kernel(xs1, bc1, sm1, smt1, e16_1, y1, st1, direction=1, n_heads=n_sh)
    _mlstm_kernel(qk0, v0, sm0, smt0, em_0, h0, cn0, m0, direction=0, n_heads=n_mh, dk=dk)
    _mlstm_kernel(qk1, v1, sm1, smt1, em_1, h1, cn1, m1, direction=1, n_heads=n_mh, dk=dk)


def _scans(xs, bc, qk, v, small, smallt, *, batch, n_ctx_chunks, n_lat_chunks, dk):
    t_all, d = xs.shape
    n_sh = d // SSD_HEAD_DIM
    n_mh = d // MLSTM_DV
    n_steps = n_ctx_chunks + n_lat_chunks
    in_specs, args = [], []
    for direction in range(2):
        idx = _chunk_block_map(direction, batch, n_ctx_chunks, n_lat_chunks)
        rows = lambda b, s, idx=idx: (idx(b, s), 0)
        in_specs += [pl.BlockSpec((CHUNK, d), rows), pl.BlockSpec((CHUNK, bc.shape[1]), rows),
                     pl.BlockSpec((CHUNK, qk.shape[1]), rows), pl.BlockSpec((CHUNK, d), rows),
                     pl.BlockSpec((CHUNK, 3 * LANES), rows),
                     pl.BlockSpec((1, 2, LANES, CHUNK), lambda b, s, idx=idx: (idx(b, s), 0, 0, 0))]
        args += [xs, bc, qk, v, small, smallt]
    out_specs = []
    for direction in (0, 1, 0, 1):
        idx = _chunk_block_map(direction, batch, n_ctx_chunks, n_lat_chunks)
        out_specs.append(pl.BlockSpec((CHUNK, d), lambda b, s, idx=idx: (idx(b, s), 0)))
    for direction in range(2):
        sel = _selectors(direction, d)
        in_specs += [pl.BlockSpec(a.shape, lambda b, s: (0, 0)) for a in sel]
        args += list(sel)
    state = [pltpu.VMEM((SSD_STATE, d), F32)] * 2 + [pltpu.VMEM((n_mh, dk, 2 * MLSTM_DV), F32)] * 2 \
        + [pltpu.VMEM((PACK_ROWS, LANES), F32)] * 2
    return pl.pallas_call(
        functools.partial(_scans_kernel, n_sh=n_sh, n_mh=n_mh, dk=dk),
        out_shape=[jax.ShapeDtypeStruct((t_all, d), BF16)] * 4,
        grid=(batch, n_steps),
        in_specs=in_specs, out_specs=out_specs, scratch_shapes=state,
        compiler_params=pltpu.CompilerParams(dimension_semantics=("arbitrary", "arbitrary"),
                                             vmem_limit_bytes=VMEM_LIMIT),
        name="scans",
    )(*args)


def _merge_kernel(y0_ref, y1_ref, xs_ref, z_ref, h0_ref, h1_ref, og_ref, mg_ref, x_ref, mod_ref, dexp_ref, gs_ref,
                  gm_ref, gf_ref, wso_ref, wmo_ref, wo_ref, x1_ref, u2_ref, *, d_model):
    d = d_model
    m = mod_ref[0]
    y = y0_ref[...].astype(F32) + y1_ref[...].astype(F32) + dexp_ref[...] * xs_ref[...].astype(F32)
    zf = z_ref[...].astype(F32)
    y = y * (zf * _sigmoid(zf))
    y = y * lax.rsqrt(jnp.mean(y * y, axis=-1, keepdims=True) + EPS) * gs_ref[...]
    a = _dot(y.astype(BF16), wso_ref[...])
    hm = h0_ref[...].astype(F32) + h1_ref[...].astype(F32)
    parts = []
    for h in range(d // MLSTM_DV):
        blk = hm[:, h * MLSTM_DV:(h + 1) * MLSTM_DV]
        parts.append(blk * lax.rsqrt(jnp.mean(blk * blk, axis=-1, keepdims=True) + EPS))
    hn = jnp.concatenate(parts, axis=1) * gm_ref[...]
    hh = (og_ref[...].astype(F32) * hn).astype(BF16)
    bm = _dot(hh, wmo_ref[...])
    merged = mg_ref[:, :d].astype(F32) * a + mg_ref[:, d:].astype(F32) * bm
    r = _dot(merged.astype(BF16), wo_ref[...])
    x1 = x_ref[...] + m[2:3] * r
    x1_ref[...] = x1
    u2 = x1 * lax.rsqrt(jnp.mean(x1 * x1, axis=-1, keepdims=True) + EPS) * gf_ref[...] * (1.0 + m[4:5]) + m[3:4]
    u2_ref[...] = _pack_halves(u2)


def _merge(y0, y1, xs, z, h0, h1, og, mg, x2, mods3, dexp, gs, gm, gf, wso, wmo, wo, *, batch, seq, n_ctx_tok):
    t, d = x2.shape
    tm = TOKEN_TILE
    off = n_ctx_tok // tm
    tiles_per_batch = seq // tm
    lat = lambda i: (i + off, 0)
    row = lambda i: (i, 0)
    const = lambda i: (0, 0)
    kern = functools.partial(_merge_kernel, d_model=d)
    wspec = pl.BlockSpec((d, d), const, pipeline_mode=pl.Buffered(1))
    return pl.pallas_call(
        kern,
        out_shape=[jax.ShapeDtypeStruct((t, d), F32), jax.ShapeDtypeStruct((t, d // 2), U32)],
        grid=(t // tm,),
        in_specs=[pl.BlockSpec((tm, d), lat), pl.BlockSpec((tm, d), lat), pl.BlockSpec((tm, d), lat),
                  pl.BlockSpec((tm, d), lat), pl.BlockSpec((tm, d), lat), pl.BlockSpec((tm, d), lat),
                  pl.BlockSpec((tm, d), lat), pl.BlockSpec((tm, 2 * d), lat), pl.BlockSpec((tm, d), row),
                  pl.BlockSpec((1, 6, d), lambda i: (i // tiles_per_batch, 0, 0)),
                  pl.BlockSpec((1, d), const), pl.BlockSpec((1, d), const), pl.BlockSpec((1, d), const),
                  pl.BlockSpec((1, d), const), wspec, wspec, wspec],
        out_specs=[pl.BlockSpec((tm, d), row), pl.BlockSpec((tm, d // 2), row)],
        compiler_params=pltpu.CompilerParams(dimension_semantics=("arbitrary",), vmem_limit_bytes=VMEM_LIMIT),
        name="merge",
    )(y0, y1, xs, z, h0, h1, og, mg, x2, mods3, dexp, gs, gm, gf, wso, wmo, wo)


def _first_index_of_max(vals, row_iota, n_rows):
    mx = jnp.max(vals, axis=0, keepdims=True)
    idx = jnp.min(jnp.where(vals == mx, row_iota, n_rows), axis=0, keepdims=True)
    return mx, idx


def _router_kernel(u_ref, wt_ref, bias_ref, su_ref, idx_ref, pos_ref, wts_ref, cnt_ref, run_ref, *, n_experts, tr):
    i = pl.program_id(0)

    @pl.when(i == 0)
    def _():
        run_ref[...] = jnp.zeros_like(run_ref)

    ua, ub = _unpack_halves(u_ref[...])
    half = ua.shape[1]
    scores = _sigmoid(_dot_nt(wt_ref[:, :half], ua.astype(BF16)) + _dot_nt(wt_ref[:, half:], ub.astype(BF16)))
    biased = scores + bias_ref[...]
    gsz = n_experts // N_EXPERT_GROUPS
    gi = lax.broadcasted_iota(I32, (gsz, tr), 0).astype(F32)
    gscores = []
    for g in range(N_EXPERT_GROUPS):
        blk = biased[g * gsz:(g + 1) * gsz]
        m1, i1 = _first_index_of_max(blk, gi, gsz)
        m2 = jnp.max(jnp.where(gi == i1, NEG_INF, blk), axis=0, keepdims=True)
        gscores.append(m1 + m2)
    gs = jnp.concatenate(gscores, axis=0)
    g8 = lax.broadcasted_iota(I32, (N_EXPERT_GROUPS, tr), 0).astype(F32)
    gsel = jnp.zeros((N_EXPERT_GROUPS, tr), F32)
    for _ in range(TOPK_GROUPS):
        _, gidx = _first_index_of_max(gs, g8, N_EXPERT_GROUPS)
        hit = g8 == gidx
        gsel = jnp.where(hit, 1.0, gsel)
        gs = jnp.where(hit, NEG_INF, gs)
    cand = jnp.concatenate(
        [jnp.where(jnp.broadcast_to(gsel[g:g + 1], (gsz, tr)) > 0.5, biased[g * gsz:(g + 1) * gsz], NEG_INF)
         for g in range(N_EXPERT_GROUPS)], axis=0)
    ei = lax.broadcasted_iota(I32, (n_experts, tr), 0).astype(F32)
    sel = jnp.zeros((n_experts, tr), F32)
    idxs, ws = [], []
    for _ in range(TOP_K):
        _, eidx = _first_index_of_max(cand, ei, n_experts)
        hit = ei == eidx
        ws.append(jnp.sum(jnp.where(hit, scores, 0.0), axis=0, keepdims=True))
        idxs.append(eidx)
        sel = jnp.where(hit, 1.0, sel)
        cand = jnp.where(hit, NEG_INF, cand)
    wk = jnp.concatenate(ws, axis=0)
    wts_ref[...] = ROUTED_SCALE * wk / jnp.sum(wk, axis=0, keepdims=True)
    idx_ref[...] = jnp.concatenate(idxs, axis=0).astype(I32)
    selb = sel.astype(BF16)
    posmat = _dot(selb, su_ref[...]) + run_ref[:, 0:1]
    pos_ref[...] = jnp.concatenate(
        [jnp.sum(jnp.where(ei == idxs[k], posmat, 0.0), axis=0, keepdims=True) for k in range(TOP_K)],
        axis=0).astype(I32)
    run = run_ref[...] + _dot(selb, jnp.ones((tr, LANES), BF16))
    run_ref[...] = run
    cnt_ref[...] = run


def _router(u2p, router_wt, bias_col):
    t = u2p.shape[0]
    d = router_wt.shape[1]
    n_experts = router_wt.shape[0]
    tr = TOKEN_TILE
    su = jnp.asarray(np.triu(np.ones((tr, tr), np.float32), 1), BF16)
    kern = functools.partial(_router_kernel, n_experts=n_experts, tr=tr)
    col = lambda i: (0, i)
    const = lambda i: (0, 0)
    return pl.pallas_call(
        kern,
        out_shape=[jax.ShapeDtypeStruct((TOP_K, t), I32), jax.ShapeDtypeStruct((TOP_K, t), I32),
                   jax.ShapeDtypeStruct((TOP_K, t), F32), jax.ShapeDtypeStruct((n_experts, LANES), F32)],
        grid=(t // tr,),
        in_specs=[pl.BlockSpec((tr, d // 2), lambda i: (i, 0)), pl.BlockSpec((n_experts, d), const),
                  pl.BlockSpec((n_experts, 1), const), pl.BlockSpec((tr, tr), const)],
        out_specs=[pl.BlockSpec((TOP_K, tr), col), pl.BlockSpec((TOP_K, tr), col), pl.BlockSpec((TOP_K, tr), col),
                   pl.BlockSpec((n_experts, LANES), const)],
        scratch_shapes=[pltpu.VMEM((n_experts, LANES), F32)],
        compiler_params=pltpu.CompilerParams(dimension_semantics=("arbitrary",), vmem_limit_bytes=VMEM_LIMIT),
        name="router",
    )(u2p, router_wt, bias_col, su)


def _slots_kernel(idx_ref, pos_ref, pstart_ref, dest_ref, *, n_experts, tr):
    ei = lax.broadcasted_iota(I32, (n_experts, tr), 0).astype(F32)
    pstart = pstart_ref[...]
    idx = idx_ref[...].astype(F32)
    rows = [jnp.sum(jnp.where(ei == idx[k:k + 1], pstart, 0.0), axis=0, keepdims=True) for k in range(TOP_K)]
    dest_ref[...] = jnp.concatenate(rows, axis=0).astype(I32) + pos_ref[...]


def _slots(idx, pos, pstart_col):
    t = idx.shape[1]
    n_experts = pstart_col.shape[0]
    tr = TOKEN_TILE
    col = lambda i: (0, i)
    return pl.pallas_call(
        functools.partial(_slots_kernel, n_experts=n_experts, tr=tr),
        out_shape=jax.ShapeDtypeStruct((TOP_K, t), I32),
        grid=(t // tr,),
        in_specs=[pl.BlockSpec((TOP_K, tr), col), pl.BlockSpec((TOP_K, tr), col),
                  pl.BlockSpec((n_experts, 1), lambda i: (0, 0))],
        out_specs=pl.BlockSpec((TOP_K, tr), col),
        compiler_params=pltpu.CompilerParams(dimension_semantics=("arbitrary",), vmem_limit_bytes=VMEM_LIMIT),
        name="slots",
    )(idx, pos, pstart_col)


PAD_BITS = tuple(1 << b for b in reversed(range((MOE_BLOCK - 1).bit_length())))


def _dispatch_kernel(pstart_ref, cnt_ref, dest_ref, u_ref, xs_ref, ubuf, zbuf, usem, sem, psem, *, td, n_steps,
                     n_experts):
    i = pl.program_id(0)
    slot = i % 2

    def u_copies(step, s):
        rows = pl.ds(pl.multiple_of(step * td, td), td)
        return [pltpu.make_async_copy(u_ref.at[rows, pl.ds(j * LANES, LANES)], ubuf.at[s, :, j, :], usem.at[s])
                for j in range(ROW_SUBLANES)]

    @pl.when(i == 0)
    def _():
        for cp in u_copies(i, 0):
            cp.start()
        zbuf[...] = jnp.zeros_like(zbuf)

        def pads(e, wait):
            cnt = cnt_ref[e]
            n_pad = (MOE_BLOCK - (cnt & (MOE_BLOCK - 1))) & (MOE_BLOCK - 1)
            base = pstart_ref[e] + cnt
            for bit in PAD_BITS:
                cp = pltpu.make_async_copy(zbuf.at[pl.ds(0, bit)], xs_ref.at[pl.ds(base, bit)], psem)
                has = (n_pad & bit) != 0

                @pl.when(has)
                def _():
                    if wait:
                        cp.wait()
                    else:
                        cp.start()

                base = base + jnp.where(has, bit, 0)

        def start_pads(e, carry):
            pads(e, False)
            return carry

        def wait_pads(e, carry):
            pads(e, True)
            return carry

        lax.fori_loop(0, n_experts, start_pads, 0)
        lax.fori_loop(0, n_experts, wait_pads, 0)

    @pl.when(i + 1 < n_steps)
    def _():
        for cp in u_copies(i + 1, 1 - slot):
            cp.start()

    for cp in u_copies(i, slot):
        cp.wait()

    def start(t, carry):
        for k in range(TOP_K):
            pltpu.make_async_copy(ubuf.at[slot, t], xs_ref.at[dest_ref[k, t]], sem).start(priority=k % 2)
        return carry

    def wait(t, carry):
        for k in range(TOP_K):
            pltpu.make_async_copy(ubuf.at[slot, 0], xs_ref.at[0], sem).wait()
        return carry

    lax.fori_loop(0, td, start, 0, unroll=4)
    lax.fori_loop(0, td, wait, 0, unroll=8)


def _dispatch(pad_start, counts, dest, u2p, n_slots):
    t, dh = u2p.shape
    td = DISPATCH_TILE
    n_experts = pad_start.shape[0]
    assert MOE_BLOCK & (MOE_BLOCK - 1) == 0 and dh == ROW_SUBLANES * LANES
    kern = functools.partial(_dispatch_kernel, td=td, n_steps=t // td, n_experts=n_experts)
    return pl.pallas_call(
        kern,
        out_shape=jax.ShapeDtypeStruct((n_slots, ROW_SUBLANES, LANES), U32),
        grid_spec=pltpu.PrefetchScalarGridSpec(
            num_scalar_prefetch=2, grid=(t // td,),
            in_specs=[pl.BlockSpec((TOP_K, td), lambda i, ps, cn: (0, i), memory_space=pltpu.SMEM),
                      pl.BlockSpec(memory_space=pl.ANY)],
            out_specs=pl.BlockSpec(memory_space=pl.ANY),
            scratch_shapes=[pltpu.VMEM((2, td, ROW_SUBLANES, LANES), U32),
                            pltpu.VMEM((MOE_BLOCK // 2, ROW_SUBLANES, LANES), U32),
                            pltpu.SemaphoreType.DMA((2,)), pltpu.SemaphoreType.DMA, pltpu.SemaphoreType.DMA]),
        compiler_params=pltpu.CompilerParams(dimension_semantics=("arbitrary",), vmem_limit_bytes=VMEM_LIMIT),
        name="dispatch",
    )(pad_start, counts, dest, u2p)


def _experts_kernel(bs_ref, nblk_ref, nu_ref, xs_ref, wg_ref, wu_ref, wd_ref, ys_ref, wgb, wub, wdb, xbuf, ybuf,
                    semx, semy, *, n_experts):
    e = pl.program_id(0)
    n_used = nu_ref[0]

    def x_copies(g):
        rows = pl.ds(pl.multiple_of(g * MOE_BLOCK, MOE_BLOCK), MOE_BLOCK)
        return [pltpu.make_async_copy(xs_ref.at[rows, j, :], xbuf.at[g % X_BUFFERS, :, pl.ds(j * LANES, LANES)],
                                      semx.at[g % X_BUFFERS]) for j in range(ROW_SUBLANES)]

    def x_start(g):
        for cp in x_copies(g):
            cp.start()

    def x_wait(g):
        for cp in x_copies(g):
            cp.wait()

    def y_copies(g, slot):
        rows = pl.ds(pl.multiple_of(g * MOE_BLOCK, MOE_BLOCK), MOE_BLOCK)
        return [pltpu.make_async_copy(ybuf.at[slot, :, pl.ds(j * LANES, LANES)], ys_ref.at[rows, j, :], semy.at[slot])
                for j in range(ROW_SUBLANES)]

    def y_start(g, slot):
        for cp in y_copies(g, slot):
            cp.start()

    def y_wait(g, slot):
        for cp in y_copies(g, slot):
            cp.wait()

    @pl.when(e == 0)
    def _():
        for g in range(X_BUFFERS - 1):
            @pl.when(g < n_used)
            def _():
                x_start(jnp.int32(g))

    g0 = bs_ref[e]
    nb = nblk_ref[e]

    @pl.when(nb > 0)
    def _():
        wgb[...] = wg_ref[0].astype(BF16)
        wub[...] = wu_ref[0].astype(BF16)
        wdb[...] = wd_ref[0].astype(BF16)

    def block(g, carry):
        slot = g % 2

        @pl.when(g + X_BUFFERS - 1 < n_used)
        def _():
            x_start(g + X_BUFFERS - 1)

        x_wait(g)

        @pl.when(g >= 2)
        def _():
            y_wait(g - 2, slot)

        xa, xb = _unpack_halves(xbuf[g % X_BUFFERS])
        half = xa.shape[1]
        xa = xa.astype(BF16)
        xb = xb.astype(BF16)
        hg = _dot(xa, wgb[:half]) + _dot(xb, wgb[half:])
        hu = _dot(xa, wub[:half]) + _dot(xb, wub[half:])
        hb = (hg * _sigmoid(hg) * hu).astype(BF16)
        ybuf[slot] = _pack_halves(_dot(hb, wdb[...]))
        y_start(g, slot)
        return carry

    lax.fori_loop(g0, g0 + nb, block, 0)

    @pl.when(e == n_experts - 1)
    def _():
        @pl.when(n_used >= 2)
        def _():
            y_wait(n_used - 2, n_used % 2)

        @pl.when(n_used >= 1)
        def _():
            y_wait(n_used - 1, (n_used - 1) % 2)


def _experts(block_start, n_blocks, n_used, xsorted, wg, wu, wd):
    n_slots = xsorted.shape[0]
    dh = ROW_SUBLANES * LANES
    n_experts, d, de = wg.shape
    wmap = lambda e, bs, nb, nu: (e, 0, 0)
    assert xsorted.shape[1:] == (ROW_SUBLANES, LANES) and d == 2 * dh
    return pl.pallas_call(
        functools.partial(_experts_kernel, n_experts=n_experts),
        out_shape=jax.ShapeDtypeStruct((n_slots, ROW_SUBLANES, LANES), U32),
        grid_spec=pltpu.PrefetchScalarGridSpec(
            num_scalar_prefetch=3, grid=(n_experts,),
            in_specs=[pl.BlockSpec(memory_space=pl.ANY), pl.BlockSpec((1, d, de), wmap),
                      pl.BlockSpec((1, d, de), wmap), pl.BlockSpec((1, de, d), wmap)],
            out_specs=pl.BlockSpec(memory_space=pl.ANY),
            scratch_shapes=[pltpu.VMEM((d, de), BF16), pltpu.VMEM((d, de), BF16), pltpu.VMEM((de, d), BF16),
                            pltpu.VMEM((X_BUFFERS, MOE_BLOCK, dh), U32),
                            pltpu.VMEM((2, MOE_BLOCK, dh), U32),
                            pltpu.SemaphoreType.DMA((X_BUFFERS,)), pltpu.SemaphoreType.DMA((2,))]),
        compiler_params=pltpu.CompilerParams(dimension_semantics=("arbitrary",), vmem_limit_bytes=VMEM_LIMIT),
        name="experts",
    )(block_start, n_blocks, n_used, xsorted, wg, wu, wd)


def _combine_kernel(dest_ref, dnext_ref, ys_ref, wt_ref, x1_ref, u_ref, mod_ref, wsg_ref, wsu_ref, wsd_ref,
                    gfin_ref, o_ref, gbuf, accbuf, sem, *, tc, n_steps):
    i = pl.program_id(0)
    slot = i % 2

    def issue(dref, s):
        def body(t, carry):
            for k in range(TOP_K):
                pltpu.make_async_copy(ys_ref.at[dref[k, t]], gbuf.at[s, k, t], sem.at[s]).start(priority=k % 2)
            return carry
        lax.fori_loop(0, tc, body, 0, unroll=4)

    @pl.when(i == 0)
    def _():
        issue(dest_ref, 0)

    @pl.when(i + 1 < n_steps)
    def _():
        issue(dnext_ref, 1 - slot)

    ua, ub = _unpack_halves(u_ref[...])
    half = ua.shape[1]
    ua = ua.astype(BF16)
    ub = ub.astype(BF16)
    hg = _dot(ua, wsg_ref[:half]) + _dot(ub, wsg_ref[half:])
    hu = _dot(ua, wsu_ref[:half]) + _dot(ub, wsu_ref[half:])
    shared = _dot((hg * _sigmoid(hg) * hu).astype(BF16), wsd_ref[...])
    acc_l = shared[:, :half]
    acc_r = shared[:, half:]

    def wait(t, carry):
        for k in range(TOP_K):
            pltpu.make_async_copy(ys_ref.at[0], gbuf.at[slot, 0, 0], sem.at[slot]).wait()
        return carry

    lax.fori_loop(0, tc, wait, 0, unroll=8)
    sum_l = jnp.zeros((tc, ROW_SUBLANES, LANES), F32)
    sum_r = jnp.zeros((tc, ROW_SUBLANES, LANES), F32)
    for k in range(TOP_K):
        ga, gb = _unpack_halves(gbuf[slot, k])
        wk = jnp.broadcast_to(wt_ref[:, k:k + 1, :], (tc, ROW_SUBLANES, LANES))
        sum_l = sum_l + ga * wk
        sum_r = sum_r + gb * wk
    accbuf[0] = sum_l
    accbuf[1] = sum_r
    acc_l = jnp.concatenate([acc_l[:, j * LANES:(j + 1) * LANES] + accbuf[0, :, j, :] for j in range(ROW_SUBLANES)],
                            axis=1)
    acc_r = jnp.concatenate([acc_r[:, j * LANES:(j + 1) * LANES] + accbuf[1, :, j, :] for j in range(ROW_SUBLANES)],
                            axis=1)
    m = mod_ref[0]
    xo_l = x1_ref[:, :half] + m[5:6, :half] * acc_l
    xo_r = x1_ref[:, half:] + m[5:6, half:] * acc_r
    ms = (jnp.sum(xo_l * xo_l, axis=-1, keepdims=True) + jnp.sum(xo_r * xo_r, axis=-1, keepdims=True)) / (2 * half)
    inv = lax.rsqrt(ms + EPS)
    o_ref[:, :half] = xo_l * inv * gfin_ref[:, :half]
    o_ref[:, half:] = xo_r * inv * gfin_ref[:, half:]


def _combine(dest, ysorted, wts_t, x1, u2p, mods3, wsg, wsu, wsd, gfin, *, seq):
    t, d = x1.shape
    dh = d // 2
    tc = DISPATCH_TILE
    tiles_per_batch = seq // tc
    n_steps = t // tc
    kern = functools.partial(_combine_kernel, tc=tc, n_steps=n_steps)
    row = lambda i: (i, 0)
    const = lambda i: (0, 0)
    return pl.pallas_call(
        kern,
        out_shape=jax.ShapeDtypeStruct((t, d), F32),
        grid=(n_steps,),
        in_specs=[pl.BlockSpec((TOP_K, tc), lambda i: (0, i), memory_space=pltpu.SMEM),
                  pl.BlockSpec((TOP_K, tc), lambda i: (0, jnp.minimum(i + 1, n_steps - 1)), memory_space=pltpu.SMEM),
                  pl.BlockSpec(memory_space=pl.ANY), pl.BlockSpec((tc, TOP_K, LANES), lambda i: (i, 0, 0)),
                  pl.BlockSpec((tc, d), row), pl.BlockSpec((tc, dh), row),
                  pl.BlockSpec((1, 6, d), lambda i: (i // tiles_per_batch, 0, 0)),
                  pl.BlockSpec(wsg.shape, const), pl.BlockSpec(wsu.shape, const), pl.BlockSpec(wsd.shape, const),
                  pl.BlockSpec((1, d), const)],
        out_specs=pl.BlockSpec((tc, d), row),
        scratch_shapes=[pltpu.VMEM((2, TOP_K, tc, ROW_SUBLANES, LANES), U32),
                        pltpu.VMEM((2, tc, ROW_SUBLANES, LANES), F32), pltpu.SemaphoreType.DMA((2,))],
        compiler_params=pltpu.CompilerParams(dimension_semantics=("arbitrary",), vmem_limit_bytes=VMEM_LIMIT),
        name="combine",
    )(dest, dest, ysorted, wts_t, x1, u2p, mods3, wsg, wsu, wsd, gfin)


def kernel(x, c, ctx, c_ctx, ada_w, ada_b, norm_mix_g, norm_ffn_g, w_in, conv_xbc_w, conv_xbc_b, ssd_dt_bias, ssd_a_log, ssd_d, ssd_norm_g, conv_qk_w, conv_qk_b, mlstm_i_bias, mlstm_f_bias, mlstm_norm_g, w_ssd_out, w_mlstm_out, w_out, router_w, router_bias, moe_w_gate, moe_w_up, moe_w_down, shared_w_gate, shared_w_up, shared_w_down, norm_final_g):
    batch, seq, d = x.shape
    ctx_len = ctx.shape[1]
    depth = ada_w.shape[0]
    assert depth == 1, "only the single-layer configuration is implemented"
    assert seq % CHUNK == 0 and ctx_len % CHUNK == 0 and seq % GRID_W == 0
    l = 0
    n_sh = d // SSD_HEAD_DIM
    n_mh = d // MLSTM_DV
    dk = MLSTM_DV // 2
    d_xbc = d + 2 * SSD_GROUPS * SSD_STATE
    d_qk = 2 * n_mh * dk
    sizes = (d, d_xbc, 2 * n_sh, d_qk, d, 4 * n_mh, d, 2 * d)
    offs = np.concatenate([[0], np.cumsum(sizes)])
    assert offs[-1] == w_in.shape[2] and 2 * n_sh + 2 * n_mh <= LANES

    cond = jnp.concatenate([c, c_ctx[None], jnp.zeros((8 - (batch + 1) % 8, d), F32)], axis=0)
    mods = _adaln(cond, ada_w[l], ada_b[l])
    mods3 = mods.reshape(mods.shape[0], 6, d)

    w = w_in[l]
    seg = lambda k: w[:, offs[k]:offs[k + 1]]
    wbig = jnp.concatenate([seg(0), seg(1), seg(3), seg(4), seg(6), seg(7)], axis=1).astype(BF16)
    w_dt = seg(2)
    w_g = seg(5).reshape(d, 2, 2, n_mh)
    w_i = w_g[:, :, 0].reshape(d, 2 * n_mh)
    w_f = w_g[:, :, 1].reshape(d, 2 * n_mh)
    pad = jnp.zeros((d, LANES - 2 * n_sh - 2 * n_mh), F32)
    wsm = jnp.concatenate([w_dt, w_f, pad, w_dt, w_i, pad], axis=1)
    wsh = wsm.astype(BF16)
    wsl = (wsm - wsh.astype(F32)).astype(BF16)
    padb = jnp.zeros((LANES - 2 * n_sh - 2 * n_mh,), F32)
    dtb = ssd_dt_bias[l].reshape(-1).astype(F32)
    smb = jnp.concatenate([dtb, mlstm_f_bias[l].reshape(-1).astype(F32), padb,
                           dtb, mlstm_i_bias[l].reshape(-1).astype(F32), padb]).reshape(1, 2 * LANES)
    aneg = jnp.concatenate([-jnp.exp(ssd_a_log[l].astype(F32)).reshape(-1),
                            jnp.zeros((LANES - 2 * n_sh,), F32)]).reshape(1, LANES)

    x2 = x.reshape(batch * seq, d)
    ctx2 = ctx.reshape(batch * ctx_len, d)
    z, xs, bc, qk, v, og, mg, small, smallt = _inproj(
        x2, ctx2, mods3, norm_mix_g[l].reshape(1, d), wbig, wsh, wsl,
        conv_xbc_w[l], conv_xbc_b[l].reshape(1, d_xbc), conv_qk_w[l], conv_qk_b[l].reshape(1, d_qk), smb, aneg,
        batch=batch, seq=seq, ctx_len=ctx_len, dk=dk)

    ncc = ctx_len // CHUNK
    ncl = seq // CHUNK
    y0, y1, h0, h1 = _scans(xs, bc, qk, v, small, smallt, batch=batch, n_ctx_chunks=ncc, n_lat_chunks=ncl, dk=dk)

    dexp = jnp.repeat(ssd_d[l].astype(F32), SSD_HEAD_DIM).reshape(1, d)
    x1, u2 = _merge(y0, y1, xs, z, h0, h1, og, mg, x2, mods3, dexp, ssd_norm_g[l].reshape(1, d),
                    mlstm_norm_g[l].reshape(1, d), norm_ffn_g[l].reshape(1, d),
                    w_ssd_out[l].astype(BF16), w_mlstm_out[l].astype(BF16), w_out[l].astype(BF16),
                    batch=batch, seq=seq, n_ctx_tok=batch * ctx_len)

    n_experts = router_w.shape[2]
    idx, pos, wts, cnt = _router(u2, router_w[l].T.astype(BF16), router_bias[l].astype(F32).reshape(n_experts, 1))
    counts = cnt[:, 0].astype(I32)
    padded = (counts + MOE_BLOCK - 1) // MOE_BLOCK * MOE_BLOCK
    pad_end = jnp.cumsum(padded)
    pad_start = (pad_end - padded).astype(I32)
    t = batch * seq
    nb = t * TOP_K // MOE_BLOCK + n_experts
    n_used = (pad_end[-1] // MOE_BLOCK).astype(I32).reshape(1)

    dest = _slots(idx, pos, pad_start.astype(F32).reshape(n_experts, 1))
    xsorted = _dispatch(pad_start, counts, dest, u2, nb * MOE_BLOCK)
    ysorted = _experts(pad_start // MOE_BLOCK, (padded // MOE_BLOCK).astype(I32), n_used, xsorted,
                       moe_w_gate[l], moe_w_up[l], moe_w_down[l])
    wts_lanes = jnp.broadcast_to(wts.T[:, :, None], (t, TOP_K, LANES))
    out = _combine(dest, ysorted, wts_lanes, x1, u2, mods3,
                   shared_w_gate[l].astype(BF16), shared_w_up[l].astype(BF16), shared_w_down[l].astype(BF16),
                   norm_final_g.reshape(1, d), seq=seq)
    return out.reshape(batch, seq, d)
```

```python
import functools

import numpy as np
import jax
import jax.numpy as jnp
from jax import lax
from jax.experimental import pallas as pl
from jax.experimental.pallas import tpu as pltpu

F32 = jnp.float32
BF16 = jnp.bfloat16
I32 = jnp.int32
U32 = jnp.uint32

EPS = 1e-6
CHUNK = 128
CONV_K = 5
GRID_W = 64
SSD_HEAD_DIM = 64
SSD_STATE = 128
SSD_GROUPS = 2
MLSTM_DV = 128
N_EXPERT_GROUPS = 8
TOPK_GROUPS = 4
TOP_K = 8
ROUTED_SCALE = 2.5

LANES = 128
SUBLANES = 8
PACK_ROWS = 16
TOKEN_TILE = 512
COL_CHUNK = 512
MOE_BLOCK = 256
DISPATCH_TILE = 256
GATHER_CHUNK = 4
X_BUFFERS = 4
ROW_SUBLANES = 4
VMEM_LIMIT = 56 * 1024 * 1024
NEG_INF = float("-inf")


def _dot(a, b):
    return jnp.dot(a, b, preferred_element_type=F32)


def _dot_nt(a, b):
    return lax.dot_general(a, b, (((1,), (1,)), ((), ())), preferred_element_type=F32)


def _dot_tn(a, b):
    return lax.dot_general(a, b, (((0,), (0,)), ((), ())), preferred_element_type=F32)


def _spread(parts, e):
    res = _dot(jnp.concatenate(parts, axis=0).astype(BF16), e)
    out, r0 = [], 0
    for p in parts:
        out.append(res[r0:r0 + p.shape[0]])
        r0 += p.shape[0]
    return out


def _rows16(row):
    r8 = jnp.broadcast_to(row, (SUBLANES, row.shape[1]))
    return jnp.concatenate([r8, r8], axis=0)


def _sigmoid(v):
    return 1.0 / (1.0 + jnp.exp(-v))


def _pack_halves(v):
    n = v.shape[1] // 2
    hi = lax.bitcast_convert_type(v[:, :n].astype(BF16).astype(F32), U32)
    lo = lax.bitcast_convert_type(v[:, n:].astype(BF16).astype(F32), U32)
    return hi | (lo >> 16)


def _unpack_halves(p):
    left = lax.bitcast_convert_type(p & jnp.uint32(0xFFFF0000), F32)
    right = lax.bitcast_convert_type(p << 16, F32)
    return left, right


def _softplus(v):
    return jnp.maximum(v, 0.0) + jnp.log1p(jnp.exp(-jnp.abs(v)))


def _adaln_kernel(c_ref, w_ref, b_ref, o_ref):
    c = c_ref[...]
    s = c * _sigmoid(c)
    w = w_ref[...]
    s_hi = s.astype(BF16)
    s_lo = (s - s_hi.astype(F32)).astype(BF16)
    w_hi = w.astype(BF16)
    w_lo = (w - w_hi.astype(F32)).astype(BF16)
    o_ref[...] = _dot(s_hi, w_hi) + _dot(s_lo, w_hi) + _dot(s_hi, w_lo) + b_ref[...]


def _adaln(cond, w, b):
    rows, d = cond.shape
    n = w.shape[1]
    tn = 1536 if n % 1536 == 0 else n
    return pl.pallas_call(
        _adaln_kernel,
        out_shape=jax.ShapeDtypeStruct((rows, n), F32),
        grid=(n // tn,),
        in_specs=[pl.BlockSpec((rows, d), lambda j: (0, 0)),
                  pl.BlockSpec((d, tn), lambda j: (0, j)),
                  pl.BlockSpec((1, tn), lambda j: (0, j))],
        out_specs=pl.BlockSpec((rows, tn), lambda j: (0, j)),
        compiler_params=pltpu.CompilerParams(dimension_semantics=("arbitrary",), vmem_limit_bytes=VMEM_LIMIT),
        name="adaln",
    )(cond, w, b.reshape(1, n))


CONV_SHIFTS = tuple(j - CONV_K // 2 for j in range(CONV_K) if j != CONV_K // 2)


def _conv_masks(tm, seg_len):
    pos = np.arange(tm) % seg_len
    m = np.zeros((tm, 8), np.float32)
    for i, s in enumerate(CONV_SHIFTS):
        m[:, i] = ((pos + s >= 0) & (pos + s < seg_len)).astype(np.float32)
    return m


def _conv_silu(acc, w5, bias, vm, tm):
    out = acc * w5[CONV_K // 2:CONV_K // 2 + 1] + bias
    for i, s in enumerate(CONV_SHIFTS):
        shifted = pltpu.roll(acc, (-s) % tm, axis=0)
        j = s + CONV_K // 2
        out = out + (shifted * vm[:, i:i + 1]) * w5[j:j + 1]
    return out * _sigmoid(out)


def _inproj_kernel(x_ref, ctx_ref, mod_ref, g_ref, wbig_ref, wsh_ref, wsl_ref, cwx_ref, cbx_ref, cwq_ref, cbq_ref,
                   smb_ref, aneg_ref, tril_ref, triu_ref, cmask_ref,
                   z_ref, xs_ref, bc_ref, qk_ref, v_ref, og_ref, mg_ref, small_ref, smallt_ref,
                   *, n_ctx_tiles, tm, d_model, dk):
    i = pl.program_id(0)
    is_ctx = i < n_ctx_tiles
    xt = jnp.where(is_ctx, ctx_ref[...], x_ref[...])
    m = mod_ref[0]
    ms = jnp.mean(xt * xt, axis=-1, keepdims=True)
    u = xt * lax.rsqrt(ms + EPS) * g_ref[...] * (1.0 + m[1:2]) + m[0:1]
    u_hi = u.astype(BF16)
    u_lo = (u - u_hi.astype(F32)).astype(BF16)
    vm = jnp.where(is_ctx, cmask_ref[1], cmask_ref[0])

    d = d_model
    d_bc = 2 * SSD_GROUPS * SSD_STATE
    pieces = [(z_ref, d, "plain", None), (xs_ref, d, "convx", 0), (bc_ref, d_bc, "convx", d),
              (qk_ref, d, "convq", 0), (v_ref, d, "plain", None), (og_ref, d, "sigmoid", None),
              (mg_ref, 2 * d, "sigmoid", None)]
    col = 0
    heavy, light = [], []
    for ref, width, kind, coff in pieces:
        for c0 in range(0, width, COL_CHUNK):
            (heavy if kind.startswith("conv") else light).append((ref, width, kind, coff, c0, col + c0))
        col += width
    order = []
    while heavy or light:
        if heavy:
            order.append(heavy.pop(0))
        for _ in range(2):
            if light:
                order.append(light.pop(0))
    for ref, width, kind, coff, c0, wcol in order:
        acc = _dot(u_hi, wbig_ref[:, wcol:wcol + COL_CHUNK])
        if kind == "convx":
            cs = coff + c0
            acc = _conv_silu(acc, cwx_ref[:, cs:cs + COL_CHUNK], cbx_ref[:, cs:cs + COL_CHUNK], vm, tm)
        elif kind == "convq":
            acc = _conv_silu(acc, cwq_ref[:, c0:c0 + COL_CHUNK], cbq_ref[:, c0:c0 + COL_CHUNK], vm, tm)
            if c0 < width // 2:
                acc = acc * (dk ** -0.5)
        elif kind == "sigmoid":
            acc = _sigmoid(acc)
        ref[:, c0:c0 + COL_CHUNK] = acc.astype(ref.dtype)

    wsh = wsh_ref[...]
    raw = _dot(u_hi, wsh) + _dot(u_lo, wsh) + _dot(u_hi, wsl_ref[...]) + smb_ref[...]
    p1 = raw[:, :LANES]
    p2 = raw[:, LANES:]
    lane = lax.broadcasted_iota(I32, (tm, LANES), 1)
    n_dt = 2 * (d_model // SSD_HEAD_DIM)
    n_g = 2 * (d_model // MLSTM_DV)
    is_dt = lane < n_dt
    is_gate = jnp.logical_and(lane >= n_dt, lane < n_dt + n_g)
    dt = _softplus(p2)
    pa = jnp.where(is_dt, dt * aneg_ref[...], jnp.where(is_gate, -_softplus(-p1), 0.0))
    pb = jnp.where(is_dt, dt, jnp.where(is_gate, p2, 0.0))
    lane_c = lax.broadcasted_iota(I32, (CHUNK, LANES), 1)
    is_dt_c = lane_c < n_dt
    rev = jnp.logical_or(jnp.logical_and(lane_c >= n_dt // 2, lane_c < n_dt),
                         jnp.logical_and(lane_c >= n_dt + n_g // 2, lane_c < n_dt + n_g))
    tril = tril_ref[...]
    triu = triu_ref[...]
    tq = lax.broadcasted_iota(I32, (CHUNK, LANES), 0)
    for c in range(tm // CHUNK):
        r0 = c * CHUNK
        a_c = pa[r0:r0 + CHUNK]
        hi = a_c.astype(BF16)
        r1 = a_c - hi.astype(F32)
        mid = r1.astype(BF16)
        lo = (r1 - mid.astype(F32)).astype(BF16)
        cs_f = _dot(tril, hi) + _dot(tril, mid) + _dot(tril, lo)
        cs_b = _dot(triu, hi) + _dot(triu, mid) + _dot(triu, lo)
        plane_a = jnp.where(rev, cs_b, cs_f)
        plane_b = jnp.where(is_dt_c, pb[r0:r0 + CHUNK], pb[r0:r0 + CHUNK] - plane_a)
        yf = plane_b
        yb = plane_b
        s = 1
        while s < CHUNK:
            sh = pltpu.roll(yf, s, axis=0)
            yf = jnp.maximum(yf, jnp.where(tq >= s, sh, NEG_INF))
            sh = pltpu.roll(yb, CHUNK - s, axis=0)
            yb = jnp.maximum(yb, jnp.where(tq + s < CHUNK, sh, NEG_INF))
            s *= 2
        plane_c = jnp.where(rev, yb, yf)
        small_ref[r0:r0 + CHUNK, 0:LANES] = plane_a
        small_ref[r0:r0 + CHUNK, LANES:2 * LANES] = plane_b
        small_ref[r0:r0 + CHUNK, 2 * LANES:3 * LANES] = plane_c
        smallt_ref[c, 0] = plane_a.T
        smallt_ref[c, 1] = plane_b.T


def _inproj(x2, ctx2, mods3, g, wbig, wsh, wsl, cwx, cbx, cwq, cbq, smb, aneg, *, batch, seq, ctx_len, dk):
    d = x2.shape[1]
    tm = TOKEN_TILE
    n_ctx_tok = batch * ctx_len
    assert n_ctx_tok % tm == 0 and seq % tm == 0 and tm % ctx_len == 0 and tm % GRID_W == 0
    n_ctx_tiles = n_ctx_tok // tm
    tiles_per_batch = seq // tm
    n_tiles = n_ctx_tiles + batch * tiles_per_batch
    t_all = n_tiles * tm
    n_big = wbig.shape[1]
    tril = jnp.asarray(np.tril(np.ones((CHUNK, CHUNK), np.float32)), BF16)
    triu = jnp.asarray(np.triu(np.ones((CHUNK, CHUNK), np.float32)), BF16)

    def x_map(i):
        return (jnp.maximum(i - n_ctx_tiles, 0), 0)

    def ctx_map(i):
        return (jnp.minimum(i, n_ctx_tiles - 1), 0)

    def mod_map(i):
        return (jnp.where(i < n_ctx_tiles, batch, jnp.maximum(i - n_ctx_tiles, 0) // tiles_per_batch), 0, 0)

    const = lambda i: (0, 0)
    row = lambda i: (i, 0)
    cmask = jnp.asarray(np.stack([_conv_masks(tm, GRID_W), _conv_masks(tm, ctx_len)]))
    kern = functools.partial(_inproj_kernel, n_ctx_tiles=n_ctx_tiles, tm=tm, d_model=d, dk=dk)
    d_bc = 2 * SSD_GROUPS * SSD_STATE
    outs = [jax.ShapeDtypeStruct((t_all, d), BF16), jax.ShapeDtypeStruct((t_all, d), BF16),
            jax.ShapeDtypeStruct((t_all, d_bc), BF16), jax.ShapeDtypeStruct((t_all, d), BF16),
            jax.ShapeDtypeStruct((t_all, d), BF16), jax.ShapeDtypeStruct((t_all, d), BF16),
            jax.ShapeDtypeStruct((t_all, 2 * d), BF16), jax.ShapeDtypeStruct((t_all, 3 * LANES), F32),
            jax.ShapeDtypeStruct((t_all // CHUNK, 2, LANES, CHUNK), F32)]
    out_specs = [pl.BlockSpec((tm, d), row), pl.BlockSpec((tm, d), row), pl.BlockSpec((tm, d_bc), row),
                 pl.BlockSpec((tm, d), row), pl.BlockSpec((tm, d), row), pl.BlockSpec((tm, d), row),
                 pl.BlockSpec((tm, 2 * d), row), pl.BlockSpec((tm, 3 * LANES), row),
                 pl.BlockSpec((tm // CHUNK, 2, LANES, CHUNK), lambda i: (i, 0, 0, 0))]
    in_specs = [pl.BlockSpec((tm, d), x_map), pl.BlockSpec((tm, d), ctx_map),
                pl.BlockSpec((1, 6, d), mod_map), pl.BlockSpec((1, d), const),
                pl.BlockSpec((d, n_big), const, pipeline_mode=pl.Buffered(1)),
                pl.BlockSpec((d, 2 * LANES), const), pl.BlockSpec((d, 2 * LANES), const),
                pl.BlockSpec(cwx.shape, const), pl.BlockSpec(cbx.shape, const),
                pl.BlockSpec(cwq.shape, const), pl.BlockSpec(cbq.shape, const),
                pl.BlockSpec((1, 2 * LANES), const), pl.BlockSpec((1, LANES), const),
                pl.BlockSpec((CHUNK, CHUNK), const), pl.BlockSpec((CHUNK, CHUNK), const),
                pl.BlockSpec((2, tm, 8), lambda i: (0, 0, 0))]
    return pl.pallas_call(
        kern, out_shape=outs, grid=(n_tiles,), in_specs=in_specs, out_specs=out_specs,
        compiler_params=pltpu.CompilerParams(dimension_semantics=("arbitrary",), vmem_limit_bytes=VMEM_LIMIT),
        name="inproj",
    )(x2, ctx2, mods3, g, wbig, wsh, wsl, cwx, cbx, cwq, cbq, smb, aneg, tril, triu, cmask)


def _chunk_block_map(direction, batch, n_ctx_chunks, n_lat_chunks):
    def idx(b, s):
        if direction == 0:
            c_ctx = s
            c_lat = s - n_ctx_chunks
        else:
            c_ctx = n_ctx_chunks - 1 - s
            c_lat = n_lat_chunks - 1 - (s - n_ctx_chunks)
        return jnp.where(s < n_ctx_chunks, b * n_ctx_chunks + c_ctx, batch * n_ctx_chunks + b * n_lat_chunks + c_lat)
    return idx


def _ssd_kernel(xs_ref, bc_ref, small_ref, smallt_ref, e16_ref, y_ref, st_ref, *, direction, n_heads):
    s = pl.program_id(1)

    @pl.when(s == 0)
    def _():
        st_ref[...] = jnp.zeros_like(st_ref)

    hpg = n_heads // SSD_GROUPS
    lane0 = direction * n_heads
    last = CHUNK - 1 if direction == 0 else 0
    lane = lax.broadcasted_iota(I32, (CHUNK, LANES), 1)
    lm = jnp.logical_and(lane >= lane0, lane < lane0 + n_heads)
    plane_a = jnp.where(lm, small_ref[:, 0:LANES], 0.0)
    plane_b = jnp.where(lm, small_ref[:, LANES:2 * LANES], 0.0)
    cum_t = smallt_ref[0, 0]
    tot = plane_a[last:last + 1]
    lane16 = lax.broadcasted_iota(I32, (PACK_ROWS, LANES), 1)
    lm16 = jnp.logical_and(lane16 >= lane0, lane16 < lane0 + n_heads)
    dec_rows = jnp.where(lm16, jnp.exp(_rows16(tot)), 0.0)
    dtx, ecx, wx, decx = _spread(
        [plane_b, jnp.where(lm, jnp.exp(plane_a), 0.0), plane_b * jnp.exp(tot - plane_a), dec_rows], e16_ref[...])
    decx = decx[0:1]

    xf = xs_ref[...].astype(F32)
    xdt = xf * dtx
    xw = (xf * wx).astype(BF16)
    plane64 = lax.broadcasted_iota(I32, (CHUNK, 2 * SSD_HEAD_DIM), 1) < SSD_HEAD_DIM
    iq = lax.broadcasted_iota(I32, (CHUNK, CHUNK), 0)
    ik = lax.broadcasted_iota(I32, (CHUNK, CHUNK), 1)
    mask = (iq >= ik) if direction == 0 else (iq <= ik)
    gw = hpg * SSD_HEAD_DIM
    for g in range(SSD_GROUPS):
        bg = bc_ref[:, g * SSD_STATE:(g + 1) * SSD_STATE]
        cg = bc_ref[:, (SSD_GROUPS + g) * SSD_STATE:(SSD_GROUPS + g + 1) * SSD_STATE]
        cb = _dot_nt(cg, bg)
        sg = st_ref[:, g * gw:(g + 1) * gw]
        y_inter = _dot(cg, sg.astype(BF16)) * ecx[:, g * gw:(g + 1) * gw]
        for pair in range(hpg // 2):
            h0 = g * hpg + 2 * pair
            c0 = h0 * SSD_HEAD_DIM
            xpair = xdt[:, c0:c0 + 2 * SSD_HEAD_DIM]
            acc = y_inter[:, 2 * pair * SSD_HEAD_DIM:(2 * pair + 2) * SSD_HEAD_DIM]
            for par in range(2):
                h = h0 + par
                cum_q = jnp.broadcast_to(plane_a[:, lane0 + h:lane0 + h + 1], (CHUNK, CHUNK))
                seg = cum_q - cum_t[lane0 + h:lane0 + h + 1, :]
                m_h = (cb * jnp.exp(jnp.where(mask, seg, NEG_INF))).astype(BF16)
                keep = plane64 if par == 0 else jnp.logical_not(plane64)
                acc = acc + _dot(m_h, jnp.where(keep, xpair, 0.0).astype(BF16))
            y_ref[:, c0:c0 + 2 * SSD_HEAD_DIM] = acc.astype(y_ref.dtype)
        st_ref[:, g * gw:(g + 1) * gw] = decx[:, g * gw:(g + 1) * gw] * sg + _dot_tn(bg, xw[:, g * gw:(g + 1) * gw])


def _selectors(direction, d):
    n_sh = d // SSD_HEAD_DIM
    n_mh = d // MLSTM_DV
    e16 = np.zeros((LANES, d), np.float32)
    em = np.zeros((LANES, d), np.float32)
    for h in range(n_sh):
        e16[direction * n_sh + h, h * SSD_HEAD_DIM:(h + 1) * SSD_HEAD_DIM] = 1.0
    for h in range(n_mh):
        em[2 * n_sh + direction * n_mh + h, h * MLSTM_DV:(h + 1) * MLSTM_DV] = 1.0
    return jnp.asarray(e16, BF16), jnp.asarray(em, BF16)


def _mlstm_kernel(qk_ref, v_ref, small_ref, smallt_ref, em_ref, h_ref, cn_ref, m_ref, *, direction, n_heads, dk):
    s = pl.program_id(1)

    @pl.when(s == 0)
    def _():
        cn_ref[...] = jnp.zeros_like(cn_ref)
        m_ref[...] = jnp.zeros_like(m_ref)

    n_dt = 2 * (n_heads * MLSTM_DV // SSD_HEAD_DIM)
    lane0 = n_dt + direction * n_heads
    last = CHUNK - 1 if direction == 0 else 0
    lane = lax.broadcasted_iota(I32, (CHUNK, LANES), 1)
    lm = jnp.logical_and(lane >= lane0, lane < lane0 + n_heads)
    b_q = jnp.where(lm, small_ref[:, 0:LANES], 0.0)
    r_k = jnp.where(lm, small_ref[:, LANES:2 * LANES], 0.0)
    cmr = jnp.where(lm, small_ref[:, 2 * LANES:3 * LANES], 0.0)
    r_t = smallt_ref[0, 1]
    m_all = m_ref[...]
    m_row = m_all[0:1]
    mm = jnp.maximum(cmr, m_row)
    w_state = jnp.exp(m_row - mm)
    e_mq = jnp.exp(-(b_q + mm))
    m_base8 = jnp.maximum(m_all, _rows16(cmr[last:last + 1]))
    m_base = m_base8[0:1]
    w_k = jnp.where(lm, jnp.exp(r_k - m_base), 0.0)
    lane16 = lax.broadcasted_iota(I32, (PACK_ROWS, LANES), 1)
    lm16 = jnp.logical_and(lane16 >= lane0, lane16 < lane0 + n_heads)
    dec_rows = jnp.where(lm16, jnp.exp(m_all - m_base8), 0.0)
    wsx, emqx, wkx, decx = _spread(
        [jnp.where(lm, w_state, 0.0), jnp.where(lm, e_mq, 0.0), w_k, dec_rows], em_ref[...])
    decx = decx[0:1]
    m_ref[...] = jnp.where(lm16, _rows16(b_q[last:last + 1]) + m_base8, 0.0)

    iq = lax.broadcasted_iota(I32, (CHUNK, CHUNK), 0)
    ik = lax.broadcasted_iota(I32, (CHUNK, CHUNK), 1)
    mask = (iq >= ik) if direction == 0 else (iq <= ik)
    ones = jnp.ones((CHUNK, MLSTM_DV), BF16)
    d_qk = n_heads * dk
    for h in range(n_heads):
        qh = qk_ref[:, h * dk:(h + 1) * dk]
        kh = qk_ref[:, d_qk + h * dk:d_qk + (h + 1) * dk]
        vh = v_ref[:, h * MLSTM_DV:(h + 1) * MLSTM_DV]
        hs = slice(h * MLSTM_DV, (h + 1) * MLSTM_DV)
        mm_q = jnp.broadcast_to(mm[:, lane0 + h:lane0 + h + 1], (CHUNK, CHUNK))
        dmat = jnp.exp(jnp.where(mask, r_t[lane0 + h:lane0 + h + 1, :] - mm_q, NEG_INF))
        smat = (_dot_nt(qh, kh) * dmat).astype(BF16)
        cn = cn_ref[h]
        wsh = wsx[:, hs]
        tot = _dot(smat, jnp.concatenate([vh, ones], axis=1)) + jnp.concatenate([wsh, wsh], axis=1) * _dot(qh, cn.astype(BF16))
        num = tot[:, :MLSTM_DV]
        den = tot[:, MLSTM_DV:]
        h_ref[:, hs] = (num / jnp.maximum(jnp.abs(den), emqx[:, hs])).astype(h_ref.dtype)
        wkh = wkx[:, hs]
        rhs = jnp.concatenate([(vh.astype(F32) * wkh).astype(BF16), wkh.astype(BF16)], axis=1)
        dech = decx[:, hs]
        cn_ref[h] = jnp.concatenate([dech, dech], axis=1) * cn + _dot_tn(kh, rhs)


def _scans_kernel(*refs, n_sh, n_mh, dk):
    (xs0, bc0, qk0, v0, sm0, smt0, xs1, bc1, qk1, v1, sm1, smt1, e16_0, em_0, e16_1, em_1,
     y0, y1, h0, h1, st0, st1, cn0, cn1, m0, m1) = refs
    _ssd_kernel(xs0, bc0, sm0, smt0, e16_0, y0, st0, direction=0, n_heads=n_sh)
    _ssd_kernel(xs1, bc1, sm1, smt1, e16_1, y1, st1, direction=1, n_heads=n_sh)
    _mlstm_kernel(qk0, v0, sm0, smt0, em_0, h0, cn0, m0, direction=0, n_heads=n_mh, dk=dk)
    _mlstm_kernel(qk1, v1, sm1, smt1, em_1, h1, cn1, m1, direction=1, n_heads=n_mh, dk=dk)


def _scans(xs, bc, qk, v, small, smallt, *, batch, n_ctx_chunks, n_lat_chunks, dk):
    t_all, d = xs.shape
    n_sh = d // SSD_HEAD_DIM
    n_mh = d // MLSTM_DV
    n_steps = n_ctx_chunks + n_lat_chunks
    in_specs, args = [], []
    for direction in range(2):
        idx = _chunk_block_map(direction, batch, n_ctx_chunks, n_lat_chunks)
        rows = lambda b, s, idx=idx: (idx(b, s), 0)
        in_specs += [pl.BlockSpec((CHUNK, d), rows), pl.BlockSpec((CHUNK, bc.shape[1]), rows),
                     pl.BlockSpec((CHUNK, qk.shape[1]), rows), pl.BlockSpec((CHUNK, d), rows),
                     pl.BlockSpec((CHUNK, 3 * LANES), rows),
                     pl.BlockSpec((1, 2, LANES, CHUNK), lambda b, s, idx=idx: (idx(b, s), 0, 0, 0))]
        args += [xs, bc, qk, v, small, smallt]
    out_specs = []
    for direction in (0, 1, 0, 1):
        idx = _chunk_block_map(direction, batch, n_ctx_chunks, n_lat_chunks)
        out_specs.append(pl.BlockSpec((CHUNK, d), lambda b, s, idx=idx: (idx(b, s), 0)))
    for direction in range(2):
        sel = _selectors(direction, d)
        in_specs += [pl.BlockSpec(a.shape, lambda b, s: (0, 0)) for a in sel]
        args += list(sel)
    state = [pltpu.VMEM((SSD_STATE, d), F32)] * 2 + [pltpu.VMEM((n_mh, dk, 2 * MLSTM_DV), F32)] * 2 \
        + [pltpu.VMEM((PACK_ROWS, LANES), F32)] * 2
    return pl.pallas_call(
        functools.partial(_scans_kernel, n_sh=n_sh, n_mh=n_mh, dk=dk),
        out_shape=[jax.ShapeDtypeStruct((t_all, d), BF16)] * 4,
        grid=(batch, n_steps),
        in_specs=in_specs, out_specs=out_specs, scratch_shapes=state,
        compiler_params=pltpu.CompilerParams(dimension_semantics=("arbitrary", "arbitrary"),
                                             vmem_limit_bytes=VMEM_LIMIT),
        name="scans",
    )(*args)


def _merge_kernel(y0_ref, y1_ref, xs_ref, z_ref, h0_ref, h1_ref, og_ref, mg_ref, x_ref, mod_ref, dexp_ref, gs_ref,
                  gm_ref, gf_ref, wso_ref, wmo_ref, wo_ref, x1_ref, u2_ref, *, d_model):
    d = d_model
    m = mod_ref[0]
    y = y0_ref[...].astype(F32) + y1_ref[...].astype(F32) + dexp_ref[...] * xs_ref[...].astype(F32)
    zf = z_ref[...].astype(F32)
    y = y * (zf * _sigmoid(zf))
    y = y * lax.rsqrt(jnp.mean(y * y, axis=-1, keepdims=True) + EPS) * gs_ref[...]
    a = _dot(y.astype(BF16), wso_ref[...])
    hm = h0_ref[...].astype(F32) + h1_ref[...].astype(F32)
    parts = []
    for h in range(d // MLSTM_DV):
        blk = hm[:, h * MLSTM_DV:(h + 1) * MLSTM_DV]
        parts.append(blk * lax.rsqrt(jnp.mean(blk * blk, axis=-1, keepdims=True) + EPS))
    hn = jnp.concatenate(parts, axis=1) * gm_ref[...]
    hh = (og_ref[...].astype(F32) * hn).astype(BF16)
    bm = _dot(hh, wmo_ref[...])
    merged = mg_ref[:, :d].astype(F32) * a + mg_ref[:, d:].astype(F32) * bm
    r = _dot(merged.astype(BF16), wo_ref[...])
    x1 = x_ref[...] + m[2:3] * r
    x1_ref[...] = x1
    u2 = x1 * lax.rsqrt(jnp.mean(x1 * x1, axis=-1, keepdims=True) + EPS) * gf_ref[...] * (1.0 + m[4:5]) + m[3:4]
    u2_ref[...] = _pack_halves(u2)


def _merge(y0, y1, xs, z, h0, h1, og, mg, x2, mods3, dexp, gs, gm, gf, wso, wmo, wo, *, batch, seq, n_ctx_tok):
    t, d = x2.shape
    tm = TOKEN_TILE
    off = n_ctx_tok // tm
    tiles_per_batch = seq // tm
    lat = lambda i: (i + off, 0)
    row = lambda i: (i, 0)
    const = lambda i: (0, 0)
    kern = functools.partial(_merge_kernel, d_model=d)
    wspec = pl.BlockSpec((d, d), const, pipeline_mode=pl.Buffered(1))
    return pl.pallas_call(
        kern,
        out_shape=[jax.ShapeDtypeStruct((t, d), F32), jax.ShapeDtypeStruct((t, d // 2), U32)],
        grid=(t // tm,),
        in_specs=[pl.BlockSpec((tm, d), lat), pl.BlockSpec((tm, d), lat), pl.BlockSpec((tm, d), lat),
                  pl.BlockSpec((tm, d), lat), pl.BlockSpec((tm, d), lat), pl.BlockSpec((tm, d), lat),
                  pl.BlockSpec((tm, d), lat), pl.BlockSpec((tm, 2 * d), lat), pl.BlockSpec((tm, d), row),
                  pl.BlockSpec((1, 6, d), lambda i: (i // tiles_per_batch, 0, 0)),
                  pl.BlockSpec((1, d), const), pl.BlockSpec((1, d), const), pl.BlockSpec((1, d), const),
                  pl.BlockSpec((1, d), const), wspec, wspec, wspec],
        out_specs=[pl.BlockSpec((tm, d), row), pl.BlockSpec((tm, d // 2), row)],
        compiler_params=pltpu.CompilerParams(dimension_semantics=("arbitrary",), vmem_limit_bytes=VMEM_LIMIT),
        name="merge",
    )(y0, y1, xs, z, h0, h1, og, mg, x2, mods3, dexp, gs, gm, gf, wso, wmo, wo)


def _first_index_of_max(vals, row_iota, n_rows):
    mx = jnp.max(vals, axis=0, keepdims=True)
    idx = jnp.min(jnp.where(vals == mx, row_iota, n_rows), axis=0, keepdims=True)
    return mx, idx


def _router_kernel(u_ref, wt_ref, bias_ref, su_ref, idx_ref, pos_ref, wts_ref, cnt_ref, run_ref, *, n_experts, tr):
    i = pl.program_id(0)

    @pl.when(i == 0)
    def _():
        run_ref[...] = jnp.zeros_like(run_ref)

    ua, ub = _unpack_halves(u_ref[...])
    urow = jnp.concatenate([ua.astype(BF16), ub.astype(BF16)], axis=1)
    scores = _sigmoid(_dot_nt(wt_ref[...], urow))
    biased = scores + bias_ref[...]
    gsz = n_experts // N_EXPERT_GROUPS
    gi = lax.broadcasted_iota(I32, (gsz, tr), 0).astype(F32)
    gscores = []
    for g in range(N_EXPERT_GROUPS):
        blk = biased[g * gsz:(g + 1) * gsz]
        m1, i1 = _first_index_of_max(blk, gi, gsz)
        m2 = jnp.max(jnp.where(gi == i1, NEG_INF, blk), axis=0, keepdims=True)
        gscores.append(m1 + m2)
    gs = jnp.concatenate(gscores, axis=0)
    g8 = lax.broadcasted_iota(I32, (N_EXPERT_GROUPS, tr), 0).astype(F32)
    gsel = jnp.zeros((N_EXPERT_GROUPS, tr), F32)
    for _ in range(TOPK_GROUPS):
        _, gidx = _first_index_of_max(gs, g8, N_EXPERT_GROUPS)
        hit = g8 == gidx
        gsel = jnp.where(hit, 1.0, gsel)
        gs = jnp.where(hit, NEG_INF, gs)
    cand = jnp.concatenate(
        [jnp.where(jnp.broadcast_to(gsel[g:g + 1], (gsz, tr)) > 0.5, biased[g * gsz:(g + 1) * gsz], NEG_INF)
         for g in range(N_EXPERT_GROUPS)], axis=0)
    ei = lax.broadcasted_iota(I32, (n_experts, tr), 0).astype(F32)
    sel = jnp.zeros((n_experts, tr), F32)
    idxs, ws = [], []
    for _ in range(TOP_K):
        _, eidx = _first_index_of_max(cand, ei, n_experts)
        hit = ei == eidx
        ws.append(jnp.sum(jnp.where(hit, scores, 0.0), axis=0, keepdims=True))
        idxs.append(eidx)
        sel = jnp.where(hit, 1.0, sel)
        cand = jnp.where(hit, NEG_INF, cand)
    wk = jnp.concatenate(ws, axis=0)
    wts_ref[...] = ROUTED_SCALE * wk / jnp.sum(wk, axis=0, keepdims=True)
    idx_ref[...] = jnp.concatenate(idxs, axis=0).astype(I32)
    selb = sel.astype(BF16)
    posmat = _dot(selb, su_ref[...]) + run_ref[:, 0:1]
    pos_ref[...] = jnp.concatenate(
        [jnp.sum(jnp.where(ei == idxs[k], posmat, 0.0), axis=0, keepdims=True) for k in range(TOP_K)],
        axis=0).astype(I32)
    run = run_ref[...] + _dot(selb, jnp.ones((tr, LANES), BF16))
    run_ref[...] = run
    cnt_ref[...] = run


def _router(u2p, router_wt, bias_col):
    t = u2p.shape[0]
    d = router_wt.shape[1]
    n_experts = router_wt.shape[0]
    tr = TOKEN_TILE
    su = jnp.asarray(np.triu(np.ones((tr, tr), np.float32), 1), BF16)
    kern = functools.partial(_router_kernel, n_experts=n_experts, tr=tr)
    col = lambda i: (0, i)
    const = lambda i: (0, 0)
    return pl.pallas_call(
        kern,
        out_shape=[jax.ShapeDtypeStruct((TOP_K, t), I32), jax.ShapeDtypeStruct((TOP_K, t), I32),
                   jax.ShapeDtypeStruct((TOP_K, t), F32), jax.ShapeDtypeStruct((n_experts, LANES), F32)],
        grid=(t // tr,),
        in_specs=[pl.BlockSpec((tr, d // 2), lambda i: (i, 0)), pl.BlockSpec((n_experts, d), const),
                  pl.BlockSpec((n_experts, 1), const), pl.BlockSpec((tr, tr), const)],
        out_specs=[pl.BlockSpec((TOP_K, tr), col), pl.BlockSpec((TOP_K, tr), col), pl.BlockSpec((TOP_K, tr), col),
                   pl.BlockSpec((n_experts, LANES), const)],
        scratch_shapes=[pltpu.VMEM((n_experts, LANES), F32)],
        compiler_params=pltpu.CompilerParams(dimension_semantics=("arbitrary",), vmem_limit_bytes=VMEM_LIMIT),
        name="router",
    )(u2p, router_wt, bias_col, su)


def _slots_kernel(idx_ref, pos_ref, pstart_ref, dest_ref, *, n_experts, tr):
    ei = lax.broadcasted_iota(I32, (n_experts, tr), 0).astype(F32)
    pstart = pstart_ref[...]
    idx = idx_ref[...].astype(F32)
    rows = [jnp.sum(jnp.where(ei == idx[k:k + 1], pstart, 0.0), axis=0, keepdims=True) for k in range(TOP_K)]
    dest_ref[...] = jnp.concatenate(rows, axis=0).astype(I32) + pos_ref[...]


def _slots(idx, pos, pstart_col):
    t = idx.shape[1]
    n_experts = pstart_col.shape[0]
    tr = TOKEN_TILE
    col = lambda i: (0, i)
    return pl.pallas_call(
        functools.partial(_slots_kernel, n_experts=n_experts, tr=tr),
        out_shape=jax.ShapeDtypeStruct((TOP_K, t), I32),
        grid=(t // tr,),
        in_specs=[pl.BlockSpec((TOP_K, tr), col), pl.BlockSpec((TOP_K, tr), col),
                  pl.BlockSpec((n_experts, 1), lambda i: (0, 0))],
        out_specs=pl.BlockSpec((TOP_K, tr), col),
        compiler_params=pltpu.CompilerParams(dimension_semantics=("arbitrary",), vmem_limit_bytes=VMEM_LIMIT),
        name="slots",
    )(idx, pos, pstart_col)


PAD_BITS = tuple(1 << b for b in reversed(range((MOE_BLOCK - 1).bit_length())))


def _dispatch_kernel(pstart_ref, cnt_ref, dest_ref, u_ref, xs_ref, ubuf, zbuf, usem, sem, psem, *, td, n_steps,
                     n_experts):
    i = pl.program_id(0)
    slot = i % 2

    def u_copies(step, s):
        rows = pl.ds(pl.multiple_of(step * td, td), td)
        return [pltpu.make_async_copy(u_ref.at[rows, pl.ds(j * LANES, LANES)], ubuf.at[s, :, j, :], usem.at[s])
                for j in range(ROW_SUBLANES)]

    @pl.when(i == 0)
    def _():
        for cp in u_copies(i, 0):
            cp.start()
        zbuf[...] = jnp.zeros_like(zbuf)

        def pads(e, wait):
            cnt = cnt_ref[e]
            n_pad = (MOE_BLOCK - (cnt & (MOE_BLOCK - 1))) & (MOE_BLOCK - 1)
            base = pstart_ref[e] + cnt
            for bit in PAD_BITS:
                cp = pltpu.make_async_copy(zbuf.at[pl.ds(0, bit)], xs_ref.at[pl.ds(base, bit)], psem)
                has = (n_pad & bit) != 0

                @pl.when(has)
                def _():
                    if wait:
                        cp.wait()
                    else:
                        cp.start()

                base = base + jnp.where(has, bit, 0)

        def start_pads(e, carry):
            pads(e, False)
            return carry

        def wait_pads(e, carry):
            pads(e, True)
            return carry

        lax.fori_loop(0, n_experts, start_pads, 0)
        lax.fori_loop(0, n_experts, wait_pads, 0)

    @pl.when(i + 1 < n_steps)
    def _():
        for cp in u_copies(i + 1, 1 - slot):
            cp.start()

    for cp in u_copies(i, slot):
        cp.wait()

    def start(t, carry):
        for k in range(TOP_K):
            pltpu.make_async_copy(ubuf.at[slot, t], xs_ref.at[dest_ref[k, t]], sem).start(priority=k % 2)
        return carry

    def wait(t, carry):
        for k in range(TOP_K):
            pltpu.make_async_copy(ubuf.at[slot, 0], xs_ref.at[0], sem).wait()
        return carry

    lax.fori_loop(0, td, start, 0, unroll=4)
    lax.fori_loop(0, td, wait, 0, unroll=8)


def _dispatch(pad_start, counts, dest, u2p, n_slots):
    t, dh = u2p.shape
    td = DISPATCH_TILE
    n_experts = pad_start.shape[0]
    assert MOE_BLOCK & (MOE_BLOCK - 1) == 0 and dh == ROW_SUBLANES * LANES
    kern = functools.partial(_dispatch_kernel, td=td, n_steps=t // td, n_experts=n_experts)
    return pl.pallas_call(
        kern,
        out_shape=jax.ShapeDtypeStruct((n_slots, ROW_SUBLANES, LANES), U32),
        grid_spec=pltpu.PrefetchScalarGridSpec(
            num_scalar_prefetch=2, grid=(t // td,),
            in_specs=[pl.BlockSpec((TOP_K, td), lambda i, ps, cn: (0, i), memory_space=pltpu.SMEM),
                      pl.BlockSpec(memory_space=pl.ANY)],
            out_specs=pl.BlockSpec(memory_space=pl.ANY),
            scratch_shapes=[pltpu.VMEM((2, td, ROW_SUBLANES, LANES), U32),
                            pltpu.VMEM((MOE_BLOCK // 2, ROW_SUBLANES, LANES), U32),
                            pltpu.SemaphoreType.DMA((2,)), pltpu.SemaphoreType.DMA, pltpu.SemaphoreType.DMA]),
        compiler_params=pltpu.CompilerParams(dimension_semantics=("arbitrary",), vmem_limit_bytes=VMEM_LIMIT),
        name="dispatch",
    )(pad_start, counts, dest, u2p)


def _experts_kernel(bs_ref, nblk_ref, nu_ref, xs_ref, wg_ref, wu_ref, wd_ref, ys_ref, wgub, wdb, xbuf, ybuf,
                    semx, semy, *, n_experts):
    e = pl.program_id(0)
    n_used = nu_ref[0]

    def x_copies(g):
        rows = pl.ds(pl.multiple_of(g * MOE_BLOCK, MOE_BLOCK), MOE_BLOCK)
        return [pltpu.make_async_copy(xs_ref.at[rows, j, :], xbuf.at[g % X_BUFFERS, :, pl.ds(j * LANES, LANES)],
                                      semx.at[g % X_BUFFERS]) for j in range(ROW_SUBLANES)]

    def x_start(g):
        for cp in x_copies(g):
            cp.start()

    def x_wait(g):
        for cp in x_copies(g):
            cp.wait()

    def y_copies(g, slot):
        rows = pl.ds(pl.multiple_of(g * MOE_BLOCK, MOE_BLOCK), MOE_BLOCK)
        return [pltpu.make_async_copy(ybuf.at[slot, :, pl.ds(j * LANES, LANES)], ys_ref.at[rows, j, :], semy.at[slot])
                for j in range(ROW_SUBLANES)]

    def y_start(g, slot):
        for cp in y_copies(g, slot):
            cp.start()

    def y_wait(g, slot):
        for cp in y_copies(g, slot):
            cp.wait()

    @pl.when(e == 0)
    def _():
        for g in range(X_BUFFERS - 1):
            @pl.when(g < n_used)
            def _():
                x_start(jnp.int32(g))

    g0 = bs_ref[e]
    nb = nblk_ref[e]

    de = wg_ref.shape[2]

    @pl.when(nb > 0)
    def _():
        wgub[:, :de] = wg_ref[0].astype(BF16)
        wgub[:, de:] = wu_ref[0].astype(BF16)
        wdb[...] = wd_ref[0].astype(BF16)

    def block(g, carry):
        slot = g % 2

        @pl.when(g + X_BUFFERS - 1 < n_used)
        def _():
            x_start(g + X_BUFFERS - 1)

        x_wait(g)

        @pl.when(g >= 2)
        def _():
            y_wait(g - 2, slot)

        xa, xb = _unpack_halves(xbuf[g % X_BUFFERS])
        xrow = jnp.concatenate([xa.astype(BF16), xb.astype(BF16)], axis=1)
        hgu = _dot(xrow, wgub[...])
        hg = hgu[:, :de]
        hb = (hg * _sigmoid(hg) * hgu[:, de:]).astype(BF16)
        ybuf[slot] = _pack_halves(_dot(hb, wdb[...]))
        y_start(g, slot)
        return carry

    lax.fori_loop(g0, g0 + nb, block, 0)

    @pl.when(e == n_experts - 1)
    def _():
        @pl.when(n_used >= 2)
        def _():
            y_wait(n_used - 2, n_used % 2)

        @pl.when(n_used >= 1)
        def _():
            y_wait(n_used - 1, (n_used - 1) % 2)


def _experts(block_start, n_blocks, n_used, xsorted, wg, wu, wd):
    n_slots = xsorted.shape[0]
    dh = ROW_SUBLANES * LANES
    n_experts, d, de = wg.shape
    wmap = lambda e, bs, nb, nu: (e, 0, 0)
    assert xsorted.shape[1:] == (ROW_SUBLANES, LANES) and d == 2 * dh
    return pl.pallas_call(
        functools.partial(_experts_kernel, n_experts=n_experts),
        out_shape=jax.ShapeDtypeStruct((n_slots, ROW_SUBLANES, LANES), U32),
        grid_spec=pltpu.PrefetchScalarGridSpec(
            num_scalar_prefetch=3, grid=(n_experts,),
            in_specs=[pl.BlockSpec(memory_space=pl.ANY), pl.BlockSpec((1, d, de), wmap),
                      pl.BlockSpec((1, d, de), wmap), pl.BlockSpec((1, de, d), wmap)],
            out_specs=pl.BlockSpec(memory_space=pl.ANY),
            scratch_shapes=[pltpu.VMEM((d, 2 * de), BF16), pltpu.VMEM((de, d), BF16),
                            pltpu.VMEM((X_BUFFERS, MOE_BLOCK, dh), U32),
                            pltpu.VMEM((2, MOE_BLOCK, dh), U32),
                            pltpu.SemaphoreType.DMA((X_BUFFERS,)), pltpu.SemaphoreType.DMA((2,))]),
        compiler_params=pltpu.CompilerParams(dimension_semantics=("arbitrary",), vmem_limit_bytes=VMEM_LIMIT),
        name="experts",
    )(block_start, n_blocks, n_used, xsorted, wg, wu, wd)


def _combine_kernel(dest_ref, dnext_ref, ys_ref, wt_ref, x1_ref, u_ref, mod_ref, wsgu_ref, wsd_ref,
                    gfin_ref, o_ref, gbuf, accbuf, sem, *, tc, n_steps):
    i = pl.program_id(0)
    slot = i % 2

    def issue(dref, s):
        def body(t, carry):
            for k in range(TOP_K):
                pltpu.make_async_copy(ys_ref.at[dref[k, t]], gbuf.at[s, k, t], sem.at[s]).start(priority=k % 2)
            return carry
        lax.fori_loop(0, tc, body, 0, unroll=4)

    @pl.when(i == 0)
    def _():
        issue(dest_ref, 0)

    def wait(t, carry):
        for k in range(TOP_K):
            pltpu.make_async_copy(ys_ref.at[0], gbuf.at[slot, 0, 0], sem.at[slot]).wait()
        return carry

    lax.fori_loop(0, tc, wait, 0, unroll=8)

    def sum_and_issue(with_issue):
        def body(c, carry):
            t0 = c * GATHER_CHUNK
            if with_issue:
                for tt in range(GATHER_CHUNK):
                    for k in range(TOP_K):
                        pltpu.make_async_copy(ys_ref.at[dnext_ref[k, t0 + tt]], gbuf.at[1 - slot, k, t0 + tt],
                                              sem.at[1 - slot]).start(priority=k % 2)
            rows = pl.ds(t0, GATHER_CHUNK)
            wts = wt_ref[rows]
            sum_l = jnp.zeros((GATHER_CHUNK, ROW_SUBLANES, LANES), F32)
            sum_r = jnp.zeros((GATHER_CHUNK, ROW_SUBLANES, LANES), F32)
            for k in range(TOP_K):
                ga, gb = _unpack_halves(gbuf[slot, k, rows])
                wk = jnp.broadcast_to(wts[:, k:k + 1, :], (GATHER_CHUNK, ROW_SUBLANES, LANES))
                sum_l = sum_l + ga * wk
                sum_r = sum_r + gb * wk
            accbuf[0, rows] = sum_l
            accbuf[1, rows] = sum_r
            return carry
        lax.fori_loop(0, tc // GATHER_CHUNK, body, 0)

    @pl.when(i + 1 < n_steps)
    def _():
        sum_and_issue(True)

    @pl.when(i + 1 >= n_steps)
    def _():
        sum_and_issue(False)

    ua, ub = _unpack_halves(u_ref[...])
    half = ua.shape[1]
    hgu = _dot(jnp.concatenate([ua.astype(BF16), ub.astype(BF16)], axis=1), wsgu_ref[...])
    dsh = wsd_ref.shape[0]
    hg = hgu[:, :dsh]
    shared = _dot((hg * _sigmoid(hg) * hgu[:, dsh:]).astype(BF16), wsd_ref[...])
    acc_l = shared[:, :half]
    acc_r = shared[:, half:]
    acc_l = jnp.concatenate([acc_l[:, j * LANES:(j + 1) * LANES] + accbuf[0, :, j, :] for j in range(ROW_SUBLANES)],
                            axis=1)
    acc_r = jnp.concatenate([acc_r[:, j * LANES:(j + 1) * LANES] + accbuf[1, :, j, :] for j in range(ROW_SUBLANES)],
                            axis=1)
    m = mod_ref[0]
    xo_l = x1_ref[:, :half] + m[5:6, :half] * acc_l
    xo_r = x1_ref[:, half:] + m[5:6, half:] * acc_r
    ms = (jnp.sum(xo_l * xo_l, axis=-1, keepdims=True) + jnp.sum(xo_r * xo_r, axis=-1, keepdims=True)) / (2 * half)
    inv = lax.rsqrt(ms + EPS)
    o_ref[:, :half] = xo_l * inv * gfin_ref[:, :half]
    o_ref[:, half:] = xo_r * inv * gfin_ref[:, half:]


def _combine(dest, ysorted, wts_t, x1, u2p, mods3, wsgu, wsd, gfin, *, seq):
    t, d = x1.shape
    dh = d // 2
    tc = DISPATCH_TILE
    tiles_per_batch = seq // tc
    n_steps = t // tc
    kern = functools.partial(_combine_kernel, tc=tc, n_steps=n_steps)
    row = lambda i: (i, 0)
    const = lambda i: (0, 0)
    return pl.pallas_call(
        kern,
        out_shape=jax.ShapeDtypeStruct((t, d), F32),
        grid=(n_steps,),
        in_specs=[pl.BlockSpec((TOP_K, tc), lambda i: (0, i), memory_space=pltpu.SMEM),
                  pl.BlockSpec((TOP_K, tc), lambda i: (0, jnp.minimum(i + 1, n_steps - 1)), memory_space=pltpu.SMEM),
                  pl.BlockSpec(memory_space=pl.ANY), pl.BlockSpec((tc, TOP_K, LANES), lambda i: (i, 0, 0)),
                  pl.BlockSpec((tc, d), row), pl.BlockSpec((tc, dh), row),
                  pl.BlockSpec((1, 6, d), lambda i: (i // tiles_per_batch, 0, 0)),
                  pl.BlockSpec(wsgu.shape, const), pl.BlockSpec(wsd.shape, const), pl.BlockSpec((1, d), const)],
        out_specs=pl.BlockSpec((tc, d), row),
        scratch_shapes=[pltpu.VMEM((2, TOP_K, tc, ROW_SUBLANES, LANES), U32),
                        pltpu.VMEM((2, tc, ROW_SUBLANES, LANES), F32), pltpu.SemaphoreType.DMA((2,))],
        compiler_params=pltpu.CompilerParams(dimension_semantics=("arbitrary",), vmem_limit_bytes=VMEM_LIMIT),
        name="combine",
    )(dest, dest, ysorted, wts_t, x1, u2p, mods3, wsgu, wsd, gfin)


def kernel(x, c, ctx, c_ctx, ada_w, ada_b, norm_mix_g, norm_ffn_g, w_in, conv_xbc_w, conv_xbc_b, ssd_dt_bias, ssd_a_log, ssd_d, ssd_norm_g, conv_qk_w, conv_qk_b, mlstm_i_bias, mlstm_f_bias, mlstm_norm_g, w_ssd_out, w_mlstm_out, w_out, router_w, router_bias, moe_w_gate, moe_w_up, moe_w_down, shared_w_gate, shared_w_up, shared_w_down, norm_final_g):
    batch, seq, d = x.shape
    ctx_len = ctx.shape[1]
    depth = ada_w.shape[0]
    assert depth == 1, "only the single-layer configuration is implemented"
    assert seq % CHUNK == 0 and ctx_len % CHUNK == 0 and seq % GRID_W == 0
    l = 0
    n_sh = d // SSD_HEAD_DIM
    n_mh = d // MLSTM_DV
    dk = MLSTM_DV // 2
    d_xbc = d + 2 * SSD_GROUPS * SSD_STATE
    d_qk = 2 * n_mh * dk
    sizes = (d, d_xbc, 2 * n_sh, d_qk, d, 4 * n_mh, d, 2 * d)
    offs = np.concatenate([[0], np.cumsum(sizes)])
    assert offs[-1] == w_in.shape[2] and 2 * n_sh + 2 * n_mh <= LANES

    cond = jnp.concatenate([c, c_ctx[None], jnp.zeros((8 - (batch + 1) % 8, d), F32)], axis=0)
    mods = _adaln(cond, ada_w[l], ada_b[l])
    mods3 = mods.reshape(mods.shape[0], 6, d)

    w = w_in[l]
    seg = lambda k: w[:, offs[k]:offs[k + 1]]
    wbig = jnp.concatenate([seg(0), seg(1), seg(3), seg(4), seg(6), seg(7)], axis=1).astype(BF16)
    w_dt = seg(2)
    w_g = seg(5).reshape(d, 2, 2, n_mh)
    w_i = w_g[:, :, 0].reshape(d, 2 * n_mh)
    w_f = w_g[:, :, 1].reshape(d, 2 * n_mh)
    pad = jnp.zeros((d, LANES - 2 * n_sh - 2 * n_mh), F32)
    wsm = jnp.concatenate([w_dt, w_f, pad, w_dt, w_i, pad], axis=1)
    wsh = wsm.astype(BF16)
    wsl = (wsm - wsh.astype(F32)).astype(BF16)
    padb = jnp.zeros((LANES - 2 * n_sh - 2 * n_mh,), F32)
    dtb = ssd_dt_bias[l].reshape(-1).astype(F32)
    smb = jnp.concatenate([dtb, mlstm_f_bias[l].reshape(-1).astype(F32), padb,
                           dtb, mlstm_i_bias[l].reshape(-1).astype(F32), padb]).reshape(1, 2 * LANES)
    aneg = jnp.concatenate([-jnp.exp(ssd_a_log[l].astype(F32)).reshape(-1),
                            jnp.zeros((LANES - 2 * n_sh,), F32)]).reshape(1, LANES)

    x2 = x.reshape(batch * seq, d)
    ctx2 = ctx.reshape(batch * ctx_len, d)
    z, xs, bc, qk, v, og, mg, small, smallt = _inproj(
        x2, ctx2, mods3, norm_mix_g[l].reshape(1, d), wbig, wsh, wsl,
        conv_xbc_w[l], conv_xbc_b[l].reshape(1, d_xbc), conv_qk_w[l], conv_qk_b[l].reshape(1, d_qk), smb, aneg,
        batch=batch, seq=seq, ctx_len=ctx_len, dk=dk)

    ncc = ctx_len // CHUNK
    ncl = seq // CHUNK
    y0, y1, h0, h1 = _scans(xs, bc, qk, v, small, smallt, batch=batch, n_ctx_chunks=ncc, n_lat_chunks=ncl, dk=dk)

    dexp = jnp.repeat(ssd_d[l].astype(F32), SSD_HEAD_DIM).reshape(1, d)
    x1, u2 = _merge(y0, y1, xs, z, h0, h1, og, mg, x2, mods3, dexp, ssd_norm_g[l].reshape(1, d),
                    mlstm_norm_g[l].reshape(1, d), norm_ffn_g[l].reshape(1, d),
                    w_ssd_out[l].astype(BF16), w_mlstm_out[l].astype(BF16), w_out[l].astype(BF16),
                    batch=batch, seq=seq, n_ctx_tok=batch * ctx_len)

    n_experts = router_w.shape[2]
    idx, pos, wts, cnt = _router(u2, router_w[l].T.astype(BF16), router_bias[l].astype(F32).reshape(n_experts, 1))
    counts = cnt[:, 0].astype(I32)
    padded = (counts + MOE_BLOCK - 1) // MOE_BLOCK * MOE_BLOCK
    pad_end = jnp.cumsum(padded)
    pad_start = (pad_end - padded).astype(I32)
    t = batch * seq
    nb = t * TOP_K // MOE_BLOCK + n_experts
    n_used = (pad_end[-1] // MOE_BLOCK).astype(I32).reshape(1)

    dest = _slots(idx, pos, pad_start.astype(F32).reshape(n_experts, 1))
    xsorted = _dispatch(pad_start, counts, dest, u2, nb * MOE_BLOCK)
    ysorted = _experts(pad_start // MOE_BLOCK, (padded // MOE_BLOCK).astype(I32), n_used, xsorted,
                       moe_w_gate[l], moe_w_up[l], moe_w_down[l])
    wts_lanes = jnp.broadcast_to(wts.T[:, :, None], (t, TOP_K, LANES))
    out = _combine(dest, ysorted, wts_lanes, x1, u2, mods3,
                   jnp.concatenate([shared_w_gate[l], shared_w_up[l]], axis=1).astype(BF16), shared_w_down[l].astype(BF16),
                   norm_final_g.reshape(1, d), seq=seq)
    return out.reshape(batch, seq, d)
```

```python
import functools

import numpy as np
import jax
import jax.numpy as jnp
from jax import lax
from jax.experimental import pallas as pl
from jax.experimental.pallas import tpu as pltpu

F32 = jnp.float32
BF16 = jnp.bfloat16
I32 = jnp.int32
U32 = jnp.uint32

EPS = 1e-6
CHUNK = 128
CONV_K = 5
GRID_W = 64
SSD_HEAD_DIM = 64
SSD_STATE = 128
SSD_GROUPS = 2
MLSTM_DV = 128
N_EXPERT_GROUPS = 8
TOPK_GROUPS = 4
TOP_K = 8
ROUTED_SCALE = 2.5

LANES = 128
SUBLANES = 8
PACK_ROWS = 16
TOKEN_TILE = 512
COL_CHUNK = 512
MOE_BLOCK = 512
DISPATCH_TILE = 256
GATHER_CHUNK = 4
X_BUFFERS = 4
ROW_SUBLANES = 4
VMEM_LIMIT = 56 * 1024 * 1024
NEG_INF = float("-inf")


def _dot(a, b):
    return jnp.dot(a, b, preferred_element_type=F32)


def _dot_nt(a, b):
    return lax.dot_general(a, b, (((1,), (1,)), ((), ())), preferred_element_type=F32)


def _dot_tn(a, b):
    return lax.dot_general(a, b, (((0,), (0,)), ((), ())), preferred_element_type=F32)


def _spread(parts, e):
    res = _dot(jnp.concatenate(parts, axis=0).astype(BF16), e)
    out, r0 = [], 0
    for p in parts:
        out.append(res[r0:r0 + p.shape[0]])
        r0 += p.shape[0]
    return out


def _rows16(row):
    r8 = jnp.broadcast_to(row, (SUBLANES, row.shape[1]))
    return jnp.concatenate([r8, r8], axis=0)


def _sigmoid(v):
    return 1.0 / (1.0 + jnp.exp(-v))


def _pack_halves(v):
    n = v.shape[1] // 2
    hi = lax.bitcast_convert_type(v[:, :n].astype(BF16).astype(F32), U32)
    lo = lax.bitcast_convert_type(v[:, n:].astype(BF16).astype(F32), U32)
    return hi | (lo >> 16)


def _unpack_halves(p):
    left = lax.bitcast_convert_type(p & jnp.uint32(0xFFFF0000), F32)
    right = lax.bitcast_convert_type(p << 16, F32)
    return left, right


def _softplus(v):
    return jnp.maximum(v, 0.0) + jnp.log1p(jnp.exp(-jnp.abs(v)))


def _adaln_kernel(c_ref, w_ref, b_ref, o_ref):
    c = c_ref[...]
    s = c * _sigmoid(c)
    w = w_ref[...]
    s_hi = s.astype(BF16)
    s_lo = (s - s_hi.astype(F32)).astype(BF16)
    w_hi = w.astype(BF16)
    w_lo = (w - w_hi.astype(F32)).astype(BF16)
    o_ref[...] = _dot(s_hi, w_hi) + _dot(s_lo, w_hi) + _dot(s_hi, w_lo) + b_ref[...]


def _adaln(cond, w, b):
    rows, d = cond.shape
    n = w.shape[1]
    tn = 1536 if n % 1536 == 0 else n
    return pl.pallas_call(
        _adaln_kernel,
        out_shape=jax.ShapeDtypeStruct((rows, n), F32),
        grid=(n // tn,),
        in_specs=[pl.BlockSpec((rows, d), lambda j: (0, 0)),
                  pl.BlockSpec((d, tn), lambda j: (0, j)),
                  pl.BlockSpec((1, tn), lambda j: (0, j))],
        out_specs=pl.BlockSpec((rows, tn), lambda j: (0, j)),
        compiler_params=pltpu.CompilerParams(dimension_semantics=("arbitrary",), vmem_limit_bytes=VMEM_LIMIT),
        name="adaln",
    )(cond, w, b.reshape(1, n))


CONV_SHIFTS = tuple(j - CONV_K // 2 for j in range(CONV_K) if j != CONV_K // 2)


def _conv_masks(tm, seg_len):
    pos = np.arange(tm) % seg_len
    m = np.zeros((tm, 8), np.float32)
    for i, s in enumerate(CONV_SHIFTS):
        m[:, i] = ((pos + s >= 0) & (pos + s < seg_len)).astype(np.float32)
    return m


def _conv_silu(acc, w5, bias, vm, tm):
    out = acc * w5[CONV_K // 2:CONV_K // 2 + 1] + bias
    for i, s in enumerate(CONV_SHIFTS):
        shifted = pltpu.roll(acc, (-s) % tm, axis=0)
        j = s + CONV_K // 2
        out = out + (shifted * vm[:, i:i + 1]) * w5[j:j + 1]
    return out * _sigmoid(out)


def _inproj_kernel(x_ref, ctx_ref, mod_ref, g_ref, wbig_ref, wsh_ref, wsl_ref, cwx_ref, cbx_ref, cwq_ref, cbq_ref,
                   smb_ref, aneg_ref, tril_ref, triu_ref, cmask_ref,
                   z_ref, xs_ref, bc_ref, qk_ref, v_ref, og_ref, mg_ref, small_ref, smallt_ref,
                   *, n_ctx_tiles, tm, d_model, dk):
    i = pl.program_id(0)
    is_ctx = i < n_ctx_tiles
    xt = jnp.where(is_ctx, ctx_ref[...], x_ref[...])
    m = mod_ref[0]
    ms = jnp.mean(xt * xt, axis=-1, keepdims=True)
    u = xt * lax.rsqrt(ms + EPS) * g_ref[...] * (1.0 + m[1:2]) + m[0:1]
    u_hi = u.astype(BF16)
    u_lo = (u - u_hi.astype(F32)).astype(BF16)
    vm = jnp.where(is_ctx, cmask_ref[1], cmask_ref[0])

    d = d_model
    d_bc = 2 * SSD_GROUPS * SSD_STATE
    pieces = [(z_ref, d, "plain", None), (xs_ref, d, "convx", 0), (bc_ref, d_bc, "convx", d),
              (qk_ref, d, "convq", 0), (v_ref, d, "plain", None), (og_ref, d, "sigmoid", None),
              (mg_ref, 2 * d, "sigmoid", None)]
    col = 0
    for ref, width, kind, coff in pieces:
        for c0 in range(0, width, COL_CHUNK):
            acc = _dot(u_hi, wbig_ref[:, col + c0:col + c0 + COL_CHUNK])
            if kind == "convx":
                cs = coff + c0
                acc = _conv_silu(acc, cwx_ref[:, cs:cs + COL_CHUNK], cbx_ref[:, cs:cs + COL_CHUNK], vm, tm)
            elif kind == "convq":
                acc = _conv_silu(acc, cwq_ref[:, c0:c0 + COL_CHUNK], cbq_ref[:, c0:c0 + COL_CHUNK], vm, tm)
                if c0 < width // 2:
                    acc = acc * (dk ** -0.5)
            elif kind == "sigmoid":
                acc = _sigmoid(acc)
            ref[:, c0:c0 + COL_CHUNK] = acc.astype(ref.dtype)
        col += width

    wsh = wsh_ref[...]
    raw = _dot(u_hi, wsh) + _dot(u_lo, wsh) + _dot(u_hi, wsl_ref[...]) + smb_ref[...]
    p1 = raw[:, :LANES]
    p2 = raw[:, LANES:]
    lane = lax.broadcasted_iota(I32, (tm, LANES), 1)
    n_dt = 2 * (d_model // SSD_HEAD_DIM)
    n_g = 2 * (d_model // MLSTM_DV)
    is_dt = lane < n_dt
    is_gate = jnp.logical_and(lane >= n_dt, lane < n_dt + n_g)
    dt = _softplus(p2)
    pa = jnp.where(is_dt, dt * aneg_ref[...], jnp.where(is_gate, -_softplus(-p1), 0.0))
    pb = jnp.where(is_dt, dt, jnp.where(is_gate, p2, 0.0))
    lane_c = lax.broadcasted_iota(I32, (CHUNK, LANES), 1)
    is_dt_c = lane_c < n_dt
    rev = jnp.logical_or(jnp.logical_and(lane_c >= n_dt // 2, lane_c < n_dt),
                         jnp.logical_and(lane_c >= n_dt + n_g // 2, lane_c < n_dt + n_g))
    tril = tril_ref[...]
    triu = triu_ref[...]
    tq = lax.broadcasted_iota(I32, (CHUNK, LANES), 0)
    for c in range(tm // CHUNK):
        r0 = c * CHUNK
        a_c = pa[r0:r0 + CHUNK]
        hi = a_c.astype(BF16)
        r1 = a_c - hi.astype(F32)
        mid = r1.astype(BF16)
        lo = (r1 - mid.astype(F32)).astype(BF16)
        cs_f = _dot(tril, hi) + _dot(tril, mid) + _dot(tril, lo)
        cs_b = _dot(triu, hi) + _dot(triu, mid) + _dot(triu, lo)
        plane_a = jnp.where(rev, cs_b, cs_f)
        plane_b = jnp.where(is_dt_c, pb[r0:r0 + CHUNK], pb[r0:r0 + CHUNK] - plane_a)
        yf = plane_b
        yb = plane_b
        s = 1
        while s < CHUNK:
            sh = pltpu.roll(yf, s, axis=0)
            yf = jnp.maximum(yf, jnp.where(tq >= s, sh, NEG_INF))
            sh = pltpu.roll(yb, CHUNK - s, axis=0)
            yb = jnp.maximum(yb, jnp.where(tq + s < CHUNK, sh, NEG_INF))
            s *= 2
        plane_c = jnp.where(rev, yb, yf)
        small_ref[r0:r0 + CHUNK, 0:LANES] = plane_a
        small_ref[r0:r0 + CHUNK, LANES:2 * LANES] = plane_b
        small_ref[r0:r0 + CHUNK, 2 * LANES:3 * LANES] = plane_c
        smallt_ref[c, 0] = plane_a.T
        smallt_ref[c, 1] = plane_b.T


def _inproj(x2, ctx2, mods3, g, wbig, wsh, wsl, cwx, cbx, cwq, cbq, smb, aneg, *, batch, seq, ctx_len, dk):
    d = x2.shape[1]
    tm = TOKEN_TILE
    n_ctx_tok = batch * ctx_len
    assert n_ctx_tok % tm == 0 and seq % tm == 0 and tm % ctx_len == 0 and tm % GRID_W == 0
    n_ctx_tiles = n_ctx_tok // tm
    tiles_per_batch = seq // tm
    n_tiles = n_ctx_tiles + batch * tiles_per_batch
    t_all = n_tiles * tm
    n_big = wbig.shape[1]
    tril = jnp.asarray(np.tril(np.ones((CHUNK, CHUNK), np.float32)), BF16)
    triu = jnp.asarray(np.triu(np.ones((CHUNK, CHUNK), np.float32)), BF16)

    def x_map(i):
        return (jnp.maximum(i - n_ctx_tiles, 0), 0)

    def ctx_map(i):
        return (jnp.minimum(i, n_ctx_tiles - 1), 0)

    def mod_map(i):
        return (jnp.where(i < n_ctx_tiles, batch, jnp.maximum(i - n_ctx_tiles, 0) // tiles_per_batch), 0, 0)

    const = lambda i: (0, 0)
    row = lambda i: (i, 0)
    cmask = jnp.asarray(np.stack([_conv_masks(tm, GRID_W), _conv_masks(tm, ctx_len)]))
    kern = functools.partial(_inproj_kernel, n_ctx_tiles=n_ctx_tiles, tm=tm, d_model=d, dk=dk)
    d_bc = 2 * SSD_GROUPS * SSD_STATE
    outs = [jax.ShapeDtypeStruct((t_all, d), BF16), jax.ShapeDtypeStruct((t_all, d), BF16),
            jax.ShapeDtypeStruct((t_all, d_bc), BF16), jax.ShapeDtypeStruct((t_all, d), BF16),
            jax.ShapeDtypeStruct((t_all, d), BF16), jax.ShapeDtypeStruct((t_all, d), BF16),
            jax.ShapeDtypeStruct((t_all, 2 * d), BF16), jax.ShapeDtypeStruct((t_all, 3 * LANES), F32),
            jax.ShapeDtypeStruct((t_all // CHUNK, 2, LANES, CHUNK), F32)]
    out_specs = [pl.BlockSpec((tm, d), row), pl.BlockSpec((tm, d), row), pl.BlockSpec((tm, d_bc), row),
                 pl.BlockSpec((tm, d), row), pl.BlockSpec((tm, d), row), pl.BlockSpec((tm, d), row),
                 pl.BlockSpec((tm, 2 * d), row), pl.BlockSpec((tm, 3 * LANES), row),
                 pl.BlockSpec((tm // CHUNK, 2, LANES, CHUNK), lambda i: (i, 0, 0, 0))]
    in_specs = [pl.BlockSpec((tm, d), x_map), pl.BlockSpec((tm, d), ctx_map),
                pl.BlockSpec((1, 6, d), mod_map), pl.BlockSpec((1, d), const),
                pl.BlockSpec((d, n_big), const, pipeline_mode=pl.Buffered(1)),
                pl.BlockSpec((d, 2 * LANES), const), pl.BlockSpec((d, 2 * LANES), const),
                pl.BlockSpec(cwx.shape, const), pl.BlockSpec(cbx.shape, const),
                pl.BlockSpec(cwq.shape, const), pl.BlockSpec(cbq.shape, const),
                pl.BlockSpec((1, 2 * LANES), const), pl.BlockSpec((1, LANES), const),
                pl.BlockSpec((CHUNK, CHUNK), const), pl.BlockSpec((CHUNK, CHUNK), const),
                pl.BlockSpec((2, tm, 8), lambda i: (0, 0, 0))]
    return pl.pallas_call(
        kern, out_shape=outs, grid=(n_tiles,), in_specs=in_specs, out_specs=out_specs,
        compiler_params=pltpu.CompilerParams(dimension_semantics=("arbitrary",), vmem_limit_bytes=VMEM_LIMIT),
        name="inproj",
    )(x2, ctx2, mods3, g, wbig, wsh, wsl, cwx, cbx, cwq, cbq, smb, aneg, tril, triu, cmask)


def _chunk_block_map(direction, batch, n_ctx_chunks, n_lat_chunks):
    def idx(b, s):
        if direction == 0:
            c_ctx = s
            c_lat = s - n_ctx_chunks
        else:
            c_ctx = n_ctx_chunks - 1 - s
            c_lat = n_lat_chunks - 1 - (s - n_ctx_chunks)
        return jnp.where(s < n_ctx_chunks, b * n_ctx_chunks + c_ctx, batch * n_ctx_chunks + b * n_lat_chunks + c_lat)
    return idx


def _ssd_kernel(xs_ref, bc_ref, small_ref, smallt_ref, e16_ref, y_ref, st_ref, *, direction, n_heads):
    s = pl.program_id(1)

    @pl.when(s == 0)
    def _():
        st_ref[...] = jnp.zeros_like(st_ref)

    hpg = n_heads // SSD_GROUPS
    lane0 = direction * n_heads
    last = CHUNK - 1 if direction == 0 else 0
    lane = lax.broadcasted_iota(I32, (CHUNK, LANES), 1)
    lm = jnp.logical_and(lane >= lane0, lane < lane0 + n_heads)
    plane_a = jnp.where(lm, small_ref[:, 0:LANES], 0.0)
    plane_b = jnp.where(lm, small_ref[:, LANES:2 * LANES], 0.0)
    cum_t = smallt_ref[0, 0]
    tot = plane_a[last:last + 1]
    lane16 = lax.broadcasted_iota(I32, (PACK_ROWS, LANES), 1)
    lm16 = jnp.logical_and(lane16 >= lane0, lane16 < lane0 + n_heads)
    dec_rows = jnp.where(lm16, jnp.exp(_rows16(tot)), 0.0)
    dtx, ecx, wx, decx = _spread(
        [plane_b, jnp.where(lm, jnp.exp(plane_a), 0.0), plane_b * jnp.exp(tot - plane_a), dec_rows], e16_ref[...])
    decx = decx[0:1]

    xf = xs_ref[...].astype(F32)
    xdt = xf * dtx
    xw = (xf * wx).astype(BF16)
    plane64 = lax.broadcasted_iota(I32, (CHUNK, 2 * SSD_HEAD_DIM), 1) < SSD_HEAD_DIM
    iq = lax.broadcasted_iota(I32, (CHUNK, CHUNK), 0)
    ik = lax.broadcasted_iota(I32, (CHUNK, CHUNK), 1)
    mask = (iq >= ik) if direction == 0 else (iq <= ik)
    gw = hpg * SSD_HEAD_DIM
    for g in range(SSD_GROUPS):
        bg = bc_ref[:, g * SSD_STATE:(g + 1) * SSD_STATE]
        cg = bc_ref[:, (SSD_GROUPS + g) * SSD_STATE:(SSD_GROUPS + g + 1) * SSD_STATE]
        cb = _dot_nt(cg, bg)
        sg = st_ref[:, g * gw:(g + 1) * gw]
        y_inter = _dot(cg, sg.astype(BF16)) * ecx[:, g * gw:(g + 1) * gw]
        for pair in range(hpg // 2):
            h0 = g * hpg + 2 * pair
            c0 = h0 * SSD_HEAD_DIM
            xpair = xdt[:, c0:c0 + 2 * SSD_HEAD_DIM]
            acc = y_inter[:, 2 * pair * SSD_HEAD_DIM:(2 * pair + 2) * SSD_HEAD_DIM]
            for par in range(2):
                h = h0 + par
                cum_q = jnp.broadcast_to(plane_a[:, lane0 + h:lane0 + h + 1], (CHUNK, CHUNK))
                seg = cum_q - cum_t[lane0 + h:lane0 + h + 1, :]
                m_h = (cb * jnp.exp(jnp.where(mask, seg, NEG_INF))).astype(BF16)
                keep = plane64 if par == 0 else jnp.logical_not(plane64)
                acc = acc + _dot(m_h, jnp.where(keep, xpair, 0.0).astype(BF16))
            y_ref[:, c0:c0 + 2 * SSD_HEAD_DIM] = acc.astype(y_ref.dtype)
        st_ref[:, g * gw:(g + 1) * gw] = decx[:, g * gw:(g + 1) * gw] * sg + _dot_tn(bg, xw[:, g * gw:(g + 1) * gw])


def _selectors(direction, d):
    n_sh = d // SSD_HEAD_DIM
    n_mh = d // MLSTM_DV
    e16 = np.zeros((LANES, d), np.float32)
    em = np.zeros((LANES, d), np.float32)
    for h in range(n_sh):
        e16[direction * n_sh + h, h * SSD_HEAD_DIM:(h + 1) * SSD_HEAD_DIM] = 1.0
    for h in range(n_mh):
        em[2 * n_sh + direction * n_mh + h, h * MLSTM_DV:(h + 1) * MLSTM_DV] = 1.0
    return jnp.asarray(e16, BF16), jnp.asarray(em, BF16)


def _mlstm_kernel(qk_ref, v_ref, small_ref, smallt_ref, em_ref, h_ref, cn_ref, m_ref, *, direction, n_heads, dk):
    s = pl.program_id(1)

    @pl.when(s == 0)
    def _():
        cn_ref[...] = jnp.zeros_like(cn_ref)
        m_ref[...] = jnp.zeros_like(m_ref)

    n_dt = 2 * (n_heads * MLSTM_DV // SSD_HEAD_DIM)
    lane0 = n_dt + direction * n_heads
    last = CHUNK - 1 if direction == 0 else 0
    lane = lax.broadcasted_iota(I32, (CHUNK, LANES), 1)
    lm = jnp.logical_and(lane >= lane0, lane < lane0 + n_heads)
    b_q = jnp.where(lm, small_ref[:, 0:LANES], 0.0)
    r_k = jnp.where(lm, small_ref[:, LANES:2 * LANES], 0.0)
    cmr = jnp.where(lm, small_ref[:, 2 * LANES:3 * LANES], 0.0)
    r_t = smallt_ref[0, 1]
    m_all = m_ref[...]
    m_row = m_all[0:1]
    mm = jnp.maximum(cmr, m_row)
    w_state = jnp.exp(m_row - mm)
    e_mq = jnp.exp(-(b_q + mm))
    m_base8 = jnp.maximum(m_all, _rows16(cmr[last:last + 1]))
    m_base = m_base8[0:1]
    w_k = jnp.where(lm, jnp.exp(r_k - m_base), 0.0)
    lane16 = lax.broadcasted_iota(I32, (PACK_ROWS, LANES), 1)
    lm16 = jnp.logical_and(lane16 >= lane0, lane16 < lane0 + n_heads)
    dec_rows = jnp.where(lm16, jnp.exp(m_all - m_base8), 0.0)
    wsx, emqx, wkx, decx = _spread(
        [jnp.where(lm, w_state, 0.0), jnp.where(lm, e_mq, 0.0), w_k, dec_rows], em_ref[...])
    decx = decx[0:1]
    m_ref[...] = jnp.where(lm16, _rows16(b_q[last:last + 1]) + m_base8, 0.0)

    iq = lax.broadcasted_iota(I32, (CHUNK, CHUNK), 0)
    ik = lax.broadcasted_iota(I32, (CHUNK, CHUNK), 1)
    mask = (iq >= ik) if direction == 0 else (iq <= ik)
    ones = jnp.ones((CHUNK, MLSTM_DV), BF16)
    d_qk = n_heads * dk
    for h in range(n_heads):
        qh = qk_ref[:, h * dk:(h + 1) * dk]
        kh = qk_ref[:, d_qk + h * dk:d_qk + (h + 1) * dk]
        vh = v_ref[:, h * MLSTM_DV:(h + 1) * MLSTM_DV]
        hs = slice(h * MLSTM_DV, (h + 1) * MLSTM_DV)
        mm_q = jnp.broadcast_to(mm[:, lane0 + h:lane0 + h + 1], (CHUNK, CHUNK))
        dmat = jnp.exp(jnp.where(mask, r_t[lane0 + h:lane0 + h + 1, :] - mm_q, NEG_INF))
        smat = (_dot_nt(qh, kh) * dmat).astype(BF16)
        cn = cn_ref[h]
        wsh = wsx[:, hs]
        tot = _dot(smat, jnp.concatenate([vh, ones], axis=1)) + jnp.concatenate([wsh, wsh], axis=1) * _dot(qh, cn.astype(BF16))
        num = tot[:, :MLSTM_DV]
        den = tot[:, MLSTM_DV:]
        h_ref[:, hs] = (num / jnp.maximum(jnp.abs(den), emqx[:, hs])).astype(h_ref.dtype)
        wkh = wkx[:, hs]
        rhs = jnp.concatenate([(vh.astype(F32) * wkh).astype(BF16), wkh.astype(BF16)], axis=1)
        dech = decx[:, hs]
        cn_ref[h] = jnp.concatenate([dech, dech], axis=1) * cn + _dot_tn(kh, rhs)


def _scans_kernel(*refs, n_sh, n_mh, dk):
    (xs0, bc0, qk0, v0, sm0, smt0, xs1, bc1, qk1, v1, sm1, smt1, e16_0, em_0, e16_1, em_1,
     y0, y1, h0, h1, st0, st1, cn0, cn1, m0, m1) = refs
    _ssd_kernel(xs0, bc0, sm0, smt0, e16_0, y0, st0, direction=0, n_heads=n_sh)
    _ssd_kernel(xs1, bc1, sm1, smt1, e16_1, y1, st1, direction=1, n_heads=n_sh)
    _mlstm_kernel(qk0, v0, sm0, smt0, em_0, h0, cn0, m0, direction=0, n_heads=n_mh, dk=dk)
    _mlstm_kernel(qk1, v1, sm1, smt1, em_1, h1, cn1, m1, direction=1, n_heads=n_mh, dk=dk)


def _scans(xs, bc, qk, v, small, smallt, *, batch, n_ctx_chunks, n_lat_chunks, dk):
    t_all, d = xs.shape
    n_sh = d // SSD_HEAD_DIM
    n_mh = d // MLSTM_DV
    n_steps = n_ctx_chunks + n_lat_chunks
    in_specs, args = [], []
    for direction in range(2):
        idx = _chunk_block_map(direction, batch, n_ctx_chunks, n_lat_chunks)
        rows = lambda b, s, idx=idx: (idx(b, s), 0)
        in_specs += [pl.BlockSpec((CHUNK, d), rows), pl.BlockSpec((CHUNK, bc.shape[1]), rows),
                     pl.BlockSpec((CHUNK, qk.shape[1]), rows), pl.BlockSpec((CHUNK, d), rows),
                     pl.BlockSpec((CHUNK, 3 * LANES), rows),
                     pl.BlockSpec((1, 2, LANES, CHUNK), lambda b, s, idx=idx: (idx(b, s), 0, 0, 0))]
        args += [xs, bc, qk, v, small, smallt]
    out_specs = []
    for direction in (0, 1, 0, 1):
        idx = _chunk_block_map(direction, batch, n_ctx_chunks, n_lat_chunks)
        out_specs.append(pl.BlockSpec((CHUNK, d), lambda b, s, idx=idx: (idx(b, s), 0)))
    for direction in range(2):
        sel = _selectors(direction, d)
        in_specs += [pl.BlockSpec(a.shape, lambda b, s: (0, 0)) for a in sel]
        args += list(sel)
    state = [pltpu.VMEM((SSD_STATE, d), F32)] * 2 + [pltpu.VMEM((n_mh, dk, 2 * MLSTM_DV), F32)] * 2 \
        + [pltpu.VMEM((PACK_ROWS, LANES), F32)] * 2
    return pl.pallas_call(
        functools.partial(_scans_kernel, n_sh=n_sh, n_mh=n_mh, dk=dk),
        out_shape=[jax.ShapeDtypeStruct((t_all, d), BF16)] * 4,
        grid=(batch, n_steps),
        in_specs=in_specs, out_specs=out_specs, scratch_shapes=state,
        compiler_params=pltpu.CompilerParams(dimension_semantics=("arbitrary", "arbitrary"),
                                             vmem_limit_bytes=VMEM_LIMIT),
        name="scans",
    )(*args)


def _merge_kernel(y0_ref, y1_ref, xs_ref, z_ref, h0_ref, h1_ref, og_ref, mg_ref, x_ref, mod_ref, dexp_ref, gs_ref,
                  gm_ref, gf_ref, wso_ref, wmo_ref, wo_ref, x1_ref, u2_ref, *, d_model):
    d = d_model
    m = mod_ref[0]
    y = y0_ref[...].astype(F32) + y1_ref[...].astype(F32) + dexp_ref[...] * xs_ref[...].astype(F32)
    zf = z_ref[...].astype(F32)
    y = y * (zf * _sigmoid(zf))
    y = y * lax.rsqrt(jnp.mean(y * y, axis=-1, keepdims=True) + EPS) * gs_ref[...]
    a = _dot(y.astype(BF16), wso_ref[...])
    hm = h0_ref[...].astype(F32) + h1_ref[...].astype(F32)
    parts = []
    for h in range(d // MLSTM_DV):
        blk = hm[:, h * MLSTM_DV:(h + 1) * MLSTM_DV]
        parts.append(blk * lax.rsqrt(jnp.mean(blk * blk, axis=-1, keepdims=True) + EPS))
    hn = jnp.concatenate(parts, axis=1) * gm_ref[...]
    hh = (og_ref[...].astype(F32) * hn).astype(BF16)
    bm = _dot(hh, wmo_ref[...])
    merged = mg_ref[:, :d].astype(F32) * a + mg_ref[:, d:].astype(F32) * bm
    r = _dot(merged.astype(BF16), wo_ref[...])
    x1 = x_ref[...] + m[2:3] * r
    x1_ref[...] = x1
    u2 = x1 * lax.rsqrt(jnp.mean(x1 * x1, axis=-1, keepdims=True) + EPS) * gf_ref[...] * (1.0 + m[4:5]) + m[3:4]
    u2_ref[...] = _pack_halves(u2)


def _merge(y0, y1, xs, z, h0, h1, og, mg, x2, mods3, dexp, gs, gm, gf, wso, wmo, wo, *, batch, seq, n_ctx_tok):
    t, d = x2.shape
    tm = TOKEN_TILE
    off = n_ctx_tok // tm
    tiles_per_batch = seq // tm
    lat = lambda i: (i + off, 0)
    row = lambda i: (i, 0)
    const = lambda i: (0, 0)
    kern = functools.partial(_merge_kernel, d_model=d)
    wspec = pl.BlockSpec((d, d), const, pipeline_mode=pl.Buffered(1))
    return pl.pallas_call(
        kern,
        out_shape=[jax.ShapeDtypeStruct((t, d), F32), jax.ShapeDtypeStruct((t, d // 2), U32)],
        grid=(t // tm,),
        in_specs=[pl.BlockSpec((tm, d), lat), pl.BlockSpec((tm, d), lat), pl.BlockSpec((tm, d), lat),
                  pl.BlockSpec((tm, d), lat), pl.BlockSpec((tm, d), lat), pl.BlockSpec((tm, d), lat),
                  pl.BlockSpec((tm, d), lat), pl.BlockSpec((tm, 2 * d), lat), pl.BlockSpec((tm, d), row),
                  pl.BlockSpec((1, 6, d), lambda i: (i // tiles_per_batch, 0, 0)),
                  pl.BlockSpec((1, d), const), pl.BlockSpec((1, d), const), pl.BlockSpec((1, d), const),
                  pl.BlockSpec((1, d), const), wspec, wspec, wspec],
        out_specs=[pl.BlockSpec((tm, d), row), pl.BlockSpec((tm, d // 2), row)],
        compiler_params=pltpu.CompilerParams(dimension_semantics=("arbitrary",), vmem_limit_bytes=VMEM_LIMIT),
        name="merge",
    )(y0, y1, xs, z, h0, h1, og, mg, x2, mods3, dexp, gs, gm, gf, wso, wmo, wo)


def _first_index_of_max(vals, row_iota, n_rows):
    mx = jnp.max(vals, axis=0, keepdims=True)
    idx = jnp.min(jnp.where(vals == mx, row_iota, n_rows), axis=0, keepdims=True)
    return mx, idx


def _router_kernel(u_ref, wt_ref, bias_ref, su_ref, idx_ref, pos_ref, wts_ref, cnt_ref, run_ref, *, n_experts, tr):
    i = pl.program_id(0)

    @pl.when(i == 0)
    def _():
        run_ref[...] = jnp.zeros_like(run_ref)

    ua, ub = _unpack_halves(u_ref[...])
    urow = jnp.concatenate([ua.astype(BF16), ub.astype(BF16)], axis=1)
    scores = _sigmoid(_dot_nt(wt_ref[...], urow))
    biased = scores + bias_ref[...]
    gsz = n_experts // N_EXPERT_GROUPS
    gi = lax.broadcasted_iota(I32, (gsz, tr), 0).astype(F32)
    gscores = []
    for g in range(N_EXPERT_GROUPS):
        blk = biased[g * gsz:(g + 1) * gsz]
        m1, i1 = _first_index_of_max(blk, gi, gsz)
        m2 = jnp.max(jnp.where(gi == i1, NEG_INF, blk), axis=0, keepdims=True)
        gscores.append(m1 + m2)
    gs = jnp.concatenate(gscores, axis=0)
    g8 = lax.broadcasted_iota(I32, (N_EXPERT_GROUPS, tr), 0).astype(F32)
    gsel = jnp.zeros((N_EXPERT_GROUPS, tr), F32)
    for _ in range(TOPK_GROUPS):
        _, gidx = _first_index_of_max(gs, g8, N_EXPERT_GROUPS)
        hit = g8 == gidx
        gsel = jnp.where(hit, 1.0, gsel)
        gs = jnp.where(hit, NEG_INF, gs)
    cand = jnp.concatenate(
        [jnp.where(jnp.broadcast_to(gsel[g:g + 1], (gsz, tr)) > 0.5, biased[g * gsz:(g + 1) * gsz], NEG_INF)
         for g in range(N_EXPERT_GROUPS)], axis=0)
    ei = lax.broadcasted_iota(I32, (n_experts, tr), 0).astype(F32)
    sel = jnp.zeros((n_experts, tr), F32)
    idxs, ws = [], []
    for _ in range(TOP_K):
        _, eidx = _first_index_of_max(cand, ei, n_experts)
        hit = ei == eidx
        ws.append(jnp.sum(jnp.where(hit, scores, 0.0), axis=0, keepdims=True))
        idxs.append(eidx)
        sel = jnp.where(hit, 1.0, sel)
        cand = jnp.where(hit, NEG_INF, cand)
    wk = jnp.concatenate(ws, axis=0)
    wts_ref[...] = ROUTED_SCALE * wk / jnp.sum(wk, axis=0, keepdims=True)
    idx_ref[...] = jnp.concatenate(idxs, axis=0).astype(I32)
    selb = sel.astype(BF16)
    posmat = _dot(selb, su_ref[...]) + run_ref[:, 0:1]
    pos_ref[...] = jnp.concatenate(
        [jnp.sum(jnp.where(ei == idxs[k], posmat, 0.0), axis=0, keepdims=True) for k in range(TOP_K)],
        axis=0).astype(I32)
    run = run_ref[...] + _dot(selb, jnp.ones((tr, LANES), BF16))
    run_ref[...] = run
    cnt_ref[...] = run


def _router(u2p, router_wt, bias_col):
    t = u2p.shape[0]
    d = router_wt.shape[1]
    n_experts = router_wt.shape[0]
    tr = TOKEN_TILE
    su = jnp.asarray(np.triu(np.ones((tr, tr), np.float32), 1), BF16)
    kern = functools.partial(_router_kernel, n_experts=n_experts, tr=tr)
    col = lambda i: (0, i)
    const = lambda i: (0, 0)
    return pl.pallas_call(
        kern,
        out_shape=[jax.ShapeDtypeStruct((TOP_K, t), I32), jax.ShapeDtypeStruct((TOP_K, t), I32),
                   jax.ShapeDtypeStruct((TOP_K, t), F32), jax.ShapeDtypeStruct((n_experts, LANES), F32)],
        grid=(t // tr,),
        in_specs=[pl.BlockSpec((tr, d // 2), lambda i: (i, 0)), pl.BlockSpec((n_experts, d), const),
                  pl.BlockSpec((n_experts, 1), const), pl.BlockSpec((tr, tr), const)],
        out_specs=[pl.BlockSpec((TOP_K, tr), col), pl.BlockSpec((TOP_K, tr), col), pl.BlockSpec((TOP_K, tr), col),
                   pl.BlockSpec((n_experts, LANES), const)],
        scratch_shapes=[pltpu.VMEM((n_experts, LANES), F32)],
        compiler_params=pltpu.CompilerParams(dimension_semantics=("arbitrary",), vmem_limit_bytes=VMEM_LIMIT),
        name="router",
    )(u2p, router_wt, bias_col, su)


def _slots_kernel(idx_ref, pos_ref, pstart_ref, dest_ref, *, n_experts, tr):
    ei = lax.broadcasted_iota(I32, (n_experts, tr), 0).astype(F32)
    pstart = pstart_ref[...]
    idx = idx_ref[...].astype(F32)
    rows = [jnp.sum(jnp.where(ei == idx[k:k + 1], pstart, 0.0), axis=0, keepdims=True) for k in range(TOP_K)]
    dest_ref[...] = jnp.concatenate(rows, axis=0).astype(I32) + pos_ref[...]


def _slots(idx, pos, pstart_col):
    t = idx.shape[1]
    n_experts = pstart_col.shape[0]
    tr = TOKEN_TILE
    col = lambda i: (0, i)
    return pl.pallas_call(
        functools.partial(_slots_kernel, n_experts=n_experts, tr=tr),
        out_shape=jax.ShapeDtypeStruct((TOP_K, t), I32),
        grid=(t // tr,),
        in_specs=[pl.BlockSpec((TOP_K, tr), col), pl.BlockSpec((TOP_K, tr), col),
                  pl.BlockSpec((n_experts, 1), lambda i: (0, 0))],
        out_specs=pl.BlockSpec((TOP_K, tr), col),
        compiler_params=pltpu.CompilerParams(dimension_semantics=("arbitrary",), vmem_limit_bytes=VMEM_LIMIT),
        name="slots",
    )(idx, pos, pstart_col)


PAD_BITS = tuple(1 << b for b in reversed(range((MOE_BLOCK - 1).bit_length())))


def _dispatch_kernel(pstart_ref, cnt_ref, dest_ref, u_ref, xs_ref, ubuf, zbuf, usem, sem, psem, *, td, n_steps,
                     n_experts):
    i = pl.program_id(0)
    slot = i % 2

    def u_copies(step, s):
        rows = pl.ds(pl.multiple_of(step * td, td), td)
        return [pltpu.make_async_copy(u_ref.at[rows, pl.ds(j * LANES, LANES)], ubuf.at[s, :, j, :], usem.at[s])
                for j in range(ROW_SUBLANES)]

    @pl.when(i == 0)
    def _():
        for cp in u_copies(i, 0):
            cp.start()
        zbuf[...] = jnp.zeros_like(zbuf)

        def pads(e, wait):
            cnt = cnt_ref[e]
            n_pad = (MOE_BLOCK - (cnt & (MOE_BLOCK - 1))) & (MOE_BLOCK - 1)
            base = pstart_ref[e] + cnt
            for bit in PAD_BITS:
                cp = pltpu.make_async_copy(zbuf.at[pl.ds(0, bit)], xs_ref.at[pl.ds(base, bit)], psem)
                has = (n_pad & bit) != 0

                @pl.when(has)
                def _():
                    if wait:
                        cp.wait()
                    else:
                        cp.start()

                base = base + jnp.where(has, bit, 0)

        def start_pads(e, carry):
            pads(e, False)
            return carry

        def wait_pads(e, carry):
            pads(e, True)
            return carry

        lax.fori_loop(0, n_experts, start_pads, 0)
        lax.fori_loop(0, n_experts, wait_pads, 0)

    @pl.when(i + 1 < n_steps)
    def _():
        for cp in u_copies(i + 1, 1 - slot):
            cp.start()

    for cp in u_copies(i, slot):
        cp.wait()

    def start(t, carry):
        for k in range(TOP_K):
            pltpu.make_async_copy(ubuf.at[slot, t], xs_ref.at[dest_ref[k, t]], sem).start(priority=k % 2)
        return carry

    def wait(t, carry):
        for k in range(TOP_K):
            pltpu.make_async_copy(ubuf.at[slot, 0], xs_ref.at[0], sem).wait()
        return carry

    lax.fori_loop(0, td, start, 0, unroll=4)
    lax.fori_loop(0, td, wait, 0, unroll=8)


def _dispatch(pad_start, counts, dest, u2p, n_slots):
    t, dh = u2p.shape
    td = DISPATCH_TILE
    n_experts = pad_start.shape[0]
    assert MOE_BLOCK & (MOE_BLOCK - 1) == 0 and dh == ROW_SUBLANES * LANES
    kern = functools.partial(_dispatch_kernel, td=td, n_steps=t // td, n_experts=n_experts)
    return pl.pallas_call(
        kern,
        out_shape=jax.ShapeDtypeStruct((n_slots, ROW_SUBLANES, LANES), U32),
        grid_spec=pltpu.PrefetchScalarGridSpec(
            num_scalar_prefetch=2, grid=(t // td,),
            in_specs=[pl.BlockSpec((TOP_K, td), lambda i, ps, cn: (0, i), memory_space=pltpu.SMEM),
                      pl.BlockSpec(memory_space=pl.ANY)],
            out_specs=pl.BlockSpec(memory_space=pl.ANY),
            scratch_shapes=[pltpu.VMEM((2, td, ROW_SUBLANES, LANES), U32),
                            pltpu.VMEM((MOE_BLOCK // 2, ROW_SUBLANES, LANES), U32),
                            pltpu.SemaphoreType.DMA((2,)), pltpu.SemaphoreType.DMA, pltpu.SemaphoreType.DMA]),
        compiler_params=pltpu.CompilerParams(dimension_semantics=("arbitrary",), vmem_limit_bytes=VMEM_LIMIT),
        name="dispatch",
    )(pad_start, counts, dest, u2p)


def _experts_kernel(bs_ref, nblk_ref, nu_ref, xs_ref, wg_ref, wu_ref, wd_ref, ys_ref, wgub, wdb, xbuf, ybuf,
                    semx, semy, *, n_experts):
    e = pl.program_id(0)
    n_used = nu_ref[0]

    def x_copies(g):
        rows = pl.ds(pl.multiple_of(g * MOE_BLOCK, MOE_BLOCK), MOE_BLOCK)
        return [pltpu.make_async_copy(xs_ref.at[rows, j, :], xbuf.at[g % X_BUFFERS, :, pl.ds(j * LANES, LANES)],
                                      semx.at[g % X_BUFFERS]) for j in range(ROW_SUBLANES)]

    def x_start(g):
        for cp in x_copies(g):
            cp.start()

    def x_wait(g):
        for cp in x_copies(g):
            cp.wait()

    def y_copies(g, slot):
        rows = pl.ds(pl.multiple_of(g * MOE_BLOCK, MOE_BLOCK), MOE_BLOCK)
        return [pltpu.make_async_copy(ybuf.at[slot, :, pl.ds(j * LANES, LANES)], ys_ref.at[rows, j, :], semy.at[slot])
                for j in range(ROW_SUBLANES)]

    def y_start(g, slot):
        for cp in y_copies(g, slot):
            cp.start()

    def y_wait(g, slot):
        for cp in y_copies(g, slot):
            cp.wait()

    @pl.when(e == 0)
    def _():
        for g in range(X_BUFFERS - 1):
            @pl.when(g < n_used)
            def _():
                x_start(jnp.int32(g))

    g0 = bs_ref[e]
    nb = nblk_ref[e]

    de = wg_ref.shape[2]

    @pl.when(nb > 0)
    def _():
        wgub[:, :de] = wg_ref[0].astype(BF16)
        wgub[:, de:] = wu_ref[0].astype(BF16)
        wdb[...] = wd_ref[0].astype(BF16)

    def block(g, carry):
        slot = g % 2

        @pl.when(g + X_BUFFERS - 1 < n_used)
        def _():
            x_start(g + X_BUFFERS - 1)

        x_wait(g)

        @pl.when(g >= 2)
        def _():
            y_wait(g - 2, slot)

        xa, xb = _unpack_halves(xbuf[g % X_BUFFERS])
        xrow = jnp.concatenate([xa.astype(BF16), xb.astype(BF16)], axis=1)
        hgu = _dot(xrow, wgub[...])
        hg = hgu[:, :de]
        hb = (hg * _sigmoid(hg) * hgu[:, de:]).astype(BF16)
        ybuf[slot] = _pack_halves(_dot(hb, wdb[...]))
        y_start(g, slot)
        return carry

    lax.fori_loop(g0, g0 + nb, block, 0)

    @pl.when(e == n_experts - 1)
    def _():
        @pl.when(n_used >= 2)
        def _():
            y_wait(n_used - 2, n_used % 2)

        @pl.when(n_used >= 1)
        def _():
            y_wait(n_used - 1, (n_used - 1) % 2)


def _experts(block_start, n_blocks, n_used, xsorted, wg, wu, wd):
    n_slots = xsorted.shape[0]
    dh = ROW_SUBLANES * LANES
    n_experts, d, de = wg.shape
    wmap = lambda e, bs, nb, nu: (e, 0, 0)
    assert xsorted.shape[1:] == (ROW_SUBLANES, LANES) and d == 2 * dh
    return pl.pallas_call(
        functools.partial(_experts_kernel, n_experts=n_experts),
        out_shape=jax.ShapeDtypeStruct((n_slots, ROW_SUBLANES, LANES), U32),
        grid_spec=pltpu.PrefetchScalarGridSpec(
            num_scalar_prefetch=3, grid=(n_experts,),
            in_specs=[pl.BlockSpec(memory_space=pl.ANY), pl.BlockSpec((1, d, de), wmap),
                      pl.BlockSpec((1, d, de), wmap), pl.BlockSpec((1, de, d), wmap)],
            out_specs=pl.BlockSpec(memory_space=pl.ANY),
            scratch_shapes=[pltpu.VMEM((d, 2 * de), BF16), pltpu.VMEM((de, d), BF16),
                            pltpu.VMEM((X_BUFFERS, MOE_BLOCK, dh), U32),
                            pltpu.VMEM((2, MOE_BLOCK, dh), U32),
                            pltpu.SemaphoreType.DMA((X_BUFFERS,)), pltpu.SemaphoreType.DMA((2,))]),
        compiler_params=pltpu.CompilerParams(dimension_semantics=("arbitrary",), vmem_limit_bytes=VMEM_LIMIT),
        name="experts",
    )(block_start, n_blocks, n_used, xsorted, wg, wu, wd)


def _combine_kernel(dest_ref, dnext_ref, ys_ref, wt_ref, x1_ref, u_ref, mod_ref, wsgu_ref, wsd_ref,
                    gfin_ref, o_ref, gbuf, accbuf, sem, *, tc, n_steps):
    i = pl.program_id(0)
    slot = i % 2

    def issue(dref, s):
        def body(t, carry):
            for k in range(TOP_K):
                pltpu.make_async_copy(ys_ref.at[dref[k, t]], gbuf.at[s, k, t], sem.at[s]).start(priority=k % 2)
            return carry
        lax.fori_loop(0, tc, body, 0, unroll=4)

    @pl.when(i == 0)
    def _():
        issue(dest_ref, 0)

    def wait(t, carry):
        for k in range(TOP_K):
            pltpu.make_async_copy(ys_ref.at[0], gbuf.at[slot, 0, 0], sem.at[slot]).wait()
        return carry

    lax.fori_loop(0, tc, wait, 0, unroll=8)

    def sum_and_issue(with_issue):
        def body(c, carry):
            t0 = c * GATHER_CHUNK
            if with_issue:
                for tt in range(GATHER_CHUNK):
                    for k in range(TOP_K):
                        pltpu.make_async_copy(ys_ref.at[dnext_ref[k, t0 + tt]], gbuf.at[1 - slot, k, t0 + tt],
                                              sem.at[1 - slot]).start(priority=k % 2)
            rows = pl.ds(t0, GATHER_CHUNK)
            wts = wt_ref[rows]
            sum_l = jnp.zeros((GATHER_CHUNK, ROW_SUBLANES, LANES), F32)
            sum_r = jnp.zeros((GATHER_CHUNK, ROW_SUBLANES, LANES), F32)
            for k in range(TOP_K):
                ga, gb = _unpack_halves(gbuf[slot, k, rows])
                wk = jnp.broadcast_to(wts[:, k:k + 1, :], (GATHER_CHUNK, ROW_SUBLANES, LANES))
                sum_l = sum_l + ga * wk
                sum_r = sum_r + gb * wk
            accbuf[0, rows] = sum_l
            accbuf[1, rows] = sum_r
            return carry
        lax.fori_loop(0, tc // GATHER_CHUNK, body, 0)

    @pl.when(i + 1 < n_steps)
    def _():
        sum_and_issue(True)

    @pl.when(i + 1 >= n_steps)
    def _():
        sum_and_issue(False)

    ua, ub = _unpack_halves(u_ref[...])
    half = ua.shape[1]
    hgu = _dot(jnp.concatenate([ua.astype(BF16), ub.astype(BF16)], axis=1), wsgu_ref[...])
    dsh = wsd_ref.shape[0]
    hg = hgu[:, :dsh]
    shared = _dot((hg * _sigmoid(hg) * hgu[:, dsh:]).astype(BF16), wsd_ref[...])
    acc_l = shared[:, :half]
    acc_r = shared[:, half:]
    acc_l = jnp.concatenate([acc_l[:, j * LANES:(j + 1) * LANES] + accbuf[0, :, j, :] for j in range(ROW_SUBLANES)],
                            axis=1)
    acc_r = jnp.concatenate([acc_r[:, j * LANES:(j + 1) * LANES] + accbuf[1, :, j, :] for j in range(ROW_SUBLANES)],
                            axis=1)
    m = mod_ref[0]
    xo_l = x1_ref[:, :half] + m[5:6, :half] * acc_l
    xo_r = x1_ref[:, half:] + m[5:6, half:] * acc_r
    ms = (jnp.sum(xo_l * xo_l, axis=-1, keepdims=True) + jnp.sum(xo_r * xo_r, axis=-1, keepdims=True)) / (2 * half)
    inv = lax.rsqrt(ms + EPS)
    o_ref[:, :half] = xo_l * inv * gfin_ref[:, :half]
    o_ref[:, half:] = xo_r * inv * gfin_ref[:, half:]


def _combine(dest, ysorted, wts_t, x1, u2p, mods3, wsgu, wsd, gfin, *, seq):
    t, d = x1.shape
    dh = d // 2
    tc = DISPATCH_TILE
    tiles_per_batch = seq // tc
    n_steps = t // tc
    kern = functools.partial(_combine_kernel, tc=tc, n_steps=n_steps)
    row = lambda i: (i, 0)
    const = lambda i: (0, 0)
    return pl.pallas_call(
        kern,
        out_shape=jax.ShapeDtypeStruct((t, d), F32),
        grid=(n_steps,),
        in_specs=[pl.BlockSpec((TOP_K, tc), lambda i: (0, i), memory_space=pltpu.SMEM),
                  pl.BlockSpec((TOP_K, tc), lambda i: (0, jnp.minimum(i + 1, n_steps - 1)), memory_space=pltpu.SMEM),
                  pl.BlockSpec(memory_space=pl.ANY), pl.BlockSpec((tc, TOP_K, LANES), lambda i: (i, 0, 0)),
                  pl.BlockSpec((tc, d), row), pl.BlockSpec((tc, dh), row),
                  pl.BlockSpec((1, 6, d), lambda i: (i // tiles_per_batch, 0, 0)),
                  pl.BlockSpec(wsgu.shape, const), pl.BlockSpec(wsd.shape, const), pl.BlockSpec((1, d), const)],
        out_specs=pl.BlockSpec((tc, d), row),
        scratch_shapes=[pltpu.VMEM((2, TOP_K, tc, ROW_SUBLANES, LANES), U32),
                        pltpu.VMEM((2, tc, ROW_SUBLANES, LANES), F32), pltpu.SemaphoreType.DMA((2,))],
        compiler_params=pltpu.CompilerParams(dimension_semantics=("arbitrary",), vmem_limit_bytes=VMEM_LIMIT),
        name="combine",
    )(dest, dest, ysorted, wts_t, x1, u2p, mods3, wsgu, wsd, gfin)


def kernel(x, c, ctx, c_ctx, ada_w, ada_b, norm_mix_g, norm_ffn_g, w_in, conv_xbc_w, conv_xbc_b, ssd_dt_bias, ssd_a_log, ssd_d, ssd_norm_g, conv_qk_w, conv_qk_b, mlstm_i_bias, mlstm_f_bias, mlstm_norm_g, w_ssd_out, w_mlstm_out, w_out, router_w, router_bias, moe_w_gate, moe_w_up, moe_w_down, shared_w_gate, shared_w_up, shared_w_down, norm_final_g):
    batch, seq, d = x.shape
    ctx_len = ctx.shape[1]
    depth = ada_w.shape[0]
    assert depth == 1, "only the single-layer configuration is implemented"
    assert seq % CHUNK == 0 and ctx_len % CHUNK == 0 and seq % GRID_W == 0
    l = 0
    n_sh = d // SSD_HEAD_DIM
    n_mh = d // MLSTM_DV
    dk = MLSTM_DV // 2
    d_xbc = d + 2 * SSD_GROUPS * SSD_STATE
    d_qk = 2 * n_mh * dk
    sizes = (d, d_xbc, 2 * n_sh, d_qk, d, 4 * n_mh, d, 2 * d)
    offs = np.concatenate([[0], np.cumsum(sizes)])
    assert offs[-1] == w_in.shape[2] and 2 * n_sh + 2 * n_mh <= LANES

    cond = jnp.concatenate([c, c_ctx[None], jnp.zeros((8 - (batch + 1) % 8, d), F32)], axis=0)
    mods = _adaln(cond, ada_w[l], ada_b[l])
    mods3 = mods.reshape(mods.shape[0], 6, d)

    w = w_in[l]
    seg = lambda k: w[:, offs[k]:offs[k + 1]]
    wbig = jnp.concatenate([seg(0), seg(1), seg(3), seg(4), seg(6), seg(7)], axis=1).astype(BF16)
    w_dt = seg(2)
    w_g = seg(5).reshape(d, 2, 2, n_mh)
    w_i = w_g[:, :, 0].reshape(d, 2 * n_mh)
    w_f = w_g[:, :, 1].reshape(d, 2 * n_mh)
    pad = jnp.zeros((d, LANES - 2 * n_sh - 2 * n_mh), F32)
    wsm = jnp.concatenate([w_dt, w_f, pad, w_dt, w_i, pad], axis=1)
    wsh = wsm.astype(BF16)
    wsl = (wsm - wsh.astype(F32)).astype(BF16)
    padb = jnp.zeros((LANES - 2 * n_sh - 2 * n_mh,), F32)
    dtb = ssd_dt_bias[l].reshape(-1).astype(F32)
    smb = jnp.concatenate([dtb, mlstm_f_bias[l].reshape(-1).astype(F32), padb,
                           dtb, mlstm_i_bias[l].reshape(-1).astype(F32), padb]).reshape(1, 2 * LANES)
    aneg = jnp.concatenate([-jnp.exp(ssd_a_log[l].astype(F32)).reshape(-1),
                            jnp.zeros((LANES - 2 * n_sh,), F32)]).reshape(1, LANES)

    x2 = x.reshape(batch * seq, d)
    ctx2 = ctx.reshape(batch * ctx_len, d)
    z, xs, bc, qk, v, og, mg, small, smallt = _inproj(
        x2, ctx2, mods3, norm_mix_g[l].reshape(1, d), wbig, wsh, wsl,
        conv_xbc_w[l], conv_xbc_b[l].reshape(1, d_xbc), conv_qk_w[l], conv_qk_b[l].reshape(1, d_qk), smb, aneg,
        batch=batch, seq=seq, ctx_len=ctx_len, dk=dk)

    ncc = ctx_len // CHUNK
    ncl = seq // CHUNK
    y0, y1, h0, h1 = _scans(xs, bc, qk, v, small, smallt, batch=batch, n_ctx_chunks=ncc, n_lat_chunks=ncl, dk=dk)

    dexp = jnp.repeat(ssd_d[l].astype(F32), SSD_HEAD_DIM).reshape(1, d)
    x1, u2 = _merge(y0, y1, xs, z, h0, h1, og, mg, x2, mods3, dexp, ssd_norm_g[l].reshape(1, d),
                    mlstm_norm_g[l].reshape(1, d), norm_ffn_g[l].reshape(1, d),
                    w_ssd_out[l].astype(BF16), w_mlstm_out[l].astype(BF16), w_out[l].astype(BF16),
                    batch=batch, seq=seq, n_ctx_tok=batch * ctx_len)

    n_experts = router_w.shape[2]
    idx, pos, wts, cnt = _router(u2, router_w[l].T.astype(BF16), router_bias[l].astype(F32).reshape(n_experts, 1))
    counts = cnt[:, 0].astype(I32)
    padded = (counts + MOE_BLOCK - 1) // MOE_BLOCK * MOE_BLOCK
    pad_end = jnp.cumsum(padded)
    pad_start = (pad_end - padded).astype(I32)
    t = batch * seq
    nb = t * TOP_K // MOE_BLOCK + n_experts
    n_used = (pad_end[-1] // MOE_BLOCK).astype(I32).reshape(1)

    dest = _slots(idx, pos, pad_start.astype(F32).reshape(n_experts, 1))
    xsorted = _dispatch(pad_start, counts, dest, u2, nb * MOE_BLOCK)
    ysorted = _experts(pad_start // MOE_BLOCK, (padded // MOE_BLOCK).astype(I32), n_used, xsorted,
                       moe_w_gate[l], moe_w_up[l], moe_w_down[l])
    wts_lanes = jnp.broadcast_to(wts.T[:, :, None], (t, TOP_K, LANES))
    out = _combine(dest, ysorted, wts_lanes, x1, u2, mods3,
                   jnp.concatenate([shared_w_gate[l], shared_w_up[l]], axis=1).astype(BF16), shared_w_down[l].astype(BF16),
                   norm_final_g.reshape(1, d), seq=seq)
    return out.reshape(batch, seq, d)
```

```python
import functools

import numpy as np
import jax
import jax.numpy as jnp
from jax import lax
from jax.experimental import pallas as pl
from jax.experimental.pallas import tpu as pltpu

F32 = jnp.float32
BF16 = jnp.bfloat16
I32 = jnp.int32
U32 = jnp.uint32

EPS = 1e-6
CHUNK = 128
CONV_K = 5
GRID_W = 64
SSD_HEAD_DIM = 64
SSD_STATE = 128
SSD_GROUPS = 2
MLSTM_DV = 128
N_EXPERT_GROUPS = 8
TOPK_GROUPS = 4
TOP_K = 8
ROUTED_SCALE = 2.5

LANES = 128
SUBLANES = 8
PACK_ROWS = 16
TOKEN_TILE = 512
COL_CHUNK = 512
MOE_BLOCK = 512
DISPATCH_TILE = 256
GATHER_CHUNK = 4
X_BUFFERS = 4
ROW_SUBLANES = 4
VMEM_LIMIT = 56 * 1024 * 1024
NEG_INF = float("-inf")


def _dot(a, b):
    return jnp.dot(a, b, preferred_element_type=F32)


def _dot_nt(a, b):
    return lax.dot_general(a, b, (((1,), (1,)), ((), ())), preferred_element_type=F32)


def _dot_tn(a, b):
    return lax.dot_general(a, b, (((0,), (0,)), ((), ())), preferred_element_type=F32)


def _spread(parts, e):
    res = _dot(jnp.concatenate(parts, axis=0).astype(BF16), e)
    out, r0 = [], 0
    for p in parts:
        out.append(res[r0:r0 + p.shape[0]])
        r0 += p.shape[0]
    return out


def _rows16(row):
    r8 = jnp.broadcast_to(row, (SUBLANES, row.shape[1]))
    return jnp.concatenate([r8, r8], axis=0)


def _sigmoid(v):
    return 1.0 / (1.0 + jnp.exp(-v))


def _pack_halves(v):
    n = v.shape[1] // 2
    hi = lax.bitcast_convert_type(v[:, :n].astype(BF16).astype(F32), U32)
    lo = lax.bitcast_convert_type(v[:, n:].astype(BF16).astype(F32), U32)
    return hi | (lo >> 16)


def _unpack_halves(p):
    left = lax.bitcast_convert_type(p & jnp.uint32(0xFFFF0000), F32)
    right = lax.bitcast_convert_type(p << 16, F32)
    return left, right


def _softplus(v):
    return jnp.maximum(v, 0.0) + jnp.log1p(jnp.exp(-jnp.abs(v)))


def _adaln_kernel(c_ref, w_ref, b_ref, o_ref):
    c = c_ref[...]
    s = c * _sigmoid(c)
    w = w_ref[...]
    s_hi = s.astype(BF16)
    s_lo = (s - s_hi.astype(F32)).astype(BF16)
    w_hi = w.astype(BF16)
    w_lo = (w - w_hi.astype(F32)).astype(BF16)
    o_ref[...] = _dot(s_hi, w_hi) + _dot(s_lo, w_hi) + _dot(s_hi, w_lo) + b_ref[...]


def _adaln(cond, w, b):
    rows, d = cond.shape
    n = w.shape[1]
    tn = 1536 if n % 1536 == 0 else n
    return pl.pallas_call(
        _adaln_kernel,
        out_shape=jax.ShapeDtypeStruct((rows, n), F32),
        grid=(n // tn,),
        in_specs=[pl.BlockSpec((rows, d), lambda j: (0, 0)),
                  pl.BlockSpec((d, tn), lambda j: (0, j)),
                  pl.BlockSpec((1, tn), lambda j: (0, j))],
        out_specs=pl.BlockSpec((rows, tn), lambda j: (0, j)),
        compiler_params=pltpu.CompilerParams(dimension_semantics=("arbitrary",), vmem_limit_bytes=VMEM_LIMIT),
        name="adaln",
    )(cond, w, b.reshape(1, n))


CONV_SHIFTS = tuple(j - CONV_K // 2 for j in range(CONV_K) if j != CONV_K // 2)


def _conv_masks(tm, seg_len):
    pos = np.arange(tm) % seg_len
    m = np.zeros((tm, 8), np.float32)
    for i, s in enumerate(CONV_SHIFTS):
        m[:, i] = ((pos + s >= 0) & (pos + s < seg_len)).astype(np.float32)
    return m


def _conv_silu(acc, w5, bias, vm, tm):
    out = acc * w5[CONV_K // 2:CONV_K // 2 + 1] + bias
    for i, s in enumerate(CONV_SHIFTS):
        shifted = pltpu.roll(acc, (-s) % tm, axis=0)
        j = s + CONV_K // 2
        out = out + (shifted * vm[:, i:i + 1]) * w5[j:j + 1]
    return out * _sigmoid(out)


def _inproj_kernel(x_ref, ctx_ref, mod_ref, g_ref, wbig_ref, wsh_ref, wsl_ref, cwx_ref, cbx_ref, cwq_ref, cbq_ref,
                   smb_ref, aneg_ref, tril_ref, triu_ref, cmask_ref,
                   z_ref, xs_ref, bc_ref, qk_ref, v_ref, og_ref, mg_ref, small_ref, smallt_ref,
                   *, n_ctx_tiles, tm, d_model, dk):
    i = pl.program_id(0)
    is_ctx = i < n_ctx_tiles
    xt = jnp.where(is_ctx, ctx_ref[...], x_ref[...])
    m = mod_ref[0]
    ms = jnp.mean(xt * xt, axis=-1, keepdims=True)
    u = xt * lax.rsqrt(ms + EPS) * g_ref[...] * (1.0 + m[1:2]) + m[0:1]
    u_hi = u.astype(BF16)
    u_lo = (u - u_hi.astype(F32)).astype(BF16)
    vm = jnp.where(is_ctx, cmask_ref[1], cmask_ref[0])

    d = d_model
    d_bc = 2 * SSD_GROUPS * SSD_STATE
    pieces = [(z_ref, d, "plain", None), (xs_ref, d, "convx", 0), (bc_ref, d_bc, "convx", d),
              (qk_ref, d, "convq", 0), (v_ref, d, "plain", None), (og_ref, d, "sigmoid", None),
              (mg_ref, 2 * d, "sigmoid", None)]
    col = 0
    for ref, width, kind, coff in pieces:
        for c0 in range(0, width, COL_CHUNK):
            acc = _dot(u_hi, wbig_ref[:, col + c0:col + c0 + COL_CHUNK])
            if kind == "convx":
                cs = coff + c0
                acc = _conv_silu(acc, cwx_ref[:, cs:cs + COL_CHUNK], cbx_ref[:, cs:cs + COL_CHUNK], vm, tm)
            elif kind == "convq":
                acc = _conv_silu(acc, cwq_ref[:, c0:c0 + COL_CHUNK], cbq_ref[:, c0:c0 + COL_CHUNK], vm, tm)
                if c0 < width // 2:
                    acc = acc * (dk ** -0.5)
            elif kind == "sigmoid":
                acc = _sigmoid(acc)
            ref[:, c0:c0 + COL_CHUNK] = acc.astype(ref.dtype)
        col += width

    wsh = wsh_ref[...]
    raw = _dot(u_hi, wsh) + _dot(u_lo, wsh) + _dot(u_hi, wsl_ref[...]) + smb_ref[...]
    p1 = raw[:, :LANES]
    p2 = raw[:, LANES:]
    lane = lax.broadcasted_iota(I32, (tm, LANES), 1)
    n_dt = 2 * (d_model // SSD_HEAD_DIM)
    n_g = 2 * (d_model // MLSTM_DV)
    is_dt = lane < n_dt
    is_gate = jnp.logical_and(lane >= n_dt, lane < n_dt + n_g)
    dt = _softplus(p2)
    pa = jnp.where(is_dt, dt * aneg_ref[...], jnp.where(is_gate, -_softplus(-p1), 0.0))
    pb = jnp.where(is_dt, dt, jnp.where(is_gate, p2, 0.0))
    lane_c = lax.broadcasted_iota(I32, (CHUNK, LANES), 1)
    is_dt_c = lane_c < n_dt
    rev = jnp.logical_or(jnp.logical_and(lane_c >= n_dt // 2, lane_c < n_dt),
                         jnp.logical_and(lane_c >= n_dt + n_g // 2, lane_c < n_dt + n_g))
    tril = tril_ref[...]
    triu = triu_ref[...]
    tq = lax.broadcasted_iota(I32, (CHUNK, LANES), 0)
    for c in range(tm // CHUNK):
        r0 = c * CHUNK
        a_c = pa[r0:r0 + CHUNK]
        hi = a_c.astype(BF16)
        r1 = a_c - hi.astype(F32)
        mid = r1.astype(BF16)
        lo = (r1 - mid.astype(F32)).astype(BF16)
        cs_f = _dot(tril, hi) + _dot(tril, mid) + _dot(tril, lo)
        cs_b = _dot(triu, hi) + _dot(triu, mid) + _dot(triu, lo)
        plane_a = jnp.where(rev, cs_b, cs_f)
        plane_b = jnp.where(is_dt_c, pb[r0:r0 + CHUNK], pb[r0:r0 + CHUNK] - plane_a)
        yf = plane_b
        yb = plane_b
        s = 1
        while s < CHUNK:
            sh = pltpu.roll(yf, s, axis=0)
            yf = jnp.maximum(yf, jnp.where(tq >= s, sh, NEG_INF))
            sh = pltpu.roll(yb, CHUNK - s, axis=0)
            yb = jnp.maximum(yb, jnp.where(tq + s < CHUNK, sh, NEG_INF))
            s *= 2
        plane_c = jnp.where(rev, yb, yf)
        small_ref[r0:r0 + CHUNK, 0:LANES] = plane_a
        small_ref[r0:r0 + CHUNK, LANES:2 * LANES] = plane_b
        small_ref[r0:r0 + CHUNK, 2 * LANES:3 * LANES] = plane_c
        smallt_ref[c, 0] = plane_a.T
        smallt_ref[c, 1] = plane_b.T


def _inproj(x2, ctx2, mods3, g, wbig, wsh, wsl, cwx, cbx, cwq, cbq, smb, aneg, *, batch, seq, ctx_len, dk):
    d = x2.shape[1]
    tm = TOKEN_TILE
    n_ctx_tok = batch * ctx_len
    assert n_ctx_tok % tm == 0 and seq % tm == 0 and tm % ctx_len == 0 and tm % GRID_W == 0
    n_ctx_tiles = n_ctx_tok // tm
    tiles_per_batch = seq // tm
    n_tiles = n_ctx_tiles + batch * tiles_per_batch
    t_all = n_tiles * tm
    n_big = wbig.shape[1]
    tril = jnp.asarray(np.tril(np.ones((CHUNK, CHUNK), np.float32)), BF16)
    triu = jnp.asarray(np.triu(np.ones((CHUNK, CHUNK), np.float32)), BF16)

    def x_map(i):
        return (jnp.maximum(i - n_ctx_tiles, 0), 0)

    def ctx_map(i):
        return (jnp.minimum(i, n_ctx_tiles - 1), 0)

    def mod_map(i):
        return (jnp.where(i < n_ctx_tiles, batch, jnp.maximum(i - n_ctx_tiles, 0) // tiles_per_batch), 0, 0)

    const = lambda i: (0, 0)
    row = lambda i: (i, 0)
    cmask = jnp.asarray(np.stack([_conv_masks(tm, GRID_W), _conv_masks(tm, ctx_len)]))
    kern = functools.partial(_inproj_kernel, n_ctx_tiles=n_ctx_tiles, tm=tm, d_model=d, dk=dk)
    d_bc = 2 * SSD_GROUPS * SSD_STATE
    outs = [jax.ShapeDtypeStruct((t_all, d), BF16), jax.ShapeDtypeStruct((t_all, d), BF16),
            jax.ShapeDtypeStruct((t_all, d_bc), BF16), jax.ShapeDtypeStruct((t_all, d), BF16),
            jax.ShapeDtypeStruct((t_all, d), BF16), jax.ShapeDtypeStruct((t_all, d), BF16),
            jax.ShapeDtypeStruct((t_all, 2 * d), BF16), jax.ShapeDtypeStruct((t_all, 3 * LANES), F32),
            jax.ShapeDtypeStruct((t_all // CHUNK, 2, LANES, CHUNK), F32)]
    out_specs = [pl.BlockSpec((tm, d), row), pl.BlockSpec((tm, d), row), pl.BlockSpec((tm, d_bc), row),
                 pl.BlockSpec((tm, d), row), pl.BlockSpec((tm, d), row), pl.BlockSpec((tm, d), row),
                 pl.BlockSpec((tm, 2 * d), row), pl.BlockSpec((tm, 3 * LANES), row),
                 pl.BlockSpec((tm // CHUNK, 2, LANES, CHUNK), lambda i: (i, 0, 0, 0))]
    in_specs = [pl.BlockSpec((tm, d), x_map), pl.BlockSpec((tm, d), ctx_map),
                pl.BlockSpec((1, 6, d), mod_map), pl.BlockSpec((1, d), const),
                pl.BlockSpec((d, n_big), const, pipeline_mode=pl.Buffered(1)),
                pl.BlockSpec((d, 2 * LANES), const), pl.BlockSpec((d, 2 * LANES), const),
                pl.BlockSpec(cwx.shape, const), pl.BlockSpec(cbx.shape, const),
                pl.BlockSpec(cwq.shape, const), pl.BlockSpec(cbq.shape, const),
                pl.BlockSpec((1, 2 * LANES), const), pl.BlockSpec((1, LANES), const),
                pl.BlockSpec((CHUNK, CHUNK), const), pl.BlockSpec((CHUNK, CHUNK), const),
                pl.BlockSpec((2, tm, 8), lambda i: (0, 0, 0))]
    return pl.pallas_call(
        kern, out_shape=outs, grid=(n_tiles,), in_specs=in_specs, out_specs=out_specs,
        compiler_params=pltpu.CompilerParams(dimension_semantics=("arbitrary",), vmem_limit_bytes=VMEM_LIMIT),
        name="inproj",
    )(x2, ctx2, mods3, g, wbig, wsh, wsl, cwx, cbx, cwq, cbq, smb, aneg, tril, triu, cmask)


def _chunk_block_map(direction, batch, n_ctx_chunks, n_lat_chunks):
    def idx(b, s):
        if direction == 0:
            c_ctx = s
            c_lat = s - n_ctx_chunks
        else:
            c_ctx = n_ctx_chunks - 1 - s
            c_lat = n_lat_chunks - 1 - (s - n_ctx_chunks)
        return jnp.where(s < n_ctx_chunks, b * n_ctx_chunks + c_ctx, batch * n_ctx_chunks + b * n_lat_chunks + c_lat)
    return idx


def _ssd_kernel(xs_ref, bc_ref, small_ref, smallt_ref, e16_ref, y_ref, st_ref, *, direction, n_heads):
    s = pl.program_id(1)

    @pl.when(s == 0)
    def _():
        st_ref[...] = jnp.zeros_like(st_ref)

    hpg = n_heads // SSD_GROUPS
    lane0 = direction * n_heads
    last = CHUNK - 1 if direction == 0 else 0
    lane = lax.broadcasted_iota(I32, (CHUNK, LANES), 1)
    lm = jnp.logical_and(lane >= lane0, lane < lane0 + n_heads)
    plane_a = jnp.where(lm, small_ref[:, 0:LANES], 0.0)
    plane_b = jnp.where(lm, small_ref[:, LANES:2 * LANES], 0.0)
    cum_t = smallt_ref[0, 0]
    tot = plane_a[last:last + 1]
    lane16 = lax.broadcasted_iota(I32, (PACK_ROWS, LANES), 1)
    lm16 = jnp.logical_and(lane16 >= lane0, lane16 < lane0 + n_heads)
    dec_rows = jnp.where(lm16, jnp.exp(_rows16(tot)), 0.0)
    dtx, ecx, wx, decx = _spread(
        [plane_b, jnp.where(lm, jnp.exp(plane_a), 0.0), plane_b * jnp.exp(tot - plane_a), dec_rows], e16_ref[...])
    decx = decx[0:1]

    xf = xs_ref[...].astype(F32)
    xdt = xf * dtx
    xw = (xf * wx).astype(BF16)
    plane64 = lax.broadcasted_iota(I32, (CHUNK, 2 * SSD_HEAD_DIM), 1) < SSD_HEAD_DIM
    iq = lax.broadcasted_iota(I32, (CHUNK, CHUNK), 0)
    ik = lax.broadcasted_iota(I32, (CHUNK, CHUNK), 1)
    mask = (iq >= ik) if direction == 0 else (iq <= ik)
    gw = hpg * SSD_HEAD_DIM
    groups = range(SSD_GROUPS)
    bgs = [bc_ref[:, g * SSD_STATE:(g + 1) * SSD_STATE] for g in groups]
    cgs = [bc_ref[:, (SSD_GROUPS + g) * SSD_STATE:(SSD_GROUPS + g + 1) * SSD_STATE] for g in groups]
    sgs = [st_ref[:, g * gw:(g + 1) * gw] for g in groups]
    cbs = [_dot_nt(cgs[g], bgs[g]) for g in groups]
    y_inters = [_dot(cgs[g], sgs[g].astype(BF16)) * ecx[:, g * gw:(g + 1) * gw] for g in groups]
    m_hs = []
    for h in range(n_heads):
        cum_q = jnp.broadcast_to(plane_a[:, lane0 + h:lane0 + h + 1], (CHUNK, CHUNK))
        seg = cum_q - cum_t[lane0 + h:lane0 + h + 1, :]
        m_hs.append((cbs[h // hpg] * jnp.exp(jnp.where(mask, seg, NEG_INF))).astype(BF16))
    for pair in range(n_heads // 2):
        g = (2 * pair) // hpg
        c0 = 2 * pair * SSD_HEAD_DIM
        xpair = xdt[:, c0:c0 + 2 * SSD_HEAD_DIM]
        acc = y_inters[g][:, c0 - g * gw:c0 - g * gw + 2 * SSD_HEAD_DIM]
        for par in range(2):
            keep = plane64 if par == 0 else jnp.logical_not(plane64)
            acc = acc + _dot(m_hs[2 * pair + par], jnp.where(keep, xpair, 0.0).astype(BF16))
        y_ref[:, c0:c0 + 2 * SSD_HEAD_DIM] = acc.astype(y_ref.dtype)
    for g in groups:
        st_ref[:, g * gw:(g + 1) * gw] = (decx[:, g * gw:(g + 1) * gw] * sgs[g]
                                          + _dot_tn(bgs[g], xw[:, g * gw:(g + 1) * gw]))


def _selectors(direction, d):
    n_sh = d // SSD_HEAD_DIM
    n_mh = d // MLSTM_DV
    e16 = np.zeros((LANES, d), np.float32)
    em = np.zeros((LANES, d), np.float32)
    for h in range(n_sh):
        e16[direction * n_sh + h, h * SSD_HEAD_DIM:(h + 1) * SSD_HEAD_DIM] = 1.0
    for h in range(n_mh):
        em[2 * n_sh + direction * n_mh + h, h * MLSTM_DV:(h + 1) * MLSTM_DV] = 1.0
    return jnp.asarray(e16, BF16), jnp.asarray(em, BF16)


def _mlstm_kernel(qk_ref, v_ref, small_ref, smallt_ref, em_ref, h_ref, cn_ref, m_ref, *, direction, n_heads, dk):
    s = pl.program_id(1)

    @pl.when(s == 0)
    def _():
        cn_ref[...] = jnp.zeros_like(cn_ref)
        m_ref[...] = jnp.zeros_like(m_ref)

    n_dt = 2 * (n_heads * MLSTM_DV // SSD_HEAD_DIM)
    lane0 = n_dt + direction * n_heads
    last = CHUNK - 1 if direction == 0 else 0
    lane = lax.broadcasted_iota(I32, (CHUNK, LANES), 1)
    lm = jnp.logical_and(lane >= lane0, lane < lane0 + n_heads)
    b_q = jnp.where(lm, small_ref[:, 0:LANES], 0.0)
    r_k = jnp.where(lm, small_ref[:, LANES:2 * LANES], 0.0)
    cmr = jnp.where(lm, small_ref[:, 2 * LANES:3 * LANES], 0.0)
    r_t = smallt_ref[0, 1]
    m_all = m_ref[...]
    m_row = m_all[0:1]
    mm = jnp.maximum(cmr, m_row)
    w_state = jnp.exp(m_row - mm)
    e_mq = jnp.exp(-(b_q + mm))
    m_base8 = jnp.maximum(m_all, _rows16(cmr[last:last + 1]))
    m_base = m_base8[0:1]
    w_k = jnp.where(lm, jnp.exp(r_k - m_base), 0.0)
    lane16 = lax.broadcasted_iota(I32, (PACK_ROWS, LANES), 1)
    lm16 = jnp.logical_and(lane16 >= lane0, lane16 < lane0 + n_heads)
    dec_rows = jnp.where(lm16, jnp.exp(m_all - m_base8), 0.0)
    wsx, emqx, wkx, decx = _spread(
        [jnp.where(lm, w_state, 0.0), jnp.where(lm, e_mq, 0.0), w_k, dec_rows], em_ref[...])
    decx = decx[0:1]
    m_ref[...] = jnp.where(lm16, _rows16(b_q[last:last + 1]) + m_base8, 0.0)

    iq = lax.broadcasted_iota(I32, (CHUNK, CHUNK), 0)
    ik = lax.broadcasted_iota(I32, (CHUNK, CHUNK), 1)
    mask = (iq >= ik) if direction == 0 else (iq <= ik)
    ones = jnp.ones((CHUNK, MLSTM_DV), BF16)
    d_qk = n_heads * dk
    heads = range(n_heads)
    hsl = [slice(h * MLSTM_DV, (h + 1) * MLSTM_DV) for h in heads]
    qs = [qk_ref[:, h * dk:(h + 1) * dk] for h in heads]
    ks = [qk_ref[:, d_qk + h * dk:d_qk + (h + 1) * dk] for h in heads]
    vs = [v_ref[:, hsl[h]] for h in heads]
    cns = [cn_ref[h] for h in heads]
    scores = [_dot_nt(qs[h], ks[h]) for h in heads]
    inter = [_dot(qs[h], cns[h].astype(BF16)) for h in heads]
    smats = []
    for h in heads:
        mm_q = jnp.broadcast_to(mm[:, lane0 + h:lane0 + h + 1], (CHUNK, CHUNK))
        dmat = jnp.exp(jnp.where(mask, r_t[lane0 + h:lane0 + h + 1, :] - mm_q, NEG_INF))
        smats.append((scores[h] * dmat).astype(BF16))
    tots = [_dot(smats[h], jnp.concatenate([vs[h], ones], axis=1)) for h in heads]
    for h in heads:
        hs = hsl[h]
        kh, vh, cn = ks[h], vs[h], cns[h]
        wsh = wsx[:, hs]
        tot = tots[h] + jnp.concatenate([wsh, wsh], axis=1) * inter[h]
        num = tot[:, :MLSTM_DV]
        den = tot[:, MLSTM_DV:]
        h_ref[:, hs] = (num / jnp.maximum(jnp.abs(den), emqx[:, hs])).astype(h_ref.dtype)
        wkh = wkx[:, hs]
        rhs = jnp.concatenate([(vh.astype(F32) * wkh).astype(BF16), wkh.astype(BF16)], axis=1)
        dech = decx[:, hs]
        cn_ref[h] = jnp.concatenate([dech, dech], axis=1) * cn + _dot_tn(kh, rhs)


def _scans_kernel(*refs, n_sh, n_mh, dk):
    (xs0, bc0, qk0, v0, sm0, smt0, xs1, bc1, qk1, v1, sm1, smt1, e16_0, em_0, e16_1, em_1,
     y0, y1, h0, h1, st0, st1, cn0, cn1, m0, m1) = refs
    _ssd_kernel(xs0, bc0, sm0, smt0, e16_0, y0, st0, direction=0, n_heads=n_sh)
    _ssd_kernel(xs1, bc1, sm1, smt1, e16_1, y1, st1, direction=1, n_heads=n_sh)
    _mlstm_kernel(qk0, v0, sm0, smt0, em_0, h0, cn0, m0, direction=0, n_heads=n_mh, dk=dk)
    _mlstm_kernel(qk1, v1, sm1, smt1, em_1, h1, cn1, m1, direction=1, n_heads=n_mh, dk=dk)


def _scans(xs, bc, qk, v, small, smallt, *, batch, n_ctx_chunks, n_lat_chunks, dk):
    t_all, d = xs.shape
    n_sh = d // SSD_HEAD_DIM
    n_mh = d // MLSTM_DV
    n_steps = n_ctx_chunks + n_lat_chunks
    in_specs, args = [], []
    for direction in range(2):
        idx = _chunk_block_map(direction, batch, n_ctx_chunks, n_lat_chunks)
        rows = lambda b, s, idx=idx: (idx(b, s), 0)
        in_specs += [pl.BlockSpec((CHUNK, d), rows), pl.BlockSpec((CHUNK, bc.shape[1]), rows),
                     pl.BlockSpec((CHUNK, qk.shape[1]), rows), pl.BlockSpec((CHUNK, d), rows),
                     pl.BlockSpec((CHUNK, 3 * LANES), rows),
                     pl.BlockSpec((1, 2, LANES, CHUNK), lambda b, s, idx=idx: (idx(b, s), 0, 0, 0))]
        args += [xs, bc, qk, v, small, smallt]
    out_specs = []
    for direction in (0, 1, 0, 1):
        idx = _chunk_block_map(direction, batch, n_ctx_chunks, n_lat_chunks)
        out_specs.append(pl.BlockSpec((CHUNK, d), lambda b, s, idx=idx: (idx(b, s), 0)))
    for direction in range(2):
        sel = _selectors(direction, d)
        in_specs += [pl.BlockSpec(a.shape, lambda b, s: (0, 0)) for a in sel]
        args += list(sel)
    state = [pltpu.VMEM((SSD_STATE, d), F32)] * 2 + [pltpu.VMEM((n_mh, dk, 2 * MLSTM_DV), F32)] * 2 \
        + [pltpu.VMEM((PACK_ROWS, LANES), F32)] * 2
    return pl.pallas_call(
        functools.partial(_scans_kernel, n_sh=n_sh, n_mh=n_mh, dk=dk),
        out_shape=[jax.ShapeDtypeStruct((t_all, d), BF16)] * 4,
        grid=(batch, n_steps),
        in_specs=in_specs, out_specs=out_specs, scratch_shapes=state,
        compiler_params=pltpu.CompilerParams(dimension_semantics=("arbitrary", "arbitrary"),
                                             vmem_limit_bytes=VMEM_LIMIT),
        name="scans",
    )(*args)


def _merge_kernel(y0_ref, y1_ref, xs_ref, z_ref, h0_ref, h1_ref, og_ref, mg_ref, x_ref, mod_ref, dexp_ref, gs_ref,
                  gm_ref, gf_ref, wso_ref, wmo_ref, wo_ref, x1_ref, u2_ref, *, d_model):
    d = d_model
    m = mod_ref[0]
    y = y0_ref[...].astype(F32) + y1_ref[...].astype(F32) + dexp_ref[...] * xs_ref[...].astype(F32)
    zf = z_ref[...].astype(F32)
    y = y * (zf * _sigmoid(zf))
    y = y * lax.rsqrt(jnp.mean(y * y, axis=-1, keepdims=True) + EPS) * gs_ref[...]
    a = _dot(y.astype(BF16), wso_ref[...])
    hm = h0_ref[...].astype(F32) + h1_ref[...].astype(F32)
    parts = []
    for h in range(d // MLSTM_DV):
        blk = hm[:, h * MLSTM_DV:(h + 1) * MLSTM_DV]
        parts.append(blk * lax.rsqrt(jnp.mean(blk * blk, axis=-1, keepdims=True) + EPS))
    hn = jnp.concatenate(parts, axis=1) * gm_ref[...]
    hh = (og_ref[...].astype(F32) * hn).astype(BF16)
    bm = _dot(hh, wmo_ref[...])
    merged = mg_ref[:, :d].astype(F32) * a + mg_ref[:, d:].astype(F32) * bm
    r = _dot(merged.astype(BF16), wo_ref[...])
    x1 = x_ref[...] + m[2:3] * r
    x1_ref[...] = x1
    u2 = x1 * lax.rsqrt(jnp.mean(x1 * x1, axis=-1, keepdims=True) + EPS) * gf_ref[...] * (1.0 + m[4:5]) + m[3:4]
    u2_ref[...] = _pack_halves(u2)


def _merge(y0, y1, xs, z, h0, h1, og, mg, x2, mods3, dexp, gs, gm, gf, wso, wmo, wo, *, batch, seq, n_ctx_tok):
    t, d = x2.shape
    tm = TOKEN_TILE
    off = n_ctx_tok // tm
    tiles_per_batch = seq // tm
    lat = lambda i: (i + off, 0)
    row = lambda i: (i, 0)
    const = lambda i: (0, 0)
    kern = functools.partial(_merge_kernel, d_model=d)
    wspec = pl.BlockSpec((d, d), const, pipeline_mode=pl.Buffered(1))
    return pl.pallas_call(
        kern,
        out_shape=[jax.ShapeDtypeStruct((t, d), F32), jax.ShapeDtypeStruct((t, d // 2), U32)],
        grid=(t // tm,),
        in_specs=[pl.BlockSpec((tm, d), lat), pl.BlockSpec((tm, d), lat), pl.BlockSpec((tm, d), lat),
                  pl.BlockSpec((tm, d), lat), pl.BlockSpec((tm, d), lat), pl.BlockSpec((tm, d), lat),
                  pl.BlockSpec((tm, d), lat), pl.BlockSpec((tm, 2 * d), lat), pl.BlockSpec((tm, d), row),
                  pl.BlockSpec((1, 6, d), lambda i: (i // tiles_per_batch, 0, 0)),
                  pl.BlockSpec((1, d), const), pl.BlockSpec((1, d), const), pl.BlockSpec((1, d), const),
                  pl.BlockSpec((1, d), const), wspec, wspec, wspec],
        out_specs=[pl.BlockSpec((tm, d), row), pl.BlockSpec((tm, d // 2), row)],
        compiler_params=pltpu.CompilerParams(dimension_semantics=("arbitrary",), vmem_limit_bytes=VMEM_LIMIT),
        name="merge",
    )(y0, y1, xs, z, h0, h1, og, mg, x2, mods3, dexp, gs, gm, gf, wso, wmo, wo)


def _first_index_of_max(vals, row_iota, n_rows):
    mx = jnp.max(vals, axis=0, keepdims=True)
    idx = jnp.min(jnp.where(vals == mx, row_iota, n_rows), axis=0, keepdims=True)
    return mx, idx


def _router_kernel(u_ref, wt_ref, bias_ref, su_ref, idx_ref, pos_ref, wts_ref, cnt_ref, run_ref, *, n_experts, tr):
    i = pl.program_id(0)

    @pl.when(i == 0)
    def _():
        run_ref[...] = jnp.zeros_like(run_ref)

    ua, ub = _unpack_halves(u_ref[...])
    urow = jnp.concatenate([ua.astype(BF16), ub.astype(BF16)], axis=1)
    scores = _sigmoid(_dot_nt(wt_ref[...], urow))
    biased = scores + bias_ref[...]
    gsz = n_experts // N_EXPERT_GROUPS
    gi = lax.broadcasted_iota(I32, (gsz, tr), 0).astype(F32)
    gscores = []
    for g in range(N_EXPERT_GROUPS):
        blk = biased[g * gsz:(g + 1) * gsz]
        m1, i1 = _first_index_of_max(blk, gi, gsz)
        m2 = jnp.max(jnp.where(gi == i1, NEG_INF, blk), axis=0, keepdims=True)
        gscores.append(m1 + m2)
    gs = jnp.concatenate(gscores, axis=0)
    g8 = lax.broadcasted_iota(I32, (N_EXPERT_GROUPS, tr), 0).astype(F32)
    gsel = jnp.zeros((N_EXPERT_GROUPS, tr), F32)
    for _ in range(TOPK_GROUPS):
        _, gidx = _first_index_of_max(gs, g8, N_EXPERT_GROUPS)
        hit = g8 == gidx
        gsel = jnp.where(hit, 1.0, gsel)
        gs = jnp.where(hit, NEG_INF, gs)
    cand = jnp.concatenate(
        [jnp.where(jnp.broadcast_to(gsel[g:g + 1], (gsz, tr)) > 0.5, biased[g * gsz:(g + 1) * gsz], NEG_INF)
         for g in range(N_EXPERT_GROUPS)], axis=0)
    ei = lax.broadcasted_iota(I32, (n_experts, tr), 0).astype(F32)
    sel = jnp.zeros((n_experts, tr), F32)
    idxs, ws = [], []
    for _ in range(TOP_K):
        _, eidx = _first_index_of_max(cand, ei, n_experts)
        hit = ei == eidx
        ws.append(jnp.sum(jnp.where(hit, scores, 0.0), axis=0, keepdims=True))
        idxs.append(eidx)
        sel = jnp.where(hit, 1.0, sel)
        cand = jnp.where(hit, NEG_INF, cand)
    wk = jnp.concatenate(ws, axis=0)
    wts_ref[...] = ROUTED_SCALE * wk / jnp.sum(wk, axis=0, keepdims=True)
    idx_ref[...] = jnp.concatenate(idxs, axis=0).astype(I32)
    selb = sel.astype(BF16)
    posmat = _dot(selb, su_ref[...]) + run_ref[:, 0:1]
    pos_ref[...] = jnp.concatenate(
        [jnp.sum(jnp.where(ei == idxs[k], posmat, 0.0), axis=0, keepdims=True) for k in range(TOP_K)],
        axis=0).astype(I32)
    run = run_ref[...] + _dot(selb, jnp.ones((tr, LANES), BF16))
    run_ref[...] = run
    cnt_ref[...] = run


def _router(u2p, router_wt, bias_col):
    t = u2p.shape[0]
    d = router_wt.shape[1]
    n_experts = router_wt.shape[0]
    tr = TOKEN_TILE
    su = jnp.asarray(np.triu(np.ones((tr, tr), np.float32), 1), BF16)
    kern = functools.partial(_router_kernel, n_experts=n_experts, tr=tr)
    col = lambda i: (0, i)
    const = lambda i: (0, 0)
    return pl.pallas_call(
        kern,
        out_shape=[jax.ShapeDtypeStruct((TOP_K, t), I32), jax.ShapeDtypeStruct((TOP_K, t), I32),
                   jax.ShapeDtypeStruct((TOP_K, t), F32), jax.ShapeDtypeStruct((n_experts, LANES), F32)],
        grid=(t // tr,),
        in_specs=[pl.BlockSpec((tr, d // 2), lambda i: (i, 0)), pl.BlockSpec((n_experts, d), const),
                  pl.BlockSpec((n_experts, 1), const), pl.BlockSpec((tr, tr), const)],
        out_specs=[pl.BlockSpec((TOP_K, tr), col), pl.BlockSpec((TOP_K, tr), col), pl.BlockSpec((TOP_K, tr), col),
                   pl.BlockSpec((n_experts, LANES), const)],
        scratch_shapes=[pltpu.VMEM((n_experts, LANES), F32)],
        compiler_params=pltpu.CompilerParams(dimension_semantics=("arbitrary",), vmem_limit_bytes=VMEM_LIMIT),
        name="router",
    )(u2p, router_wt, bias_col, su)


def _slots_kernel(idx_ref, pos_ref, pstart_ref, dest_ref, *, n_experts, tr):
    ei = lax.broadcasted_iota(I32, (n_experts, tr), 0).astype(F32)
    pstart = pstart_ref[...]
    idx = idx_ref[...].astype(F32)
    rows = [jnp.sum(jnp.where(ei == idx[k:k + 1], pstart, 0.0), axis=0, keepdims=True) for k in range(TOP_K)]
    dest_ref[...] = jnp.concatenate(rows, axis=0).astype(I32) + pos_ref[...]


def _slots(idx, pos, pstart_col):
    t = idx.shape[1]
    n_experts = pstart_col.shape[0]
    tr = TOKEN_TILE
    col = lambda i: (0, i)
    return pl.pallas_call(
        functools.partial(_slots_kernel, n_experts=n_experts, tr=tr),
        out_shape=jax.ShapeDtypeStruct((TOP_K, t), I32),
        grid=(t // tr,),
        in_specs=[pl.BlockSpec((TOP_K, tr), col), pl.BlockSpec((TOP_K, tr), col),
                  pl.BlockSpec((n_experts, 1), lambda i: (0, 0))],
        out_specs=pl.BlockSpec((TOP_K, tr), col),
        compiler_params=pltpu.CompilerParams(dimension_semantics=("arbitrary",), vmem_limit_bytes=VMEM_LIMIT),
        name="slots",
    )(idx, pos, pstart_col)


PAD_BITS = tuple(1 << b for b in reversed(range((MOE_BLOCK - 1).bit_length())))


def _dispatch_kernel(pstart_ref, cnt_ref, dest_ref, u_ref, xs_ref, ubuf, zbuf, usem, sem, psem, *, td, n_steps,
                     n_experts):
    i = pl.program_id(0)
    slot = i % 2

    def u_copies(step, s):
        rows = pl.ds(pl.multiple_of(step * td, td), td)
        return [pltpu.make_async_copy(u_ref.at[rows, pl.ds(j * LANES, LANES)], ubuf.at[s, :, j, :], usem.at[s])
                for j in range(ROW_SUBLANES)]

    @pl.when(i == 0)
    def _():
        for cp in u_copies(i, 0):
            cp.start()
        zbuf[...] = jnp.zeros_like(zbuf)

        def pads(e, wait):
            cnt = cnt_ref[e]
            n_pad = (MOE_BLOCK - (cnt & (MOE_BLOCK - 1))) & (MOE_BLOCK - 1)
            base = pstart_ref[e] + cnt
            for bit in PAD_BITS:
                cp = pltpu.make_async_copy(zbuf.at[pl.ds(0, bit)], xs_ref.at[pl.ds(base, bit)], psem)
                has = (n_pad & bit) != 0

                @pl.when(has)
                def _():
                    if wait:
                        cp.wait()
                    else:
                        cp.start()

                base = base + jnp.where(has, bit, 0)

        def start_pads(e, carry):
            pads(e, False)
            return carry

        def wait_pads(e, carry):
            pads(e, True)
            return carry

        lax.fori_loop(0, n_experts, start_pads, 0)
        lax.fori_loop(0, n_experts, wait_pads, 0)

    @pl.when(i + 1 < n_steps)
    def _():
        for cp in u_copies(i + 1, 1 - slot):
            cp.start()

    for cp in u_copies(i, slot):
        cp.wait()

    def start(t, carry):
        for k in range(TOP_K):
            pltpu.make_async_copy(ubuf.at[slot, t], xs_ref.at[dest_ref[k, t]], sem).start(priority=k % 2)
        return carry

    def wait(t, carry):
        for k in range(TOP_K):
            pltpu.make_async_copy(ubuf.at[slot, 0], xs_ref.at[0], sem).wait()
        return carry

    lax.fori_loop(0, td, start, 0, unroll=4)
    lax.fori_loop(0, td, wait, 0, unroll=8)


def _dispatch(pad_start, counts, dest, u2p, n_slots):
    t, dh = u2p.shape
    td = DISPATCH_TILE
    n_experts = pad_start.shape[0]
    assert MOE_BLOCK & (MOE_BLOCK - 1) == 0 and dh == ROW_SUBLANES * LANES
    kern = functools.partial(_dispatch_kernel, td=td, n_steps=t // td, n_experts=n_experts)
    return pl.pallas_call(
        kern,
        out_shape=jax.ShapeDtypeStruct((n_slots, ROW_SUBLANES, LANES), U32),
        grid_spec=pltpu.PrefetchScalarGridSpec(
            num_scalar_prefetch=2, grid=(t // td,),
            in_specs=[pl.BlockSpec((TOP_K, td), lambda i, ps, cn: (0, i), memory_space=pltpu.SMEM),
                      pl.BlockSpec(memory_space=pl.ANY)],
            out_specs=pl.BlockSpec(memory_space=pl.ANY),
            scratch_shapes=[pltpu.VMEM((2, td, ROW_SUBLANES, LANES), U32),
                            pltpu.VMEM((MOE_BLOCK // 2, ROW_SUBLANES, LANES), U32),
                            pltpu.SemaphoreType.DMA((2,)), pltpu.SemaphoreType.DMA, pltpu.SemaphoreType.DMA]),
        compiler_params=pltpu.CompilerParams(dimension_semantics=("arbitrary",), vmem_limit_bytes=VMEM_LIMIT),
        name="dispatch",
    )(pad_start, counts, dest, u2p)


def _experts_kernel(bs_ref, nblk_ref, nu_ref, xs_ref, wg_ref, wu_ref, wd_ref, ys_ref, wgub, wdb, xbuf, ybuf,
                    semx, semy, *, n_experts):
    e = pl.program_id(0)
    n_used = nu_ref[0]

    def x_copies(g):
        rows = pl.ds(pl.multiple_of(g * MOE_BLOCK, MOE_BLOCK), MOE_BLOCK)
        return [pltpu.make_async_copy(xs_ref.at[rows, j, :], xbuf.at[g % X_BUFFERS, :, pl.ds(j * LANES, LANES)],
                                      semx.at[g % X_BUFFERS]) for j in range(ROW_SUBLANES)]

    def x_start(g):
        for cp in x_copies(g):
            cp.start()

    def x_wait(g):
        for cp in x_copies(g):
            cp.wait()

    def y_copies(g, slot):
        rows = pl.ds(pl.multiple_of(g * MOE_BLOCK, MOE_BLOCK), MOE_BLOCK)
        return [pltpu.make_async_copy(ybuf.at[slot, :, pl.ds(j * LANES, LANES)], ys_ref.at[rows, j, :], semy.at[slot])
                for j in range(ROW_SUBLANES)]

    def y_start(g, slot):
        for cp in y_copies(g, slot):
            cp.start()

    def y_wait(g, slot):
        for cp in y_copies(g, slot):
            cp.wait()

    @pl.when(e == 0)
    def _():
        for g in range(X_BUFFERS - 1):
            @pl.when(g < n_used)
            def _():
                x_start(jnp.int32(g))

    g0 = bs_ref[e]
    nb = nblk_ref[e]

    de = wg_ref.shape[2]

    @pl.when(nb > 0)
    def _():
        wgub[:, :de] = wg_ref[0].astype(BF16)
        wgub[:, de:] = wu_ref[0].astype(BF16)
        wdb[...] = wd_ref[0].astype(BF16)

    def block(g, carry):
        slot = g % 2

        @pl.when(g + X_BUFFERS - 1 < n_used)
        def _():
            x_start(g + X_BUFFERS - 1)

        x_wait(g)

        @pl.when(g >= 2)
        def _():
            y_wait(g - 2, slot)

        xa, xb = _unpack_halves(xbuf[g % X_BUFFERS])
        xrow = jnp.concatenate([xa.astype(BF16), xb.astype(BF16)], axis=1)
        hgu = _dot(xrow, wgub[...])
        hg = hgu[:, :de]
        hb = (hg * _sigmoid(hg) * hgu[:, de:]).astype(BF16)
        ybuf[slot] = _pack_halves(_dot(hb, wdb[...]))
        y_start(g, slot)
        return carry

    lax.fori_loop(g0, g0 + nb, block, 0)

    @pl.when(e == n_experts - 1)
    def _():
        @pl.when(n_used >= 2)
        def _():
            y_wait(n_used - 2, n_used % 2)

        @pl.when(n_used >= 1)
        def _():
            y_wait(n_used - 1, (n_used - 1) % 2)


def _experts(block_start, n_blocks, n_used, xsorted, wg, wu, wd):
    n_slots = xsorted.shape[0]
    dh = ROW_SUBLANES * LANES
    n_experts, d, de = wg.shape
    wmap = lambda e, bs, nb, nu: (e, 0, 0)
    assert xsorted.shape[1:] == (ROW_SUBLANES, LANES) and d == 2 * dh
    return pl.pallas_call(
        functools.partial(_experts_kernel, n_experts=n_experts),
        out_shape=jax.ShapeDtypeStruct((n_slots, ROW_SUBLANES, LANES), U32),
        grid_spec=pltpu.PrefetchScalarGridSpec(
            num_scalar_prefetch=3, grid=(n_experts,),
            in_specs=[pl.BlockSpec(memory_space=pl.ANY), pl.BlockSpec((1, d, de), wmap),
                      pl.BlockSpec((1, d, de), wmap), pl.BlockSpec((1, de, d), wmap)],
            out_specs=pl.BlockSpec(memory_space=pl.ANY),
            scratch_shapes=[pltpu.VMEM((d, 2 * de), BF16), pltpu.VMEM((de, d), BF16),
                            pltpu.VMEM((X_BUFFERS, MOE_BLOCK, dh), U32),
                            pltpu.VMEM((2, MOE_BLOCK, dh), U32),
                            pltpu.SemaphoreType.DMA((X_BUFFERS,)), pltpu.SemaphoreType.DMA((2,))]),
        compiler_params=pltpu.CompilerParams(dimension_semantics=("arbitrary",), vmem_limit_bytes=VMEM_LIMIT),
        name="experts",
    )(block_start, n_blocks, n_used, xsorted, wg, wu, wd)


def _combine_kernel(dest_ref, dnext_ref, ys_ref, wt_ref, x1_ref, u_ref, mod_ref, wsgu_ref, wsd_ref,
                    gfin_ref, o_ref, gbuf, accbuf, sem, *, tc, n_steps):
    i = pl.program_id(0)
    slot = i % 2

    def issue(dref, s):
        def body(t, carry):
            for k in range(TOP_K):
                pltpu.make_async_copy(ys_ref.at[dref[k, t]], gbuf.at[s, k, t], sem.at[s]).start(priority=k % 2)
            return carry
        lax.fori_loop(0, tc, body, 0, unroll=4)

    @pl.when(i == 0)
    def _():
        issue(dest_ref, 0)

    def wait(t, carry):
        for k in range(TOP_K):
            pltpu.make_async_copy(ys_ref.at[0], gbuf.at[slot, 0, 0], sem.at[slot]).wait()
        return carry

    lax.fori_loop(0, tc, wait, 0, unroll=8)

    def sum_and_issue(with_issue):
        def body(c, carry):
            t0 = c * GATHER_CHUNK
            if with_issue:
                for tt in range(GATHER_CHUNK):
                    for k in range(TOP_K):
                        pltpu.make_async_copy(ys_ref.at[dnext_ref[k, t0 + tt]], gbuf.at[1 - slot, k, t0 + tt],
                                              sem.at[1 - slot]).start(priority=k % 2)
            rows = pl.ds(t0, GATHER_CHUNK)
            wts = wt_ref[rows]
            sum_l = jnp.zeros((GATHER_CHUNK, ROW_SUBLANES, LANES), F32)
            sum_r = jnp.zeros((GATHER_CHUNK, ROW_SUBLANES, LANES), F32)
            for k in range(TOP_K):
                ga, gb = _unpack_halves(gbuf[slot, k, rows])
                wk = jnp.broadcast_to(wts[:, k:k + 1, :], (GATHER_CHUNK, ROW_SUBLANES, LANES))
                sum_l = sum_l + ga * wk
                sum_r = sum_r + gb * wk
            accbuf[0, rows] = sum_l
            accbuf[1, rows] = sum_r
            return carry
        lax.fori_loop(0, tc // GATHER_CHUNK, body, 0)

    @pl.when(i + 1 < n_steps)
    def _():
        sum_and_issue(True)

    @pl.when(i + 1 >= n_steps)
    def _():
        sum_and_issue(False)

    ua, ub = _unpack_halves(u_ref[...])
    half = ua.shape[1]
    hgu = _dot(jnp.concatenate([ua.astype(BF16), ub.astype(BF16)], axis=1), wsgu_ref[...])
    dsh = wsd_ref.shape[0]
    hg = hgu[:, :dsh]
    shared = _dot((hg * _sigmoid(hg) * hgu[:, dsh:]).astype(BF16), wsd_ref[...])
    acc_l = shared[:, :half]
    acc_r = shared[:, half:]
    acc_l = jnp.concatenate([acc_l[:, j * LANES:(j + 1) * LANES] + accbuf[0, :, j, :] for j in range(ROW_SUBLANES)],
                            axis=1)
    acc_r = jnp.concatenate([acc_r[:, j * LANES:(j + 1) * LANES] + accbuf[1, :, j, :] for j in range(ROW_SUBLANES)],
                            axis=1)
    m = mod_ref[0]
    xo_l = x1_ref[:, :half] + m[5:6, :half] * acc_l
    xo_r = x1_ref[:, half:] + m[5:6, half:] * acc_r
    ms = (jnp.sum(xo_l * xo_l, axis=-1, keepdims=True) + jnp.sum(xo_r * xo_r, axis=-1, keepdims=True)) / (2 * half)
    inv = lax.rsqrt(ms + EPS)
    o_ref[:, :half] = xo_l * inv * gfin_ref[:, :half]
    o_ref[:, half:] = xo_r * inv * gfin_ref[:, half:]


def _combine(dest, ysorted, wts_t, x1, u2p, mods3, wsgu, wsd, gfin, *, seq):
    t, d = x1.shape
    dh = d // 2
    tc = DISPATCH_TILE
    tiles_per_batch = seq // tc
    n_steps = t // tc
    kern = functools.partial(_combine_kernel, tc=tc, n_steps=n_steps)
    row = lambda i: (i, 0)
    const = lambda i: (0, 0)
    return pl.pallas_call(
        kern,
        out_shape=jax.ShapeDtypeStruct((t, d), F32),
        grid=(n_steps,),
        in_specs=[pl.BlockSpec((TOP_K, tc), lambda i: (0, i), memory_space=pltpu.SMEM),
                  pl.BlockSpec((TOP_K, tc), lambda i: (0, jnp.minimum(i + 1, n_steps - 1)), memory_space=pltpu.SMEM),
                  pl.BlockSpec(memory_space=pl.ANY), pl.BlockSpec((tc, TOP_K, LANES), lambda i: (i, 0, 0)),
                  pl.BlockSpec((tc, d), row), pl.BlockSpec((tc, dh), row),
                  pl.BlockSpec((1, 6, d), lambda i: (i // tiles_per_batch, 0, 0)),
                  pl.BlockSpec(wsgu.shape, const), pl.BlockSpec(wsd.shape, const), pl.BlockSpec((1, d), const)],
        out_specs=pl.BlockSpec((tc, d), row),
        scratch_shapes=[pltpu.VMEM((2, TOP_K, tc, ROW_SUBLANES, LANES), U32),
                        pltpu.VMEM((2, tc, ROW_SUBLANES, LANES), F32), pltpu.SemaphoreType.DMA((2,))],
        compiler_params=pltpu.CompilerParams(dimension_semantics=("arbitrary",), vmem_limit_bytes=VMEM_LIMIT),
        name="combine",
    )(dest, dest, ysorted, wts_t, x1, u2p, mods3, wsgu, wsd, gfin)


def kernel(x, c, ctx, c_ctx, ada_w, ada_b, norm_mix_g, norm_ffn_g, w_in, conv_xbc_w, conv_xbc_b, ssd_dt_bias, ssd_a_log, ssd_d, ssd_norm_g, conv_qk_w, conv_qk_b, mlstm_i_bias, mlstm_f_bias, mlstm_norm_g, w_ssd_out, w_mlstm_out, w_out, router_w, router_bias, moe_w_gate, moe_w_up, moe_w_down, shared_w_gate, shared_w_up, shared_w_down, norm_final_g):
    batch, seq, d = x.shape
    ctx_len = ctx.shape[1]
    depth = ada_w.shape[0]
    assert depth == 1, "only the single-layer configuration is implemented"
    assert seq % CHUNK == 0 and ctx_len % CHUNK == 0 and seq % GRID_W == 0
    l = 0
    n_sh = d // SSD_HEAD_DIM
    n_mh = d // MLSTM_DV
    dk = MLSTM_DV // 2
    d_xbc = d + 2 * SSD_GROUPS * SSD_STATE
    d_qk = 2 * n_mh * dk
    sizes = (d, d_xbc, 2 * n_sh, d_qk, d, 4 * n_mh, d, 2 * d)
    offs = np.concatenate([[0], np.cumsum(sizes)])
    assert offs[-1] == w_in.shape[2] and 2 * n_sh + 2 * n_mh <= LANES

    cond = jnp.concatenate([c, c_ctx[None], jnp.zeros((8 - (batch + 1) % 8, d), F32)], axis=0)
    mods = _adaln(cond, ada_w[l], ada_b[l])
    mods3 = mods.reshape(mods.shape[0], 6, d)

    w = w_in[l]
    seg = lambda k: w[:, offs[k]:offs[k + 1]]
    wbig = jnp.concatenate([seg(0), seg(1), seg(3), seg(4), seg(6), seg(7)], axis=1).astype(BF16)
    w_dt = seg(2)
    w_g = seg(5).reshape(d, 2, 2, n_mh)
    w_i = w_g[:, :, 0].reshape(d, 2 * n_mh)
    w_f = w_g[:, :, 1].reshape(d, 2 * n_mh)
    pad = jnp.zeros((d, LANES - 2 * n_sh - 2 * n_mh), F32)
    wsm = jnp.concatenate([w_dt, w_f, pad, w_dt, w_i, pad], axis=1)
    wsh = wsm.astype(BF16)
    wsl = (wsm - wsh.astype(F32)).astype(BF16)
    padb = jnp.zeros((LANES - 2 * n_sh - 2 * n_mh,), F32)
    dtb = ssd_dt_bias[l].reshape(-1).astype(F32)
    smb = jnp.concatenate([dtb, mlstm_f_bias[l].reshape(-1).astype(F32), padb,
                           dtb, mlstm_i_bias[l].reshape(-1).astype(F32), padb]).reshape(1, 2 * LANES)
    aneg = jnp.concatenate([-jnp.exp(ssd_a_log[l].astype(F32)).reshape(-1),
                            jnp.zeros((LANES - 2 * n_sh,), F32)]).reshape(1, LANES)

    x2 = x.reshape(batch * seq, d)
    ctx2 = ctx.reshape(batch * ctx_len, d)
    z, xs, bc, qk, v, og, mg, small, smallt = _inproj(
        x2, ctx2, mods3, norm_mix_g[l].reshape(1, d), wbig, wsh, wsl,
        conv_xbc_w[l], conv_xbc_b[l].reshape(1, d_xbc), conv_qk_w[l], conv_qk_b[l].reshape(1, d_qk), smb, aneg,
        batch=batch, seq=seq, ctx_len=ctx_len, dk=dk)

    ncc = ctx_len // CHUNK
    ncl = seq // CHUNK
    y0, y1, h0, h1 = _scans(xs, bc, qk, v, small, smallt, batch=batch, n_ctx_chunks=ncc, n_lat_chunks=ncl, dk=dk)

    dexp = jnp.repeat(ssd_d[l].astype(F32), SSD_HEAD_DIM).reshape(1, d)
    x1, u2 = _merge(y0, y1, xs, z, h0, h1, og, mg, x2, mods3, dexp, ssd_norm_g[l].reshape(1, d),
                    mlstm_norm_g[l].reshape(1, d), norm_ffn_g[l].reshape(1, d),
                    w_ssd_out[l].astype(BF16), w_mlstm_out[l].astype(BF16), w_out[l].astype(BF16),
                    batch=batch, seq=seq, n_ctx_tok=batch * ctx_len)

    n_experts = router_w.shape[2]
    idx, pos, wts, cnt = _router(u2, router_w[l].T.astype(BF16), router_bias[l].astype(F32).reshape(n_experts, 1))
    counts = cnt[:, 0].astype(I32)
    padded = (counts + MOE_BLOCK - 1) // MOE_BLOCK * MOE_BLOCK
    pad_end = jnp.cumsum(padded)
    pad_start = (pad_end - padded).astype(I32)
    t = batch * seq
    nb = t * TOP_K // MOE_BLOCK + n_experts
    n_used = (pad_end[-1] // MOE_BLOCK).astype(I32).reshape(1)

    dest = _slots(idx, pos, pad_start.astype(F32).reshape(n_experts, 1))
    xsorted = _dispatch(pad_start, counts, dest, u2, nb * MOE_BLOCK)
    ysorted = _experts(pad_start // MOE_BLOCK, (padded // MOE_BLOCK).astype(I32), n_used, xsorted,
                       moe_w_gate[l], moe_w_up[l], moe_w_down[l])
    wts_lanes = jnp.broadcast_to(wts.T[:, :, None], (t, TOP_K, LANES))
    out = _combine(dest, ysorted, wts_lanes, x1, u2, mods3,
                   jnp.concatenate([shared_w_gate[l], shared_w_up[l]], axis=1).astype(BF16), shared_w_down[l].astype(BF16),
                   norm_final_g.reshape(1, d), seq=seq)
    return out.reshape(batch, seq, d)
```

```python
import functools

import numpy as np
import jax
import jax.numpy as jnp
from jax import lax
from jax.experimental import pallas as pl
from jax.experimental.pallas import tpu as pltpu

F32 = jnp.float32
BF16 = jnp.bfloat16
I32 = jnp.int32
U32 = jnp.uint32

EPS = 1e-6
CHUNK = 128
CONV_K = 5
GRID_W = 64
SSD_HEAD_DIM = 64
SSD_STATE = 128
SSD_GROUPS = 2
MLSTM_DV = 128
N_EXPERT_GROUPS = 8
TOPK_GROUPS = 4
TOP_K = 8
ROUTED_SCALE = 2.5

LANES = 128
SUBLANES = 8
PACK_ROWS = 16
TOKEN_TILE = 512
COL_CHUNK = 512
MERGE_PARTS = 4
INPROJ_PARTS = 2
MOE_BLOCK = 512
DISPATCH_TILE = 256
GATHER_CHUNK = 4
X_BUFFERS = 4
ROW_SUBLANES = 4
VMEM_LIMIT = 56 * 1024 * 1024
NEG_INF = float("-inf")


def _dot(a, b):
    return jnp.dot(a, b, preferred_element_type=F32)


def _dot_nt(a, b):
    return lax.dot_general(a, b, (((1,), (1,)), ((), ())), preferred_element_type=F32)


def _dot_tn(a, b):
    return lax.dot_general(a, b, (((0,), (0,)), ((), ())), preferred_element_type=F32)


def _spread(parts, e):
    res = _dot(jnp.concatenate(parts, axis=0).astype(BF16), e)
    out, r0 = [], 0
    for p in parts:
        out.append(res[r0:r0 + p.shape[0]])
        r0 += p.shape[0]
    return out


def _rows16(row):
    r8 = jnp.broadcast_to(row, (SUBLANES, row.shape[1]))
    return jnp.concatenate([r8, r8], axis=0)


def _sigmoid(v):
    return 1.0 / (1.0 + jnp.exp(-v))


def _pack_halves(v):
    n = v.shape[1] // 2
    hi = lax.bitcast_convert_type(v[:, :n].astype(BF16).astype(F32), U32)
    lo = lax.bitcast_convert_type(v[:, n:].astype(BF16).astype(F32), U32)
    return hi | (lo >> 16)


def _unpack_halves(p):
    left = lax.bitcast_convert_type(p & jnp.uint32(0xFFFF0000), F32)
    right = lax.bitcast_convert_type(p << 16, F32)
    return left, right


def _softplus(v):
    return jnp.maximum(v, 0.0) + jnp.log1p(jnp.exp(-jnp.abs(v)))


def _adaln_kernel(c_ref, w_ref, b_ref, o_ref):
    c = c_ref[...]
    s = c * _sigmoid(c)
    w = w_ref[...]
    s_hi = s.astype(BF16)
    s_lo = (s - s_hi.astype(F32)).astype(BF16)
    w_hi = w.astype(BF16)
    w_lo = (w - w_hi.astype(F32)).astype(BF16)
    o_ref[...] = _dot(s_hi, w_hi) + _dot(s_lo, w_hi) + _dot(s_hi, w_lo) + b_ref[...]


def _adaln(cond, w, b):
    rows, d = cond.shape
    n = w.shape[1]
    tn = 1536 if n % 1536 == 0 else n
    return pl.pallas_call(
        _adaln_kernel,
        out_shape=jax.ShapeDtypeStruct((rows, n), F32),
        grid=(n // tn,),
        in_specs=[pl.BlockSpec((rows, d), lambda j: (0, 0)),
                  pl.BlockSpec((d, tn), lambda j: (0, j)),
                  pl.BlockSpec((1, tn), lambda j: (0, j))],
        out_specs=pl.BlockSpec((rows, tn), lambda j: (0, j)),
        compiler_params=pltpu.CompilerParams(dimension_semantics=("arbitrary",), vmem_limit_bytes=VMEM_LIMIT),
        name="adaln",
    )(cond, w, b.reshape(1, n))


CONV_SHIFTS = tuple(j - CONV_K // 2 for j in range(CONV_K) if j != CONV_K // 2)


def _conv_masks(tm, seg_len):
    pos = np.arange(tm) % seg_len
    m = np.zeros((tm, 8), np.float32)
    for i, s in enumerate(CONV_SHIFTS):
        m[:, i] = ((pos + s >= 0) & (pos + s < seg_len)).astype(np.float32)
    return m


def _conv_silu(acc, w5, bias, vm, tm):
    out = acc * w5[CONV_K // 2:CONV_K // 2 + 1] + bias
    for i, s in enumerate(CONV_SHIFTS):
        shifted = pltpu.roll(acc, (-s) % tm, axis=0)
        j = s + CONV_K // 2
        out = out + (shifted * vm[:, i:i + 1]) * w5[j:j + 1]
    return out * _sigmoid(out)


def _inproj_kernel(x_ref, ctx_ref, mod_ref, g_ref, wbig_ref, wsh_ref, wsl_ref, cwx_ref, cbx_ref, cwq_ref, cbq_ref,
                   smb_ref, aneg_ref, tril_ref, triu_ref, cmask_ref,
                   z_ref, xs_ref, bc_ref, qk_ref, v_ref, og_ref, mg_ref, small_ref, smallt_ref,
                   *, n_ctx_tiles, tm, d_model, dk):
    i = pl.program_id(0)
    is_ctx = i < n_ctx_tiles
    xt = jnp.where(is_ctx, ctx_ref[...], x_ref[...])
    m = mod_ref[0]
    ms = jnp.mean(xt * xt, axis=-1, keepdims=True)
    u = xt * lax.rsqrt(ms + EPS) * g_ref[...] * (1.0 + m[1:2]) + m[0:1]
    u_hi = u.astype(BF16)
    u_lo = (u - u_hi.astype(F32)).astype(BF16)
    vm = jnp.where(is_ctx, cmask_ref[1], cmask_ref[0])

    d = d_model
    d_bc = 2 * SSD_GROUPS * SSD_STATE
    pieces = [(z_ref, d, "plain", None), (xs_ref, d, "convx", 0), (bc_ref, d_bc, "convx", d),
              (qk_ref, d, "convq", 0), (v_ref, d, "plain", None), (og_ref, d, "sigmoid", None),
              (mg_ref, 2 * d, "sigmoid", None)]
    col = 0
    rp = tm // INPROJ_PARTS
    row_parts = [slice(p * rp, (p + 1) * rp) for p in range(INPROJ_PARTS)]
    u_parts = [u_hi[rows] for rows in row_parts]
    vm_parts = [vm[rows] for rows in row_parts]
    for ref, width, kind, coff in pieces:
        for c0 in range(0, width, COL_CHUNK):
            accs = [_dot(up, wbig_ref[:, col + c0:col + c0 + COL_CHUNK]) for up in u_parts]
            for rows, vmp, acc in zip(row_parts, vm_parts, accs):
                if kind == "convx":
                    cs = coff + c0
                    acc = _conv_silu(acc, cwx_ref[:, cs:cs + COL_CHUNK], cbx_ref[:, cs:cs + COL_CHUNK], vmp, rp)
                elif kind == "convq":
                    acc = _conv_silu(acc, cwq_ref[:, c0:c0 + COL_CHUNK], cbq_ref[:, c0:c0 + COL_CHUNK], vmp, rp)
                    if c0 < width // 2:
                        acc = acc * (dk ** -0.5)
                elif kind == "sigmoid":
                    acc = _sigmoid(acc)
                ref[rows, c0:c0 + COL_CHUNK] = acc.astype(ref.dtype)
        col += width

    wsh = wsh_ref[...]
    raw = _dot(u_hi, wsh) + _dot(u_lo, wsh) + _dot(u_hi, wsl_ref[...]) + smb_ref[...]
    p1 = raw[:, :LANES]
    p2 = raw[:, LANES:]
    lane = lax.broadcasted_iota(I32, (tm, LANES), 1)
    n_dt = 2 * (d_model // SSD_HEAD_DIM)
    n_g = 2 * (d_model // MLSTM_DV)
    is_dt = lane < n_dt
    is_gate = jnp.logical_and(lane >= n_dt, lane < n_dt + n_g)
    dt = _softplus(p2)
    pa = jnp.where(is_dt, dt * aneg_ref[...], jnp.where(is_gate, -_softplus(-p1), 0.0))
    pb = jnp.where(is_dt, dt, jnp.where(is_gate, p2, 0.0))
    lane_c = lax.broadcasted_iota(I32, (CHUNK, LANES), 1)
    is_dt_c = lane_c < n_dt
    rev = jnp.logical_or(jnp.logical_and(lane_c >= n_dt // 2, lane_c < n_dt),
                         jnp.logical_and(lane_c >= n_dt + n_g // 2, lane_c < n_dt + n_g))
    tril = tril_ref[...]
    triu = triu_ref[...]
    tq = lax.broadcasted_iota(I32, (CHUNK, LANES), 0)
    chunks = range(tm // CHUNK)
    planes_a, planes_b = [], []
    for c in chunks:
        a_c = pa[c * CHUNK:(c + 1) * CHUNK]
        hi = a_c.astype(BF16)
        r1 = a_c - hi.astype(F32)
        mid = r1.astype(BF16)
        lo = (r1 - mid.astype(F32)).astype(BF16)
        cs_f = _dot(tril, hi) + _dot(tril, mid) + _dot(tril, lo)
        cs_b = _dot(triu, hi) + _dot(triu, mid) + _dot(triu, lo)
        planes_a.append(jnp.where(rev, cs_b, cs_f))
    for c in chunks:
        pb_c = pb[c * CHUNK:(c + 1) * CHUNK]
        planes_b.append(jnp.where(is_dt_c, pb_c, pb_c - planes_a[c]))
    yfs = list(planes_b)
    ybs = list(planes_b)
    s = 1
    while s < CHUNK:
        for c in chunks:
            sh = pltpu.roll(yfs[c], s, axis=0)
            yfs[c] = jnp.maximum(yfs[c], jnp.where(tq >= s, sh, NEG_INF))
            sh = pltpu.roll(ybs[c], CHUNK - s, axis=0)
            ybs[c] = jnp.maximum(ybs[c], jnp.where(tq + s < CHUNK, sh, NEG_INF))
        s *= 2
    for c in chunks:
        r0 = c * CHUNK
        small_ref[r0:r0 + CHUNK, 0:LANES] = planes_a[c]
        small_ref[r0:r0 + CHUNK, LANES:2 * LANES] = planes_b[c]
        small_ref[r0:r0 + CHUNK, 2 * LANES:3 * LANES] = jnp.where(rev, ybs[c], yfs[c])
        smallt_ref[c, 0] = planes_a[c].T
        smallt_ref[c, 1] = planes_b[c].T


def _inproj(x2, ctx2, mods3, g, wbig, wsh, wsl, cwx, cbx, cwq, cbq, smb, aneg, *, batch, seq, ctx_len, dk):
    d = x2.shape[1]
    tm = TOKEN_TILE
    n_ctx_tok = batch * ctx_len
    rp = tm // INPROJ_PARTS
    assert n_ctx_tok % tm == 0 and seq % tm == 0 and rp % ctx_len == 0 and rp % GRID_W == 0
    n_ctx_tiles = n_ctx_tok // tm
    tiles_per_batch = seq // tm
    n_tiles = n_ctx_tiles + batch * tiles_per_batch
    t_all = n_tiles * tm
    n_big = wbig.shape[1]
    tril = jnp.asarray(np.tril(np.ones((CHUNK, CHUNK), np.float32)), BF16)
    triu = jnp.asarray(np.triu(np.ones((CHUNK, CHUNK), np.float32)), BF16)

    def x_map(i):
        return (jnp.maximum(i - n_ctx_tiles, 0), 0)

    def ctx_map(i):
        return (jnp.minimum(i, n_ctx_tiles - 1), 0)

    def mod_map(i):
        return (jnp.where(i < n_ctx_tiles, batch, jnp.maximum(i - n_ctx_tiles, 0) // tiles_per_batch), 0, 0)

    const = lambda i: (0, 0)
    row = lambda i: (i, 0)
    cmask = jnp.asarray(np.stack([_conv_masks(tm, GRID_W), _conv_masks(tm, ctx_len)]))
    kern = functools.partial(_inproj_kernel, n_ctx_tiles=n_ctx_tiles, tm=tm, d_model=d, dk=dk)
    d_bc = 2 * SSD_GROUPS * SSD_STATE
    outs = [jax.ShapeDtypeStruct((t_all, d), BF16), jax.ShapeDtypeStruct((t_all, d), BF16),
            jax.ShapeDtypeStruct((t_all, d_bc), BF16), jax.ShapeDtypeStruct((t_all, d), BF16),
            jax.ShapeDtypeStruct((t_all, d), BF16), jax.ShapeDtypeStruct((t_all, d), BF16),
            jax.ShapeDtypeStruct((t_all, 2 * d), BF16), jax.ShapeDtypeStruct((t_all, 3 * LANES), F32),
            jax.ShapeDtypeStruct((t_all // CHUNK, 2, LANES, CHUNK), F32)]
    out_specs = [pl.BlockSpec((tm, d), row), pl.BlockSpec((tm, d), row), pl.BlockSpec((tm, d_bc), row),
                 pl.BlockSpec((tm, d), row), pl.BlockSpec((tm, d), row), pl.BlockSpec((tm, d), row),
                 pl.BlockSpec((tm, 2 * d), row), pl.BlockSpec((tm, 3 * LANES), row),
                 pl.BlockSpec((tm // CHUNK, 2, LANES, CHUNK), lambda i: (i, 0, 0, 0))]
    in_specs = [pl.BlockSpec((tm, d), x_map), pl.BlockSpec((tm, d), ctx_map),
                pl.BlockSpec((1, 6, d), mod_map), pl.BlockSpec((1, d), const),
                pl.BlockSpec((d, n_big), const, pipeline_mode=pl.Buffered(1)),
                pl.BlockSpec((d, 2 * LANES), const), pl.BlockSpec((d, 2 * LANES), const),
                pl.BlockSpec(cwx.shape, const), pl.BlockSpec(cbx.shape, const),
                pl.BlockSpec(cwq.shape, const), pl.BlockSpec(cbq.shape, const),
                pl.BlockSpec((1, 2 * LANES), const), pl.BlockSpec((1, LANES), const),
                pl.BlockSpec((CHUNK, CHUNK), const), pl.BlockSpec((CHUNK, CHUNK), const),
                pl.BlockSpec((2, tm, 8), lambda i: (0, 0, 0))]
    return pl.pallas_call(
        kern, out_shape=outs, grid=(n_tiles,), in_specs=in_specs, out_specs=out_specs,
        compiler_params=pltpu.CompilerParams(dimension_semantics=("arbitrary",), vmem_limit_bytes=VMEM_LIMIT),
        name="inproj",
    )(x2, ctx2, mods3, g, wbig, wsh, wsl, cwx, cbx, cwq, cbq, smb, aneg, tril, triu, cmask)


def _chunk_block_map(direction, batch, n_ctx_chunks, n_lat_chunks):
    def idx(b, s):
        if direction == 0:
            c_ctx = s
            c_lat = s - n_ctx_chunks
        else:
            c_ctx = n_ctx_chunks - 1 - s
            c_lat = n_lat_chunks - 1 - (s - n_ctx_chunks)
        return jnp.where(s < n_ctx_chunks, b * n_ctx_chunks + c_ctx, batch * n_ctx_chunks + b * n_lat_chunks + c_lat)
    return idx


def _ssd_kernel(xs_ref, bc_ref, small_ref, smallt_ref, e16_ref, y_ref, st_ref, *, direction, n_heads):
    s = pl.program_id(1)

    @pl.when(s == 0)
    def _():
        st_ref[...] = jnp.zeros_like(st_ref)

    hpg = n_heads // SSD_GROUPS
    lane0 = direction * n_heads
    last = CHUNK - 1 if direction == 0 else 0
    lane = lax.broadcasted_iota(I32, (CHUNK, LANES), 1)
    lm = jnp.logical_and(lane >= lane0, lane < lane0 + n_heads)
    plane_a = jnp.where(lm, small_ref[:, 0:LANES], 0.0)
    plane_b = jnp.where(lm, small_ref[:, LANES:2 * LANES], 0.0)
    cum_t = smallt_ref[0, 0]
    tot = plane_a[last:last + 1]
    lane16 = lax.broadcasted_iota(I32, (PACK_ROWS, LANES), 1)
    lm16 = jnp.logical_and(lane16 >= lane0, lane16 < lane0 + n_heads)
    dec_rows = jnp.where(lm16, jnp.exp(_rows16(tot)), 0.0)
    dtx, ecx, wx, decx = _spread(
        [plane_b, jnp.where(lm, jnp.exp(plane_a), 0.0), plane_b * jnp.exp(tot - plane_a), dec_rows], e16_ref[...])
    decx = decx[0:1]

    xf = xs_ref[...].astype(F32)
    xdt = xf * dtx
    xw = (xf * wx).astype(BF16)
    plane64 = lax.broadcasted_iota(I32, (CHUNK, 2 * SSD_HEAD_DIM), 1) < SSD_HEAD_DIM
    iq = lax.broadcasted_iota(I32, (CHUNK, CHUNK), 0)
    ik = lax.broadcasted_iota(I32, (CHUNK, CHUNK), 1)
    mask = (iq >= ik) if direction == 0 else (iq <= ik)
    gw = hpg * SSD_HEAD_DIM
    groups = range(SSD_GROUPS)
    bgs = [bc_ref[:, g * SSD_STATE:(g + 1) * SSD_STATE] for g in groups]
    cgs = [bc_ref[:, (SSD_GROUPS + g) * SSD_STATE:(SSD_GROUPS + g + 1) * SSD_STATE] for g in groups]
    sgs = [st_ref[:, g * gw:(g + 1) * gw] for g in groups]
    cbs = [_dot_nt(cgs[g], bgs[g]) for g in groups]
    y_inters = [_dot(cgs[g], sgs[g].astype(BF16)) * ecx[:, g * gw:(g + 1) * gw] for g in groups]
    m_hs = []
    for h in range(n_heads):
        cum_q = jnp.broadcast_to(plane_a[:, lane0 + h:lane0 + h + 1], (CHUNK, CHUNK))
        seg = cum_q - cum_t[lane0 + h:lane0 + h + 1, :]
        m_hs.append((cbs[h // hpg] * jnp.exp(jnp.where(mask, seg, NEG_INF))).astype(BF16))
    for pair in range(n_heads // 2):
        g = (2 * pair) // hpg
        c0 = 2 * pair * SSD_HEAD_DIM
        xpair = xdt[:, c0:c0 + 2 * SSD_HEAD_DIM]
        acc = y_inters[g][:, c0 - g * gw:c0 - g * gw + 2 * SSD_HEAD_DIM]
        for par in range(2):
            keep = plane64 if par == 0 else jnp.logical_not(plane64)
            acc = acc + _dot(m_hs[2 * pair + par], jnp.where(keep, xpair, 0.0).astype(BF16))
        y_ref[:, c0:c0 + 2 * SSD_HEAD_DIM] = acc.astype(y_ref.dtype)
    for g in groups:
        st_ref[:, g * gw:(g + 1) * gw] = (decx[:, g * gw:(g + 1) * gw] * sgs[g]
                                          + _dot_tn(bgs[g], xw[:, g * gw:(g + 1) * gw]))


def _selectors(direction, d):
    n_sh = d // SSD_HEAD_DIM
    n_mh = d // MLSTM_DV
    e16 = np.zeros((LANES, d), np.float32)
    em = np.zeros((LANES, d), np.float32)
    for h in range(n_sh):
        e16[direction * n_sh + h, h * SSD_HEAD_DIM:(h + 1) * SSD_HEAD_DIM] = 1.0
    for h in range(n_mh):
        em[2 * n_sh + direction * n_mh + h, h * MLSTM_DV:(h + 1) * MLSTM_DV] = 1.0
    return jnp.asarray(e16, BF16), jnp.asarray(em, BF16)


def _mlstm_kernel(qk_ref, v_ref, small_ref, smallt_ref, em_ref, h_ref, cn_ref, m_ref, *, direction, n_heads, dk):
    s = pl.program_id(1)

    @pl.when(s == 0)
    def _():
        cn_ref[...] = jnp.zeros_like(cn_ref)
        m_ref[...] = jnp.zeros_like(m_ref)

    n_dt = 2 * (n_heads * MLSTM_DV // SSD_HEAD_DIM)
    lane0 = n_dt + direction * n_heads
    last = CHUNK - 1 if direction == 0 else 0
    lane = lax.broadcasted_iota(I32, (CHUNK, LANES), 1)
    lm = jnp.logical_and(lane >= lane0, lane < lane0 + n_heads)
    b_q = jnp.where(lm, small_ref[:, 0:LANES], 0.0)
    r_k = jnp.where(lm, small_ref[:, LANES:2 * LANES], 0.0)
    cmr = jnp.where(lm, small_ref[:, 2 * LANES:3 * LANES], 0.0)
    r_t = smallt_ref[0, 1]
    m_all = m_ref[...]
    m_row = m_all[0:1]
    mm = jnp.maximum(cmr, m_row)
    w_state = jnp.exp(m_row - mm)
    e_mq = jnp.exp(-(b_q + mm))
    m_base8 = jnp.maximum(m_all, _rows16(cmr[last:last + 1]))
    m_base = m_base8[0:1]
    w_k = jnp.where(lm, jnp.exp(r_k - m_base), 0.0)
    lane16 = lax.broadcasted_iota(I32, (PACK_ROWS, LANES), 1)
    lm16 = jnp.logical_and(lane16 >= lane0, lane16 < lane0 + n_heads)
    dec_rows = jnp.where(lm16, jnp.exp(m_all - m_base8), 0.0)
    wsx, emqx, wkx, decx = _spread(
        [jnp.where(lm, w_state, 0.0), jnp.where(lm, e_mq, 0.0), w_k, dec_rows], em_ref[...])
    decx = decx[0:1]
    m_ref[...] = jnp.where(lm16, _rows16(b_q[last:last + 1]) + m_base8, 0.0)

    iq = lax.broadcasted_iota(I32, (CHUNK, CHUNK), 0)
    ik = lax.broadcasted_iota(I32, (CHUNK, CHUNK), 1)
    mask = (iq >= ik) if direction == 0 else (iq <= ik)
    ones = jnp.ones((CHUNK, MLSTM_DV), BF16)
    d_qk = n_heads * dk
    heads = range(n_heads)
    hsl = [slice(h * MLSTM_DV, (h + 1) * MLSTM_DV) for h in heads]
    qs = [qk_ref[:, h * dk:(h + 1) * dk] for h in heads]
    ks = [qk_ref[:, d_qk + h * dk:d_qk + (h + 1) * dk] for h in heads]
    vs = [v_ref[:, hsl[h]] for h in heads]
    cns = [cn_ref[h] for h in heads]
    scores = [_dot_nt(qs[h], ks[h]) for h in heads]
    inter = [_dot(qs[h], cns[h].astype(BF16)) for h in heads]
    smats = []
    for h in heads:
        mm_q = jnp.broadcast_to(mm[:, lane0 + h:lane0 + h + 1], (CHUNK, CHUNK))
        dmat = jnp.exp(jnp.where(mask, r_t[lane0 + h:lane0 + h + 1, :] - mm_q, NEG_INF))
        smats.append((scores[h] * dmat).astype(BF16))
    tots = [_dot(smats[h], jnp.concatenate([vs[h], ones], axis=1)) for h in heads]
    for h in heads:
        hs = hsl[h]
        kh, vh, cn = ks[h], vs[h], cns[h]
        wsh = wsx[:, hs]
        tot = tots[h] + jnp.concatenate([wsh, wsh], axis=1) * inter[h]
        num = tot[:, :MLSTM_DV]
        den = tot[:, MLSTM_DV:]
        h_ref[:, hs] = (num / jnp.maximum(jnp.abs(den), emqx[:, hs])).astype(h_ref.dtype)
        wkh = wkx[:, hs]
        rhs = jnp.concatenate([(vh.astype(F32) * wkh).astype(BF16), wkh.astype(BF16)], axis=1)
        dech = decx[:, hs]
        cn_ref[h] = jnp.concatenate([dech, dech], axis=1) * cn + _dot_tn(kh, rhs)


def _scans_kernel(*refs, n_sh, n_mh, dk):
    (xs0, bc0, qk0, v0, sm0, smt0, xs1, bc1, qk1, v1, sm1, smt1, e16_0, em_0, e16_1, em_1,
     y0, y1, h0, h1, st0, st1, cn0, cn1, m0, m1) = refs
    _ssd_kernel(xs0, bc0, sm0, smt0, e16_0, y0, st0, direction=0, n_heads=n_sh)
    _ssd_kernel(xs1, bc1, sm1, smt1, e16_1, y1, st1, direction=1, n_heads=n_sh)
    _mlstm_kernel(qk0, v0, sm0, smt0, em_0, h0, cn0, m0, direction=0, n_heads=n_mh, dk=dk)
    _mlstm_kernel(qk1, v1, sm1, smt1, em_1, h1, cn1, m1, direction=1, n_heads=n_mh, dk=dk)


def _scans(xs, bc, qk, v, small, smallt, *, batch, n_ctx_chunks, n_lat_chunks, dk):
    t_all, d = xs.shape
    n_sh = d // SSD_HEAD_DIM
    n_mh = d // MLSTM_DV
    n_steps = n_ctx_chunks + n_lat_chunks
    in_specs, args = [], []
    for direction in range(2):
        idx = _chunk_block_map(direction, batch, n_ctx_chunks, n_lat_chunks)
        rows = lambda b, s, idx=idx: (idx(b, s), 0)
        in_specs += [pl.BlockSpec((CHUNK, d), rows), pl.BlockSpec((CHUNK, bc.shape[1]), rows),
                     pl.BlockSpec((CHUNK, qk.shape[1]), rows), pl.BlockSpec((CHUNK, d), rows),
                     pl.BlockSpec((CHUNK, 3 * LANES), rows),
                     pl.BlockSpec((1, 2, LANES, CHUNK), lambda b, s, idx=idx: (idx(b, s), 0, 0, 0))]
        args += [xs, bc, qk, v, small, smallt]
    out_specs = []
    for direction in (0, 1, 0, 1):
        idx = _chunk_block_map(direction, batch, n_ctx_chunks, n_lat_chunks)
        out_specs.append(pl.BlockSpec((CHUNK, d), lambda b, s, idx=idx: (idx(b, s), 0)))
    for direction in range(2):
        sel = _selectors(direction, d)
        in_specs += [pl.BlockSpec(a.shape, lambda b, s: (0, 0)) for a in sel]
        args += list(sel)
    state = [pltpu.VMEM((SSD_STATE, d), F32)] * 2 + [pltpu.VMEM((n_mh, dk, 2 * MLSTM_DV), F32)] * 2 \
        + [pltpu.VMEM((PACK_ROWS, LANES), F32)] * 2
    return pl.pallas_call(
        functools.partial(_scans_kernel, n_sh=n_sh, n_mh=n_mh, dk=dk),
        out_shape=[jax.ShapeDtypeStruct((t_all, d), BF16)] * 4,
        grid=(batch, n_steps),
        in_specs=in_specs, out_specs=out_specs, scratch_shapes=state,
        compiler_params=pltpu.CompilerParams(dimension_semantics=("arbitrary", "arbitrary"),
                                             vmem_limit_bytes=VMEM_LIMIT),
        name="scans",
    )(*args)


def _merge_kernel(y0_ref, y1_ref, xs_ref, z_ref, h0_ref, h1_ref, og_ref, mg_ref, x_ref, mod_ref, dexp_ref, gs_ref,
                  gm_ref, gf_ref, wso_ref, wmo_ref, wo_ref, x1_ref, u2_ref, *, d_model):
    d = d_model
    m = mod_ref[0]
    tm = x_ref.shape[0]
    parts = [slice(p * tm // MERGE_PARTS, (p + 1) * tm // MERGE_PARTS) for p in range(MERGE_PARTS)]

    def ssd_in(rows):
        y = y0_ref[rows].astype(F32) + y1_ref[rows].astype(F32) + dexp_ref[...] * xs_ref[rows].astype(F32)
        zf = z_ref[rows].astype(F32)
        y = y * (zf * _sigmoid(zf))
        return (y * lax.rsqrt(jnp.mean(y * y, axis=-1, keepdims=True) + EPS) * gs_ref[...]).astype(BF16)

    def mlstm_in(rows):
        hm = h0_ref[rows].astype(F32) + h1_ref[rows].astype(F32)
        blocks = []
        for h in range(d // MLSTM_DV):
            blk = hm[:, h * MLSTM_DV:(h + 1) * MLSTM_DV]
            blocks.append(blk * lax.rsqrt(jnp.mean(blk * blk, axis=-1, keepdims=True) + EPS))
        hn = jnp.concatenate(blocks, axis=1) * gm_ref[...]
        return (og_ref[rows].astype(F32) * hn).astype(BF16)

    ys = [ssd_in(rows) for rows in parts]
    a = [_dot(y, wso_ref[...]) for y in ys]
    hhs = [mlstm_in(rows) for rows in parts]
    bm = [_dot(hh, wmo_ref[...]) for hh in hhs]
    merged = [(mg_ref[rows, :d].astype(F32) * a[p] + mg_ref[rows, d:].astype(F32) * bm[p]).astype(BF16)
              for p, rows in enumerate(parts)]
    r = [_dot(mp, wo_ref[...]) for mp in merged]
    for p, rows in enumerate(parts):
        x1 = x_ref[rows] + m[2:3] * r[p]
        x1_ref[rows] = x1
        u2 = x1 * lax.rsqrt(jnp.mean(x1 * x1, axis=-1, keepdims=True) + EPS) * gf_ref[...] * (1.0 + m[4:5]) + m[3:4]
        u2_ref[rows] = _pack_halves(u2)


def _merge(y0, y1, xs, z, h0, h1, og, mg, x2, mods3, dexp, gs, gm, gf, wso, wmo, wo, *, batch, seq, n_ctx_tok):
    t, d = x2.shape
    tm = TOKEN_TILE
    off = n_ctx_tok // tm
    tiles_per_batch = seq // tm
    lat = lambda i: (i + off, 0)
    row = lambda i: (i, 0)
    const = lambda i: (0, 0)
    kern = functools.partial(_merge_kernel, d_model=d)
    wspec = pl.BlockSpec((d, d), const, pipeline_mode=pl.Buffered(1))
    return pl.pallas_call(
        kern,
        out_shape=[jax.ShapeDtypeStruct((t, d), F32), jax.ShapeDtypeStruct((t, d // 2), U32)],
        grid=(t // tm,),
        in_specs=[pl.BlockSpec((tm, d), lat), pl.BlockSpec((tm, d), lat), pl.BlockSpec((tm, d), lat),
                  pl.BlockSpec((tm, d), lat), pl.BlockSpec((tm, d), lat), pl.BlockSpec((tm, d), lat),
                  pl.BlockSpec((tm, d), lat), pl.BlockSpec((tm, 2 * d), lat), pl.BlockSpec((tm, d), row),
                  pl.BlockSpec((1, 6, d), lambda i: (i // tiles_per_batch, 0, 0)),
                  pl.BlockSpec((1, d), const), pl.BlockSpec((1, d), const), pl.BlockSpec((1, d), const),
                  pl.BlockSpec((1, d), const), wspec, wspec, wspec],
        out_specs=[pl.BlockSpec((tm, d), row), pl.BlockSpec((tm, d // 2), row)],
        compiler_params=pltpu.CompilerParams(dimension_semantics=("arbitrary",), vmem_limit_bytes=VMEM_LIMIT),
        name="merge",
    )(y0, y1, xs, z, h0, h1, og, mg, x2, mods3, dexp, gs, gm, gf, wso, wmo, wo)


def _first_index_of_max(vals, row_iota, n_rows):
    mx = jnp.max(vals, axis=0, keepdims=True)
    idx = jnp.min(jnp.where(vals == mx, row_iota, n_rows), axis=0, keepdims=True)
    return mx, idx


def _router_kernel(u_ref, wt_ref, bias_ref, su_ref, idx_ref, pos_ref, wts_ref, cnt_ref, run_ref, *, n_experts, tr):
    i = pl.program_id(0)

    @pl.when(i == 0)
    def _():
        run_ref[...] = jnp.zeros_like(run_ref)

    ua, ub = _unpack_halves(u_ref[...])
    urow = jnp.concatenate([ua.astype(BF16), ub.astype(BF16)], axis=1)
    scores = _sigmoid(_dot_nt(wt_ref[...], urow))
    biased = scores + bias_ref[...]
    gsz = n_experts // N_EXPERT_GROUPS
    gi = lax.broadcasted_iota(I32, (gsz, tr), 0).astype(F32)
    gscores = []
    for g in range(N_EXPERT_GROUPS):
        blk = biased[g * gsz:(g + 1) * gsz]
        m1, i1 = _first_index_of_max(blk, gi, gsz)
        m2 = jnp.max(jnp.where(gi == i1, NEG_INF, blk), axis=0, keepdims=True)
        gscores.append(m1 + m2)
    gs = jnp.concatenate(gscores, axis=0)
    g8 = lax.broadcasted_iota(I32, (N_EXPERT_GROUPS, tr), 0).astype(F32)
    gsel = jnp.zeros((N_EXPERT_GROUPS, tr), F32)
    for _ in range(TOPK_GROUPS):
        _, gidx = _first_index_of_max(gs, g8, N_EXPERT_GROUPS)
        hit = g8 == gidx
        gsel = jnp.where(hit, 1.0, gsel)
        gs = jnp.where(hit, NEG_INF, gs)
    cand = jnp.concatenate(
        [jnp.where(jnp.broadcast_to(gsel[g:g + 1], (gsz, tr)) > 0.5, biased[g * gsz:(g + 1) * gsz], NEG_INF)
         for g in range(N_EXPERT_GROUPS)], axis=0)
    ei = lax.broadcasted_iota(I32, (n_experts, tr), 0).astype(F32)
    sel = jnp.zeros((n_experts, tr), F32)
    idxs, ws = [], []
    for _ in range(TOP_K):
        _, eidx = _first_index_of_max(cand, ei, n_experts)
        hit = ei == eidx
        ws.append(jnp.sum(jnp.where(hit, scores, 0.0), axis=0, keepdims=True))
        idxs.append(eidx)
        sel = jnp.where(hit, 1.0, sel)
        cand = jnp.where(hit, NEG_INF, cand)
    wk = jnp.concatenate(ws, axis=0)
    wts_ref[...] = ROUTED_SCALE * wk / jnp.sum(wk, axis=0, keepdims=True)
    idx_ref[...] = jnp.concatenate(idxs, axis=0).astype(I32)
    selb = sel.astype(BF16)
    posmat = _dot(selb, su_ref[...]) + run_ref[:, 0:1]
    pos_ref[...] = jnp.concatenate(
        [jnp.sum(jnp.where(ei == idxs[k], posmat, 0.0), axis=0, keepdims=True) for k in range(TOP_K)],
        axis=0).astype(I32)
    run = run_ref[...] + _dot(selb, jnp.ones((tr, LANES), BF16))
    run_ref[...] = run
    cnt_ref[...] = run


def _router(u2p, router_wt, bias_col):
    t = u2p.shape[0]
    d = router_wt.shape[1]
    n_experts = router_wt.shape[0]
    tr = TOKEN_TILE
    su = jnp.asarray(np.triu(np.ones((tr, tr), np.float32), 1), BF16)
    kern = functools.partial(_router_kernel, n_experts=n_experts, tr=tr)
    col = lambda i: (0, i)
    const = lambda i: (0, 0)
    return pl.pallas_call(
        kern,
        out_shape=[jax.ShapeDtypeStruct((TOP_K, t), I32), jax.ShapeDtypeStruct((TOP_K, t), I32),
                   jax.ShapeDtypeStruct((TOP_K, t), F32), jax.ShapeDtypeStruct((n_experts, LANES), F32)],
        grid=(t // tr,),
        in_specs=[pl.BlockSpec((tr, d // 2), lambda i: (i, 0)), pl.BlockSpec((n_experts, d), const),
                  pl.BlockSpec((n_experts, 1), const), pl.BlockSpec((tr, tr), const)],
        out_specs=[pl.BlockSpec((TOP_K, tr), col), pl.BlockSpec((TOP_K, tr), col), pl.BlockSpec((TOP_K, tr), col),
                   pl.BlockSpec((n_experts, LANES), const)],
        scratch_shapes=[pltpu.VMEM((n_experts, LANES), F32)],
        compiler_params=pltpu.CompilerParams(dimension_semantics=("arbitrary",), vmem_limit_bytes=VMEM_LIMIT),
        name="router",
    )(u2p, router_wt, bias_col, su)


def _slots_kernel(idx_ref, pos_ref, pstart_ref, dest_ref, *, n_experts, tr):
    ei = lax.broadcasted_iota(I32, (n_experts, tr), 0).astype(F32)
    pstart = pstart_ref[...]
    idx = idx_ref[...].astype(F32)
    rows = [jnp.sum(jnp.where(ei == idx[k:k + 1], pstart, 0.0), axis=0, keepdims=True) for k in range(TOP_K)]
    dest_ref[...] = jnp.concatenate(rows, axis=0).astype(I32) + pos_ref[...]


def _slots(idx, pos, pstart_col):
    t = idx.shape[1]
    n_experts = pstart_col.shape[0]
    tr = TOKEN_TILE
    col = lambda i: (0, i)
    return pl.pallas_call(
        functools.partial(_slots_kernel, n_experts=n_experts, tr=tr),
        out_shape=jax.ShapeDtypeStruct((TOP_K, t), I32),
        grid=(t // tr,),
        in_specs=[pl.BlockSpec((TOP_K, tr), col), pl.BlockSpec((TOP_K, tr), col),
                  pl.BlockSpec((n_experts, 1), lambda i: (0, 0))],
        out_specs=pl.BlockSpec((TOP_K, tr), col),
        compiler_params=pltpu.CompilerParams(dimension_semantics=("arbitrary",), vmem_limit_bytes=VMEM_LIMIT),
        name="slots",
    )(idx, pos, pstart_col)


PAD_BITS = tuple(1 << b for b in reversed(range((MOE_BLOCK - 1).bit_length())))


def _dispatch_kernel(pstart_ref, cnt_ref, dest_ref, u_ref, xs_ref, ubuf, zbuf, usem, sem, psem, *, td, n_steps,
                     n_experts):
    i = pl.program_id(0)
    slot = i % 2

    def u_copies(step, s):
        rows = pl.ds(pl.multiple_of(step * td, td), td)
        return [pltpu.make_async_copy(u_ref.at[rows, pl.ds(j * LANES, LANES)], ubuf.at[s, :, j, :], usem.at[s])
                for j in range(ROW_SUBLANES)]

    @pl.when(i == 0)
    def _():
        for cp in u_copies(i, 0):
            cp.start()
        zbuf[...] = jnp.zeros_like(zbuf)

        def pads(e, wait):
            cnt = cnt_ref[e]
            n_pad = (MOE_BLOCK - (cnt & (MOE_BLOCK - 1))) & (MOE_BLOCK - 1)
            base = pstart_ref[e] + cnt
            for bit in PAD_BITS:
                cp = pltpu.make_async_copy(zbuf.at[pl.ds(0, bit)], xs_ref.at[pl.ds(base, bit)], psem)
                has = (n_pad & bit) != 0

                @pl.when(has)
                def _():
                    if wait:
                        cp.wait()
                    else:
                        cp.start()

                base = base + jnp.where(has, bit, 0)

        def start_pads(e, carry):
            pads(e, False)
            return carry

        def wait_pads(e, carry):
            pads(e, True)
            return carry

        lax.fori_loop(0, n_experts, start_pads, 0)
        lax.fori_loop(0, n_experts, wait_pads, 0)

    @pl.when(i + 1 < n_steps)
    def _():
        for cp in u_copies(i + 1, 1 - slot):
            cp.start()

    for cp in u_copies(i, slot):
        cp.wait()

    def start(t, carry):
        for k in range(TOP_K):
            pltpu.make_async_copy(ubuf.at[slot, t], xs_ref.at[dest_ref[k, t]], sem).start(priority=k % 2)
        return carry

    def wait(t, carry):
        for k in range(TOP_K):
            pltpu.make_async_copy(ubuf.at[slot, 0], xs_ref.at[0], sem).wait()
        return carry

    lax.fori_loop(0, td, start, 0, unroll=4)
    lax.fori_loop(0, td, wait, 0, unroll=8)


def _dispatch(pad_start, counts, dest, u2p, n_slots):
    t, dh = u2p.shape
    td = DISPATCH_TILE
    n_experts = pad_start.shape[0]
    assert MOE_BLOCK & (MOE_BLOCK - 1) == 0 and dh == ROW_SUBLANES * LANES
    kern = functools.partial(_dispatch_kernel, td=td, n_steps=t // td, n_experts=n_experts)
    return pl.pallas_call(
        kern,
        out_shape=jax.ShapeDtypeStruct((n_slots, ROW_SUBLANES, LANES), U32),
        grid_spec=pltpu.PrefetchScalarGridSpec(
            num_scalar_prefetch=2, grid=(t // td,),
            in_specs=[pl.BlockSpec((TOP_K, td), lambda i, ps, cn: (0, i), memory_space=pltpu.SMEM),
                      pl.BlockSpec(memory_space=pl.ANY)],
            out_specs=pl.BlockSpec(memory_space=pl.ANY),
            scratch_shapes=[pltpu.VMEM((2, td, ROW_SUBLANES, LANES), U32),
                            pltpu.VMEM((MOE_BLOCK // 2, ROW_SUBLANES, LANES), U32),
                            pltpu.SemaphoreType.DMA((2,)), pltpu.SemaphoreType.DMA, pltpu.SemaphoreType.DMA]),
        compiler_params=pltpu.CompilerParams(dimension_semantics=("arbitrary",), vmem_limit_bytes=VMEM_LIMIT),
        name="dispatch",
    )(pad_start, counts, dest, u2p)


def _experts_kernel(bs_ref, nblk_ref, nu_ref, xs_ref, wg_ref, wu_ref, wd_ref, ys_ref, wgub, wdb, xbuf, ybuf,
                    semx, semy, *, n_experts):
    e = pl.program_id(0)
    n_used = nu_ref[0]

    def x_copies(g):
        rows = pl.ds(pl.multiple_of(g * MOE_BLOCK, MOE_BLOCK), MOE_BLOCK)
        return [pltpu.make_async_copy(xs_ref.at[rows, j, :], xbuf.at[g % X_BUFFERS, :, pl.ds(j * LANES, LANES)],
                                      semx.at[g % X_BUFFERS]) for j in range(ROW_SUBLANES)]

    def x_start(g):
        for cp in x_copies(g):
            cp.start()

    def x_wait(g):
        for cp in x_copies(g):
            cp.wait()

    def y_copies(g, slot):
        rows = pl.ds(pl.multiple_of(g * MOE_BLOCK, MOE_BLOCK), MOE_BLOCK)
        return [pltpu.make_async_copy(ybuf.at[slot, :, pl.ds(j * LANES, LANES)], ys_ref.at[rows, j, :], semy.at[slot])
                for j in range(ROW_SUBLANES)]

    def y_start(g, slot):
        for cp in y_copies(g, slot):
            cp.start()

    def y_wait(g, slot):
        for cp in y_copies(g, slot):
            cp.wait()

    @pl.when(e == 0)
    def _():
        for g in range(X_BUFFERS - 1):
            @pl.when(g < n_used)
            def _():
                x_start(jnp.int32(g))

    g0 = bs_ref[e]
    nb = nblk_ref[e]

    de = wg_ref.shape[2]

    @pl.when(nb > 0)
    def _():
        wgub[:, :de] = wg_ref[0].astype(BF16)
        wgub[:, de:] = wu_ref[0].astype(BF16)
        wdb[...] = wd_ref[0].astype(BF16)

    def block(g, carry):
        slot = g % 2

        @pl.when(g + X_BUFFERS - 1 < n_used)
        def _():
            x_start(g + X_BUFFERS - 1)

        x_wait(g)

        @pl.when(g >= 2)
        def _():
            y_wait(g - 2, slot)

        xa, xb = _unpack_halves(xbuf[g % X_BUFFERS])
        xrow = jnp.concatenate([xa.astype(BF16), xb.astype(BF16)], axis=1)
        hgu = _dot(xrow, wgub[...])
        hg = hgu[:, :de]
        hb = (hg * _sigmoid(hg) * hgu[:, de:]).astype(BF16)
        ybuf[slot] = _pack_halves(_dot(hb, wdb[...]))
        y_start(g, slot)
        return carry

    lax.fori_loop(g0, g0 + nb, block, 0)

    @pl.when(e == n_experts - 1)
    def _():
        @pl.when(n_used >= 2)
        def _():
            y_wait(n_used - 2, n_used % 2)

        @pl.when(n_used >= 1)
        def _():
            y_wait(n_used - 1, (n_used - 1) % 2)


def _experts(block_start, n_blocks, n_used, xsorted, wg, wu, wd):
    n_slots = xsorted.shape[0]
    dh = ROW_SUBLANES * LANES
    n_experts, d, de = wg.shape
    wmap = lambda e, bs, nb, nu: (e, 0, 0)
    assert xsorted.shape[1:] == (ROW_SUBLANES, LANES) and d == 2 * dh
    return pl.pallas_call(
        functools.partial(_experts_kernel, n_experts=n_experts),
        out_shape=jax.ShapeDtypeStruct((n_slots, ROW_SUBLANES, LANES), U32),
        grid_spec=pltpu.PrefetchScalarGridSpec(
            num_scalar_prefetch=3, grid=(n_experts,),
            in_specs=[pl.BlockSpec(memory_space=pl.ANY), pl.BlockSpec((1, d, de), wmap),
                      pl.BlockSpec((1, d, de), wmap), pl.BlockSpec((1, de, d), wmap)],
            out_specs=pl.BlockSpec(memory_space=pl.ANY),
            scratch_shapes=[pltpu.VMEM((d, 2 * de), BF16), pltpu.VMEM((de, d), BF16),
                            pltpu.VMEM((X_BUFFERS, MOE_BLOCK, dh), U32),
                            pltpu.VMEM((2, MOE_BLOCK, dh), U32),
                            pltpu.SemaphoreType.DMA((X_BUFFERS,)), pltpu.SemaphoreType.DMA((2,))]),
        compiler_params=pltpu.CompilerParams(dimension_semantics=("arbitrary",), vmem_limit_bytes=VMEM_LIMIT),
        name="experts",
    )(block_start, n_blocks, n_used, xsorted, wg, wu, wd)


def _combine_kernel(dest_ref, dnext_ref, ys_ref, wt_ref, x1_ref, u_ref, mod_ref, wsgu_ref, wsd_ref,
                    gfin_ref, o_ref, gbuf, accbuf, sem, *, tc, n_steps):
    i = pl.program_id(0)
    slot = i % 2

    def issue(dref, s):
        def body(t, carry):
            for k in range(TOP_K):
                pltpu.make_async_copy(ys_ref.at[dref[k, t]], gbuf.at[s, k, t], sem.at[s]).start(priority=k % 2)
            return carry
        lax.fori_loop(0, tc, body, 0, unroll=4)

    @pl.when(i == 0)
    def _():
        issue(dest_ref, 0)

    def wait(t, carry):
        for k in range(TOP_K):
            pltpu.make_async_copy(ys_ref.at[0], gbuf.at[slot, 0, 0], sem.at[slot]).wait()
        return carry

    lax.fori_loop(0, tc, wait, 0, unroll=8)

    def sum_and_issue(with_issue):
        def body(c, carry):
            t0 = c * GATHER_CHUNK
            if with_issue:
                for tt in range(GATHER_CHUNK):
                    for k in range(TOP_K):
                        pltpu.make_async_copy(ys_ref.at[dnext_ref[k, t0 + tt]], gbuf.at[1 - slot, k, t0 + tt],
                                              sem.at[1 - slot]).start(priority=k % 2)
            rows = pl.ds(t0, GATHER_CHUNK)
            wts = wt_ref[rows]
            sum_l = jnp.zeros((GATHER_CHUNK, ROW_SUBLANES, LANES), F32)
            sum_r = jnp.zeros((GATHER_CHUNK, ROW_SUBLANES, LANES), F32)
            for k in range(TOP_K):
                ga, gb = _unpack_halves(gbuf[slot, k, rows])
                wk = jnp.broadcast_to(wts[:, k:k + 1, :], (GATHER_CHUNK, ROW_SUBLANES, LANES))
                sum_l = sum_l + ga * wk
                sum_r = sum_r + gb * wk
            accbuf[0, rows] = sum_l
            accbuf[1, rows] = sum_r
            return carry
        lax.fori_loop(0, tc // GATHER_CHUNK, body, 0)

    @pl.when(i + 1 < n_steps)
    def _():
        sum_and_issue(True)

    @pl.when(i + 1 >= n_steps)
    def _():
        sum_and_issue(False)

    ua, ub = _unpack_halves(u_ref[...])
    half = ua.shape[1]
    hgu = _dot(jnp.concatenate([ua.astype(BF16), ub.astype(BF16)], axis=1), wsgu_ref[...])
    dsh = wsd_ref.shape[0]
    hg = hgu[:, :dsh]
    shared = _dot((hg * _sigmoid(hg) * hgu[:, dsh:]).astype(BF16), wsd_ref[...])
    acc_l = shared[:, :half]
    acc_r = shared[:, half:]
    acc_l = jnp.concatenate([acc_l[:, j * LANES:(j + 1) * LANES] + accbuf[0, :, j, :] for j in range(ROW_SUBLANES)],
                            axis=1)
    acc_r = jnp.concatenate([acc_r[:, j * LANES:(j + 1) * LANES] + accbuf[1, :, j, :] for j in range(ROW_SUBLANES)],
                            axis=1)
    m = mod_ref[0]
    xo_l = x1_ref[:, :half] + m[5:6, :half] * acc_l
    xo_r = x1_ref[:, half:] + m[5:6, half:] * acc_r
    ms = (jnp.sum(xo_l * xo_l, axis=-1, keepdims=True) + jnp.sum(xo_r * xo_r, axis=-1, keepdims=True)) / (2 * half)
    inv = lax.rsqrt(ms + EPS)
    o_ref[:, :half] = xo_l * inv * gfin_ref[:, :half]
    o_ref[:, half:] = xo_r * inv * gfin_ref[:, half:]


def _combine(dest, ysorted, wts_t, x1, u2p, mods3, wsgu, wsd, gfin, *, seq):
    t, d = x1.shape
    dh = d // 2
    tc = DISPATCH_TILE
    tiles_per_batch = seq // tc
    n_steps = t // tc
    kern = functools.partial(_combine_kernel, tc=tc, n_steps=n_steps)
    row = lambda i: (i, 0)
    const = lambda i: (0, 0)
    return pl.pallas_call(
        kern,
        out_shape=jax.ShapeDtypeStruct((t, d), F32),
        grid=(n_steps,),
        in_specs=[pl.BlockSpec((TOP_K, tc), lambda i: (0, i), memory_space=pltpu.SMEM),
                  pl.BlockSpec((TOP_K, tc), lambda i: (0, jnp.minimum(i + 1, n_steps - 1)), memory_space=pltpu.SMEM),
                  pl.BlockSpec(memory_space=pl.ANY), pl.BlockSpec((tc, TOP_K, LANES), lambda i: (i, 0, 0)),
                  pl.BlockSpec((tc, d), row), pl.BlockSpec((tc, dh), row),
                  pl.BlockSpec((1, 6, d), lambda i: (i // tiles_per_batch, 0, 0)),
                  pl.BlockSpec(wsgu.shape, const), pl.BlockSpec(wsd.shape, const), pl.BlockSpec((1, d), const)],
        out_specs=pl.BlockSpec((tc, d), row),
        scratch_shapes=[pltpu.VMEM((2, TOP_K, tc, ROW_SUBLANES, LANES), U32),
                        pltpu.VMEM((2, tc, ROW_SUBLANES, LANES), F32), pltpu.SemaphoreType.DMA((2,))],
        compiler_params=pltpu.CompilerParams(dimension_semantics=("arbitrary",), vmem_limit_bytes=VMEM_LIMIT),
        name="combine",
    )(dest, dest, ysorted, wts_t, x1, u2p, mods3, wsgu, wsd, gfin)


def kernel(x, c, ctx, c_ctx, ada_w, ada_b, norm_mix_g, norm_ffn_g, w_in, conv_xbc_w, conv_xbc_b, ssd_dt_bias, ssd_a_log, ssd_d, ssd_norm_g, conv_qk_w, conv_qk_b, mlstm_i_bias, mlstm_f_bias, mlstm_norm_g, w_ssd_out, w_mlstm_out, w_out, router_w, router_bias, moe_w_gate, moe_w_up, moe_w_down, shared_w_gate, shared_w_up, shared_w_down, norm_final_g):
    batch, seq, d = x.shape
    ctx_len = ctx.shape[1]
    depth = ada_w.shape[0]
    assert depth == 1, "only the single-layer configuration is implemented"
    assert seq % CHUNK == 0 and ctx_len % CHUNK == 0 and seq % GRID_W == 0
    l = 0
    n_sh = d // SSD_HEAD_DIM
    n_mh = d // MLSTM_DV
    dk = MLSTM_DV // 2
    d_xbc = d + 2 * SSD_GROUPS * SSD_STATE
    d_qk = 2 * n_mh * dk
    sizes = (d, d_xbc, 2 * n_sh, d_qk, d, 4 * n_mh, d, 2 * d)
    offs = np.concatenate([[0], np.cumsum(sizes)])
    assert offs[-1] == w_in.shape[2] and 2 * n_sh + 2 * n_mh <= LANES

    cond = jnp.concatenate([c, c_ctx[None], jnp.zeros((8 - (batch + 1) % 8, d), F32)], axis=0)
    mods = _adaln(cond, ada_w[l], ada_b[l])
    mods3 = mods.reshape(mods.shape[0], 6, d)

    w = w_in[l]
    seg = lambda k: w[:, offs[k]:offs[k + 1]]
    wbig = jnp.concatenate([seg(0), seg(1), seg(3), seg(4), seg(6), seg(7)], axis=1).astype(BF16)
    w_dt = seg(2)
    w_g = seg(5).reshape(d, 2, 2, n_mh)
    w_i = w_g[:, :, 0].reshape(d, 2 * n_mh)
    w_f = w_g[:, :, 1].reshape(d, 2 * n_mh)
    pad = jnp.zeros((d, LANES - 2 * n_sh - 2 * n_mh), F32)
    wsm = jnp.concatenate([w_dt, w_f, pad, w_dt, w_i, pad], axis=1)
    wsh = wsm.astype(BF16)
    wsl = (wsm - wsh.astype(F32)).astype(BF16)
    padb = jnp.zeros((LANES - 2 * n_sh - 2 * n_mh,), F32)
    dtb = ssd_dt_bias[l].reshape(-1).astype(F32)
    smb = jnp.concatenate([dtb, mlstm_f_bias[l].reshape(-1).astype(F32), padb,
                           dtb, mlstm_i_bias[l].reshape(-1).astype(F32), padb]).reshape(1, 2 * LANES)
    aneg = jnp.concatenate([-jnp.exp(ssd_a_log[l].astype(F32)).reshape(-1),
                            jnp.zeros((LANES - 2 * n_sh,), F32)]).reshape(1, LANES)

    x2 = x.reshape(batch * seq, d)
    ctx2 = ctx.reshape(batch * ctx_len, d)
    z, xs, bc, qk, v, og, mg, small, smallt = _inproj(
        x2, ctx2, mods3, norm_mix_g[l].reshape(1, d), wbig, wsh, wsl,
        conv_xbc_w[l], conv_xbc_b[l].reshape(1, d_xbc), conv_qk_w[l], conv_qk_b[l].reshape(1, d_qk), smb, aneg,
        batch=batch, seq=seq, ctx_len=ctx_len, dk=dk)

    ncc = ctx_len // CHUNK
    ncl = seq // CHUNK
    y0, y1, h0, h1 = _scans(xs, bc, qk, v, small, smallt, batch=batch, n_ctx_chunks=ncc, n_lat_chunks=ncl, dk=dk)

    dexp = jnp.repeat(ssd_d[l].astype(F32), SSD_HEAD_DIM).reshape(1, d)
    x1, u2 = _merge(y0, y1, xs, z, h0, h1, og, mg, x2, mods3, dexp, ssd_norm_g[l].reshape(1, d),
                    mlstm_norm_g[l].reshape(1, d), norm_ffn_g[l].reshape(1, d),
                    w_ssd_out[l].astype(BF16), w_mlstm_out[l].astype(BF16), w_out[l].astype(BF16),
                    batch=batch, seq=seq, n_ctx_tok=batch * ctx_len)

    n_experts = router_w.shape[2]
    idx, pos, wts, cnt = _router(u2, router_w[l].T.astype(BF16), router_bias[l].astype(F32).reshape(n_experts, 1))
    counts = cnt[:, 0].astype(I32)
    padded = (counts + MOE_BLOCK - 1) // MOE_BLOCK * MOE_BLOCK
    pad_end = jnp.cumsum(padded)
    pad_start = (pad_end - padded).astype(I32)
    t = batch * seq
    nb = t * TOP_K // MOE_BLOCK + n_experts
    n_used = (pad_end[-1] // MOE_BLOCK).astype(I32).reshape(1)

    dest = _slots(idx, pos, pad_start.astype(F32).reshape(n_experts, 1))
    xsorted = _dispatch(pad_start, counts, dest, u2, nb * MOE_BLOCK)
    ysorted = _experts(pad_start // MOE_BLOCK, (padded // MOE_BLOCK).astype(I32), n_used, xsorted,
                       moe_w_gate[l], moe_w_up[l], moe_w_down[l])
    wts_lanes = jnp.broadcast_to(wts.T[:, :, None], (t, TOP_K, LANES))
    out = _combine(dest, ysorted, wts_lanes, x1, u2, mods3,
                   jnp.concatenate([shared_w_gate[l], shared_w_up[l]], axis=1).astype(BF16), shared_w_down[l].astype(BF16),
                   norm_final_g.reshape(1, d), seq=seq)
    return out.reshape(batch, seq, d)
```

```python
import functools

import numpy as np
import jax
import jax.numpy as jnp
from jax import lax
from jax.experimental import pallas as pl
from jax.experimental.pallas import tpu as pltpu

F32 = jnp.float32
BF16 = jnp.bfloat16
I32 = jnp.int32
U32 = jnp.uint32

EPS = 1e-6
CHUNK = 128
CONV_K = 5
GRID_W = 64
SSD_HEAD_DIM = 64
SSD_STATE = 128
SSD_GROUPS = 2
MLSTM_DV = 128
N_EXPERT_GROUPS = 8
TOPK_GROUPS = 4
TOP_K = 8
ROUTED_SCALE = 2.5

LANES = 128
SUBLANES = 8
PACK_ROWS = 16
TOKEN_TILE = 512
COL_CHUNK = 512
MERGE_PARTS = 1
INPROJ_PARTS = 2
MOE_BLOCK = 512
DISPATCH_TILE = 256
GATHER_CHUNK = 4
X_BUFFERS = 4
ROW_SUBLANES = 4
VMEM_LIMIT = 56 * 1024 * 1024
NEG_INF = float("-inf")


def _dot(a, b):
    return jnp.dot(a, b, preferred_element_type=F32)


def _dot_nt(a, b):
    return lax.dot_general(a, b, (((1,), (1,)), ((), ())), preferred_element_type=F32)


def _dot_tn(a, b):
    return lax.dot_general(a, b, (((0,), (0,)), ((), ())), preferred_element_type=F32)


def _spread(parts, e):
    res = _dot(jnp.concatenate(parts, axis=0).astype(BF16), e)
    out, r0 = [], 0
    for p in parts:
        out.append(res[r0:r0 + p.shape[0]])
        r0 += p.shape[0]
    return out


def _rows16(row):
    r8 = jnp.broadcast_to(row, (SUBLANES, row.shape[1]))
    return jnp.concatenate([r8, r8], axis=0)


def _sigmoid(v):
    return 1.0 / (1.0 + jnp.exp(-v))


def _pack_halves(v):
    n = v.shape[1] // 2
    hi = lax.bitcast_convert_type(v[:, :n].astype(BF16).astype(F32), U32)
    lo = lax.bitcast_convert_type(v[:, n:].astype(BF16).astype(F32), U32)
    return hi | (lo >> 16)


def _unpack_halves(p):
    left = lax.bitcast_convert_type(p & jnp.uint32(0xFFFF0000), F32)
    right = lax.bitcast_convert_type(p << 16, F32)
    return left, right


def _softplus(v):
    return jnp.maximum(v, 0.0) + jnp.log1p(jnp.exp(-jnp.abs(v)))


def _adaln_kernel(c_ref, w_ref, b_ref, o_ref):
    c = c_ref[...]
    s = c * _sigmoid(c)
    w = w_ref[...]
    s_hi = s.astype(BF16)
    s_lo = (s - s_hi.astype(F32)).astype(BF16)
    w_hi = w.astype(BF16)
    w_lo = (w - w_hi.astype(F32)).astype(BF16)
    o_ref[...] = _dot(s_hi, w_hi) + _dot(s_lo, w_hi) + _dot(s_hi, w_lo) + b_ref[...]


def _adaln(cond, w, b):
    rows, d = cond.shape
    n = w.shape[1]
    tn = 1536 if n % 1536 == 0 else n
    return pl.pallas_call(
        _adaln_kernel,
        out_shape=jax.ShapeDtypeStruct((rows, n), F32),
        grid=(n // tn,),
        in_specs=[pl.BlockSpec((rows, d), lambda j: (0, 0)),
                  pl.BlockSpec((d, tn), lambda j: (0, j)),
                  pl.BlockSpec((1, tn), lambda j: (0, j))],
        out_specs=pl.BlockSpec((rows, tn), lambda j: (0, j)),
        compiler_params=pltpu.CompilerParams(dimension_semantics=("arbitrary",), vmem_limit_bytes=VMEM_LIMIT),
        name="adaln",
    )(cond, w, b.reshape(1, n))


CONV_SHIFTS = tuple(j - CONV_K // 2 for j in range(CONV_K) if j != CONV_K // 2)


def _conv_masks(tm, seg_len):
    pos = np.arange(tm) % seg_len
    m = np.zeros((tm, 8), np.float32)
    for i, s in enumerate(CONV_SHIFTS):
        m[:, i] = ((pos + s >= 0) & (pos + s < seg_len)).astype(np.float32)
    return m


def _conv_silu(acc, w5, bias, vm, tm):
    out = acc * w5[CONV_K // 2:CONV_K // 2 + 1] + bias
    for i, s in enumerate(CONV_SHIFTS):
        shifted = pltpu.roll(acc, (-s) % tm, axis=0)
        j = s + CONV_K // 2
        out = out + (shifted * vm[:, i:i + 1]) * w5[j:j + 1]
    return out * _sigmoid(out)


def _inproj_kernel(x_ref, ctx_ref, mod_ref, g_ref, wbig_ref, wsh_ref, wsl_ref, cwx_ref, cbx_ref, cwq_ref, cbq_ref,
                   smb_ref, aneg_ref, tril_ref, triu_ref, cmask_ref,
                   z_ref, xs_ref, bc_ref, qk_ref, v_ref, og_ref, mg_ref, small_ref, smallt_ref,
                   *, n_ctx_tiles, tm, d_model, dk):
    i = pl.program_id(0)
    is_ctx = i < n_ctx_tiles
    xt = jnp.where(is_ctx, ctx_ref[...], x_ref[...])
    m = mod_ref[0]
    ms = jnp.mean(xt * xt, axis=-1, keepdims=True)
    u = xt * lax.rsqrt(ms + EPS) * g_ref[...] * (1.0 + m[1:2]) + m[0:1]
    u_hi = u.astype(BF16)
    u_lo = (u - u_hi.astype(F32)).astype(BF16)
    vm = jnp.where(is_ctx, cmask_ref[1], cmask_ref[0])

    d = d_model
    d_bc = 2 * SSD_GROUPS * SSD_STATE
    pieces = [(z_ref, d, "plain", None), (xs_ref, d, "convx", 0), (bc_ref, d_bc, "convx", d),
              (qk_ref, d, "convq", 0), (v_ref, d, "plain", None), (og_ref, d, "sigmoid", None),
              (mg_ref, 2 * d, "sigmoid", None)]
    col = 0
    rp = tm // INPROJ_PARTS
    row_parts = [slice(p * rp, (p + 1) * rp) for p in range(INPROJ_PARTS)]
    u_parts = [u_hi[rows] for rows in row_parts]
    vm_parts = [vm[rows] for rows in row_parts]
    for ref, width, kind, coff in pieces:
        for c0 in range(0, width, COL_CHUNK):
            accs = [_dot(up, wbig_ref[:, col + c0:col + c0 + COL_CHUNK]) for up in u_parts]
            for rows, vmp, acc in zip(row_parts, vm_parts, accs):
                if kind == "convx":
                    cs = coff + c0
                    acc = _conv_silu(acc, cwx_ref[:, cs:cs + COL_CHUNK], cbx_ref[:, cs:cs + COL_CHUNK], vmp, rp)
                elif kind == "convq":
                    acc = _conv_silu(acc, cwq_ref[:, c0:c0 + COL_CHUNK], cbq_ref[:, c0:c0 + COL_CHUNK], vmp, rp)
                    if c0 < width // 2:
                        acc = acc * (dk ** -0.5)
                elif kind == "sigmoid":
                    acc = _sigmoid(acc)
                ref[rows, c0:c0 + COL_CHUNK] = acc.astype(ref.dtype)
        col += width

    wsh = wsh_ref[...]
    raw = _dot(u_hi, wsh) + _dot(u_lo, wsh) + _dot(u_hi, wsl_ref[...]) + smb_ref[...]
    p1 = raw[:, :LANES]
    p2 = raw[:, LANES:]
    lane = lax.broadcasted_iota(I32, (tm, LANES), 1)
    n_dt = 2 * (d_model // SSD_HEAD_DIM)
    n_g = 2 * (d_model // MLSTM_DV)
    is_dt = lane < n_dt
    is_gate = jnp.logical_and(lane >= n_dt, lane < n_dt + n_g)
    dt = _softplus(p2)
    pa = jnp.where(is_dt, dt * aneg_ref[...], jnp.where(is_gate, -_softplus(-p1), 0.0))
    pb = jnp.where(is_dt, dt, jnp.where(is_gate, p2, 0.0))
    lane_c = lax.broadcasted_iota(I32, (CHUNK, LANES), 1)
    is_dt_c = lane_c < n_dt
    rev = jnp.logical_or(jnp.logical_and(lane_c >= n_dt // 2, lane_c < n_dt),
                         jnp.logical_and(lane_c >= n_dt + n_g // 2, lane_c < n_dt + n_g))
    tril = tril_ref[...]
    triu = triu_ref[...]
    tq = lax.broadcasted_iota(I32, (CHUNK, LANES), 0)
    chunks = range(tm // CHUNK)
    planes_a, planes_b = [], []
    for c in chunks:
        a_c = pa[c * CHUNK:(c + 1) * CHUNK]
        hi = a_c.astype(BF16)
        r1 = a_c - hi.astype(F32)
        mid = r1.astype(BF16)
        lo = (r1 - mid.astype(F32)).astype(BF16)
        cs_f = _dot(tril, hi) + _dot(tril, mid) + _dot(tril, lo)
        cs_b = _dot(triu, hi) + _dot(triu, mid) + _dot(triu, lo)
        planes_a.append(jnp.where(rev, cs_b, cs_f))
    for c in chunks:
        pb_c = pb[c * CHUNK:(c + 1) * CHUNK]
        planes_b.append(jnp.where(is_dt_c, pb_c, pb_c - planes_a[c]))
    yfs = list(planes_b)
    ybs = list(planes_b)
    s = 1
    while s < CHUNK:
        for c in chunks:
            sh = pltpu.roll(yfs[c], s, axis=0)
            yfs[c] = jnp.maximum(yfs[c], jnp.where(tq >= s, sh, NEG_INF))
            sh = pltpu.roll(ybs[c], CHUNK - s, axis=0)
            ybs[c] = jnp.maximum(ybs[c], jnp.where(tq + s < CHUNK, sh, NEG_INF))
        s *= 2
    for c in chunks:
        r0 = c * CHUNK
        small_ref[r0:r0 + CHUNK, 0:LANES] = planes_a[c]
        small_ref[r0:r0 + CHUNK, LANES:2 * LANES] = planes_b[c]
        small_ref[r0:r0 + CHUNK, 2 * LANES:3 * LANES] = jnp.where(rev, ybs[c], yfs[c])
        smallt_ref[c, 0] = planes_a[c].T
        smallt_ref[c, 1] = planes_b[c].T


def _inproj(x2, ctx2, mods3, g, wbig, wsh, wsl, cwx, cbx, cwq, cbq, smb, aneg, *, batch, seq, ctx_len, dk):
    d = x2.shape[1]
    tm = TOKEN_TILE
    n_ctx_tok = batch * ctx_len
    rp = tm // INPROJ_PARTS
    assert n_ctx_tok % tm == 0 and seq % tm == 0 and rp % ctx_len == 0 and rp % GRID_W == 0
    n_ctx_tiles = n_ctx_tok // tm
    tiles_per_batch = seq // tm
    n_tiles = n_ctx_tiles + batch * tiles_per_batch
    t_all = n_tiles * tm
    n_big = wbig.shape[1]
    tril = jnp.asarray(np.tril(np.ones((CHUNK, CHUNK), np.float32)), BF16)
    triu = jnp.asarray(np.triu(np.ones((CHUNK, CHUNK), np.float32)), BF16)

    def x_map(i):
        return (jnp.maximum(i - n_ctx_tiles, 0), 0)

    def ctx_map(i):
        return (jnp.minimum(i, n_ctx_tiles - 1), 0)

    def mod_map(i):
        return (jnp.where(i < n_ctx_tiles, batch, jnp.maximum(i - n_ctx_tiles, 0) // tiles_per_batch), 0, 0)

    const = lambda i: (0, 0)
    row = lambda i: (i, 0)
    cmask = jnp.asarray(np.stack([_conv_masks(tm, GRID_W), _conv_masks(tm, ctx_len)]))
    kern = functools.partial(_inproj_kernel, n_ctx_tiles=n_ctx_tiles, tm=tm, d_model=d, dk=dk)
    d_bc = 2 * SSD_GROUPS * SSD_STATE
    outs = [jax.ShapeDtypeStruct((t_all, d), BF16), jax.ShapeDtypeStruct((t_all, d), BF16),
            jax.ShapeDtypeStruct((t_all, d_bc), BF16), jax.ShapeDtypeStruct((t_all, d), BF16),
            jax.ShapeDtypeStruct((t_all, d), BF16), jax.ShapeDtypeStruct((t_all, d), BF16),
            jax.ShapeDtypeStruct((t_all, 2 * d), BF16), jax.ShapeDtypeStruct((t_all, 3 * LANES), F32),
            jax.ShapeDtypeStruct((t_all // CHUNK, 2, LANES, CHUNK), F32)]
    out_specs = [pl.BlockSpec((tm, d), row), pl.BlockSpec((tm, d), row), pl.BlockSpec((tm, d_bc), row),
                 pl.BlockSpec((tm, d), row), pl.BlockSpec((tm, d), row), pl.BlockSpec((tm, d), row),
                 pl.BlockSpec((tm, 2 * d), row), pl.BlockSpec((tm, 3 * LANES), row),
                 pl.BlockSpec((tm // CHUNK, 2, LANES, CHUNK), lambda i: (i, 0, 0, 0))]
    in_specs = [pl.BlockSpec((tm, d), x_map), pl.BlockSpec((tm, d), ctx_map),
                pl.BlockSpec((1, 6, d), mod_map), pl.BlockSpec((1, d), const),
                pl.BlockSpec((d, n_big), const, pipeline_mode=pl.Buffered(1)),
                pl.BlockSpec((d, 2 * LANES), const), pl.BlockSpec((d, 2 * LANES), const),
                pl.BlockSpec(cwx.shape, const), pl.BlockSpec(cbx.shape, const),
                pl.BlockSpec(cwq.shape, const), pl.BlockSpec(cbq.shape, const),
                pl.BlockSpec((1, 2 * LANES), const), pl.BlockSpec((1, LANES), const),
                pl.BlockSpec((CHUNK, CHUNK), const), pl.BlockSpec((CHUNK, CHUNK), const),
                pl.BlockSpec((2, tm, 8), lambda i: (0, 0, 0))]
    return pl.pallas_call(
        kern, out_shape=outs, grid=(n_tiles,), in_specs=in_specs, out_specs=out_specs,
        compiler_params=pltpu.CompilerParams(dimension_semantics=("arbitrary",), vmem_limit_bytes=VMEM_LIMIT),
        name="inproj",
    )(x2, ctx2, mods3, g, wbig, wsh, wsl, cwx, cbx, cwq, cbq, smb, aneg, tril, triu, cmask)


def _chunk_block_map(direction, batch, n_ctx_chunks, n_lat_chunks):
    def idx(b, s):
        if direction == 0:
            c_ctx = s
            c_lat = s - n_ctx_chunks
        else:
            c_ctx = n_ctx_chunks - 1 - s
            c_lat = n_lat_chunks - 1 - (s - n_ctx_chunks)
        return jnp.where(s < n_ctx_chunks, b * n_ctx_chunks + c_ctx, batch * n_ctx_chunks + b * n_lat_chunks + c_lat)
    return idx


def _ssd_kernel(xs_ref, bc_ref, small_ref, smallt_ref, e16_ref, y_ref, st_ref, *, direction, n_heads):
    s = pl.program_id(1)

    @pl.when(s == 0)
    def _():
        st_ref[...] = jnp.zeros_like(st_ref)

    hpg = n_heads // SSD_GROUPS
    lane0 = direction * n_heads
    last = CHUNK - 1 if direction == 0 else 0
    lane = lax.broadcasted_iota(I32, (CHUNK, LANES), 1)
    lm = jnp.logical_and(lane >= lane0, lane < lane0 + n_heads)
    plane_a = jnp.where(lm, small_ref[:, 0:LANES], 0.0)
    plane_b = jnp.where(lm, small_ref[:, LANES:2 * LANES], 0.0)
    cum_t = smallt_ref[0, 0]
    tot = plane_a[last:last + 1]
    lane16 = lax.broadcasted_iota(I32, (PACK_ROWS, LANES), 1)
    lm16 = jnp.logical_and(lane16 >= lane0, lane16 < lane0 + n_heads)
    dec_rows = jnp.where(lm16, jnp.exp(_rows16(tot)), 0.0)
    dtx, ecx, wx, decx = _spread(
        [plane_b, jnp.where(lm, jnp.exp(plane_a), 0.0), plane_b * jnp.exp(tot - plane_a), dec_rows], e16_ref[...])
    decx = decx[0:1]

    xf = xs_ref[...].astype(F32)
    xdt = xf * dtx
    xw = (xf * wx).astype(BF16)
    plane64 = lax.broadcasted_iota(I32, (CHUNK, 2 * SSD_HEAD_DIM), 1) < SSD_HEAD_DIM
    iq = lax.broadcasted_iota(I32, (CHUNK, CHUNK), 0)
    ik = lax.broadcasted_iota(I32, (CHUNK, CHUNK), 1)
    mask = (iq >= ik) if direction == 0 else (iq <= ik)
    gw = hpg * SSD_HEAD_DIM
    groups = range(SSD_GROUPS)
    bgs = [bc_ref[:, g * SSD_STATE:(g + 1) * SSD_STATE] for g in groups]
    cgs = [bc_ref[:, (SSD_GROUPS + g) * SSD_STATE:(SSD_GROUPS + g + 1) * SSD_STATE] for g in groups]
    sgs = [st_ref[:, g * gw:(g + 1) * gw] for g in groups]
    cbs = [_dot_nt(cgs[g], bgs[g]) for g in groups]
    y_inters = [_dot(cgs[g], sgs[g].astype(BF16)) * ecx[:, g * gw:(g + 1) * gw] for g in groups]
    m_hs = []
    for h in range(n_heads):
        cum_q = jnp.broadcast_to(plane_a[:, lane0 + h:lane0 + h + 1], (CHUNK, CHUNK))
        seg = cum_q - cum_t[lane0 + h:lane0 + h + 1, :]
        m_hs.append((cbs[h // hpg] * jnp.exp(jnp.where(mask, seg, NEG_INF))).astype(BF16))
    for pair in range(n_heads // 2):
        g = (2 * pair) // hpg
        c0 = 2 * pair * SSD_HEAD_DIM
        xpair = xdt[:, c0:c0 + 2 * SSD_HEAD_DIM]
        acc = y_inters[g][:, c0 - g * gw:c0 - g * gw + 2 * SSD_HEAD_DIM]
        for par in range(2):
            keep = plane64 if par == 0 else jnp.logical_not(plane64)
            acc = acc + _dot(m_hs[2 * pair + par], jnp.where(keep, xpair, 0.0).astype(BF16))
        y_ref[:, c0:c0 + 2 * SSD_HEAD_DIM] = acc.astype(y_ref.dtype)
    for g in groups:
        st_ref[:, g * gw:(g + 1) * gw] = (decx[:, g * gw:(g + 1) * gw] * sgs[g]
                                          + _dot_tn(bgs[g], xw[:, g * gw:(g + 1) * gw]))


def _selectors(direction, d):
    n_sh = d // SSD_HEAD_DIM
    n_mh = d // MLSTM_DV
    e16 = np.zeros((LANES, d), np.float32)
    em = np.zeros((LANES, d), np.float32)
    for h in range(n_sh):
        e16[direction * n_sh + h, h * SSD_HEAD_DIM:(h + 1) * SSD_HEAD_DIM] = 1.0
    for h in range(n_mh):
        em[2 * n_sh + direction * n_mh + h, h * MLSTM_DV:(h + 1) * MLSTM_DV] = 1.0
    return jnp.asarray(e16, BF16), jnp.asarray(em, BF16)


def _mlstm_kernel(qk_ref, v_ref, small_ref, smallt_ref, em_ref, h_ref, cn_ref, m_ref, *, direction, n_heads, dk):
    s = pl.program_id(1)

    @pl.when(s == 0)
    def _():
        cn_ref[...] = jnp.zeros_like(cn_ref)
        m_ref[...] = jnp.zeros_like(m_ref)

    n_dt = 2 * (n_heads * MLSTM_DV // SSD_HEAD_DIM)
    lane0 = n_dt + direction * n_heads
    last = CHUNK - 1 if direction == 0 else 0
    lane = lax.broadcasted_iota(I32, (CHUNK, LANES), 1)
    lm = jnp.logical_and(lane >= lane0, lane < lane0 + n_heads)
    b_q = jnp.where(lm, small_ref[:, 0:LANES], 0.0)
    r_k = jnp.where(lm, small_ref[:, LANES:2 * LANES], 0.0)
    cmr = jnp.where(lm, small_ref[:, 2 * LANES:3 * LANES], 0.0)
    r_t = smallt_ref[0, 1]
    m_all = m_ref[...]
    m_row = m_all[0:1]
    mm = jnp.maximum(cmr, m_row)
    w_state = jnp.exp(m_row - mm)
    e_mq = jnp.exp(-(b_q + mm))
    m_base8 = jnp.maximum(m_all, _rows16(cmr[last:last + 1]))
    m_base = m_base8[0:1]
    w_k = jnp.where(lm, jnp.exp(r_k - m_base), 0.0)
    lane16 = lax.broadcasted_iota(I32, (PACK_ROWS, LANES), 1)
    lm16 = jnp.logical_and(lane16 >= lane0, lane16 < lane0 + n_heads)
    dec_rows = jnp.where(lm16, jnp.exp(m_all - m_base8), 0.0)
    wsx, emqx, wkx, decx = _spread(
        [jnp.where(lm, w_state, 0.0), jnp.where(lm, e_mq, 0.0), w_k, dec_rows], em_ref[...])
    decx = decx[0:1]
    m_ref[...] = jnp.where(lm16, _rows16(b_q[last:last + 1]) + m_base8, 0.0)

    iq = lax.broadcasted_iota(I32, (CHUNK, CHUNK), 0)
    ik = lax.broadcasted_iota(I32, (CHUNK, CHUNK), 1)
    mask = (iq >= ik) if direction == 0 else (iq <= ik)
    ones = jnp.ones((CHUNK, MLSTM_DV), BF16)
    d_qk = n_heads * dk
    heads = range(n_heads)
    hsl = [slice(h * MLSTM_DV, (h + 1) * MLSTM_DV) for h in heads]
    qs = [qk_ref[:, h * dk:(h + 1) * dk] for h in heads]
    ks = [qk_ref[:, d_qk + h * dk:d_qk + (h + 1) * dk] for h in heads]
    vs = [v_ref[:, hsl[h]] for h in heads]
    cns = [cn_ref[h] for h in heads]
    scores = [_dot_nt(qs[h], ks[h]) for h in heads]
    inter = [_dot(qs[h], cns[h].astype(BF16)) for h in heads]
    smats = []
    for h in heads:
        mm_q = jnp.broadcast_to(mm[:, lane0 + h:lane0 + h + 1], (CHUNK, CHUNK))
        dmat = jnp.exp(jnp.where(mask, r_t[lane0 + h:lane0 + h + 1, :] - mm_q, NEG_INF))
        smats.append((scores[h] * dmat).astype(BF16))
    tots = [_dot(smats[h], jnp.concatenate([vs[h], ones], axis=1)) for h in heads]
    for h in heads:
        hs = hsl[h]
        kh, vh, cn = ks[h], vs[h], cns[h]
        wsh = wsx[:, hs]
        tot = tots[h] + jnp.concatenate([wsh, wsh], axis=1) * inter[h]
        num = tot[:, :MLSTM_DV]
        den = tot[:, MLSTM_DV:]
        h_ref[:, hs] = (num / jnp.maximum(jnp.abs(den), emqx[:, hs])).astype(h_ref.dtype)
        wkh = wkx[:, hs]
        rhs = jnp.concatenate([(vh.astype(F32) * wkh).astype(BF16), wkh.astype(BF16)], axis=1)
        dech = decx[:, hs]
        cn_ref[h] = jnp.concatenate([dech, dech], axis=1) * cn + _dot_tn(kh, rhs)


def _scans_kernel(*refs, n_sh, n_mh, dk):
    (xs0, bc0, qk0, v0, sm0, smt0, xs1, bc1, qk1, v1, sm1, smt1, e16_0, em_0, e16_1, em_1,
     y0, y1, h0, h1, st0, st1, cn0, cn1, m0, m1) = refs
    _ssd_kernel(xs0, bc0, sm0, smt0, e16_0, y0, st0, direction=0, n_heads=n_sh)
    _ssd_kernel(xs1, bc1, sm1, smt1, e16_1, y1, st1, direction=1, n_heads=n_sh)
    _mlstm_kernel(qk0, v0, sm0, smt0, em_0, h0, cn0, m0, direction=0, n_heads=n_mh, dk=dk)
    _mlstm_kernel(qk1, v1, sm1, smt1, em_1, h1, cn1, m1, direction=1, n_heads=n_mh, dk=dk)


def _scans(xs, bc, qk, v, small, smallt, *, batch, n_ctx_chunks, n_lat_chunks, dk):
    t_all, d = xs.shape
    n_sh = d // SSD_HEAD_DIM
    n_mh = d // MLSTM_DV
    n_steps = n_ctx_chunks + n_lat_chunks
    in_specs, args = [], []
    for direction in range(2):
        idx = _chunk_block_map(direction, batch, n_ctx_chunks, n_lat_chunks)
        rows = lambda b, s, idx=idx: (idx(b, s), 0)
        in_specs += [pl.BlockSpec((CHUNK, d), rows), pl.BlockSpec((CHUNK, bc.shape[1]), rows),
                     pl.BlockSpec((CHUNK, qk.shape[1]), rows), pl.BlockSpec((CHUNK, d), rows),
                     pl.BlockSpec((CHUNK, 3 * LANES), rows),
                     pl.BlockSpec((1, 2, LANES, CHUNK), lambda b, s, idx=idx: (idx(b, s), 0, 0, 0))]
        args += [xs, bc, qk, v, small, smallt]
    out_specs = []
    for direction in (0, 1, 0, 1):
        idx = _chunk_block_map(direction, batch, n_ctx_chunks, n_lat_chunks)
        out_specs.append(pl.BlockSpec((CHUNK, d), lambda b, s, idx=idx: (idx(b, s), 0)))
    for direction in range(2):
        sel = _selectors(direction, d)
        in_specs += [pl.BlockSpec(a.shape, lambda b, s: (0, 0)) for a in sel]
        args += list(sel)
    state = [pltpu.VMEM((SSD_STATE, d), F32)] * 2 + [pltpu.VMEM((n_mh, dk, 2 * MLSTM_DV), F32)] * 2 \
        + [pltpu.VMEM((PACK_ROWS, LANES), F32)] * 2
    return pl.pallas_call(
        functools.partial(_scans_kernel, n_sh=n_sh, n_mh=n_mh, dk=dk),
        out_shape=[jax.ShapeDtypeStruct((t_all, d), BF16)] * 4,
        grid=(batch, n_steps),
        in_specs=in_specs, out_specs=out_specs, scratch_shapes=state,
        compiler_params=pltpu.CompilerParams(dimension_semantics=("arbitrary", "arbitrary"),
                                             vmem_limit_bytes=VMEM_LIMIT),
        name="scans",
    )(*args)


def _merge_kernel(y0_ref, y1_ref, xs_ref, z_ref, h0_ref, h1_ref, og_ref, mg_ref, x_ref, mod_ref, dexp_ref, gs_ref,
                  gm_ref, gf_ref, wso_ref, wmo_ref, wo_ref, x1_ref, u2_ref, *, d_model):
    d = d_model
    m = mod_ref[0]
    tm = x_ref.shape[0]
    parts = [slice(p * tm // MERGE_PARTS, (p + 1) * tm // MERGE_PARTS) for p in range(MERGE_PARTS)]

    def ssd_in(rows):
        y = y0_ref[rows].astype(F32) + y1_ref[rows].astype(F32) + dexp_ref[...] * xs_ref[rows].astype(F32)
        zf = z_ref[rows].astype(F32)
        y = y * (zf * _sigmoid(zf))
        return (y * lax.rsqrt(jnp.mean(y * y, axis=-1, keepdims=True) + EPS) * gs_ref[...]).astype(BF16)

    def mlstm_in(rows):
        hm = h0_ref[rows].astype(F32) + h1_ref[rows].astype(F32)
        blocks = []
        for h in range(d // MLSTM_DV):
            blk = hm[:, h * MLSTM_DV:(h + 1) * MLSTM_DV]
            blocks.append(blk * lax.rsqrt(jnp.mean(blk * blk, axis=-1, keepdims=True) + EPS))
        hn = jnp.concatenate(blocks, axis=1) * gm_ref[...]
        return (og_ref[rows].astype(F32) * hn).astype(BF16)

    ys = [ssd_in(rows) for rows in parts]
    a = [_dot(y, wso_ref[...]) for y in ys]
    hhs = [mlstm_in(rows) for rows in parts]
    bm = [_dot(hh, wmo_ref[...]) for hh in hhs]
    merged = [(mg_ref[rows, :d].astype(F32) * a[p] + mg_ref[rows, d:].astype(F32) * bm[p]).astype(BF16)
              for p, rows in enumerate(parts)]
    r = [_dot(mp, wo_ref[...]) for mp in merged]
    for p, rows in enumerate(parts):
        x1 = x_ref[rows] + m[2:3] * r[p]
        x1_ref[rows] = x1
        u2 = x1 * lax.rsqrt(jnp.mean(x1 * x1, axis=-1, keepdims=True) + EPS) * gf_ref[...] * (1.0 + m[4:5]) + m[3:4]
        u2_ref[rows] = _pack_halves(u2)


def _merge(y0, y1, xs, z, h0, h1, og, mg, x2, mods3, dexp, gs, gm, gf, wso, wmo, wo, *, batch, seq, n_ctx_tok):
    t, d = x2.shape
    tm = TOKEN_TILE
    off = n_ctx_tok // tm
    tiles_per_batch = seq // tm
    lat = lambda i: (i + off, 0)
    row = lambda i: (i, 0)
    const = lambda i: (0, 0)
    kern = functools.partial(_merge_kernel, d_model=d)
    wspec = pl.BlockSpec((d, d), const, pipeline_mode=pl.Buffered(1))
    return pl.pallas_call(
        kern,
        out_shape=[jax.ShapeDtypeStruct((t, d), F32), jax.ShapeDtypeStruct((t, d // 2), U32)],
        grid=(t // tm,),
        in_specs=[pl.BlockSpec((tm, d), lat), pl.BlockSpec((tm, d), lat), pl.BlockSpec((tm, d), lat),
                  pl.BlockSpec((tm, d), lat), pl.BlockSpec((tm, d), lat), pl.BlockSpec((tm, d), lat),
                  pl.BlockSpec((tm, d), lat), pl.BlockSpec((tm, 2 * d), lat), pl.BlockSpec((tm, d), row),
                  pl.BlockSpec((1, 6, d), lambda i: (i // tiles_per_batch, 0, 0)),
                  pl.BlockSpec((1, d), const), pl.BlockSpec((1, d), const), pl.BlockSpec((1, d), const),
                  pl.BlockSpec((1, d), const), wspec, wspec, wspec],
        out_specs=[pl.BlockSpec((tm, d), row), pl.BlockSpec((tm, d // 2), row)],
        compiler_params=pltpu.CompilerParams(dimension_semantics=("arbitrary",), vmem_limit_bytes=VMEM_LIMIT),
        name="merge",
    )(y0, y1, xs, z, h0, h1, og, mg, x2, mods3, dexp, gs, gm, gf, wso, wmo, wo)


def _first_index_of_max(vals, row_iota, n_rows):
    mx = jnp.max(vals, axis=0, keepdims=True)
    idx = jnp.min(jnp.where(vals == mx, row_iota, n_rows), axis=0, keepdims=True)
    return mx, idx


def _router_kernel(u_ref, wt_ref, bias_ref, su_ref, idx_ref, pos_ref, wts_ref, cnt_ref, run_ref, *, n_experts, tr):
    i = pl.program_id(0)

    @pl.when(i == 0)
    def _():
        run_ref[...] = jnp.zeros_like(run_ref)

    ua, ub = _unpack_halves(u_ref[...])
    urow = jnp.concatenate([ua.astype(BF16), ub.astype(BF16)], axis=1)
    scores = _sigmoid(_dot_nt(wt_ref[...], urow))
    biased = scores + bias_ref[...]
    gsz = n_experts // N_EXPERT_GROUPS
    gi = lax.broadcasted_iota(I32, (gsz, tr), 0).astype(F32)
    gscores = []
    for g in range(N_EXPERT_GROUPS):
        blk = biased[g * gsz:(g + 1) * gsz]
        m1, i1 = _first_index_of_max(blk, gi, gsz)
        m2 = jnp.max(jnp.where(gi == i1, NEG_INF, blk), axis=0, keepdims=True)
        gscores.append(m1 + m2)
    gs = jnp.concatenate(gscores, axis=0)
    g8 = lax.broadcasted_iota(I32, (N_EXPERT_GROUPS, tr), 0).astype(F32)
    gsel = jnp.zeros((N_EXPERT_GROUPS, tr), F32)
    for _ in range(TOPK_GROUPS):
        _, gidx = _first_index_of_max(gs, g8, N_EXPERT_GROUPS)
        hit = g8 == gidx
        gsel = jnp.where(hit, 1.0, gsel)
        gs = jnp.where(hit, NEG_INF, gs)
    cand = jnp.concatenate(
        [jnp.where(jnp.broadcast_to(gsel[g:g + 1], (gsz, tr)) > 0.5, biased[g * gsz:(g + 1) * gsz], NEG_INF)
         for g in range(N_EXPERT_GROUPS)], axis=0)
    ei = lax.broadcasted_iota(I32, (n_experts, tr), 0).astype(F32)
    sel = jnp.zeros((n_experts, tr), F32)
    idxs, ws = [], []
    for _ in range(TOP_K):
        _, eidx = _first_index_of_max(cand, ei, n_experts)
        hit = ei == eidx
        ws.append(jnp.sum(jnp.where(hit, scores, 0.0), axis=0, keepdims=True))
        idxs.append(eidx)
        sel = jnp.where(hit, 1.0, sel)
        cand = jnp.where(hit, NEG_INF, cand)
    wk = jnp.concatenate(ws, axis=0)
    wts_ref[...] = ROUTED_SCALE * wk / jnp.sum(wk, axis=0, keepdims=True)
    idx_ref[...] = jnp.concatenate(idxs, axis=0).astype(I32)
    selb = sel.astype(BF16)
    posmat = _dot(selb, su_ref[...]) + run_ref[:, 0:1]
    pos_ref[...] = jnp.concatenate(
        [jnp.sum(jnp.where(ei == idxs[k], posmat, 0.0), axis=0, keepdims=True) for k in range(TOP_K)],
        axis=0).astype(I32)
    run = run_ref[...] + _dot(selb, jnp.ones((tr, LANES), BF16))
    run_ref[...] = run
    cnt_ref[...] = run


def _router(u2p, router_wt, bias_col):
    t = u2p.shape[0]
    d = router_wt.shape[1]
    n_experts = router_wt.shape[0]
    tr = TOKEN_TILE
    su = jnp.asarray(np.triu(np.ones((tr, tr), np.float32), 1), BF16)
    kern = functools.partial(_router_kernel, n_experts=n_experts, tr=tr)
    col = lambda i: (0, i)
    const = lambda i: (0, 0)
    return pl.pallas_call(
        kern,
        out_shape=[jax.ShapeDtypeStruct((TOP_K, t), I32), jax.ShapeDtypeStruct((TOP_K, t), I32),
                   jax.ShapeDtypeStruct((TOP_K, t), F32), jax.ShapeDtypeStruct((n_experts, LANES), F32)],
        grid=(t // tr,),
        in_specs=[pl.BlockSpec((tr, d // 2), lambda i: (i, 0)), pl.BlockSpec((n_experts, d), const),
                  pl.BlockSpec((n_experts, 1), const), pl.BlockSpec((tr, tr), const)],
        out_specs=[pl.BlockSpec((TOP_K, tr), col), pl.BlockSpec((TOP_K, tr), col), pl.BlockSpec((TOP_K, tr), col),
                   pl.BlockSpec((n_experts, LANES), const)],
        scratch_shapes=[pltpu.VMEM((n_experts, LANES), F32)],
        compiler_params=pltpu.CompilerParams(dimension_semantics=("arbitrary",), vmem_limit_bytes=VMEM_LIMIT),
        name="router",
    )(u2p, router_wt, bias_col, su)


def _slots_kernel(idx_ref, pos_ref, pstart_ref, dest_ref, *, n_experts, tr):
    ei = lax.broadcasted_iota(I32, (n_experts, tr), 0).astype(F32)
    pstart = pstart_ref[...]
    idx = idx_ref[...].astype(F32)
    rows = [jnp.sum(jnp.where(ei == idx[k:k + 1], pstart, 0.0), axis=0, keepdims=True) for k in range(TOP_K)]
    dest_ref[...] = jnp.concatenate(rows, axis=0).astype(I32) + pos_ref[...]


def _slots(idx, pos, pstart_col):
    t = idx.shape[1]
    n_experts = pstart_col.shape[0]
    tr = TOKEN_TILE
    col = lambda i: (0, i)
    return pl.pallas_call(
        functools.partial(_slots_kernel, n_experts=n_experts, tr=tr),
        out_shape=jax.ShapeDtypeStruct((TOP_K, t), I32),
        grid=(t // tr,),
        in_specs=[pl.BlockSpec((TOP_K, tr), col), pl.BlockSpec((TOP_K, tr), col),
                  pl.BlockSpec((n_experts, 1), lambda i: (0, 0))],
        out_specs=pl.BlockSpec((TOP_K, tr), col),
        compiler_params=pltpu.CompilerParams(dimension_semantics=("arbitrary",), vmem_limit_bytes=VMEM_LIMIT),
        name="slots",
    )(idx, pos, pstart_col)


PAD_BITS = tuple(1 << b for b in reversed(range((MOE_BLOCK - 1).bit_length())))


def _dispatch_kernel(pstart_ref, cnt_ref, dest_ref, u_ref, xs_ref, ubuf, zbuf, usem, sem, psem, *, td, n_steps,
                     n_experts):
    i = pl.program_id(0)
    slot = i % 2

    def u_copies(step, s):
        rows = pl.ds(pl.multiple_of(step * td, td), td)
        return [pltpu.make_async_copy(u_ref.at[rows, pl.ds(j * LANES, LANES)], ubuf.at[s, :, j, :], usem.at[s])
                for j in range(ROW_SUBLANES)]

    @pl.when(i == 0)
    def _():
        for cp in u_copies(i, 0):
            cp.start()
        zbuf[...] = jnp.zeros_like(zbuf)

        def pads(e, wait):
            cnt = cnt_ref[e]
            n_pad = (MOE_BLOCK - (cnt & (MOE_BLOCK - 1))) & (MOE_BLOCK - 1)
            base = pstart_ref[e] + cnt
            for bit in PAD_BITS:
                cp = pltpu.make_async_copy(zbuf.at[pl.ds(0, bit)], xs_ref.at[pl.ds(base, bit)], psem)
                has = (n_pad & bit) != 0

                @pl.when(has)
                def _():
                    if wait:
                        cp.wait()
                    else:
                        cp.start()

                base = base + jnp.where(has, bit, 0)

        def start_pads(e, carry):
            pads(e, False)
            return carry

        def wait_pads(e, carry):
            pads(e, True)
            return carry

        lax.fori_loop(0, n_experts, start_pads, 0)
        lax.fori_loop(0, n_experts, wait_pads, 0)

    @pl.when(i + 1 < n_steps)
    def _():
        for cp in u_copies(i + 1, 1 - slot):
            cp.start()

    for cp in u_copies(i, slot):
        cp.wait()

    def start(t, carry):
        for k in range(TOP_K):
            pltpu.make_async_copy(ubuf.at[slot, t], xs_ref.at[dest_ref[k, t]], sem).start(priority=k % 2)
        return carry

    def wait(t, carry):
        for k in range(TOP_K):
            pltpu.make_async_copy(ubuf.at[slot, 0], xs_ref.at[0], sem).wait()
        return carry

    lax.fori_loop(0, td, start, 0, unroll=4)
    lax.fori_loop(0, td, wait, 0, unroll=8)


def _dispatch(pad_start, counts, dest, u2p, n_slots):
    t, dh = u2p.shape
    td = DISPATCH_TILE
    n_experts = pad_start.shape[0]
    assert MOE_BLOCK & (MOE_BLOCK - 1) == 0 and dh == ROW_SUBLANES * LANES
    kern = functools.partial(_dispatch_kernel, td=td, n_steps=t // td, n_experts=n_experts)
    return pl.pallas_call(
        kern,
        out_shape=jax.ShapeDtypeStruct((n_slots, ROW_SUBLANES, LANES), U32),
        grid_spec=pltpu.PrefetchScalarGridSpec(
            num_scalar_prefetch=2, grid=(t // td,),
            in_specs=[pl.BlockSpec((TOP_K, td), lambda i, ps, cn: (0, i), memory_space=pltpu.SMEM),
                      pl.BlockSpec(memory_space=pl.ANY)],
            out_specs=pl.BlockSpec(memory_space=pl.ANY),
            scratch_shapes=[pltpu.VMEM((2, td, ROW_SUBLANES, LANES), U32),
                            pltpu.VMEM((MOE_BLOCK // 2, ROW_SUBLANES, LANES), U32),
                            pltpu.SemaphoreType.DMA((2,)), pltpu.SemaphoreType.DMA, pltpu.SemaphoreType.DMA]),
        compiler_params=pltpu.CompilerParams(dimension_semantics=("arbitrary",), vmem_limit_bytes=VMEM_LIMIT),
        name="dispatch",
    )(pad_start, counts, dest, u2p)


def _experts_kernel(bs_ref, nblk_ref, nu_ref, xs_ref, wg_ref, wu_ref, wd_ref, ys_ref, wgub, wdb, xbuf, ybuf,
                    semx, semy, *, n_experts):
    e = pl.program_id(0)
    n_used = nu_ref[0]

    def x_copies(g):
        rows = pl.ds(pl.multiple_of(g * MOE_BLOCK, MOE_BLOCK), MOE_BLOCK)
        return [pltpu.make_async_copy(xs_ref.at[rows, j, :], xbuf.at[g % X_BUFFERS, :, pl.ds(j * LANES, LANES)],
                                      semx.at[g % X_BUFFERS]) for j in range(ROW_SUBLANES)]

    def x_start(g):
        for cp in x_copies(g):
            cp.start()

    def x_wait(g):
        for cp in x_copies(g):
            cp.wait()

    def y_copies(g, slot):
        rows = pl.ds(pl.multiple_of(g * MOE_BLOCK, MOE_BLOCK), MOE_BLOCK)
        return [pltpu.make_async_copy(ybuf.at[slot, :, pl.ds(j * LANES, LANES)], ys_ref.at[rows, j, :], semy.at[slot])
                for j in range(ROW_SUBLANES)]

    def y_start(g, slot):
        for cp in y_copies(g, slot):
            cp.start()

    def y_wait(g, slot):
        for cp in y_copies(g, slot):
            cp.wait()

    @pl.when(e == 0)
    def _():
        for g in range(X_BUFFERS - 1):
            @pl.when(g < n_used)
            def _():
                x_start(jnp.int32(g))

    g0 = bs_ref[e]
    nb = nblk_ref[e]

    de = wg_ref.shape[2]

    @pl.when(nb > 0)
    def _():
        wgub[:, :de] = wg_ref[0].astype(BF16)
        wgub[:, de:] = wu_ref[0].astype(BF16)
        wdb[...] = wd_ref[0].astype(BF16)

    def block(g, carry):
        slot = g % 2

        @pl.when(g + X_BUFFERS - 1 < n_used)
        def _():
            x_start(g + X_BUFFERS - 1)

        x_wait(g)

        @pl.when(g >= 2)
        def _():
            y_wait(g - 2, slot)

        xa, xb = _unpack_halves(xbuf[g % X_BUFFERS])
        xrow = jnp.concatenate([xa.astype(BF16), xb.astype(BF16)], axis=1)
        hgu = _dot(xrow, wgub[...])
        hg = hgu[:, :de]
        hb = (hg * _sigmoid(hg) * hgu[:, de:]).astype(BF16)
        ybuf[slot] = _pack_halves(_dot(hb, wdb[...]))
        y_start(g, slot)
        return carry

    lax.fori_loop(g0, g0 + nb, block, 0)

    @pl.when(e == n_experts - 1)
    def _():
        @pl.when(n_used >= 2)
        def _():
            y_wait(n_used - 2, n_used % 2)

        @pl.when(n_used >= 1)
        def _():
            y_wait(n_used - 1, (n_used - 1) % 2)


def _experts(block_start, n_blocks, n_used, xsorted, wg, wu, wd):
    n_slots = xsorted.shape[0]
    dh = ROW_SUBLANES * LANES
    n_experts, d, de = wg.shape
    wmap = lambda e, bs, nb, nu: (e, 0, 0)
    assert xsorted.shape[1:] == (ROW_SUBLANES, LANES) and d == 2 * dh
    return pl.pallas_call(
        functools.partial(_experts_kernel, n_experts=n_experts),
        out_shape=jax.ShapeDtypeStruct((n_slots, ROW_SUBLANES, LANES), U32),
        grid_spec=pltpu.PrefetchScalarGridSpec(
            num_scalar_prefetch=3, grid=(n_experts,),
            in_specs=[pl.BlockSpec(memory_space=pl.ANY), pl.BlockSpec((1, d, de), wmap),
                      pl.BlockSpec((1, d, de), wmap), pl.BlockSpec((1, de, d), wmap)],
            out_specs=pl.BlockSpec(memory_space=pl.ANY),
            scratch_shapes=[pltpu.VMEM((d, 2 * de), BF16), pltpu.VMEM((de, d), BF16),
                            pltpu.VMEM((X_BUFFERS, MOE_BLOCK, dh), U32),
                            pltpu.VMEM((2, MOE_BLOCK, dh), U32),
                            pltpu.SemaphoreType.DMA((X_BUFFERS,)), pltpu.SemaphoreType.DMA((2,))]),
        compiler_params=pltpu.CompilerParams(dimension_semantics=("arbitrary",), vmem_limit_bytes=VMEM_LIMIT),
        name="experts",
    )(block_start, n_blocks, n_used, xsorted, wg, wu, wd)


def _combine_kernel(dest_ref, dnext_ref, ys_ref, wt_ref, x1_ref, u_ref, mod_ref, wsgu_ref, wsd_ref,
                    gfin_ref, o_ref, gbuf, accbuf, sem, *, tc, n_steps):
    i = pl.program_id(0)
    slot = i % 2

    def issue(dref, s):
        def body(t, carry):
            for k in range(TOP_K):
                pltpu.make_async_copy(ys_ref.at[dref[k, t]], gbuf.at[s, k, t], sem.at[s]).start(priority=k % 2)
            return carry
        lax.fori_loop(0, tc, body, 0, unroll=4)

    @pl.when(i == 0)
    def _():
        issue(dest_ref, 0)

    def wait(t, carry):
        for k in range(TOP_K):
            pltpu.make_async_copy(ys_ref.at[0], gbuf.at[slot, 0, 0], sem.at[slot]).wait()
        return carry

    lax.fori_loop(0, tc, wait, 0, unroll=8)

    def sum_and_issue(with_issue):
        zeros = jnp.zeros((GATHER_CHUNK, ROW_SUBLANES, LANES), F32)

        def body(c, carry):
            prev_l, prev_r = carry
            t0 = c * GATHER_CHUNK
            prev_rows = pl.ds(jnp.maximum(t0 - GATHER_CHUNK, 0), GATHER_CHUNK)
            accbuf[0, prev_rows] = prev_l
            accbuf[1, prev_rows] = prev_r
            rows = pl.ds(t0, GATHER_CHUNK)
            wts = wt_ref[rows]
            gathered = [gbuf[slot, k, rows] for k in range(TOP_K)]
            if with_issue:
                for tt in range(GATHER_CHUNK):
                    for k in range(TOP_K):
                        pltpu.make_async_copy(ys_ref.at[dnext_ref[k, t0 + tt]], gbuf.at[1 - slot, k, t0 + tt],
                                              sem.at[1 - slot]).start(priority=k % 2)
            sum_l, sum_r = zeros, zeros
            for k in range(TOP_K):
                ga, gb = _unpack_halves(gathered[k])
                wk = jnp.broadcast_to(wts[:, k:k + 1, :], (GATHER_CHUNK, ROW_SUBLANES, LANES))
                sum_l = sum_l + ga * wk
                sum_r = sum_r + gb * wk
            return sum_l, sum_r

        n_chunks = tc // GATHER_CHUNK
        last_l, last_r = lax.fori_loop(0, n_chunks, body, (zeros, zeros))
        last_rows = pl.ds((n_chunks - 1) * GATHER_CHUNK, GATHER_CHUNK)
        accbuf[0, last_rows] = last_l
        accbuf[1, last_rows] = last_r

    @pl.when(i + 1 < n_steps)
    def _():
        sum_and_issue(True)

    @pl.when(i + 1 >= n_steps)
    def _():
        sum_and_issue(False)

    ua, ub = _unpack_halves(u_ref[...])
    half = ua.shape[1]
    hgu = _dot(jnp.concatenate([ua.astype(BF16), ub.astype(BF16)], axis=1), wsgu_ref[...])
    dsh = wsd_ref.shape[0]
    hg = hgu[:, :dsh]
    shared = _dot((hg * _sigmoid(hg) * hgu[:, dsh:]).astype(BF16), wsd_ref[...])
    acc_l = shared[:, :half]
    acc_r = shared[:, half:]
    acc_l = jnp.concatenate([acc_l[:, j * LANES:(j + 1) * LANES] + accbuf[0, :, j, :] for j in range(ROW_SUBLANES)],
                            axis=1)
    acc_r = jnp.concatenate([acc_r[:, j * LANES:(j + 1) * LANES] + accbuf[1, :, j, :] for j in range(ROW_SUBLANES)],
                            axis=1)
    m = mod_ref[0]
    xo_l = x1_ref[:, :half] + m[5:6, :half] * acc_l
    xo_r = x1_ref[:, half:] + m[5:6, half:] * acc_r
    ms = (jnp.sum(xo_l * xo_l, axis=-1, keepdims=True) + jnp.sum(xo_r * xo_r, axis=-1, keepdims=True)) / (2 * half)
    inv = lax.rsqrt(ms + EPS)
    o_ref[:, :half] = xo_l * inv * gfin_ref[:, :half]
    o_ref[:, half:] = xo_r * inv * gfin_ref[:, half:]


def _combine(dest, ysorted, wts_t, x1, u2p, mods3, wsgu, wsd, gfin, *, seq):
    t, d = x1.shape
    dh = d // 2
    tc = DISPATCH_TILE
    tiles_per_batch = seq // tc
    n_steps = t // tc
    kern = functools.partial(_combine_kernel, tc=tc, n_steps=n_steps)
    row = lambda i: (i, 0)
    const = lambda i: (0, 0)
    return pl.pallas_call(
        kern,
        out_shape=jax.ShapeDtypeStruct((t, d), F32),
        grid=(n_steps,),
        in_specs=[pl.BlockSpec((TOP_K, tc), lambda i: (0, i), memory_space=pltpu.SMEM),
                  pl.BlockSpec((TOP_K, tc), lambda i: (0, jnp.minimum(i + 1, n_steps - 1)), memory_space=pltpu.SMEM),
                  pl.BlockSpec(memory_space=pl.ANY), pl.BlockSpec((tc, TOP_K, LANES), lambda i: (i, 0, 0)),
                  pl.BlockSpec((tc, d), row), pl.BlockSpec((tc, dh), row),
                  pl.BlockSpec((1, 6, d), lambda i: (i // tiles_per_batch, 0, 0)),
                  pl.BlockSpec(wsgu.shape, const), pl.BlockSpec(wsd.shape, const), pl.BlockSpec((1, d), const)],
        out_specs=pl.BlockSpec((tc, d), row),
        scratch_shapes=[pltpu.VMEM((2, TOP_K, tc, ROW_SUBLANES, LANES), U32),
                        pltpu.VMEM((2, tc, ROW_SUBLANES, LANES), F32), pltpu.SemaphoreType.DMA((2,))],
        compiler_params=pltpu.CompilerParams(dimension_semantics=("arbitrary",), vmem_limit_bytes=VMEM_LIMIT),
        name="combine",
    )(dest, dest, ysorted, wts_t, x1, u2p, mods3, wsgu, wsd, gfin)


def kernel(x, c, ctx, c_ctx, ada_w, ada_b, norm_mix_g, norm_ffn_g, w_in, conv_xbc_w, conv_xbc_b, ssd_dt_bias, ssd_a_log, ssd_d, ssd_norm_g, conv_qk_w, conv_qk_b, mlstm_i_bias, mlstm_f_bias, mlstm_norm_g, w_ssd_out, w_mlstm_out, w_out, router_w, router_bias, moe_w_gate, moe_w_up, moe_w_down, shared_w_gate, shared_w_up, shared_w_down, norm_final_g):
    batch, seq, d = x.shape
    ctx_len = ctx.shape[1]
    depth = ada_w.shape[0]
    assert depth == 1, "only the single-layer configuration is implemented"
    assert seq % CHUNK == 0 and ctx_len % CHUNK == 0 and seq % GRID_W == 0
    l = 0
    n_sh = d // SSD_HEAD_DIM
    n_mh = d // MLSTM_DV
    dk = MLSTM_DV // 2
    d_xbc = d + 2 * SSD_GROUPS * SSD_STATE
    d_qk = 2 * n_mh * dk
    sizes = (d, d_xbc, 2 * n_sh, d_qk, d, 4 * n_mh, d, 2 * d)
    offs = np.concatenate([[0], np.cumsum(sizes)])
    assert offs[-1] == w_in.shape[2] and 2 * n_sh + 2 * n_mh <= LANES

    cond = jnp.concatenate([c, c_ctx[None], jnp.zeros((8 - (batch + 1) % 8, d), F32)], axis=0)
    mods = _adaln(cond, ada_w[l], ada_b[l])
    mods3 = mods.reshape(mods.shape[0], 6, d)

    w = w_in[l]
    seg = lambda k: w[:, offs[k]:offs[k + 1]]
    wbig = jnp.concatenate([seg(0), seg(1), seg(3), seg(4), seg(6), seg(7)], axis=1).astype(BF16)
    w_dt = seg(2)
    w_g = seg(5).reshape(d, 2, 2, n_mh)
    w_i = w_g[:, :, 0].reshape(d, 2 * n_mh)
    w_f = w_g[:, :, 1].reshape(d, 2 * n_mh)
    pad = jnp.zeros((d, LANES - 2 * n_sh - 2 * n_mh), F32)
    wsm = jnp.concatenate([w_dt, w_f, pad, w_dt, w_i, pad], axis=1)
    wsh = wsm.astype(BF16)
    wsl = (wsm - wsh.astype(F32)).astype(BF16)
    padb = jnp.zeros((LANES - 2 * n_sh - 2 * n_mh,), F32)
    dtb = ssd_dt_bias[l].reshape(-1).astype(F32)
    smb = jnp.concatenate([dtb, mlstm_f_bias[l].reshape(-1).astype(F32), padb,
                           dtb, mlstm_i_bias[l].reshape(-1).astype(F32), padb]).reshape(1, 2 * LANES)
    aneg = jnp.concatenate([-jnp.exp(ssd_a_log[l].astype(F32)).reshape(-1),
                            jnp.zeros((LANES - 2 * n_sh,), F32)]).reshape(1, LANES)

    x2 = x.reshape(batch * seq, d)
    ctx2 = ctx.reshape(batch * ctx_len, d)
    z, xs, bc, qk, v, og, mg, small, smallt = _inproj(
        x2, ctx2, mods3, norm_mix_g[l].reshape(1, d), wbig, wsh, wsl,
        conv_xbc_w[l], conv_xbc_b[l].reshape(1, d_xbc), conv_qk_w[l], conv_qk_b[l].reshape(1, d_qk), smb, aneg,
        batch=batch, seq=seq, ctx_len=ctx_len, dk=dk)

    ncc = ctx_len // CHUNK
    ncl = seq // CHUNK
    y0, y1, h0, h1 = _scans(xs, bc, qk, v, small, smallt, batch=batch, n_ctx_chunks=ncc, n_lat_chunks=ncl, dk=dk)

    dexp = jnp.repeat(ssd_d[l].astype(F32), SSD_HEAD_DIM).reshape(1, d)
    x1, u2 = _merge(y0, y1, xs, z, h0, h1, og, mg, x2, mods3, dexp, ssd_norm_g[l].reshape(1, d),
                    mlstm_norm_g[l].reshape(1, d), norm_ffn_g[l].reshape(1, d),
                    w_ssd_out[l].astype(BF16), w_mlstm_out[l].astype(BF16), w_out[l].astype(BF16),
                    batch=batch, seq=seq, n_ctx_tok=batch * ctx_len)

    n_experts = router_w.shape[2]
    idx, pos, wts, cnt = _router(u2, router_w[l].T.astype(BF16), router_bias[l].astype(F32).reshape(n_experts, 1))
    counts = cnt[:, 0].astype(I32)
    padded = (counts + MOE_BLOCK - 1) // MOE_BLOCK * MOE_BLOCK
    pad_end = jnp.cumsum(padded)
    pad_start = (pad_end - padded).astype(I32)
    t = batch * seq
    nb = t * TOP_K // MOE_BLOCK + n_experts
    n_used = (pad_end[-1] // MOE_BLOCK).astype(I32).reshape(1)

    dest = _slots(idx, pos, pad_start.astype(F32).reshape(n_experts, 1))
    xsorted = _dispatch(pad_start, counts, dest, u2, nb * MOE_BLOCK)
    ysorted = _experts(pad_start // MOE_BLOCK, (padded // MOE_BLOCK).astype(I32), n_used, xsorted,
                       moe_w_gate[l], moe_w_up[l], moe_w_down[l])
    wts_lanes = jnp.broadcast_to(wts.T[:, :, None], (t, TOP_K, LANES))
    out = _combine(dest, ysorted, wts_lanes, x1, u2, mods3,
                   jnp.concatenate([shared_w_gate[l], shared_w_up[l]], axis=1).astype(BF16), shared_w_down[l].astype(BF16),
                   norm_final_g.reshape(1, d), seq=seq)
    return out.reshape(batch, seq, d)
```

```python
import functools

import numpy as np
import jax
import jax.numpy as jnp
from jax import lax
from jax.experimental import pallas as pl
from jax.experimental.pallas import tpu as pltpu

F32 = jnp.float32
BF16 = jnp.bfloat16
I32 = jnp.int32
U32 = jnp.uint32

EPS = 1e-6
CHUNK = 128
CONV_K = 5
GRID_W = 64
SSD_HEAD_DIM = 64
SSD_STATE = 128
SSD_GROUPS = 2
MLSTM_DV = 128
N_EXPERT_GROUPS = 8
TOPK_GROUPS = 4
TOP_K = 8
ROUTED_SCALE = 2.5

LANES = 128
SUBLANES = 8
PACK_ROWS = 16
TOKEN_TILE = 512
COL_CHUNK = 512
ADALN_COLS = 1536
CONV_MASK_COLS = 8
INPROJ_PARTS = 2
MOE_BLOCK = 512
DISPATCH_TILE = 256
GATHER_CHUNK = 4
X_BUFFERS = 4
ROW_SUBLANES = 4
VMEM_LIMIT = 56 * 1024 * 1024
NEG_INF = float("-inf")


def _dot(a, b):
    return jnp.dot(a, b, preferred_element_type=F32)


def _dot_nt(a, b):
    return lax.dot_general(a, b, (((1,), (1,)), ((), ())), preferred_element_type=F32)


def _dot_tn(a, b):
    return lax.dot_general(a, b, (((0,), (0,)), ((), ())), preferred_element_type=F32)


def _spread(parts, e):
    res = _dot(jnp.concatenate(parts, axis=0).astype(BF16), e)
    out, r0 = [], 0
    for p in parts:
        out.append(res[r0:r0 + p.shape[0]])
        r0 += p.shape[0]
    return out


def _rows16(row):
    r8 = jnp.broadcast_to(row, (SUBLANES, row.shape[1]))
    return jnp.concatenate([r8, r8], axis=0)


def _sigmoid(v):
    return 1.0 / (1.0 + jnp.exp(-v))


def _pack_halves(v):
    n = v.shape[1] // 2
    hi = lax.bitcast_convert_type(v[:, :n].astype(BF16).astype(F32), U32)
    lo = lax.bitcast_convert_type(v[:, n:].astype(BF16).astype(F32), U32)
    return hi | (lo >> 16)


def _unpack_halves(p):
    left = lax.bitcast_convert_type(p & jnp.uint32(0xFFFF0000), F32)
    right = lax.bitcast_convert_type(p << 16, F32)
    return left, right


def _softplus(v):
    return jnp.maximum(v, 0.0) + jnp.log1p(jnp.exp(-jnp.abs(v)))


def _adaln_kernel(c_ref, w_ref, b_ref, o_ref):
    c = c_ref[...]
    s = c * _sigmoid(c)
    w = w_ref[...]
    s_hi = s.astype(BF16)
    s_lo = (s - s_hi.astype(F32)).astype(BF16)
    w_hi = w.astype(BF16)
    w_lo = (w - w_hi.astype(F32)).astype(BF16)
    o_ref[...] = _dot(s_hi, w_hi) + _dot(s_lo, w_hi) + _dot(s_hi, w_lo) + b_ref[...]


def _adaln(cond, w, b):
    rows, d = cond.shape
    n = w.shape[1]
    tn = ADALN_COLS if n % ADALN_COLS == 0 else n
    return pl.pallas_call(
        _adaln_kernel,
        out_shape=jax.ShapeDtypeStruct((rows, n), F32),
        grid=(n // tn,),
        in_specs=[pl.BlockSpec((rows, d), lambda j: (0, 0)),
                  pl.BlockSpec((d, tn), lambda j: (0, j)),
                  pl.BlockSpec((1, tn), lambda j: (0, j))],
        out_specs=pl.BlockSpec((rows, tn), lambda j: (0, j)),
        compiler_params=pltpu.CompilerParams(dimension_semantics=("arbitrary",), vmem_limit_bytes=VMEM_LIMIT),
        name="adaln",
    )(cond, w, b.reshape(1, n))


CONV_SHIFTS = tuple(j - CONV_K // 2 for j in range(CONV_K) if j != CONV_K // 2)


def _conv_masks(tm, seg_len):
    assert len(CONV_SHIFTS) <= CONV_MASK_COLS
    pos = np.arange(tm) % seg_len
    m = np.zeros((tm, CONV_MASK_COLS), np.float32)
    for i, s in enumerate(CONV_SHIFTS):
        m[:, i] = ((pos + s >= 0) & (pos + s < seg_len)).astype(np.float32)
    return m


def _conv_silu(acc, w5, bias, vm, tm):
    out = acc * w5[CONV_K // 2:CONV_K // 2 + 1] + bias
    for i, s in enumerate(CONV_SHIFTS):
        shifted = pltpu.roll(acc, (-s) % tm, axis=0)
        j = s + CONV_K // 2
        out = out + (shifted * vm[:, i:i + 1]) * w5[j:j + 1]
    return out * _sigmoid(out)


def _inproj_kernel(x_ref, ctx_ref, mod_ref, g_ref, wbig_ref, wsh_ref, wsl_ref, cwx_ref, cbx_ref, cwq_ref, cbq_ref,
                   smb_ref, aneg_ref, tril_ref, triu_ref, cmask_ref,
                   z_ref, xs_ref, bc_ref, qk_ref, v_ref, og_ref, mg_ref, small_ref, smallt_ref,
                   *, n_ctx_tiles, tm, d_model, dk):
    i = pl.program_id(0)
    is_ctx = i < n_ctx_tiles
    xt = jnp.where(is_ctx, ctx_ref[...], x_ref[...])
    m = mod_ref[0]
    ms = jnp.mean(xt * xt, axis=-1, keepdims=True)
    u = xt * lax.rsqrt(ms + EPS) * g_ref[...] * (1.0 + m[1:2]) + m[0:1]
    u_hi = u.astype(BF16)
    u_lo = (u - u_hi.astype(F32)).astype(BF16)
    vm = jnp.where(is_ctx, cmask_ref[1], cmask_ref[0])

    d = d_model
    d_bc = 2 * SSD_GROUPS * SSD_STATE
    pieces = [(z_ref, d, "plain", None), (xs_ref, d, "convx", 0), (bc_ref, d_bc, "convx", d),
              (qk_ref, d, "convq", 0), (v_ref, d, "plain", None), (og_ref, d, "sigmoid", None),
              (mg_ref, 2 * d, "sigmoid", None)]
    col = 0
    rp = tm // INPROJ_PARTS
    row_parts = [slice(p * rp, (p + 1) * rp) for p in range(INPROJ_PARTS)]
    u_parts = [u_hi[rows] for rows in row_parts]
    vm_parts = [vm[rows] for rows in row_parts]
    for ref, width, kind, coff in pieces:
        for c0 in range(0, width, COL_CHUNK):
            accs = [_dot(up, wbig_ref[:, col + c0:col + c0 + COL_CHUNK]) for up in u_parts]
            for rows, vmp, acc in zip(row_parts, vm_parts, accs):
                if kind == "convx":
                    cs = coff + c0
                    acc = _conv_silu(acc, cwx_ref[:, cs:cs + COL_CHUNK], cbx_ref[:, cs:cs + COL_CHUNK], vmp, rp)
                elif kind == "convq":
                    acc = _conv_silu(acc, cwq_ref[:, c0:c0 + COL_CHUNK], cbq_ref[:, c0:c0 + COL_CHUNK], vmp, rp)
                    if c0 < width // 2:
                        acc = acc * (dk ** -0.5)
                elif kind == "sigmoid":
                    acc = _sigmoid(acc)
                ref[rows, c0:c0 + COL_CHUNK] = acc.astype(ref.dtype)
        col += width

    wsh = wsh_ref[...]
    raw = _dot(u_hi, wsh) + _dot(u_lo, wsh) + _dot(u_hi, wsl_ref[...]) + smb_ref[...]
    p1 = raw[:, :LANES]
    p2 = raw[:, LANES:]
    lane = lax.broadcasted_iota(I32, (tm, LANES), 1)
    n_dt = 2 * (d_model // SSD_HEAD_DIM)
    n_g = 2 * (d_model // MLSTM_DV)
    is_dt = lane < n_dt
    is_gate = jnp.logical_and(lane >= n_dt, lane < n_dt + n_g)
    dt = _softplus(p2)
    pa = jnp.where(is_dt, dt * aneg_ref[...], jnp.where(is_gate, -_softplus(-p1), 0.0))
    pb = jnp.where(is_dt, dt, jnp.where(is_gate, p2, 0.0))
    lane_c = lax.broadcasted_iota(I32, (CHUNK, LANES), 1)
    is_dt_c = lane_c < n_dt
    rev = jnp.logical_or(jnp.logical_and(lane_c >= n_dt // 2, lane_c < n_dt),
                         jnp.logical_and(lane_c >= n_dt + n_g // 2, lane_c < n_dt + n_g))
    tril = tril_ref[...]
    triu = triu_ref[...]
    tq = lax.broadcasted_iota(I32, (CHUNK, LANES), 0)
    chunks = range(tm // CHUNK)
    planes_a, planes_b = [], []
    for c in chunks:
        a_c = pa[c * CHUNK:(c + 1) * CHUNK]
        hi = a_c.astype(BF16)
        r1 = a_c - hi.astype(F32)
        mid = r1.astype(BF16)
        lo = (r1 - mid.astype(F32)).astype(BF16)
        cs_f = _dot(tril, hi) + _dot(tril, mid) + _dot(tril, lo)
        cs_b = _dot(triu, hi) + _dot(triu, mid) + _dot(triu, lo)
        planes_a.append(jnp.where(rev, cs_b, cs_f))
    for c in chunks:
        pb_c = pb[c * CHUNK:(c + 1) * CHUNK]
        planes_b.append(jnp.where(is_dt_c, pb_c, pb_c - planes_a[c]))
    yfs = list(planes_b)
    ybs = list(planes_b)
    s = 1
    while s < CHUNK:
        for c in chunks:
            sh = pltpu.roll(yfs[c], s, axis=0)
            yfs[c] = jnp.maximum(yfs[c], jnp.where(tq >= s, sh, NEG_INF))
            sh = pltpu.roll(ybs[c], CHUNK - s, axis=0)
            ybs[c] = jnp.maximum(ybs[c], jnp.where(tq + s < CHUNK, sh, NEG_INF))
        s *= 2
    for c in chunks:
        r0 = c * CHUNK
        small_ref[r0:r0 + CHUNK, 0:LANES] = planes_a[c]
        small_ref[r0:r0 + CHUNK, LANES:2 * LANES] = planes_b[c]
        small_ref[r0:r0 + CHUNK, 2 * LANES:3 * LANES] = jnp.where(rev, ybs[c], yfs[c])
        smallt_ref[c, 0] = planes_a[c].T
        smallt_ref[c, 1] = planes_b[c].T


def _inproj(x2, ctx2, mods3, g, wbig, wsh, wsl, cwx, cbx, cwq, cbq, smb, aneg, *, batch, seq, ctx_len, dk):
    d = x2.shape[1]
    tm = TOKEN_TILE
    n_ctx_tok = batch * ctx_len
    rp = tm // INPROJ_PARTS
    assert n_ctx_tok % tm == 0 and seq % tm == 0 and rp % ctx_len == 0 and rp % GRID_W == 0
    n_ctx_tiles = n_ctx_tok // tm
    tiles_per_batch = seq // tm
    n_tiles = n_ctx_tiles + batch * tiles_per_batch
    t_all = n_tiles * tm
    n_big = wbig.shape[1]
    tril = jnp.asarray(np.tril(np.ones((CHUNK, CHUNK), np.float32)), BF16)
    triu = jnp.asarray(np.triu(np.ones((CHUNK, CHUNK), np.float32)), BF16)

    def x_map(i):
        return (jnp.maximum(i - n_ctx_tiles, 0), 0)

    def ctx_map(i):
        return (jnp.minimum(i, n_ctx_tiles - 1), 0)

    def mod_map(i):
        return (jnp.where(i < n_ctx_tiles, batch, jnp.maximum(i - n_ctx_tiles, 0) // tiles_per_batch), 0, 0)

    const = lambda i: (0, 0)
    row = lambda i: (i, 0)
    cmask = jnp.asarray(np.stack([_conv_masks(tm, GRID_W), _conv_masks(tm, ctx_len)]))
    kern = functools.partial(_inproj_kernel, n_ctx_tiles=n_ctx_tiles, tm=tm, d_model=d, dk=dk)
    d_bc = 2 * SSD_GROUPS * SSD_STATE
    outs = [jax.ShapeDtypeStruct((t_all, d), BF16), jax.ShapeDtypeStruct((t_all, d), BF16),
            jax.ShapeDtypeStruct((t_all, d_bc), BF16), jax.ShapeDtypeStruct((t_all, d), BF16),
            jax.ShapeDtypeStruct((t_all, d), BF16), jax.ShapeDtypeStruct((t_all, d), BF16),
            jax.ShapeDtypeStruct((t_all, 2 * d), BF16), jax.ShapeDtypeStruct((t_all, 3 * LANES), F32),
            jax.ShapeDtypeStruct((t_all // CHUNK, 2, LANES, CHUNK), F32)]
    out_specs = [pl.BlockSpec((tm, d), row), pl.BlockSpec((tm, d), row), pl.BlockSpec((tm, d_bc), row),
                 pl.BlockSpec((tm, d), row), pl.BlockSpec((tm, d), row), pl.BlockSpec((tm, d), row),
                 pl.BlockSpec((tm, 2 * d), row), pl.BlockSpec((tm, 3 * LANES), row),
                 pl.BlockSpec((tm // CHUNK, 2, LANES, CHUNK), lambda i: (i, 0, 0, 0))]
    in_specs = [pl.BlockSpec((tm, d), x_map), pl.BlockSpec((tm, d), ctx_map),
                pl.BlockSpec((1, 6, d), mod_map), pl.BlockSpec((1, d), const),
                pl.BlockSpec((d, n_big), const, pipeline_mode=pl.Buffered(1)),
                pl.BlockSpec((d, 2 * LANES), const), pl.BlockSpec((d, 2 * LANES), const),
                pl.BlockSpec(cwx.shape, const), pl.BlockSpec(cbx.shape, const),
                pl.BlockSpec(cwq.shape, const), pl.BlockSpec(cbq.shape, const),
                pl.BlockSpec((1, 2 * LANES), const), pl.BlockSpec((1, LANES), const),
                pl.BlockSpec((CHUNK, CHUNK), const), pl.BlockSpec((CHUNK, CHUNK), const),
                pl.BlockSpec((2, tm, CONV_MASK_COLS), lambda i: (0, 0, 0))]
    return pl.pallas_call(
        kern, out_shape=outs, grid=(n_tiles,), in_specs=in_specs, out_specs=out_specs,
        compiler_params=pltpu.CompilerParams(dimension_semantics=("arbitrary",), vmem_limit_bytes=VMEM_LIMIT),
        name="inproj",
    )(x2, ctx2, mods3, g, wbig, wsh, wsl, cwx, cbx, cwq, cbq, smb, aneg, tril, triu, cmask)


def _chunk_block_map(direction, batch, n_ctx_chunks, n_lat_chunks):
    def idx(b, s):
        if direction == 0:
            c_ctx = s
            c_lat = s - n_ctx_chunks
        else:
            c_ctx = n_ctx_chunks - 1 - s
            c_lat = n_lat_chunks - 1 - (s - n_ctx_chunks)
        return jnp.where(s < n_ctx_chunks, b * n_ctx_chunks + c_ctx, batch * n_ctx_chunks + b * n_lat_chunks + c_lat)
    return idx


def _ssd_kernel(xs_ref, bc_ref, small_ref, smallt_ref, e16_ref, y_ref, st_ref, *, direction, n_heads):
    s = pl.program_id(1)

    @pl.when(s == 0)
    def _():
        st_ref[...] = jnp.zeros_like(st_ref)

    hpg = n_heads // SSD_GROUPS
    lane0 = direction * n_heads
    last = CHUNK - 1 if direction == 0 else 0
    lane = lax.broadcasted_iota(I32, (CHUNK, LANES), 1)
    lm = jnp.logical_and(lane >= lane0, lane < lane0 + n_heads)
    plane_a = jnp.where(lm, small_ref[:, 0:LANES], 0.0)
    plane_b = jnp.where(lm, small_ref[:, LANES:2 * LANES], 0.0)
    cum_t = smallt_ref[0, 0]
    tot = plane_a[last:last + 1]
    lane16 = lax.broadcasted_iota(I32, (PACK_ROWS, LANES), 1)
    lm16 = jnp.logical_and(lane16 >= lane0, lane16 < lane0 + n_heads)
    dec_rows = jnp.where(lm16, jnp.exp(_rows16(tot)), 0.0)
    dtx, ecx, wx, decx = _spread(
        [plane_b, jnp.where(lm, jnp.exp(plane_a), 0.0), plane_b * jnp.exp(tot - plane_a), dec_rows], e16_ref[...])
    decx = decx[0:1]

    xf = xs_ref[...].astype(F32)
    xdt = xf * dtx
    xw = (xf * wx).astype(BF16)
    plane64 = lax.broadcasted_iota(I32, (CHUNK, 2 * SSD_HEAD_DIM), 1) < SSD_HEAD_DIM
    iq = lax.broadcasted_iota(I32, (CHUNK, CHUNK), 0)
    ik = lax.broadcasted_iota(I32, (CHUNK, CHUNK), 1)
    mask = (iq >= ik) if direction == 0 else (iq <= ik)
    gw = hpg * SSD_HEAD_DIM
    groups = range(SSD_GROUPS)
    bgs = [bc_ref[:, g * SSD_STATE:(g + 1) * SSD_STATE] for g in groups]
    cgs = [bc_ref[:, (SSD_GROUPS + g) * SSD_STATE:(SSD_GROUPS + g + 1) * SSD_STATE] for g in groups]
    sgs = [st_ref[:, g * gw:(g + 1) * gw] for g in groups]
    cbs = [_dot_nt(cgs[g], bgs[g]) for g in groups]
    y_inters = [_dot(cgs[g], sgs[g].astype(BF16)) * ecx[:, g * gw:(g + 1) * gw] for g in groups]
    m_hs = []
    for h in range(n_heads):
        cum_q = jnp.broadcast_to(plane_a[:, lane0 + h:lane0 + h + 1], (CHUNK, CHUNK))
        seg = cum_q - cum_t[lane0 + h:lane0 + h + 1, :]
        m_hs.append((cbs[h // hpg] * jnp.exp(jnp.where(mask, seg, NEG_INF))).astype(BF16))
    for pair in range(n_heads // 2):
        g = (2 * pair) // hpg
        c0 = 2 * pair * SSD_HEAD_DIM
        xpair = xdt[:, c0:c0 + 2 * SSD_HEAD_DIM]
        acc = y_inters[g][:, c0 - g * gw:c0 - g * gw + 2 * SSD_HEAD_DIM]
        for par in range(2):
            keep = plane64 if par == 0 else jnp.logical_not(plane64)
            acc = acc + _dot(m_hs[2 * pair + par], jnp.where(keep, xpair, 0.0).astype(BF16))
        y_ref[:, c0:c0 + 2 * SSD_HEAD_DIM] = acc.astype(y_ref.dtype)
    for g in groups:
        st_ref[:, g * gw:(g + 1) * gw] = (decx[:, g * gw:(g + 1) * gw] * sgs[g]
                                          + _dot_tn(bgs[g], xw[:, g * gw:(g + 1) * gw]))


def _selectors(direction, d):
    n_sh = d // SSD_HEAD_DIM
    n_mh = d // MLSTM_DV
    e16 = np.zeros((LANES, d), np.float32)
    em = np.zeros((LANES, d), np.float32)
    for h in range(n_sh):
        e16[direction * n_sh + h, h * SSD_HEAD_DIM:(h + 1) * SSD_HEAD_DIM] = 1.0
    for h in range(n_mh):
        em[2 * n_sh + direction * n_mh + h, h * MLSTM_DV:(h + 1) * MLSTM_DV] = 1.0
    return jnp.asarray(e16, BF16), jnp.asarray(em, BF16)


def _mlstm_kernel(qk_ref, v_ref, small_ref, smallt_ref, em_ref, h_ref, cn_ref, m_ref, *, direction, n_heads, dk):
    s = pl.program_id(1)

    @pl.when(s == 0)
    def _():
        cn_ref[...] = jnp.zeros_like(cn_ref)
        m_ref[...] = jnp.zeros_like(m_ref)

    n_dt = 2 * (n_heads * MLSTM_DV // SSD_HEAD_DIM)
    lane0 = n_dt + direction * n_heads
    last = CHUNK - 1 if direction == 0 else 0
    lane = lax.broadcasted_iota(I32, (CHUNK, LANES), 1)
    lm = jnp.logical_and(lane >= lane0, lane < lane0 + n_heads)
    b_q = jnp.where(lm, small_ref[:, 0:LANES], 0.0)
    r_k = jnp.where(lm, small_ref[:, LANES:2 * LANES], 0.0)
    cmr = jnp.where(lm, small_ref[:, 2 * LANES:3 * LANES], 0.0)
    r_t = smallt_ref[0, 1]
    m_all = m_ref[...]
    m_row = m_all[0:1]
    mm = jnp.maximum(cmr, m_row)
    w_state = jnp.exp(m_row - mm)
    e_mq = jnp.exp(-(b_q + mm))
    m_base8 = jnp.maximum(m_all, _rows16(cmr[last:last + 1]))
    m_base = m_base8[0:1]
    w_k = jnp.where(lm, jnp.exp(r_k - m_base), 0.0)
    lane16 = lax.broadcasted_iota(I32, (PACK_ROWS, LANES), 1)
    lm16 = jnp.logical_and(lane16 >= lane0, lane16 < lane0 + n_heads)
    dec_rows = jnp.where(lm16, jnp.exp(m_all - m_base8), 0.0)
    wsx, emqx, wkx, decx = _spread(
        [jnp.where(lm, w_state, 0.0), jnp.where(lm, e_mq, 0.0), w_k, dec_rows], em_ref[...])
    decx = decx[0:1]
    m_ref[...] = jnp.where(lm16, _rows16(b_q[last:last + 1]) + m_base8, 0.0)

    iq = lax.broadcasted_iota(I32, (CHUNK, CHUNK), 0)
    ik = lax.broadcasted_iota(I32, (CHUNK, CHUNK), 1)
    mask = (iq >= ik) if direction == 0 else (iq <= ik)
    ones = jnp.ones((CHUNK, MLSTM_DV), BF16)
    d_qk = n_heads * dk
    heads = range(n_heads)
    hsl = [slice(h * MLSTM_DV, (h + 1) * MLSTM_DV) for h in heads]
    qs = [qk_ref[:, h * dk:(h + 1) * dk] for h in heads]
    ks = [qk_ref[:, d_qk + h * dk:d_qk + (h + 1) * dk] for h in heads]
    vs = [v_ref[:, hsl[h]] for h in heads]
    cns = [cn_ref[h] for h in heads]
    scores = [_dot_nt(qs[h], ks[h]) for h in heads]
    inter = [_dot(qs[h], cns[h].astype(BF16)) for h in heads]
    smats = []
    for h in heads:
        mm_q = jnp.broadcast_to(mm[:, lane0 + h:lane0 + h + 1], (CHUNK, CHUNK))
        dmat = jnp.exp(jnp.where(mask, r_t[lane0 + h:lane0 + h + 1, :] - mm_q, NEG_INF))
        smats.append((scores[h] * dmat).astype(BF16))
    tots = [_dot(smats[h], jnp.concatenate([vs[h], ones], axis=1)) for h in heads]
    for h in heads:
        hs = hsl[h]
        kh, vh, cn = ks[h], vs[h], cns[h]
        wsh = wsx[:, hs]
        tot = tots[h] + jnp.concatenate([wsh, wsh], axis=1) * inter[h]
        num = tot[:, :MLSTM_DV]
        den = tot[:, MLSTM_DV:]
        h_ref[:, hs] = (num / jnp.maximum(jnp.abs(den), emqx[:, hs])).astype(h_ref.dtype)
        wkh = wkx[:, hs]
        rhs = jnp.concatenate([(vh.astype(F32) * wkh).astype(BF16), wkh.astype(BF16)], axis=1)
        dech = decx[:, hs]
        cn_ref[h] = jnp.concatenate([dech, dech], axis=1) * cn + _dot_tn(kh, rhs)


def _scans_kernel(*refs, n_sh, n_mh, dk):
    (xs0, bc0, qk0, v0, sm0, smt0, xs1, bc1, qk1, v1, sm1, smt1, e16_0, em_0, e16_1, em_1,
     y0, y1, h0, h1, st0, st1, cn0, cn1, m0, m1) = refs
    _ssd_kernel(xs0, bc0, sm0, smt0, e16_0, y0, st0, direction=0, n_heads=n_sh)
    _ssd_kernel(xs1, bc1, sm1, smt1, e16_1, y1, st1, direction=1, n_heads=n_sh)
    _mlstm_kernel(qk0, v0, sm0, smt0, em_0, h0, cn0, m0, direction=0, n_heads=n_mh, dk=dk)
    _mlstm_kernel(qk1, v1, sm1, smt1, em_1, h1, cn1, m1, direction=1, n_heads=n_mh, dk=dk)


def _scans(xs, bc, qk, v, small, smallt, *, batch, n_ctx_chunks, n_lat_chunks, dk):
    t_all, d = xs.shape
    n_sh = d // SSD_HEAD_DIM
    n_mh = d // MLSTM_DV
    n_steps = n_ctx_chunks + n_lat_chunks
    in_specs, args = [], []
    for direction in range(2):
        idx = _chunk_block_map(direction, batch, n_ctx_chunks, n_lat_chunks)
        rows = lambda b, s, idx=idx: (idx(b, s), 0)
        in_specs += [pl.BlockSpec((CHUNK, d), rows), pl.BlockSpec((CHUNK, bc.shape[1]), rows),
                     pl.BlockSpec((CHUNK, qk.shape[1]), rows), pl.BlockSpec((CHUNK, d), rows),
                     pl.BlockSpec((CHUNK, 3 * LANES), rows),
                     pl.BlockSpec((1, 2, LANES, CHUNK), lambda b, s, idx=idx: (idx(b, s), 0, 0, 0))]
        args += [xs, bc, qk, v, small, smallt]
    out_specs = []
    for direction in (0, 1, 0, 1):
        idx = _chunk_block_map(direction, batch, n_ctx_chunks, n_lat_chunks)
        out_specs.append(pl.BlockSpec((CHUNK, d), lambda b, s, idx=idx: (idx(b, s), 0)))
    for direction in range(2):
        sel = _selectors(direction, d)
        in_specs += [pl.BlockSpec(a.shape, lambda b, s: (0, 0)) for a in sel]
        args += list(sel)
    state = [pltpu.VMEM((SSD_STATE, d), F32)] * 2 + [pltpu.VMEM((n_mh, dk, 2 * MLSTM_DV), F32)] * 2 \
        + [pltpu.VMEM((PACK_ROWS, LANES), F32)] * 2
    return pl.pallas_call(
        functools.partial(_scans_kernel, n_sh=n_sh, n_mh=n_mh, dk=dk),
        out_shape=[jax.ShapeDtypeStruct((t_all, d), BF16)] * 4,
        grid=(batch, n_steps),
        in_specs=in_specs, out_specs=out_specs, scratch_shapes=state,
        compiler_params=pltpu.CompilerParams(dimension_semantics=("arbitrary", "arbitrary"),
                                             vmem_limit_bytes=VMEM_LIMIT),
        name="scans",
    )(*args)


def _merge_kernel(y0_ref, y1_ref, xs_ref, z_ref, h0_ref, h1_ref, og_ref, mg_ref, x_ref, mod_ref, dexp_ref, gs_ref,
                  gm_ref, gf_ref, wso_ref, wmo_ref, wo_ref, x1_ref, u2_ref, *, d_model):
    d = d_model
    m = mod_ref[0]
    y = y0_ref[...].astype(F32) + y1_ref[...].astype(F32) + dexp_ref[...] * xs_ref[...].astype(F32)
    zf = z_ref[...].astype(F32)
    y = y * (zf * _sigmoid(zf))
    y = y * lax.rsqrt(jnp.mean(y * y, axis=-1, keepdims=True) + EPS) * gs_ref[...]
    a = _dot(y.astype(BF16), wso_ref[...])
    hm = h0_ref[...].astype(F32) + h1_ref[...].astype(F32)
    blocks = []
    for h in range(d // MLSTM_DV):
        blk = hm[:, h * MLSTM_DV:(h + 1) * MLSTM_DV]
        blocks.append(blk * lax.rsqrt(jnp.mean(blk * blk, axis=-1, keepdims=True) + EPS))
    hn = jnp.concatenate(blocks, axis=1) * gm_ref[...]
    bm = _dot((og_ref[...].astype(F32) * hn).astype(BF16), wmo_ref[...])
    merged = mg_ref[:, :d].astype(F32) * a + mg_ref[:, d:].astype(F32) * bm
    r = _dot(merged.astype(BF16), wo_ref[...])
    x1 = x_ref[...] + m[2:3] * r
    x1_ref[...] = x1
    u2 = x1 * lax.rsqrt(jnp.mean(x1 * x1, axis=-1, keepdims=True) + EPS) * gf_ref[...] * (1.0 + m[4:5]) + m[3:4]
    u2_ref[...] = _pack_halves(u2)


def _merge(y0, y1, xs, z, h0, h1, og, mg, x2, mods3, dexp, gs, gm, gf, wso, wmo, wo, *, batch, seq, n_ctx_tok):
    t, d = x2.shape
    tm = TOKEN_TILE
    off = n_ctx_tok // tm
    tiles_per_batch = seq // tm
    lat = lambda i: (i + off, 0)
    row = lambda i: (i, 0)
    const = lambda i: (0, 0)
    kern = functools.partial(_merge_kernel, d_model=d)
    wspec = pl.BlockSpec((d, d), const, pipeline_mode=pl.Buffered(1))
    return pl.pallas_call(
        kern,
        out_shape=[jax.ShapeDtypeStruct((t, d), F32), jax.ShapeDtypeStruct((t, d // 2), U32)],
        grid=(t // tm,),
        in_specs=[pl.BlockSpec((tm, d), lat), pl.BlockSpec((tm, d), lat), pl.BlockSpec((tm, d), lat),
                  pl.BlockSpec((tm, d), lat), pl.BlockSpec((tm, d), lat), pl.BlockSpec((tm, d), lat),
                  pl.BlockSpec((tm, d), lat), pl.BlockSpec((tm, 2 * d), lat), pl.BlockSpec((tm, d), row),
                  pl.BlockSpec((1, 6, d), lambda i: (i // tiles_per_batch, 0, 0)),
                  pl.BlockSpec((1, d), const), pl.BlockSpec((1, d), const), pl.BlockSpec((1, d), const),
                  pl.BlockSpec((1, d), const), wspec, wspec, wspec],
        out_specs=[pl.BlockSpec((tm, d), row), pl.BlockSpec((tm, d // 2), row)],
        compiler_params=pltpu.CompilerParams(dimension_semantics=("arbitrary",), vmem_limit_bytes=VMEM_LIMIT),
        name="merge",
    )(y0, y1, xs, z, h0, h1, og, mg, x2, mods3, dexp, gs, gm, gf, wso, wmo, wo)


def _first_index_of_max(vals, row_iota, n_rows):
    mx = jnp.max(vals, axis=0, keepdims=True)
    idx = jnp.min(jnp.where(vals == mx, row_iota, n_rows), axis=0, keepdims=True)
    return mx, idx


def _router_kernel(u_ref, wt_ref, bias_ref, su_ref, idx_ref, pos_ref, wts_ref, cnt_ref, run_ref, *, n_experts, tr):
    i = pl.program_id(0)

    @pl.when(i == 0)
    def _():
        run_ref[...] = jnp.zeros_like(run_ref)

    ua, ub = _unpack_halves(u_ref[...])
    urow = jnp.concatenate([ua.astype(BF16), ub.astype(BF16)], axis=1)
    scores = _sigmoid(_dot_nt(wt_ref[...], urow))
    biased = scores + bias_ref[...]
    gsz = n_experts // N_EXPERT_GROUPS
    gi = lax.broadcasted_iota(I32, (gsz, tr), 0).astype(F32)
    gscores = []
    for g in range(N_EXPERT_GROUPS):
        blk = biased[g * gsz:(g + 1) * gsz]
        m1, i1 = _first_index_of_max(blk, gi, gsz)
        m2 = jnp.max(jnp.where(gi == i1, NEG_INF, blk), axis=0, keepdims=True)
        gscores.append(m1 + m2)
    gs = jnp.concatenate(gscores, axis=0)
    g8 = lax.broadcasted_iota(I32, (N_EXPERT_GROUPS, tr), 0).astype(F32)
    gsel = jnp.zeros((N_EXPERT_GROUPS, tr), F32)
    for _ in range(TOPK_GROUPS):
        _, gidx = _first_index_of_max(gs, g8, N_EXPERT_GROUPS)
        hit = g8 == gidx
        gsel = jnp.where(hit, 1.0, gsel)
        gs = jnp.where(hit, NEG_INF, gs)
    cand = jnp.concatenate(
        [jnp.where(jnp.broadcast_to(gsel[g:g + 1], (gsz, tr)) > 0.5, biased[g * gsz:(g + 1) * gsz], NEG_INF)
         for g in range(N_EXPERT_GROUPS)], axis=0)
    ei = lax.broadcasted_iota(I32, (n_experts, tr), 0).astype(F32)
    sel = jnp.zeros((n_experts, tr), F32)
    idxs, ws = [], []
    for _ in range(TOP_K):
        _, eidx = _first_index_of_max(cand, ei, n_experts)
        hit = ei == eidx
        ws.append(jnp.sum(jnp.where(hit, scores, 0.0), axis=0, keepdims=True))
        idxs.append(eidx)
        sel = jnp.where(hit, 1.0, sel)
        cand = jnp.where(hit, NEG_INF, cand)
    wk = jnp.concatenate(ws, axis=0)
    wts_ref[...] = ROUTED_SCALE * wk / jnp.sum(wk, axis=0, keepdims=True)
    idx_ref[...] = jnp.concatenate(idxs, axis=0).astype(I32)
    selb = sel.astype(BF16)
    posmat = _dot(selb, su_ref[...]) + run_ref[:, 0:1]
    pos_ref[...] = jnp.concatenate(
        [jnp.sum(jnp.where(ei == idxs[k], posmat, 0.0), axis=0, keepdims=True) for k in range(TOP_K)],
        axis=0).astype(I32)
    run = run_ref[...] + _dot(selb, jnp.ones((tr, LANES), BF16))
    run_ref[...] = run
    cnt_ref[...] = run


def _router(u2p, router_wt, bias_col):
    t = u2p.shape[0]
    d = router_wt.shape[1]
    n_experts = router_wt.shape[0]
    tr = TOKEN_TILE
    su = jnp.asarray(np.triu(np.ones((tr, tr), np.float32), 1), BF16)
    kern = functools.partial(_router_kernel, n_experts=n_experts, tr=tr)
    col = lambda i: (0, i)
    const = lambda i: (0, 0)
    return pl.pallas_call(
        kern,
        out_shape=[jax.ShapeDtypeStruct((TOP_K, t), I32), jax.ShapeDtypeStruct((TOP_K, t), I32),
                   jax.ShapeDtypeStruct((TOP_K, t), F32), jax.ShapeDtypeStruct((n_experts, LANES), F32)],
        grid=(t // tr,),
        in_specs=[pl.BlockSpec((tr, d // 2), lambda i: (i, 0)), pl.BlockSpec((n_experts, d), const),
                  pl.BlockSpec((n_experts, 1), const), pl.BlockSpec((tr, tr), const)],
        out_specs=[pl.BlockSpec((TOP_K, tr), col), pl.BlockSpec((TOP_K, tr), col), pl.BlockSpec((TOP_K, tr), col),
                   pl.BlockSpec((n_experts, LANES), const)],
        scratch_shapes=[pltpu.VMEM((n_experts, LANES), F32)],
        compiler_params=pltpu.CompilerParams(dimension_semantics=("arbitrary",), vmem_limit_bytes=VMEM_LIMIT),
        name="router",
    )(u2p, router_wt, bias_col, su)


def _slots_kernel(idx_ref, pos_ref, pstart_ref, dest_ref, *, n_experts, tr):
    ei = lax.broadcasted_iota(I32, (n_experts, tr), 0).astype(F32)
    pstart = pstart_ref[...]
    idx = idx_ref[...].astype(F32)
    rows = [jnp.sum(jnp.where(ei == idx[k:k + 1], pstart, 0.0), axis=0, keepdims=True) for k in range(TOP_K)]
    dest_ref[...] = jnp.concatenate(rows, axis=0).astype(I32) + pos_ref[...]


def _slots(idx, pos, pstart_col):
    t = idx.shape[1]
    n_experts = pstart_col.shape[0]
    tr = TOKEN_TILE
    col = lambda i: (0, i)
    return pl.pallas_call(
        functools.partial(_slots_kernel, n_experts=n_experts, tr=tr),
        out_shape=jax.ShapeDtypeStruct((TOP_K, t), I32),
        grid=(t // tr,),
        in_specs=[pl.BlockSpec((TOP_K, tr), col), pl.BlockSpec((TOP_K, tr), col),
                  pl.BlockSpec((n_experts, 1), lambda i: (0, 0))],
        out_specs=pl.BlockSpec((TOP_K, tr), col),
        compiler_params=pltpu.CompilerParams(dimension_semantics=("arbitrary",), vmem_limit_bytes=VMEM_LIMIT),
        name="slots",
    )(idx, pos, pstart_col)


PAD_BITS = tuple(1 << b for b in reversed(range((MOE_BLOCK - 1).bit_length())))


def _dispatch_kernel(pstart_ref, cnt_ref, dest_ref, u_ref, xs_ref, ubuf, zbuf, usem, sem, psem, *, td, n_steps,
                     n_experts):
    i = pl.program_id(0)
    slot = i % 2

    def u_copies(step, s):
        rows = pl.ds(pl.multiple_of(step * td, td), td)
        return [pltpu.make_async_copy(u_ref.at[rows, pl.ds(j * LANES, LANES)], ubuf.at[s, :, j, :], usem.at[s])
                for j in range(ROW_SUBLANES)]

    @pl.when(i == 0)
    def _():
        for cp in u_copies(i, 0):
            cp.start()
        zbuf[...] = jnp.zeros_like(zbuf)

        def pads(e, wait):
            cnt = cnt_ref[e]
            n_pad = (MOE_BLOCK - (cnt & (MOE_BLOCK - 1))) & (MOE_BLOCK - 1)
            base = pstart_ref[e] + cnt
            for bit in PAD_BITS:
                cp = pltpu.make_async_copy(zbuf.at[pl.ds(0, bit)], xs_ref.at[pl.ds(base, bit)], psem)
                has = (n_pad & bit) != 0

                @pl.when(has)
                def _():
                    if wait:
                        cp.wait()
                    else:
                        cp.start()

                base = base + jnp.where(has, bit, 0)

        def start_pads(e, carry):
            pads(e, False)
            return carry

        def wait_pads(e, carry):
            pads(e, True)
            return carry

        lax.fori_loop(0, n_experts, start_pads, 0)
        lax.fori_loop(0, n_experts, wait_pads, 0)

    @pl.when(i + 1 < n_steps)
    def _():
        for cp in u_copies(i + 1, 1 - slot):
            cp.start()

    for cp in u_copies(i, slot):
        cp.wait()

    def start(t, carry):
        for k in range(TOP_K):
            pltpu.make_async_copy(ubuf.at[slot, t], xs_ref.at[dest_ref[k, t]], sem).start(priority=k % 2)
        return carry

    def wait(t, carry):
        for k in range(TOP_K):
            pltpu.make_async_copy(ubuf.at[slot, 0], xs_ref.at[0], sem).wait()
        return carry

    lax.fori_loop(0, td, start, 0, unroll=4)
    lax.fori_loop(0, td, wait, 0, unroll=8)


def _dispatch(pad_start, counts, dest, u2p, n_slots):
    t, dh = u2p.shape
    td = DISPATCH_TILE
    n_experts = pad_start.shape[0]
    assert MOE_BLOCK & (MOE_BLOCK - 1) == 0 and dh == ROW_SUBLANES * LANES
    kern = functools.partial(_dispatch_kernel, td=td, n_steps=t // td, n_experts=n_experts)
    return pl.pallas_call(
        kern,
        out_shape=jax.ShapeDtypeStruct((n_slots, ROW_SUBLANES, LANES), U32),
        grid_spec=pltpu.PrefetchScalarGridSpec(
            num_scalar_prefetch=2, grid=(t // td,),
            in_specs=[pl.BlockSpec((TOP_K, td), lambda i, ps, cn: (0, i), memory_space=pltpu.SMEM),
                      pl.BlockSpec(memory_space=pl.ANY)],
            out_specs=pl.BlockSpec(memory_space=pl.ANY),
            scratch_shapes=[pltpu.VMEM((2, td, ROW_SUBLANES, LANES), U32),
                            pltpu.VMEM((MOE_BLOCK // 2, ROW_SUBLANES, LANES), U32),
                            pltpu.SemaphoreType.DMA((2,)), pltpu.SemaphoreType.DMA, pltpu.SemaphoreType.DMA]),
        compiler_params=pltpu.CompilerParams(dimension_semantics=("arbitrary",), vmem_limit_bytes=VMEM_LIMIT),
        name="dispatch",
    )(pad_start, counts, dest, u2p)


def _experts_kernel(bs_ref, nblk_ref, nu_ref, xs_ref, wg_ref, wu_ref, wd_ref, ys_ref, wgub, wdb, xbuf, ybuf,
                    semx, semy, *, n_experts):
    e = pl.program_id(0)
    n_used = nu_ref[0]

    def x_copies(g):
        rows = pl.ds(pl.multiple_of(g * MOE_BLOCK, MOE_BLOCK), MOE_BLOCK)
        return [pltpu.make_async_copy(xs_ref.at[rows, j, :], xbuf.at[g % X_BUFFERS, :, pl.ds(j * LANES, LANES)],
                                      semx.at[g % X_BUFFERS]) for j in range(ROW_SUBLANES)]

    def x_start(g):
        for cp in x_copies(g):
            cp.start()

    def x_wait(g):
        for cp in x_copies(g):
            cp.wait()

    def y_copies(g, slot):
        rows = pl.ds(pl.multiple_of(g * MOE_BLOCK, MOE_BLOCK), MOE_BLOCK)
        return [pltpu.make_async_copy(ybuf.at[slot, :, pl.ds(j * LANES, LANES)], ys_ref.at[rows, j, :], semy.at[slot])
                for j in range(ROW_SUBLANES)]

    def y_start(g, slot):
        for cp in y_copies(g, slot):
            cp.start()

    def y_wait(g, slot):
        for cp in y_copies(g, slot):
            cp.wait()

    @pl.when(e == 0)
    def _():
        for g in range(X_BUFFERS - 1):
            @pl.when(g < n_used)
            def _():
                x_start(jnp.int32(g))

    g0 = bs_ref[e]
    nb = nblk_ref[e]

    de = wg_ref.shape[2]

    @pl.when(nb > 0)
    def _():
        wgub[:, :de] = wg_ref[0].astype(BF16)
        wgub[:, de:] = wu_ref[0].astype(BF16)
        wdb[...] = wd_ref[0].astype(BF16)

    def block(g, carry):
        slot = g % 2

        @pl.when(g + X_BUFFERS - 1 < n_used)
        def _():
            x_start(g + X_BUFFERS - 1)

        x_wait(g)

        @pl.when(g >= 2)
        def _():
            y_wait(g - 2, slot)

        xa, xb = _unpack_halves(xbuf[g % X_BUFFERS])
        xrow = jnp.concatenate([xa.astype(BF16), xb.astype(BF16)], axis=1)
        hgu = _dot(xrow, wgub[...])
        hg = hgu[:, :de]
        hb = (hg * _sigmoid(hg) * hgu[:, de:]).astype(BF16)
        ybuf[slot] = _pack_halves(_dot(hb, wdb[...]))
        y_start(g, slot)
        return carry

    lax.fori_loop(g0, g0 + nb, block, 0)

    @pl.when(e == n_experts - 1)
    def _():
        @pl.when(n_used >= 2)
        def _():
            y_wait(n_used - 2, n_used % 2)

        @pl.when(n_used >= 1)
        def _():
            y_wait(n_used - 1, (n_used - 1) % 2)


def _experts(block_start, n_blocks, n_used, xsorted, wg, wu, wd):
    n_slots = xsorted.shape[0]
    dh = ROW_SUBLANES * LANES
    n_experts, d, de = wg.shape
    wmap = lambda e, bs, nb, nu: (e, 0, 0)
    assert xsorted.shape[1:] == (ROW_SUBLANES, LANES) and d == 2 * dh
    return pl.pallas_call(
        functools.partial(_experts_kernel, n_experts=n_experts),
        out_shape=jax.ShapeDtypeStruct((n_slots, ROW_SUBLANES, LANES), U32),
        grid_spec=pltpu.PrefetchScalarGridSpec(
            num_scalar_prefetch=3, grid=(n_experts,),
            in_specs=[pl.BlockSpec(memory_space=pl.ANY), pl.BlockSpec((1, d, de), wmap),
                      pl.BlockSpec((1, d, de), wmap), pl.BlockSpec((1, de, d), wmap)],
            out_specs=pl.BlockSpec(memory_space=pl.ANY),
            scratch_shapes=[pltpu.VMEM((d, 2 * de), BF16), pltpu.VMEM((de, d), BF16),
                            pltpu.VMEM((X_BUFFERS, MOE_BLOCK, dh), U32),
                            pltpu.VMEM((2, MOE_BLOCK, dh), U32),
                            pltpu.SemaphoreType.DMA((X_BUFFERS,)), pltpu.SemaphoreType.DMA((2,))]),
        compiler_params=pltpu.CompilerParams(dimension_semantics=("arbitrary",), vmem_limit_bytes=VMEM_LIMIT),
        name="experts",
    )(block_start, n_blocks, n_used, xsorted, wg, wu, wd)


def _combine_kernel(dest_ref, dnext_ref, ys_ref, wt_ref, x1_ref, u_ref, mod_ref, wsgu_ref, wsd_ref,
                    gfin_ref, o_ref, gbuf, accbuf, sem, *, tc, n_steps):
    i = pl.program_id(0)
    slot = i % 2

    def issue(dref, s):
        def body(t, carry):
            for k in range(TOP_K):
                pltpu.make_async_copy(ys_ref.at[dref[k, t]], gbuf.at[s, k, t], sem.at[s]).start(priority=k % 2)
            return carry
        lax.fori_loop(0, tc, body, 0, unroll=4)

    @pl.when(i == 0)
    def _():
        issue(dest_ref, 0)

    def wait(t, carry):
        for k in range(TOP_K):
            pltpu.make_async_copy(ys_ref.at[0], gbuf.at[slot, 0, 0], sem.at[slot]).wait()
        return carry

    lax.fori_loop(0, tc, wait, 0, unroll=8)

    def sum_and_issue(with_issue):
        zeros = jnp.zeros((GATHER_CHUNK, ROW_SUBLANES, LANES), F32)

        def body(c, carry):
            prev_l, prev_r = carry
            t0 = c * GATHER_CHUNK
            prev_rows = pl.ds(jnp.maximum(t0 - GATHER_CHUNK, 0), GATHER_CHUNK)
            accbuf[0, prev_rows] = prev_l
            accbuf[1, prev_rows] = prev_r
            rows = pl.ds(t0, GATHER_CHUNK)
            wts = wt_ref[rows]
            gathered = [gbuf[slot, k, rows] for k in range(TOP_K)]
            if with_issue:
                for tt in range(GATHER_CHUNK):
                    for k in range(TOP_K):
                        pltpu.make_async_copy(ys_ref.at[dnext_ref[k, t0 + tt]], gbuf.at[1 - slot, k, t0 + tt],
                                              sem.at[1 - slot]).start(priority=k % 2)
            sum_l, sum_r = zeros, zeros
            for k in range(TOP_K):
                ga, gb = _unpack_halves(gathered[k])
                wk = jnp.broadcast_to(wts[:, k:k + 1, :], (GATHER_CHUNK, ROW_SUBLANES, LANES))
                sum_l = sum_l + ga * wk
                sum_r = sum_r + gb * wk
            return sum_l, sum_r

        n_chunks = tc // GATHER_CHUNK
        last_l, last_r = lax.fori_loop(0, n_chunks, body, (zeros, zeros))
        last_rows = pl.ds((n_chunks - 1) * GATHER_CHUNK, GATHER_CHUNK)
        accbuf[0, last_rows] = last_l
        accbuf[1, last_rows] = last_r

    @pl.when(i + 1 < n_steps)
    def _():
        sum_and_issue(True)

    @pl.when(i + 1 >= n_steps)
    def _():
        sum_and_issue(False)

    ua, ub = _unpack_halves(u_ref[...])
    half = ua.shape[1]
    hgu = _dot(jnp.concatenate([ua.astype(BF16), ub.astype(BF16)], axis=1), wsgu_ref[...])
    dsh = wsd_ref.shape[0]
    hg = hgu[:, :dsh]
    shared = _dot((hg * _sigmoid(hg) * hgu[:, dsh:]).astype(BF16), wsd_ref[...])
    acc_l = shared[:, :half]
    acc_r = shared[:, half:]
    acc_l = jnp.concatenate([acc_l[:, j * LANES:(j + 1) * LANES] + accbuf[0, :, j, :] for j in range(ROW_SUBLANES)],
                            axis=1)
    acc_r = jnp.concatenate([acc_r[:, j * LANES:(j + 1) * LANES] + accbuf[1, :, j, :] for j in range(ROW_SUBLANES)],
                            axis=1)
    m = mod_ref[0]
    xo_l = x1_ref[:, :half] + m[5:6, :half] * acc_l
    xo_r = x1_ref[:, half:] + m[5:6, half:] * acc_r
    ms = (jnp.sum(xo_l * xo_l, axis=-1, keepdims=True) + jnp.sum(xo_r * xo_r, axis=-1, keepdims=True)) / (2 * half)
    inv = lax.rsqrt(ms + EPS)
    o_ref[:, :half] = xo_l * inv * gfin_ref[:, :half]
    o_ref[:, half:] = xo_r * inv * gfin_ref[:, half:]


def _combine(dest, ysorted, wts_t, x1, u2p, mods3, wsgu, wsd, gfin, *, seq):
    t, d = x1.shape
    dh = d // 2
    tc = DISPATCH_TILE
    tiles_per_batch = seq // tc
    n_steps = t // tc
    kern = functools.partial(_combine_kernel, tc=tc, n_steps=n_steps)
    row = lambda i: (i, 0)
    const = lambda i: (0, 0)
    return pl.pallas_call(
        kern,
        out_shape=jax.ShapeDtypeStruct((t, d), F32),
        grid=(n_steps,),
        in_specs=[pl.BlockSpec((TOP_K, tc), lambda i: (0, i), memory_space=pltpu.SMEM),
                  pl.BlockSpec((TOP_K, tc), lambda i: (0, jnp.minimum(i + 1, n_steps - 1)), memory_space=pltpu.SMEM),
                  pl.BlockSpec(memory_space=pl.ANY), pl.BlockSpec((tc, TOP_K, LANES), lambda i: (i, 0, 0)),
                  pl.BlockSpec((tc, d), row), pl.BlockSpec((tc, dh), row),
                  pl.BlockSpec((1, 6, d), lambda i: (i // tiles_per_batch, 0, 0)),
                  pl.BlockSpec(wsgu.shape, const), pl.BlockSpec(wsd.shape, const), pl.BlockSpec((1, d), const)],
        out_specs=pl.BlockSpec((tc, d), row),
        scratch_shapes=[pltpu.VMEM((2, TOP_K, tc, ROW_SUBLANES, LANES), U32),
                        pltpu.VMEM((2, tc, ROW_SUBLANES, LANES), F32), pltpu.SemaphoreType.DMA((2,))],
        compiler_params=pltpu.CompilerParams(dimension_semantics=("arbitrary",), vmem_limit_bytes=VMEM_LIMIT),
        name="combine",
    )(dest, dest, ysorted, wts_t, x1, u2p, mods3, wsgu, wsd, gfin)


def kernel(x, c, ctx, c_ctx, ada_w, ada_b, norm_mix_g, norm_ffn_g, w_in, conv_xbc_w, conv_xbc_b, ssd_dt_bias, ssd_a_log, ssd_d, ssd_norm_g, conv_qk_w, conv_qk_b, mlstm_i_bias, mlstm_f_bias, mlstm_norm_g, w_ssd_out, w_mlstm_out, w_out, router_w, router_bias, moe_w_gate, moe_w_up, moe_w_down, shared_w_gate, shared_w_up, shared_w_down, norm_final_g):
    batch, seq, d = x.shape
    ctx_len = ctx.shape[1]
    depth = ada_w.shape[0]
    assert depth == 1, "only the single-layer configuration is implemented"
    assert seq % CHUNK == 0 and ctx_len % CHUNK == 0 and seq % GRID_W == 0
    l = 0
    n_sh = d // SSD_HEAD_DIM
    n_mh = d // MLSTM_DV
    dk = MLSTM_DV // 2
    d_xbc = d + 2 * SSD_GROUPS * SSD_STATE
    d_qk = 2 * n_mh * dk
    sizes = (d, d_xbc, 2 * n_sh, d_qk, d, 4 * n_mh, d, 2 * d)
    offs = np.concatenate([[0], np.cumsum(sizes)])
    assert offs[-1] == w_in.shape[2] and 2 * n_sh + 2 * n_mh <= LANES

    cond = jnp.concatenate([c, c_ctx[None], jnp.zeros((SUBLANES - (batch + 1) % SUBLANES, d), F32)], axis=0)
    mods = _adaln(cond, ada_w[l], ada_b[l])
    mods3 = mods.reshape(mods.shape[0], 6, d)

    w = w_in[l]
    seg = lambda k: w[:, offs[k]:offs[k + 1]]
    wbig = jnp.concatenate([seg(0), seg(1), seg(3), seg(4), seg(6), seg(7)], axis=1).astype(BF16)
    w_dt = seg(2)
    w_g = seg(5).reshape(d, 2, 2, n_mh)
    w_i = w_g[:, :, 0].reshape(d, 2 * n_mh)
    w_f = w_g[:, :, 1].reshape(d, 2 * n_mh)
    pad = jnp.zeros((d, LANES - 2 * n_sh - 2 * n_mh), F32)
    wsm = jnp.concatenate([w_dt, w_f, pad, w_dt, w_i, pad], axis=1)
    wsh = wsm.astype(BF16)
    wsl = (wsm - wsh.astype(F32)).astype(BF16)
    padb = jnp.zeros((LANES - 2 * n_sh - 2 * n_mh,), F32)
    dtb = ssd_dt_bias[l].reshape(-1).astype(F32)
    smb = jnp.concatenate([dtb, mlstm_f_bias[l].reshape(-1).astype(F32), padb,
                           dtb, mlstm_i_bias[l].reshape(-1).astype(F32), padb]).reshape(1, 2 * LANES)
    aneg = jnp.concatenate([-jnp.exp(ssd_a_log[l].astype(F32)).reshape(-1),
                            jnp.zeros((LANES - 2 * n_sh,), F32)]).reshape(1, LANES)

    x2 = x.reshape(batch * seq, d)
    ctx2 = ctx.reshape(batch * ctx_len, d)
    z, xs, bc, qk, v, og, mg, small, smallt = _inproj(
        x2, ctx2, mods3, norm_mix_g[l].reshape(1, d), wbig, wsh, wsl,
        conv_xbc_w[l], conv_xbc_b[l].reshape(1, d_xbc), conv_qk_w[l], conv_qk_b[l].reshape(1, d_qk), smb, aneg,
        batch=batch, seq=seq, ctx_len=ctx_len, dk=dk)

    ncc = ctx_len // CHUNK
    ncl = seq // CHUNK
    y0, y1, h0, h1 = _scans(xs, bc, qk, v, small, smallt, batch=batch, n_ctx_chunks=ncc, n_lat_chunks=ncl, dk=dk)

    dexp = jnp.repeat(ssd_d[l].astype(F32), SSD_HEAD_DIM).reshape(1, d)
    x1, u2 = _merge(y0, y1, xs, z, h0, h1, og, mg, x2, mods3, dexp, ssd_norm_g[l].reshape(1, d),
                    mlstm_norm_g[l].reshape(1, d), norm_ffn_g[l].reshape(1, d),
                    w_ssd_out[l].astype(BF16), w_mlstm_out[l].astype(BF16), w_out[l].astype(BF16),
                    batch=batch, seq=seq, n_ctx_tok=batch * ctx_len)

    n_experts = router_w.shape[2]
    idx, pos, wts, cnt = _router(u2, router_w[l].T.astype(BF16), router_bias[l].astype(F32).reshape(n_experts, 1))
    counts = cnt[:, 0].astype(I32)
    padded = (counts + MOE_BLOCK - 1) // MOE_BLOCK * MOE_BLOCK
    pad_end = jnp.cumsum(padded)
    pad_start = (pad_end - padded).astype(I32)
    t = batch * seq
    nb = t * TOP_K // MOE_BLOCK + n_experts
    n_used = (pad_end[-1] // MOE_BLOCK).astype(I32).reshape(1)

    dest = _slots(idx, pos, pad_start.astype(F32).reshape(n_experts, 1))
    xsorted = _dispatch(pad_start, counts, dest, u2, nb * MOE_BLOCK)
    ysorted = _experts(pad_start // MOE_BLOCK, (padded // MOE_BLOCK).astype(I32), n_used, xsorted,
                       moe_w_gate[l], moe_w_up[l], moe_w_down[l])
    wts_lanes = jnp.broadcast_to(wts.T[:, :, None], (t, TOP_K, LANES))
    out = _combine(dest, ysorted, wts_lanes, x1, u2, mods3,
                   jnp.concatenate([shared_w_gate[l], shared_w_up[l]], axis=1).astype(BF16), shared_w_down[l].astype(BF16),
                   norm_final_g.reshape(1, d), seq=seq)
    return out.reshape(batch, seq, d)
```

```python
import functools

import numpy as np
import jax
import jax.numpy as jnp
from jax import lax
from jax.experimental import pallas as pl
from jax.experimental.pallas import tpu as pltpu

F32 = jnp.float32
BF16 = jnp.bfloat16
I32 = jnp.int32
U32 = jnp.uint32

EPS = 1e-6
CHUNK = 128
CONV_K = 5
GRID_W = 64
SSD_HEAD_DIM = 64
SSD_STATE = 128
SSD_GROUPS = 2
MLSTM_DV = 128
N_EXPERT_GROUPS = 8
TOPK_GROUPS = 4
TOP_K = 8
ROUTED_SCALE = 2.5

LANES = 128
SUBLANES = 8
PACK_ROWS = 16
TOKEN_TILE = 512
COL_CHUNK = 512
ADALN_COLS = 1536
CONV_MASK_COLS = 8
INPROJ_PARTS = 2
MOE_BLOCK = 512
DISPATCH_TILE = 256
GATHER_CHUNK = 4
X_BUFFERS = 4
ROW_SUBLANES = 4
VMEM_LIMIT = 56 * 1024 * 1024
NEG_INF = float("-inf")


def _dot(a, b):
    return jnp.dot(a, b, preferred_element_type=F32)


def _dot_nt(a, b):
    return lax.dot_general(a, b, (((1,), (1,)), ((), ())), preferred_element_type=F32)


def _dot_tn(a, b):
    return lax.dot_general(a, b, (((0,), (0,)), ((), ())), preferred_element_type=F32)


def _spread(parts, e):
    res = _dot(jnp.concatenate(parts, axis=0).astype(BF16), e)
    out, r0 = [], 0
    for p in parts:
        out.append(res[r0:r0 + p.shape[0]])
        r0 += p.shape[0]
    return out


def _rows16(row):
    r8 = jnp.broadcast_to(row, (SUBLANES, row.shape[1]))
    return jnp.concatenate([r8, r8], axis=0)


def _sigmoid(v):
    return 1.0 / (1.0 + jnp.exp(-v))


def _pack_halves(v):
    n = v.shape[1] // 2
    hi = lax.bitcast_convert_type(v[:, :n].astype(BF16).astype(F32), U32)
    lo = lax.bitcast_convert_type(v[:, n:].astype(BF16).astype(F32), U32)
    return hi | (lo >> 16)


def _unpack_halves(p):
    left = lax.bitcast_convert_type(p & jnp.uint32(0xFFFF0000), F32)
    right = lax.bitcast_convert_type(p << 16, F32)
    return left, right


def _softplus(v):
    return jnp.maximum(v, 0.0) + jnp.log1p(jnp.exp(-jnp.abs(v)))


def _adaln_kernel(c_ref, w_ref, b_ref, o_ref):
    c = c_ref[...]
    s = c * _sigmoid(c)
    w = w_ref[...]
    s_hi = s.astype(BF16)
    s_lo = (s - s_hi.astype(F32)).astype(BF16)
    w_hi = w.astype(BF16)
    w_lo = (w - w_hi.astype(F32)).astype(BF16)
    o_ref[...] = _dot(s_hi, w_hi) + _dot(s_lo, w_hi) + _dot(s_hi, w_lo) + b_ref[...]


def _adaln(cond, w, b):
    rows, d = cond.shape
    n = w.shape[1]
    tn = ADALN_COLS if n % ADALN_COLS == 0 else n
    return pl.pallas_call(
        _adaln_kernel,
        out_shape=jax.ShapeDtypeStruct((rows, n), F32),
        grid=(n // tn,),
        in_specs=[pl.BlockSpec((rows, d), lambda j: (0, 0)),
                  pl.BlockSpec((d, tn), lambda j: (0, j)),
                  pl.BlockSpec((1, tn), lambda j: (0, j))],
        out_specs=pl.BlockSpec((rows, tn), lambda j: (0, j)),
        compiler_params=pltpu.CompilerParams(dimension_semantics=("arbitrary",), vmem_limit_bytes=VMEM_LIMIT),
        name="adaln",
    )(cond, w, b.reshape(1, n))


CONV_SHIFTS = tuple(j - CONV_K // 2 for j in range(CONV_K) if j != CONV_K // 2)


def _conv_masks(tm, seg_len):
    assert len(CONV_SHIFTS) <= CONV_MASK_COLS
    pos = np.arange(tm) % seg_len
    m = np.zeros((tm, CONV_MASK_COLS), np.float32)
    for i, s in enumerate(CONV_SHIFTS):
        m[:, i] = ((pos + s >= 0) & (pos + s < seg_len)).astype(np.float32)
    return m


def _conv_silu(acc, w5, bias, vm, tm):
    out = acc * w5[CONV_K // 2:CONV_K // 2 + 1] + bias
    for i, s in enumerate(CONV_SHIFTS):
        shifted = pltpu.roll(acc, (-s) % tm, axis=0)
        j = s + CONV_K // 2
        out = out + (shifted * vm[:, i:i + 1]) * w5[j:j + 1]
    return out * _sigmoid(out)


def _inproj_kernel(x_ref, ctx_ref, mod_ref, g_ref, wbig_ref, wsh_ref, wsl_ref, cwx_ref, cbx_ref, cwq_ref, cbq_ref,
                   smb_ref, aneg_ref, tril_ref, triu_ref, cmask_ref,
                   z_ref, xs_ref, bc_ref, qk_ref, v_ref, og_ref, mg_ref, small_ref, smallt_ref,
                   *, n_ctx_tiles, tm, d_model, dk):
    i = pl.program_id(0)
    is_ctx = i < n_ctx_tiles
    xt = jnp.where(is_ctx, ctx_ref[...], x_ref[...])
    m = mod_ref[0]
    ms = jnp.mean(xt * xt, axis=-1, keepdims=True)
    u = xt * lax.rsqrt(ms + EPS) * g_ref[...] * (1.0 + m[1:2]) + m[0:1]
    u_hi = u.astype(BF16)
    u_lo = (u - u_hi.astype(F32)).astype(BF16)
    vm = jnp.where(is_ctx, cmask_ref[1], cmask_ref[0])

    d = d_model
    d_bc = 2 * SSD_GROUPS * SSD_STATE
    pieces = [(z_ref, d, "plain", None), (xs_ref, d, "convx", 0), (bc_ref, d_bc, "convx", d),
              (qk_ref, d, "convq", 0), (v_ref, d, "plain", None), (og_ref, d, "sigmoid", None),
              (mg_ref, 2 * d, "sigmoid", None)]
    col = 0
    rp = tm // INPROJ_PARTS
    row_parts = [slice(p * rp, (p + 1) * rp) for p in range(INPROJ_PARTS)]
    u_parts = [u_hi[rows] for rows in row_parts]
    vm_parts = [vm[rows] for rows in row_parts]
    for ref, width, kind, coff in pieces:
        for c0 in range(0, width, COL_CHUNK):
            accs = [_dot(up, wbig_ref[:, col + c0:col + c0 + COL_CHUNK]) for up in u_parts]
            for rows, vmp, acc in zip(row_parts, vm_parts, accs):
                if kind == "convx":
                    cs = coff + c0
                    acc = _conv_silu(acc, cwx_ref[:, cs:cs + COL_CHUNK], cbx_ref[:, cs:cs + COL_CHUNK], vmp, rp)
                elif kind == "convq":
                    acc = _conv_silu(acc, cwq_ref[:, c0:c0 + COL_CHUNK], cbq_ref[:, c0:c0 + COL_CHUNK], vmp, rp)
                    if c0 < width // 2:
                        acc = acc * (dk ** -0.5)
                elif kind == "sigmoid":
                    acc = _sigmoid(acc)
                ref[rows, c0:c0 + COL_CHUNK] = acc.astype(ref.dtype)
        col += width

    wsh = wsh_ref[...]
    raw = _dot(u_hi, wsh) + _dot(u_lo, wsh) + _dot(u_hi, wsl_ref[...]) + smb_ref[...]
    p1 = raw[:, :LANES]
    p2 = raw[:, LANES:]
    lane = lax.broadcasted_iota(I32, (tm, LANES), 1)
    n_dt = 2 * (d_model // SSD_HEAD_DIM)
    n_g = 2 * (d_model // MLSTM_DV)
    is_dt = lane < n_dt
    is_gate = jnp.logical_and(lane >= n_dt, lane < n_dt + n_g)
    dt = _softplus(p2)
    pa = jnp.where(is_dt, dt * aneg_ref[...], jnp.where(is_gate, -_softplus(-p1), 0.0))
    pb = jnp.where(is_dt, dt, jnp.where(is_gate, p2, 0.0))
    lane_c = lax.broadcasted_iota(I32, (CHUNK, LANES), 1)
    is_dt_c = lane_c < n_dt
    rev = jnp.logical_or(jnp.logical_and(lane_c >= n_dt // 2, lane_c < n_dt),
                         jnp.logical_and(lane_c >= n_dt + n_g // 2, lane_c < n_dt + n_g))
    tril = tril_ref[...]
    triu = triu_ref[...]
    tq = lax.broadcasted_iota(I32, (CHUNK, LANES), 0)
    chunks = range(tm // CHUNK)
    planes_a, planes_b = [], []
    for c in chunks:
        a_c = pa[c * CHUNK:(c + 1) * CHUNK]
        hi = a_c.astype(BF16)
        r1 = a_c - hi.astype(F32)
        mid = r1.astype(BF16)
        lo = (r1 - mid.astype(F32)).astype(BF16)
        cs_f = _dot(tril, hi) + _dot(tril, mid) + _dot(tril, lo)
        cs_b = _dot(triu, hi) + _dot(triu, mid) + _dot(triu, lo)
        planes_a.append(jnp.where(rev, cs_b, cs_f))
    for c in chunks:
        pb_c = pb[c * CHUNK:(c + 1) * CHUNK]
        planes_b.append(jnp.where(is_dt_c, pb_c, pb_c - planes_a[c]))
    yfs = list(planes_b)
    ybs = list(planes_b)
    s = 1
    while s < CHUNK:
        for c in chunks:
            sh = pltpu.roll(yfs[c], s, axis=0)
            yfs[c] = jnp.maximum(yfs[c], jnp.where(tq >= s, sh, NEG_INF))
            sh = pltpu.roll(ybs[c], CHUNK - s, axis=0)
            ybs[c] = jnp.maximum(ybs[c], jnp.where(tq + s < CHUNK, sh, NEG_INF))
        s *= 2
    for c in chunks:
        r0 = c * CHUNK
        small_ref[r0:r0 + CHUNK, 0:LANES] = planes_a[c]
        small_ref[r0:r0 + CHUNK, LANES:2 * LANES] = planes_b[c]
        small_ref[r0:r0 + CHUNK, 2 * LANES:3 * LANES] = jnp.where(rev, ybs[c], yfs[c])
        smallt_ref[c, 0] = planes_a[c].T
        smallt_ref[c, 1] = planes_b[c].T


def _inproj(x2, ctx2, mods3, g, wbig, wsh, wsl, cwx, cbx, cwq, cbq, smb, aneg, *, batch, seq, ctx_len, dk):
    d = x2.shape[1]
    tm = TOKEN_TILE
    n_ctx_tok = batch * ctx_len
    rp = tm // INPROJ_PARTS
    assert n_ctx_tok % tm == 0 and seq % tm == 0 and rp % ctx_len == 0 and rp % GRID_W == 0
    n_ctx_tiles = n_ctx_tok // tm
    tiles_per_batch = seq // tm
    n_tiles = n_ctx_tiles + batch * tiles_per_batch
    t_all = n_tiles * tm
    n_big = wbig.shape[1]
    tril = jnp.asarray(np.tril(np.ones((CHUNK, CHUNK), np.float32)), BF16)
    triu = jnp.asarray(np.triu(np.ones((CHUNK, CHUNK), np.float32)), BF16)

    def x_map(i):
        return (jnp.maximum(i - n_ctx_tiles, 0), 0)

    def ctx_map(i):
        return (jnp.minimum(i, n_ctx_tiles - 1), 0)

    def mod_map(i):
        return (jnp.where(i < n_ctx_tiles, batch, jnp.maximum(i - n_ctx_tiles, 0) // tiles_per_batch), 0, 0)

    const = lambda i: (0, 0)
    row = lambda i: (i, 0)
    cmask = jnp.asarray(np.stack([_conv_masks(tm, GRID_W), _conv_masks(tm, ctx_len)]))
    kern = functools.partial(_inproj_kernel, n_ctx_tiles=n_ctx_tiles, tm=tm, d_model=d, dk=dk)
    d_bc = 2 * SSD_GROUPS * SSD_STATE
    outs = [jax.ShapeDtypeStruct((t_all, d), BF16), jax.ShapeDtypeStruct((t_all, d), BF16),
            jax.ShapeDtypeStruct((t_all, d_bc), BF16), jax.ShapeDtypeStruct((t_all, d), BF16),
            jax.ShapeDtypeStruct((t_all, d), BF16), jax.ShapeDtypeStruct((t_all, d), BF16),
            jax.ShapeDtypeStruct((t_all, 2 * d), BF16), jax.ShapeDtypeStruct((t_all, 3 * LANES), F32),
            jax.ShapeDtypeStruct((t_all // CHUNK, 2, LANES, CHUNK), F32)]
    out_specs = [pl.BlockSpec((tm, d), row), pl.BlockSpec((tm, d), row), pl.BlockSpec((tm, d_bc), row),
                 pl.BlockSpec((tm, d), row), pl.BlockSpec((tm, d), row), pl.BlockSpec((tm, d), row),
                 pl.BlockSpec((tm, 2 * d), row), pl.BlockSpec((tm, 3 * LANES), row),
                 pl.BlockSpec((tm // CHUNK, 2, LANES, CHUNK), lambda i: (i, 0, 0, 0))]
    in_specs = [pl.BlockSpec((tm, d), x_map), pl.BlockSpec((tm, d), ctx_map),
                pl.BlockSpec((1, 6, d), mod_map), pl.BlockSpec((1, d), const),
                pl.BlockSpec((d, n_big), const, pipeline_mode=pl.Buffered(1)),
                pl.BlockSpec((d, 2 * LANES), const), pl.BlockSpec((d, 2 * LANES), const),
                pl.BlockSpec(cwx.shape, const), pl.BlockSpec(cbx.shape, const),
                pl.BlockSpec(cwq.shape, const), pl.BlockSpec(cbq.shape, const),
                pl.BlockSpec((1, 2 * LANES), const), pl.BlockSpec((1, LANES), const),
                pl.BlockSpec((CHUNK, CHUNK), const), pl.BlockSpec((CHUNK, CHUNK), const),
                pl.BlockSpec((2, tm, CONV_MASK_COLS), lambda i: (0, 0, 0))]
    return pl.pallas_call(
        kern, out_shape=outs, grid=(n_tiles,), in_specs=in_specs, out_specs=out_specs,
        compiler_params=pltpu.CompilerParams(dimension_semantics=("arbitrary",), vmem_limit_bytes=VMEM_LIMIT),
        name="inproj",
    )(x2, ctx2, mods3, g, wbig, wsh, wsl, cwx, cbx, cwq, cbq, smb, aneg, tril, triu, cmask)


def _chunk_block_map(direction, batch, n_ctx_chunks, n_lat_chunks):
    def idx(b, s):
        if direction == 0:
            c_ctx = s
            c_lat = s - n_ctx_chunks
        else:
            c_ctx = n_ctx_chunks - 1 - s
            c_lat = n_lat_chunks - 1 - (s - n_ctx_chunks)
        return jnp.where(s < n_ctx_chunks, b * n_ctx_chunks + c_ctx, batch * n_ctx_chunks + b * n_lat_chunks + c_lat)
    return idx


def _ssd_kernel(xs_ref, bc_ref, small_ref, smallt_ref, e16_ref, y_ref, st_ref, *, direction, n_heads):
    s = pl.program_id(1)

    @pl.when(s == 0)
    def _():
        st_ref[...] = jnp.zeros_like(st_ref)

    hpg = n_heads // SSD_GROUPS
    lane0 = direction * n_heads
    last = CHUNK - 1 if direction == 0 else 0
    lane = lax.broadcasted_iota(I32, (CHUNK, LANES), 1)
    lm = jnp.logical_and(lane >= lane0, lane < lane0 + n_heads)
    plane_a = jnp.where(lm, small_ref[:, 0:LANES], 0.0)
    plane_b = jnp.where(lm, small_ref[:, LANES:2 * LANES], 0.0)
    cum_t = smallt_ref[0, 0]
    dt_t = smallt_ref[0, 1]
    tot = plane_a[last:last + 1]
    lane16 = lax.broadcasted_iota(I32, (PACK_ROWS, LANES), 1)
    lm16 = jnp.logical_and(lane16 >= lane0, lane16 < lane0 + n_heads)
    dec_rows = jnp.where(lm16, jnp.exp(_rows16(tot)), 0.0)
    ecx, wx, decx = _spread(
        [jnp.where(lm, jnp.exp(plane_a), 0.0), plane_b * jnp.exp(tot - plane_a), dec_rows], e16_ref[...])
    decx = decx[0:1]

    xw = (xs_ref[...].astype(F32) * wx).astype(BF16)
    plane64 = lax.broadcasted_iota(I32, (CHUNK, 2 * SSD_HEAD_DIM), 1) < SSD_HEAD_DIM
    iq = lax.broadcasted_iota(I32, (CHUNK, CHUNK), 0)
    ik = lax.broadcasted_iota(I32, (CHUNK, CHUNK), 1)
    mask = (iq >= ik) if direction == 0 else (iq <= ik)
    gw = hpg * SSD_HEAD_DIM
    groups = range(SSD_GROUPS)
    bgs = [bc_ref[:, g * SSD_STATE:(g + 1) * SSD_STATE] for g in groups]
    cgs = [bc_ref[:, (SSD_GROUPS + g) * SSD_STATE:(SSD_GROUPS + g + 1) * SSD_STATE] for g in groups]
    sgs = [st_ref[:, g * gw:(g + 1) * gw] for g in groups]
    cbs = [_dot_nt(cgs[g], bgs[g]) for g in groups]
    y_inters = [_dot(cgs[g], sgs[g].astype(BF16)) * ecx[:, g * gw:(g + 1) * gw] for g in groups]
    m_hs = []
    for h in range(n_heads):
        cum_q = jnp.broadcast_to(plane_a[:, lane0 + h:lane0 + h + 1], (CHUNK, CHUNK))
        seg = cum_q - cum_t[lane0 + h:lane0 + h + 1, :]
        m_hs.append((cbs[h // hpg] * jnp.exp(jnp.where(mask, seg, NEG_INF))
                     * dt_t[lane0 + h:lane0 + h + 1, :]).astype(BF16))
    keep_lo = plane64.astype(BF16)
    keep_hi = jnp.logical_not(plane64).astype(BF16)
    for pair in range(n_heads // 2):
        g = (2 * pair) // hpg
        c0 = 2 * pair * SSD_HEAD_DIM
        xpair = xs_ref[:, c0:c0 + 2 * SSD_HEAD_DIM]
        acc = y_inters[g][:, c0 - g * gw:c0 - g * gw + 2 * SSD_HEAD_DIM]
        for par in range(2):
            acc = acc + _dot(m_hs[2 * pair + par], xpair * (keep_lo if par == 0 else keep_hi))
        y_ref[:, c0:c0 + 2 * SSD_HEAD_DIM] = acc.astype(y_ref.dtype)
    for g in groups:
        st_ref[:, g * gw:(g + 1) * gw] = (decx[:, g * gw:(g + 1) * gw] * sgs[g]
                                          + _dot_tn(bgs[g], xw[:, g * gw:(g + 1) * gw]))


def _selectors(direction, d):
    n_sh = d // SSD_HEAD_DIM
    n_mh = d // MLSTM_DV
    e16 = np.zeros((LANES, d), np.float32)
    em = np.zeros((LANES, d), np.float32)
    for h in range(n_sh):
        e16[direction * n_sh + h, h * SSD_HEAD_DIM:(h + 1) * SSD_HEAD_DIM] = 1.0
    for h in range(n_mh):
        em[2 * n_sh + direction * n_mh + h, h * MLSTM_DV:(h + 1) * MLSTM_DV] = 1.0
    return jnp.asarray(e16, BF16), jnp.asarray(em, BF16)


def _mlstm_kernel(qk_ref, v_ref, small_ref, smallt_ref, em_ref, h_ref, cn_ref, m_ref, *, direction, n_heads, dk):
    s = pl.program_id(1)

    @pl.when(s == 0)
    def _():
        cn_ref[...] = jnp.zeros_like(cn_ref)
        m_ref[...] = jnp.zeros_like(m_ref)

    n_dt = 2 * (n_heads * MLSTM_DV // SSD_HEAD_DIM)
    lane0 = n_dt + direction * n_heads
    last = CHUNK - 1 if direction == 0 else 0
    lane = lax.broadcasted_iota(I32, (CHUNK, LANES), 1)
    lm = jnp.logical_and(lane >= lane0, lane < lane0 + n_heads)
    b_q = jnp.where(lm, small_ref[:, 0:LANES], 0.0)
    r_k = jnp.where(lm, small_ref[:, LANES:2 * LANES], 0.0)
    cmr = jnp.where(lm, small_ref[:, 2 * LANES:3 * LANES], 0.0)
    r_t = smallt_ref[0, 1]
    m_all = m_ref[...]
    m_row = m_all[0:1]
    mm = jnp.maximum(cmr, m_row)
    w_state = jnp.exp(m_row - mm)
    e_mq = jnp.exp(-(b_q + mm))
    m_base8 = jnp.maximum(m_all, _rows16(cmr[last:last + 1]))
    m_base = m_base8[0:1]
    w_k = jnp.where(lm, jnp.exp(r_k - m_base), 0.0)
    lane16 = lax.broadcasted_iota(I32, (PACK_ROWS, LANES), 1)
    lm16 = jnp.logical_and(lane16 >= lane0, lane16 < lane0 + n_heads)
    dec_rows = jnp.where(lm16, jnp.exp(m_all - m_base8), 0.0)
    wsx, emqx, wkx, decx = _spread(
        [jnp.where(lm, w_state, 0.0), jnp.where(lm, e_mq, 0.0), w_k, dec_rows], em_ref[...])
    decx = decx[0:1]
    m_ref[...] = jnp.where(lm16, _rows16(b_q[last:last + 1]) + m_base8, 0.0)

    iq = lax.broadcasted_iota(I32, (CHUNK, CHUNK), 0)
    ik = lax.broadcasted_iota(I32, (CHUNK, CHUNK), 1)
    mask = (iq >= ik) if direction == 0 else (iq <= ik)
    ones = jnp.ones((CHUNK, MLSTM_DV), BF16)
    d_qk = n_heads * dk
    heads = range(n_heads)
    hsl = [slice(h * MLSTM_DV, (h + 1) * MLSTM_DV) for h in heads]
    qs = [qk_ref[:, h * dk:(h + 1) * dk] for h in heads]
    ks = [qk_ref[:, d_qk + h * dk:d_qk + (h + 1) * dk] for h in heads]
    vs = [v_ref[:, hsl[h]] for h in heads]
    cns = [cn_ref[h] for h in heads]
    scores = [_dot_nt(qs[h], ks[h]) for h in heads]
    inter = [_dot(qs[h], cns[h].astype(BF16)) for h in heads]
    smats = []
    for h in heads:
        mm_q = jnp.broadcast_to(mm[:, lane0 + h:lane0 + h + 1], (CHUNK, CHUNK))
        dmat = jnp.exp(jnp.where(mask, r_t[lane0 + h:lane0 + h + 1, :] - mm_q, NEG_INF))
        smats.append((scores[h] * dmat).astype(BF16))
    tots = [_dot(smats[h], jnp.concatenate([vs[h], ones], axis=1)) for h in heads]
    for h in heads:
        hs = hsl[h]
        kh, vh, cn = ks[h], vs[h], cns[h]
        wsh = wsx[:, hs]
        tot = tots[h] + jnp.concatenate([wsh, wsh], axis=1) * inter[h]
        num = tot[:, :MLSTM_DV]
        den = tot[:, MLSTM_DV:]
        h_ref[:, hs] = (num / jnp.maximum(jnp.abs(den), emqx[:, hs])).astype(h_ref.dtype)
        wkh = wkx[:, hs]
        rhs = jnp.concatenate([(vh.astype(F32) * wkh).astype(BF16), wkh.astype(BF16)], axis=1)
        dech = decx[:, hs]
        cn_ref[h] = jnp.concatenate([dech, dech], axis=1) * cn + _dot_tn(kh, rhs)


def _scans_kernel(*refs, n_sh, n_mh, dk):
    (xs0, bc0, qk0, v0, sm0, smt0, xs1, bc1, qk1, v1, sm1, smt1, e16_0, em_0, e16_1, em_1,
     y0, y1, h0, h1, st0, st1, cn0, cn1, m0, m1) = refs
    _ssd_kernel(xs0, bc0, sm0, smt0, e16_0, y0, st0, direction=0, n_heads=n_sh)
    _ssd_kernel(xs1, bc1, sm1, smt1, e16_1, y1, st1, direction=1, n_heads=n_sh)
    _mlstm_kernel(qk0, v0, sm0, smt0, em_0, h0, cn0, m0, direction=0, n_heads=n_mh, dk=dk)
    _mlstm_kernel(qk1, v1, sm1, smt1, em_1, h1, cn1, m1, direction=1, n_heads=n_mh, dk=dk)


def _scans(xs, bc, qk, v, small, smallt, *, batch, n_ctx_chunks, n_lat_chunks, dk):
    t_all, d = xs.shape
    n_sh = d // SSD_HEAD_DIM
    n_mh = d // MLSTM_DV
    n_steps = n_ctx_chunks + n_lat_chunks
    in_specs, args = [], []
    for direction in range(2):
        idx = _chunk_block_map(direction, batch, n_ctx_chunks, n_lat_chunks)
        rows = lambda b, s, idx=idx: (idx(b, s), 0)
        in_specs += [pl.BlockSpec((CHUNK, d), rows), pl.BlockSpec((CHUNK, bc.shape[1]), rows),
                     pl.BlockSpec((CHUNK, qk.shape[1]), rows), pl.BlockSpec((CHUNK, d), rows),
                     pl.BlockSpec((CHUNK, 3 * LANES), rows),
                     pl.BlockSpec((1, 2, LANES, CHUNK), lambda b, s, idx=idx: (idx(b, s), 0, 0, 0))]
        args += [xs, bc, qk, v, small, smallt]
    out_specs = []
    for direction in (0, 1, 0, 1):
        idx = _chunk_block_map(direction, batch, n_ctx_chunks, n_lat_chunks)
        out_specs.append(pl.BlockSpec((CHUNK, d), lambda b, s, idx=idx: (idx(b, s), 0)))
    for direction in range(2):
        sel = _selectors(direction, d)
        in_specs += [pl.BlockSpec(a.shape, lambda b, s: (0, 0)) for a in sel]
        args += list(sel)
    state = [pltpu.VMEM((SSD_STATE, d), F32)] * 2 + [pltpu.VMEM((n_mh, dk, 2 * MLSTM_DV), F32)] * 2 \
        + [pltpu.VMEM((PACK_ROWS, LANES), F32)] * 2
    return pl.pallas_call(
        functools.partial(_scans_kernel, n_sh=n_sh, n_mh=n_mh, dk=dk),
        out_shape=[jax.ShapeDtypeStruct((t_all, d), BF16)] * 4,
        grid=(batch, n_steps),
        in_specs=in_specs, out_specs=out_specs, scratch_shapes=state,
        compiler_params=pltpu.CompilerParams(dimension_semantics=("arbitrary", "arbitrary"),
                                             vmem_limit_bytes=VMEM_LIMIT),
        name="scans",
    )(*args)


def _merge_kernel(y0_ref, y1_ref, xs_ref, z_ref, h0_ref, h1_ref, og_ref, mg_ref, x_ref, mod_ref, dexp_ref, gs_ref,
                  gm_ref, gf_ref, wso_ref, wmo_ref, wo_ref, x1_ref, u2_ref, *, d_model):
    d = d_model
    m = mod_ref[0]
    y = y0_ref[...].astype(F32) + y1_ref[...].astype(F32) + dexp_ref[...] * xs_ref[...].astype(F32)
    zf = z_ref[...].astype(F32)
    y = y * (zf * _sigmoid(zf))
    y = y * lax.rsqrt(jnp.mean(y * y, axis=-1, keepdims=True) + EPS) * gs_ref[...]
    a = _dot(y.astype(BF16), wso_ref[...])
    hm = h0_ref[...].astype(F32) + h1_ref[...].astype(F32)
    blocks = []
    for h in range(d // MLSTM_DV):
        blk = hm[:, h * MLSTM_DV:(h + 1) * MLSTM_DV]
        blocks.append(blk * lax.rsqrt(jnp.mean(blk * blk, axis=-1, keepdims=True) + EPS))
    hn = jnp.concatenate(blocks, axis=1) * gm_ref[...]
    bm = _dot((og_ref[...].astype(F32) * hn).astype(BF16), wmo_ref[...])
    merged = mg_ref[:, :d].astype(F32) * a + mg_ref[:, d:].astype(F32) * bm
    r = _dot(merged.astype(BF16), wo_ref[...])
    x1 = x_ref[...] + m[2:3] * r
    x1_ref[...] = x1
    u2 = x1 * lax.rsqrt(jnp.mean(x1 * x1, axis=-1, keepdims=True) + EPS) * gf_ref[...] * (1.0 + m[4:5]) + m[3:4]
    u2_ref[...] = _pack_halves(u2)


def _merge(y0, y1, xs, z, h0, h1, og, mg, x2, mods3, dexp, gs, gm, gf, wso, wmo, wo, *, batch, seq, n_ctx_tok):
    t, d = x2.shape
    tm = TOKEN_TILE
    off = n_ctx_tok // tm
    tiles_per_batch = seq // tm
    lat = lambda i: (i + off, 0)
    row = lambda i: (i, 0)
    const = lambda i: (0, 0)
    kern = functools.partial(_merge_kernel, d_model=d)
    wspec = pl.BlockSpec((d, d), const, pipeline_mode=pl.Buffered(1))
    return pl.pallas_call(
        kern,
        out_shape=[jax.ShapeDtypeStruct((t, d), F32), jax.ShapeDtypeStruct((t, d // 2), U32)],
        grid=(t // tm,),
        in_specs=[pl.BlockSpec((tm, d), lat), pl.BlockSpec((tm, d), lat), pl.BlockSpec((tm, d), lat),
                  pl.BlockSpec((tm, d), lat), pl.BlockSpec((tm, d), lat), pl.BlockSpec((tm, d), lat),
                  pl.BlockSpec((tm, d), lat), pl.BlockSpec((tm, 2 * d), lat), pl.BlockSpec((tm, d), row),
                  pl.BlockSpec((1, 6, d), lambda i: (i // tiles_per_batch, 0, 0)),
                  pl.BlockSpec((1, d), const), pl.BlockSpec((1, d), const), pl.BlockSpec((1, d), const),
                  pl.BlockSpec((1, d), const), wspec, wspec, wspec],
        out_specs=[pl.BlockSpec((tm, d), row), pl.BlockSpec((tm, d // 2), row)],
        compiler_params=pltpu.CompilerParams(dimension_semantics=("arbitrary",), vmem_limit_bytes=VMEM_LIMIT),
        name="merge",
    )(y0, y1, xs, z, h0, h1, og, mg, x2, mods3, dexp, gs, gm, gf, wso, wmo, wo)


def _first_index_of_max(vals, row_iota, n_rows):
    mx = jnp.max(vals, axis=0, keepdims=True)
    idx = jnp.min(jnp.where(vals == mx, row_iota, n_rows), axis=0, keepdims=True)
    return mx, idx


def _router_kernel(u_ref, wt_ref, bias_ref, su_ref, idx_ref, pos_ref, wts_ref, cnt_ref, run_ref, *, n_experts, tr):
    i = pl.program_id(0)

    @pl.when(i == 0)
    def _():
        run_ref[...] = jnp.zeros_like(run_ref)

    ua, ub = _unpack_halves(u_ref[...])
    urow = jnp.concatenate([ua.astype(BF16), ub.astype(BF16)], axis=1)
    scores = _sigmoid(_dot_nt(wt_ref[...], urow))
    biased = scores + bias_ref[...]
    gsz = n_experts // N_EXPERT_GROUPS
    gi = lax.broadcasted_iota(I32, (gsz, tr), 0).astype(F32)
    gscores = []
    for g in range(N_EXPERT_GROUPS):
        blk = biased[g * gsz:(g + 1) * gsz]
        m1, i1 = _first_index_of_max(blk, gi, gsz)
        m2 = jnp.max(jnp.where(gi == i1, NEG_INF, blk), axis=0, keepdims=True)
        gscores.append(m1 + m2)
    gs = jnp.concatenate(gscores, axis=0)
    g8 = lax.broadcasted_iota(I32, (N_EXPERT_GROUPS, tr), 0).astype(F32)
    gsel = jnp.zeros((N_EXPERT_GROUPS, tr), F32)
    for _ in range(TOPK_GROUPS):
        _, gidx = _first_index_of_max(gs, g8, N_EXPERT_GROUPS)
        hit = g8 == gidx
        gsel = jnp.where(hit, 1.0, gsel)
        gs = jnp.where(hit, NEG_INF, gs)
    cand = jnp.concatenate(
        [jnp.where(jnp.broadcast_to(gsel[g:g + 1], (gsz, tr)) > 0.5, biased[g * gsz:(g + 1) * gsz], NEG_INF)
         for g in range(N_EXPERT_GROUPS)], axis=0)
    ei = lax.broadcasted_iota(I32, (n_experts, tr), 0).astype(F32)
    sel = jnp.zeros((n_experts, tr), F32)
    idxs, ws = [], []
    for _ in range(TOP_K):
        _, eidx = _first_index_of_max(cand, ei, n_experts)
        hit = ei == eidx
        ws.append(jnp.sum(jnp.where(hit, scores, 0.0), axis=0, keepdims=True))
        idxs.append(eidx)
        sel = jnp.where(hit, 1.0, sel)
        cand = jnp.where(hit, NEG_INF, cand)
    wk = jnp.concatenate(ws, axis=0)
    wts_ref[...] = ROUTED_SCALE * wk / jnp.sum(wk, axis=0, keepdims=True)
    idx_ref[...] = jnp.concatenate(idxs, axis=0).astype(I32)
    selb = sel.astype(BF16)
    posmat = _dot(selb, su_ref[...]) + run_ref[:, 0:1]
    pos_ref[...] = jnp.concatenate(
        [jnp.sum(jnp.where(ei == idxs[k], posmat, 0.0), axis=0, keepdims=True) for k in range(TOP_K)],
        axis=0).astype(I32)
    run = run_ref[...] + _dot(selb, jnp.ones((tr, LANES), BF16))
    run_ref[...] = run
    cnt_ref[...] = run


def _router(u2p, router_wt, bias_col):
    t = u2p.shape[0]
    d = router_wt.shape[1]
    n_experts = router_wt.shape[0]
    tr = TOKEN_TILE
    su = jnp.asarray(np.triu(np.ones((tr, tr), np.float32), 1), BF16)
    kern = functools.partial(_router_kernel, n_experts=n_experts, tr=tr)
    col = lambda i: (0, i)
    const = lambda i: (0, 0)
    return pl.pallas_call(
        kern,
        out_shape=[jax.ShapeDtypeStruct((TOP_K, t), I32), jax.ShapeDtypeStruct((TOP_K, t), I32),
                   jax.ShapeDtypeStruct((TOP_K, t), F32), jax.ShapeDtypeStruct((n_experts, LANES), F32)],
        grid=(t // tr,),
        in_specs=[pl.BlockSpec((tr, d // 2), lambda i: (i, 0)), pl.BlockSpec((n_experts, d), const),
                  pl.BlockSpec((n_experts, 1), const), pl.BlockSpec((tr, tr), const)],
        out_specs=[pl.BlockSpec((TOP_K, tr), col), pl.BlockSpec((TOP_K, tr), col), pl.BlockSpec((TOP_K, tr), col),
                   pl.BlockSpec((n_experts, LANES), const)],
        scratch_shapes=[pltpu.VMEM((n_experts, LANES), F32)],
        compiler_params=pltpu.CompilerParams(dimension_semantics=("arbitrary",), vmem_limit_bytes=VMEM_LIMIT),
        name="router",
    )(u2p, router_wt, bias_col, su)


def _slots_kernel(idx_ref, pos_ref, pstart_ref, dest_ref, *, n_experts, tr):
    ei = lax.broadcasted_iota(I32, (n_experts, tr), 0).astype(F32)
    pstart = pstart_ref[...]
    idx = idx_ref[...].astype(F32)
    rows = [jnp.sum(jnp.where(ei == idx[k:k + 1], pstart, 0.0), axis=0, keepdims=True) for k in range(TOP_K)]
    dest_ref[...] = jnp.concatenate(rows, axis=0).astype(I32) + pos_ref[...]


def _slots(idx, pos, pstart_col):
    t = idx.shape[1]
    n_experts = pstart_col.shape[0]
    tr = TOKEN_TILE
    col = lambda i: (0, i)
    return pl.pallas_call(
        functools.partial(_slots_kernel, n_experts=n_experts, tr=tr),
        out_shape=jax.ShapeDtypeStruct((TOP_K, t), I32),
        grid=(t // tr,),
        in_specs=[pl.BlockSpec((TOP_K, tr), col), pl.BlockSpec((TOP_K, tr), col),
                  pl.BlockSpec((n_experts, 1), lambda i: (0, 0))],
        out_specs=pl.BlockSpec((TOP_K, tr), col),
        compiler_params=pltpu.CompilerParams(dimension_semantics=("arbitrary",), vmem_limit_bytes=VMEM_LIMIT),
        name="slots",
    )(idx, pos, pstart_col)


PAD_BITS = tuple(1 << b for b in reversed(range((MOE_BLOCK - 1).bit_length())))


def _dispatch_kernel(pstart_ref, cnt_ref, dest_ref, u_ref, xs_ref, ubuf, zbuf, usem, sem, psem, *, td, n_steps,
                     n_experts):
    i = pl.program_id(0)
    slot = i % 2

    def u_copies(step, s):
        rows = pl.ds(pl.multiple_of(step * td, td), td)
        return [pltpu.make_async_copy(u_ref.at[rows, pl.ds(j * LANES, LANES)], ubuf.at[s, :, j, :], usem.at[s])
                for j in range(ROW_SUBLANES)]

    @pl.when(i == 0)
    def _():
        for cp in u_copies(i, 0):
            cp.start()
        zbuf[...] = jnp.zeros_like(zbuf)

        def pads(e, wait):
            cnt = cnt_ref[e]
            n_pad = (MOE_BLOCK - (cnt & (MOE_BLOCK - 1))) & (MOE_BLOCK - 1)
            base = pstart_ref[e] + cnt
            for bit in PAD_BITS:
                cp = pltpu.make_async_copy(zbuf.at[pl.ds(0, bit)], xs_ref.at[pl.ds(base, bit)], psem)
                has = (n_pad & bit) != 0

                @pl.when(has)
                def _():
                    if wait:
                        cp.wait()
                    else:
                        cp.start()

                base = base + jnp.where(has, bit, 0)

        def start_pads(e, carry):
            pads(e, False)
            return carry

        def wait_pads(e, carry):
            pads(e, True)
            return carry

        lax.fori_loop(0, n_experts, start_pads, 0)
        lax.fori_loop(0, n_experts, wait_pads, 0)

    @pl.when(i + 1 < n_steps)
    def _():
        for cp in u_copies(i + 1, 1 - slot):
            cp.start()

    for cp in u_copies(i, slot):
        cp.wait()

    def start(t, carry):
        for k in range(TOP_K):
            pltpu.make_async_copy(ubuf.at[slot, t], xs_ref.at[dest_ref[k, t]], sem).start(priority=k % 2)
        return carry

    def wait(t, carry):
        for k in range(TOP_K):
            pltpu.make_async_copy(ubuf.at[slot, 0], xs_ref.at[0], sem).wait()
        return carry

    lax.fori_loop(0, td, start, 0, unroll=4)
    lax.fori_loop(0, td, wait, 0, unroll=8)


def _dispatch(pad_start, counts, dest, u2p, n_slots):
    t, dh = u2p.shape
    td = DISPATCH_TILE
    n_experts = pad_start.shape[0]
    assert MOE_BLOCK & (MOE_BLOCK - 1) == 0 and dh == ROW_SUBLANES * LANES
    kern = functools.partial(_dispatch_kernel, td=td, n_steps=t // td, n_experts=n_experts)
    return pl.pallas_call(
        kern,
        out_shape=jax.ShapeDtypeStruct((n_slots, ROW_SUBLANES, LANES), U32),
        grid_spec=pltpu.PrefetchScalarGridSpec(
            num_scalar_prefetch=2, grid=(t // td,),
            in_specs=[pl.BlockSpec((TOP_K, td), lambda i, ps, cn: (0, i), memory_space=pltpu.SMEM),
                      pl.BlockSpec(memory_space=pl.ANY)],
            out_specs=pl.BlockSpec(memory_space=pl.ANY),
            scratch_shapes=[pltpu.VMEM((2, td, ROW_SUBLANES, LANES), U32),
                            pltpu.VMEM((MOE_BLOCK // 2, ROW_SUBLANES, LANES), U32),
                            pltpu.SemaphoreType.DMA((2,)), pltpu.SemaphoreType.DMA, pltpu.SemaphoreType.DMA]),
        compiler_params=pltpu.CompilerParams(dimension_semantics=("arbitrary",), vmem_limit_bytes=VMEM_LIMIT),
        name="dispatch",
    )(pad_start, counts, dest, u2p)


def _experts_kernel(bs_ref, nblk_ref, nu_ref, xs_ref, wg_ref, wu_ref, wd_ref, ys_ref, wgub, wdb, xbuf, ybuf,
                    semx, semy, *, n_experts):
    e = pl.program_id(0)
    n_used = nu_ref[0]

    def x_copies(g):
        rows = pl.ds(pl.multiple_of(g * MOE_BLOCK, MOE_BLOCK), MOE_BLOCK)
        return [pltpu.make_async_copy(xs_ref.at[rows, j, :], xbuf.at[g % X_BUFFERS, :, pl.ds(j * LANES, LANES)],
                                      semx.at[g % X_BUFFERS]) for j in range(ROW_SUBLANES)]

    def x_start(g):
        for cp in x_copies(g):
            cp.start()

    def x_wait(g):
        for cp in x_copies(g):
            cp.wait()

    def y_copies(g, slot):
        rows = pl.ds(pl.multiple_of(g * MOE_BLOCK, MOE_BLOCK), MOE_BLOCK)
        return [pltpu.make_async_copy(ybuf.at[slot, :, pl.ds(j * LANES, LANES)], ys_ref.at[rows, j, :], semy.at[slot])
                for j in range(ROW_SUBLANES)]

    def y_start(g, slot):
        for cp in y_copies(g, slot):
            cp.start()

    def y_wait(g, slot):
        for cp in y_copies(g, slot):
            cp.wait()

    @pl.when(e == 0)
    def _():
        for g in range(X_BUFFERS - 1):
            @pl.when(g < n_used)
            def _():
                x_start(jnp.int32(g))

    g0 = bs_ref[e]
    nb = nblk_ref[e]

    de = wg_ref.shape[2]

    @pl.when(nb > 0)
    def _():
        wgub[:, :de] = wg_ref[0].astype(BF16)
        wgub[:, de:] = wu_ref[0].astype(BF16)
        wdb[...] = wd_ref[0].astype(BF16)

    def block(g, carry):
        slot = g % 2

        @pl.when(g + X_BUFFERS - 1 < n_used)
        def _():
            x_start(g + X_BUFFERS - 1)

        x_wait(g)

        @pl.when(g >= 2)
        def _():
            y_wait(g - 2, slot)

        xa, xb = _unpack_halves(xbuf[g % X_BUFFERS])
        xrow = jnp.concatenate([xa.astype(BF16), xb.astype(BF16)], axis=1)
        hgu = _dot(xrow, wgub[...])
        hg = hgu[:, :de]
        hb = (hg * _sigmoid(hg) * hgu[:, de:]).astype(BF16)
        ybuf[slot] = _pack_halves(_dot(hb, wdb[...]))
        y_start(g, slot)
        return carry

    lax.fori_loop(g0, g0 + nb, block, 0)

    @pl.when(e == n_experts - 1)
    def _():
        @pl.when(n_used >= 2)
        def _():
            y_wait(n_used - 2, n_used % 2)

        @pl.when(n_used >= 1)
        def _():
            y_wait(n_used - 1, (n_used - 1) % 2)


def _experts(block_start, n_blocks, n_used, xsorted, wg, wu, wd):
    n_slots = xsorted.shape[0]
    dh = ROW_SUBLANES * LANES
    n_experts, d, de = wg.shape
    wmap = lambda e, bs, nb, nu: (e, 0, 0)
    assert xsorted.shape[1:] == (ROW_SUBLANES, LANES) and d == 2 * dh
    return pl.pallas_call(
        functools.partial(_experts_kernel, n_experts=n_experts),
        out_shape=jax.ShapeDtypeStruct((n_slots, ROW_SUBLANES, LANES), U32),
        grid_spec=pltpu.PrefetchScalarGridSpec(
            num_scalar_prefetch=3, grid=(n_experts,),
            in_specs=[pl.BlockSpec(memory_space=pl.ANY), pl.BlockSpec((1, d, de), wmap),
                      pl.BlockSpec((1, d, de), wmap), pl.BlockSpec((1, de, d), wmap)],
            out_specs=pl.BlockSpec(memory_space=pl.ANY),
            scratch_shapes=[pltpu.VMEM((d, 2 * de), BF16), pltpu.VMEM((de, d), BF16),
                            pltpu.VMEM((X_BUFFERS, MOE_BLOCK, dh), U32),
                            pltpu.VMEM((2, MOE_BLOCK, dh), U32),
                            pltpu.SemaphoreType.DMA((X_BUFFERS,)), pltpu.SemaphoreType.DMA((2,))]),
        compiler_params=pltpu.CompilerParams(dimension_semantics=("arbitrary",), vmem_limit_bytes=VMEM_LIMIT),
        name="experts",
    )(block_start, n_blocks, n_used, xsorted, wg, wu, wd)


def _combine_kernel(dest_ref, dnext_ref, ys_ref, wt_ref, x1_ref, u_ref, mod_ref, wsgu_ref, wsd_ref,
                    gfin_ref, o_ref, gbuf, accbuf, sem, *, tc, n_steps):
    i = pl.program_id(0)
    slot = i % 2

    def issue(dref, s):
        def body(t, carry):
            for k in range(TOP_K):
                pltpu.make_async_copy(ys_ref.at[dref[k, t]], gbuf.at[s, k, t], sem.at[s]).start(priority=k % 2)
            return carry
        lax.fori_loop(0, tc, body, 0, unroll=4)

    @pl.when(i == 0)
    def _():
        issue(dest_ref, 0)

    def wait(t, carry):
        for k in range(TOP_K):
            pltpu.make_async_copy(ys_ref.at[0], gbuf.at[slot, 0, 0], sem.at[slot]).wait()
        return carry

    lax.fori_loop(0, tc, wait, 0, unroll=8)

    def sum_and_issue(with_issue):
        zeros = jnp.zeros((GATHER_CHUNK, ROW_SUBLANES, LANES), F32)

        def body(c, carry):
            prev_l, prev_r = carry
            t0 = c * GATHER_CHUNK
            prev_rows = pl.ds(jnp.maximum(t0 - GATHER_CHUNK, 0), GATHER_CHUNK)
            accbuf[0, prev_rows] = prev_l
            accbuf[1, prev_rows] = prev_r
            rows = pl.ds(t0, GATHER_CHUNK)
            wts = wt_ref[rows]
            gathered = [gbuf[slot, k, rows] for k in range(TOP_K)]
            if with_issue:
                for tt in range(GATHER_CHUNK):
                    for k in range(TOP_K):
                        pltpu.make_async_copy(ys_ref.at[dnext_ref[k, t0 + tt]], gbuf.at[1 - slot, k, t0 + tt],
                                              sem.at[1 - slot]).start(priority=k % 2)
            sum_l, sum_r = zeros, zeros
            for k in range(TOP_K):
                ga, gb = _unpack_halves(gathered[k])
                wk = jnp.broadcast_to(wts[:, k:k + 1, :], (GATHER_CHUNK, ROW_SUBLANES, LANES))
                sum_l = sum_l + ga * wk
                sum_r = sum_r + gb * wk
            return sum_l, sum_r

        n_chunks = tc // GATHER_CHUNK
        last_l, last_r = lax.fori_loop(0, n_chunks, body, (zeros, zeros))
        last_rows = pl.ds((n_chunks - 1) * GATHER_CHUNK, GATHER_CHUNK)
        accbuf[0, last_rows] = last_l
        accbuf[1, last_rows] = last_r

    @pl.when(i + 1 < n_steps)
    def _():
        sum_and_issue(True)

    @pl.when(i + 1 >= n_steps)
    def _():
        sum_and_issue(False)

    ua, ub = _unpack_halves(u_ref[...])
    half = ua.shape[1]
    hgu = _dot(jnp.concatenate([ua.astype(BF16), ub.astype(BF16)], axis=1), wsgu_ref[...])
    dsh = wsd_ref.shape[0]
    hg = hgu[:, :dsh]
    shared = _dot((hg * _sigmoid(hg) * hgu[:, dsh:]).astype(BF16), wsd_ref[...])
    acc_l = shared[:, :half]
    acc_r = shared[:, half:]
    acc_l = jnp.concatenate([acc_l[:, j * LANES:(j + 1) * LANES] + accbuf[0, :, j, :] for j in range(ROW_SUBLANES)],
                            axis=1)
    acc_r = jnp.concatenate([acc_r[:, j * LANES:(j + 1) * LANES] + accbuf[1, :, j, :] for j in range(ROW_SUBLANES)],
                            axis=1)
    m = mod_ref[0]
    xo_l = x1_ref[:, :half] + m[5:6, :half] * acc_l
    xo_r = x1_ref[:, half:] + m[5:6, half:] * acc_r
    ms = (jnp.sum(xo_l * xo_l, axis=-1, keepdims=True) + jnp.sum(xo_r * xo_r, axis=-1, keepdims=True)) / (2 * half)
    inv = lax.rsqrt(ms + EPS)
    o_ref[:, :half] = xo_l * inv * gfin_ref[:, :half]
    o_ref[:, half:] = xo_r * inv * gfin_ref[:, half:]


def _combine(dest, ysorted, wts_t, x1, u2p, mods3, wsgu, wsd, gfin, *, seq):
    t, d = x1.shape
    dh = d // 2
    tc = DISPATCH_TILE
    tiles_per_batch = seq // tc
    n_steps = t // tc
    kern = functools.partial(_combine_kernel, tc=tc, n_steps=n_steps)
    row = lambda i: (i, 0)
    const = lambda i: (0, 0)
    return pl.pallas_call(
        kern,
        out_shape=jax.ShapeDtypeStruct((t, d), F32),
        grid=(n_steps,),
        in_specs=[pl.BlockSpec((TOP_K, tc), lambda i: (0, i), memory_space=pltpu.SMEM),
                  pl.BlockSpec((TOP_K, tc), lambda i: (0, jnp.minimum(i + 1, n_steps - 1)), memory_space=pltpu.SMEM),
                  pl.BlockSpec(memory_space=pl.ANY), pl.BlockSpec((tc, TOP_K, LANES), lambda i: (i, 0, 0)),
                  pl.BlockSpec((tc, d), row), pl.BlockSpec((tc, dh), row),
                  pl.BlockSpec((1, 6, d), lambda i: (i // tiles_per_batch, 0, 0)),
                  pl.BlockSpec(wsgu.shape, const), pl.BlockSpec(wsd.shape, const), pl.BlockSpec((1, d), const)],
        out_specs=pl.BlockSpec((tc, d), row),
        scratch_shapes=[pltpu.VMEM((2, TOP_K, tc, ROW_SUBLANES, LANES), U32),
                        pltpu.VMEM((2, tc, ROW_SUBLANES, LANES), F32), pltpu.SemaphoreType.DMA((2,))],
        compiler_params=pltpu.CompilerParams(dimension_semantics=("arbitrary",), vmem_limit_bytes=VMEM_LIMIT),
        name="combine",
    )(dest, dest, ysorted, wts_t, x1, u2p, mods3, wsgu, wsd, gfin)


def kernel(x, c, ctx, c_ctx, ada_w, ada_b, norm_mix_g, norm_ffn_g, w_in, conv_xbc_w, conv_xbc_b, ssd_dt_bias, ssd_a_log, ssd_d, ssd_norm_g, conv_qk_w, conv_qk_b, mlstm_i_bias, mlstm_f_bias, mlstm_norm_g, w_ssd_out, w_mlstm_out, w_out, router_w, router_bias, moe_w_gate, moe_w_up, moe_w_down, shared_w_gate, shared_w_up, shared_w_down, norm_final_g):
    batch, seq, d = x.shape
    ctx_len = ctx.shape[1]
    depth = ada_w.shape[0]
    assert depth == 1, "only the single-layer configuration is implemented"
    assert seq % CHUNK == 0 and ctx_len % CHUNK == 0 and seq % GRID_W == 0
    l = 0
    n_sh = d // SSD_HEAD_DIM
    n_mh = d // MLSTM_DV
    dk = MLSTM_DV // 2
    d_xbc = d + 2 * SSD_GROUPS * SSD_STATE
    d_qk = 2 * n_mh * dk
    sizes = (d, d_xbc, 2 * n_sh, d_qk, d, 4 * n_mh, d, 2 * d)
    offs = np.concatenate([[0], np.cumsum(sizes)])
    assert offs[-1] == w_in.shape[2] and 2 * n_sh + 2 * n_mh <= LANES

    cond = jnp.concatenate([c, c_ctx[None], jnp.zeros((SUBLANES - (batch + 1) % SUBLANES, d), F32)], axis=0)
    mods = _adaln(cond, ada_w[l], ada_b[l])
    mods3 = mods.reshape(mods.shape[0], 6, d)

    w = w_in[l]
    seg = lambda k: w[:, offs[k]:offs[k + 1]]
    wbig = jnp.concatenate([seg(0), seg(1), seg(3), seg(4), seg(6), seg(7)], axis=1).astype(BF16)
    w_dt = seg(2)
    w_g = seg(5).reshape(d, 2, 2, n_mh)
    w_i = w_g[:, :, 0].reshape(d, 2 * n_mh)
    w_f = w_g[:, :, 1].reshape(d, 2 * n_mh)
    pad = jnp.zeros((d, LANES - 2 * n_sh - 2 * n_mh), F32)
    wsm = jnp.concatenate([w_dt, w_f, pad, w_dt, w_i, pad], axis=1)
    wsh = wsm.astype(BF16)
    wsl = (wsm - wsh.astype(F32)).astype(BF16)
    padb = jnp.zeros((LANES - 2 * n_sh - 2 * n_mh,), F32)
    dtb = ssd_dt_bias[l].reshape(-1).astype(F32)
    smb = jnp.concatenate([dtb, mlstm_f_bias[l].reshape(-1).astype(F32), padb,
                           dtb, mlstm_i_bias[l].reshape(-1).astype(F32), padb]).reshape(1, 2 * LANES)
    aneg = jnp.concatenate([-jnp.exp(ssd_a_log[l].astype(F32)).reshape(-1),
                            jnp.zeros((LANES - 2 * n_sh,), F32)]).reshape(1, LANES)

    x2 = x.reshape(batch * seq, d)
    ctx2 = ctx.reshape(batch * ctx_len, d)
    z, xs, bc, qk, v, og, mg, small, smallt = _inproj(
        x2, ctx2, mods3, norm_mix_g[l].reshape(1, d), wbig, wsh, wsl,
        conv_xbc_w[l], conv_xbc_b[l].reshape(1, d_xbc), conv_qk_w[l], conv_qk_b[l].reshape(1, d_qk), smb, aneg,
        batch=batch, seq=seq, ctx_len=ctx_len, dk=dk)

    ncc = ctx_len // CHUNK
    ncl = seq // CHUNK
    y0, y1, h0, h1 = _scans(xs, bc, qk, v, small, smallt, batch=batch, n_ctx_chunks=ncc, n_lat_chunks=ncl, dk=dk)

    dexp = jnp.repeat(ssd_d[l].astype(F32), SSD_HEAD_DIM).reshape(1, d)
    x1, u2 = _merge(y0, y1, xs, z, h0, h1, og, mg, x2, mods3, dexp, ssd_norm_g[l].reshape(1, d),
                    mlstm_norm_g[l].reshape(1, d), norm_ffn_g[l].reshape(1, d),
                    w_ssd_out[l].astype(BF16), w_mlstm_out[l].astype(BF16), w_out[l].astype(BF16),
                    batch=batch, seq=seq, n_ctx_tok=batch * ctx_len)

    n_experts = router_w.shape[2]
    idx, pos, wts, cnt = _router(u2, router_w[l].T.astype(BF16), router_bias[l].astype(F32).reshape(n_experts, 1))
    counts = cnt[:, 0].astype(I32)
    padded = (counts + MOE_BLOCK - 1) // MOE_BLOCK * MOE_BLOCK
    pad_end = jnp.cumsum(padded)
    pad_start = (pad_end - padded).astype(I32)
    t = batch * seq
    nb = t * TOP_K // MOE_BLOCK + n_experts
    n_used = (pad_end[-1] // MOE_BLOCK).astype(I32).reshape(1)

    dest = _slots(idx, pos, pad_start.astype(F32).reshape(n_experts, 1))
    xsorted = _dispatch(pad_start, counts, dest, u2, nb * MOE_BLOCK)
    ysorted = _experts(pad_start // MOE_BLOCK, (padded // MOE_BLOCK).astype(I32), n_used, xsorted,
                       moe_w_gate[l], moe_w_up[l], moe_w_down[l])
    wts_lanes = jnp.broadcast_to(wts.T[:, :, None], (t, TOP_K, LANES))
    out = _combine(dest, ysorted, wts_lanes, x1, u2, mods3,
                   jnp.concatenate([shared_w_gate[l], shared_w_up[l]], axis=1).astype(BF16), shared_w_down[l].astype(BF16),
                   norm_final_g.reshape(1, d), seq=seq)
    return out.reshape(batch, seq, d)
```

```python
import functools

import numpy as np
import jax
import jax.numpy as jnp
from jax import lax
from jax.experimental import pallas as pl
from jax.experimental.pallas import tpu as pltpu

F32 = jnp.float32
BF16 = jnp.bfloat16
I32 = jnp.int32
U32 = jnp.uint32

EPS = 1e-6
CHUNK = 128
CONV_K = 5
GRID_W = 64
SSD_HEAD_DIM = 64
SSD_STATE = 128
SSD_GROUPS = 2
MLSTM_DV = 128
N_EXPERT_GROUPS = 8
TOPK_GROUPS = 4
TOP_K = 8
ROUTED_SCALE = 2.5

LANES = 128
SUBLANES = 8
PACK_ROWS = 16
TOKEN_TILE = 512
COL_CHUNK = 512
ADALN_COLS = 1536
CONV_MASK_COLS = 8
INPROJ_PARTS = 2
SCAN_CHUNKS = 2
MOE_BLOCK = 512
DISPATCH_TILE = 256
GATHER_CHUNK = 4
X_BUFFERS = 4
ROW_SUBLANES = 4
VMEM_LIMIT = 56 * 1024 * 1024
NEG_INF = float("-inf")


def _dot(a, b):
    return jnp.dot(a, b, preferred_element_type=F32)


def _dot_nt(a, b):
    return lax.dot_general(a, b, (((1,), (1,)), ((), ())), preferred_element_type=F32)


def _dot_tn(a, b):
    return lax.dot_general(a, b, (((0,), (0,)), ((), ())), preferred_element_type=F32)


def _spread(parts, e):
    res = _dot(jnp.concatenate(parts, axis=0).astype(BF16), e)
    out, r0 = [], 0
    for p in parts:
        out.append(res[r0:r0 + p.shape[0]])
        r0 += p.shape[0]
    return out


def _rows16(row):
    r8 = jnp.broadcast_to(row, (SUBLANES, row.shape[1]))
    return jnp.concatenate([r8, r8], axis=0)


def _sigmoid(v):
    return 1.0 / (1.0 + jnp.exp(-v))


def _pack_halves(v):
    n = v.shape[1] // 2
    hi = lax.bitcast_convert_type(v[:, :n].astype(BF16).astype(F32), U32)
    lo = lax.bitcast_convert_type(v[:, n:].astype(BF16).astype(F32), U32)
    return hi | (lo >> 16)


def _unpack_halves(p):
    left = lax.bitcast_convert_type(p & jnp.uint32(0xFFFF0000), F32)
    right = lax.bitcast_convert_type(p << 16, F32)
    return left, right


def _softplus(v):
    return jnp.maximum(v, 0.0) + jnp.log1p(jnp.exp(-jnp.abs(v)))


def _adaln_kernel(c_ref, w_ref, b_ref, o_ref):
    c = c_ref[...]
    s = c * _sigmoid(c)
    w = w_ref[...]
    s_hi = s.astype(BF16)
    s_lo = (s - s_hi.astype(F32)).astype(BF16)
    w_hi = w.astype(BF16)
    w_lo = (w - w_hi.astype(F32)).astype(BF16)
    o_ref[...] = _dot(s_hi, w_hi) + _dot(s_lo, w_hi) + _dot(s_hi, w_lo) + b_ref[...]


def _adaln(cond, w, b):
    rows, d = cond.shape
    n = w.shape[1]
    tn = ADALN_COLS if n % ADALN_COLS == 0 else n
    return pl.pallas_call(
        _adaln_kernel,
        out_shape=jax.ShapeDtypeStruct((rows, n), F32),
        grid=(n // tn,),
        in_specs=[pl.BlockSpec((rows, d), lambda j: (0, 0)),
                  pl.BlockSpec((d, tn), lambda j: (0, j)),
                  pl.BlockSpec((1, tn), lambda j: (0, j))],
        out_specs=pl.BlockSpec((rows, tn), lambda j: (0, j)),
        compiler_params=pltpu.CompilerParams(dimension_semantics=("arbitrary",), vmem_limit_bytes=VMEM_LIMIT),
        name="adaln",
    )(cond, w, b.reshape(1, n))


CONV_SHIFTS = tuple(j - CONV_K // 2 for j in range(CONV_K) if j != CONV_K // 2)


def _conv_masks(tm, seg_len):
    assert len(CONV_SHIFTS) <= CONV_MASK_COLS
    pos = np.arange(tm) % seg_len
    m = np.zeros((tm, CONV_MASK_COLS), np.float32)
    for i, s in enumerate(CONV_SHIFTS):
        m[:, i] = ((pos + s >= 0) & (pos + s < seg_len)).astype(np.float32)
    return m


def _conv_silu(acc, w5, bias, vm, tm):
    out = acc * w5[CONV_K // 2:CONV_K // 2 + 1] + bias
    for i, s in enumerate(CONV_SHIFTS):
        shifted = pltpu.roll(acc, (-s) % tm, axis=0)
        j = s + CONV_K // 2
        out = out + (shifted * vm[:, i:i + 1]) * w5[j:j + 1]
    return out * _sigmoid(out)


def _inproj_kernel(x_ref, ctx_ref, mod_ref, g_ref, wbig_ref, wsh_ref, wsl_ref, cwx_ref, cbx_ref, cwq_ref, cbq_ref,
                   smb_ref, aneg_ref, tril_ref, triu_ref, cmask_ref,
                   z_ref, xs_ref, bc_ref, qk_ref, v_ref, og_ref, mg_ref, small_ref, smallt_ref,
                   *, n_ctx_tiles, tm, d_model, dk):
    i = pl.program_id(0)
    is_ctx = i < n_ctx_tiles
    xt = jnp.where(is_ctx, ctx_ref[...], x_ref[...])
    m = mod_ref[0]
    ms = jnp.mean(xt * xt, axis=-1, keepdims=True)
    u = xt * lax.rsqrt(ms + EPS) * g_ref[...] * (1.0 + m[1:2]) + m[0:1]
    u_hi = u.astype(BF16)
    u_lo = (u - u_hi.astype(F32)).astype(BF16)
    vm = jnp.where(is_ctx, cmask_ref[1], cmask_ref[0])

    d = d_model
    d_bc = 2 * SSD_GROUPS * SSD_STATE
    pieces = [(z_ref, d, "plain", None), (xs_ref, d, "convx", 0), (bc_ref, d_bc, "convx", d),
              (qk_ref, d, "convq", 0), (v_ref, d, "plain", None), (og_ref, d, "sigmoid", None),
              (mg_ref, 2 * d, "sigmoid", None)]
    col = 0
    rp = tm // INPROJ_PARTS
    row_parts = [slice(p * rp, (p + 1) * rp) for p in range(INPROJ_PARTS)]
    u_parts = [u_hi[rows] for rows in row_parts]
    vm_parts = [vm[rows] for rows in row_parts]
    for ref, width, kind, coff in pieces:
        for c0 in range(0, width, COL_CHUNK):
            accs = [_dot(up, wbig_ref[:, col + c0:col + c0 + COL_CHUNK]) for up in u_parts]
            for rows, vmp, acc in zip(row_parts, vm_parts, accs):
                if kind == "convx":
                    cs = coff + c0
                    acc = _conv_silu(acc, cwx_ref[:, cs:cs + COL_CHUNK], cbx_ref[:, cs:cs + COL_CHUNK], vmp, rp)
                elif kind == "convq":
                    acc = _conv_silu(acc, cwq_ref[:, c0:c0 + COL_CHUNK], cbq_ref[:, c0:c0 + COL_CHUNK], vmp, rp)
                    if c0 < width // 2:
                        acc = acc * (dk ** -0.5)
                elif kind == "sigmoid":
                    acc = _sigmoid(acc)
                ref[rows, c0:c0 + COL_CHUNK] = acc.astype(ref.dtype)
        col += width

    wsh = wsh_ref[...]
    raw = _dot(u_hi, wsh) + _dot(u_lo, wsh) + _dot(u_hi, wsl_ref[...]) + smb_ref[...]
    p1 = raw[:, :LANES]
    p2 = raw[:, LANES:]
    lane = lax.broadcasted_iota(I32, (tm, LANES), 1)
    n_dt = 2 * (d_model // SSD_HEAD_DIM)
    n_g = 2 * (d_model // MLSTM_DV)
    is_dt = lane < n_dt
    is_gate = jnp.logical_and(lane >= n_dt, lane < n_dt + n_g)
    dt = _softplus(p2)
    pa = jnp.where(is_dt, dt * aneg_ref[...], jnp.where(is_gate, -_softplus(-p1), 0.0))
    pb = jnp.where(is_dt, dt, jnp.where(is_gate, p2, 0.0))
    lane_c = lax.broadcasted_iota(I32, (CHUNK, LANES), 1)
    is_dt_c = lane_c < n_dt
    rev = jnp.logical_or(jnp.logical_and(lane_c >= n_dt // 2, lane_c < n_dt),
                         jnp.logical_and(lane_c >= n_dt + n_g // 2, lane_c < n_dt + n_g))
    tril = tril_ref[...]
    triu = triu_ref[...]
    tq = lax.broadcasted_iota(I32, (CHUNK, LANES), 0)
    chunks = range(tm // CHUNK)
    planes_a, planes_b = [], []
    for c in chunks:
        a_c = pa[c * CHUNK:(c + 1) * CHUNK]
        hi = a_c.astype(BF16)
        r1 = a_c - hi.astype(F32)
        mid = r1.astype(BF16)
        lo = (r1 - mid.astype(F32)).astype(BF16)
        cs_f = _dot(tril, hi) + _dot(tril, mid) + _dot(tril, lo)
        cs_b = _dot(triu, hi) + _dot(triu, mid) + _dot(triu, lo)
        planes_a.append(jnp.where(rev, cs_b, cs_f))
    for c in chunks:
        pb_c = pb[c * CHUNK:(c + 1) * CHUNK]
        planes_b.append(jnp.where(is_dt_c, pb_c, pb_c - planes_a[c]))
    yfs = list(planes_b)
    ybs = list(planes_b)
    s = 1
    while s < CHUNK:
        for c in chunks:
            sh = pltpu.roll(yfs[c], s, axis=0)
            yfs[c] = jnp.maximum(yfs[c], jnp.where(tq >= s, sh, NEG_INF))
            sh = pltpu.roll(ybs[c], CHUNK - s, axis=0)
            ybs[c] = jnp.maximum(ybs[c], jnp.where(tq + s < CHUNK, sh, NEG_INF))
        s *= 2
    for c in chunks:
        r0 = c * CHUNK
        small_ref[r0:r0 + CHUNK, 0:LANES] = planes_a[c]
        small_ref[r0:r0 + CHUNK, LANES:2 * LANES] = planes_b[c]
        small_ref[r0:r0 + CHUNK, 2 * LANES:3 * LANES] = jnp.where(rev, ybs[c], yfs[c])
        smallt_ref[c, 0] = planes_a[c].T
        smallt_ref[c, 1] = planes_b[c].T


def _inproj(x2, ctx2, mods3, g, wbig, wsh, wsl, cwx, cbx, cwq, cbq, smb, aneg, *, batch, seq, ctx_len, dk):
    d = x2.shape[1]
    tm = TOKEN_TILE
    n_ctx_tok = batch * ctx_len
    rp = tm // INPROJ_PARTS
    assert n_ctx_tok % tm == 0 and seq % tm == 0 and rp % ctx_len == 0 and rp % GRID_W == 0
    n_ctx_tiles = n_ctx_tok // tm
    tiles_per_batch = seq // tm
    n_tiles = n_ctx_tiles + batch * tiles_per_batch
    t_all = n_tiles * tm
    n_big = wbig.shape[1]
    tril = jnp.asarray(np.tril(np.ones((CHUNK, CHUNK), np.float32)), BF16)
    triu = jnp.asarray(np.triu(np.ones((CHUNK, CHUNK), np.float32)), BF16)

    def x_map(i):
        return (jnp.maximum(i - n_ctx_tiles, 0), 0)

    def ctx_map(i):
        return (jnp.minimum(i, n_ctx_tiles - 1), 0)

    def mod_map(i):
        return (jnp.where(i < n_ctx_tiles, batch, jnp.maximum(i - n_ctx_tiles, 0) // tiles_per_batch), 0, 0)

    const = lambda i: (0, 0)
    row = lambda i: (i, 0)
    cmask = jnp.asarray(np.stack([_conv_masks(tm, GRID_W), _conv_masks(tm, ctx_len)]))
    kern = functools.partial(_inproj_kernel, n_ctx_tiles=n_ctx_tiles, tm=tm, d_model=d, dk=dk)
    d_bc = 2 * SSD_GROUPS * SSD_STATE
    outs = [jax.ShapeDtypeStruct((t_all, d), BF16), jax.ShapeDtypeStruct((t_all, d), BF16),
            jax.ShapeDtypeStruct((t_all, d_bc), BF16), jax.ShapeDtypeStruct((t_all, d), BF16),
            jax.ShapeDtypeStruct((t_all, d), BF16), jax.ShapeDtypeStruct((t_all, d), BF16),
            jax.ShapeDtypeStruct((t_all, 2 * d), BF16), jax.ShapeDtypeStruct((t_all, 3 * LANES), F32),
            jax.ShapeDtypeStruct((t_all // CHUNK, 2, LANES, CHUNK), F32)]
    out_specs = [pl.BlockSpec((tm, d), row), pl.BlockSpec((tm, d), row), pl.BlockSpec((tm, d_bc), row),
                 pl.BlockSpec((tm, d), row), pl.BlockSpec((tm, d), row), pl.BlockSpec((tm, d), row),
                 pl.BlockSpec((tm, 2 * d), row), pl.BlockSpec((tm, 3 * LANES), row),
                 pl.BlockSpec((tm // CHUNK, 2, LANES, CHUNK), lambda i: (i, 0, 0, 0))]
    in_specs = [pl.BlockSpec((tm, d), x_map), pl.BlockSpec((tm, d), ctx_map),
                pl.BlockSpec((1, 6, d), mod_map), pl.BlockSpec((1, d), const),
                pl.BlockSpec((d, n_big), const, pipeline_mode=pl.Buffered(1)),
                pl.BlockSpec((d, 2 * LANES), const), pl.BlockSpec((d, 2 * LANES), const),
                pl.BlockSpec(cwx.shape, const), pl.BlockSpec(cbx.shape, const),
                pl.BlockSpec(cwq.shape, const), pl.BlockSpec(cbq.shape, const),
                pl.BlockSpec((1, 2 * LANES), const), pl.BlockSpec((1, LANES), const),
                pl.BlockSpec((CHUNK, CHUNK), const), pl.BlockSpec((CHUNK, CHUNK), const),
                pl.BlockSpec((2, tm, CONV_MASK_COLS), lambda i: (0, 0, 0))]
    return pl.pallas_call(
        kern, out_shape=outs, grid=(n_tiles,), in_specs=in_specs, out_specs=out_specs,
        compiler_params=pltpu.CompilerParams(dimension_semantics=("arbitrary",), vmem_limit_bytes=VMEM_LIMIT),
        name="inproj",
    )(x2, ctx2, mods3, g, wbig, wsh, wsl, cwx, cbx, cwq, cbq, smb, aneg, tril, triu, cmask)


def _chunk_block_map(direction, batch, n_ctx_chunks, n_lat_chunks):
    def idx(b, s):
        if direction == 0:
            c_ctx = s
            c_lat = s - n_ctx_chunks
        else:
            c_ctx = n_ctx_chunks - 1 - s
            c_lat = n_lat_chunks - 1 - (s - n_ctx_chunks)
        return jnp.where(s < n_ctx_chunks, b * n_ctx_chunks + c_ctx, batch * n_ctx_chunks + b * n_lat_chunks + c_lat)
    return idx


def _ssd_kernel(xs_ref, bc_ref, small_ref, smallt_ref, e16_ref, y_ref, st_ref, *, direction, n_heads, first):
    if first:
        @pl.when(pl.program_id(1) == 0)
        def _():
            st_ref[...] = jnp.zeros_like(st_ref)

    hpg = n_heads // SSD_GROUPS
    lane0 = direction * n_heads
    last = CHUNK - 1 if direction == 0 else 0
    lane = lax.broadcasted_iota(I32, (CHUNK, LANES), 1)
    lm = jnp.logical_and(lane >= lane0, lane < lane0 + n_heads)
    plane_a = jnp.where(lm, small_ref[:, 0:LANES], 0.0)
    plane_b = jnp.where(lm, small_ref[:, LANES:2 * LANES], 0.0)
    cum_t = smallt_ref[0, 0]
    tot = plane_a[last:last + 1]
    lane16 = lax.broadcasted_iota(I32, (PACK_ROWS, LANES), 1)
    lm16 = jnp.logical_and(lane16 >= lane0, lane16 < lane0 + n_heads)
    dec_rows = jnp.where(lm16, jnp.exp(_rows16(tot)), 0.0)
    dtx, ecx, wx, decx = _spread(
        [plane_b, jnp.where(lm, jnp.exp(plane_a), 0.0), plane_b * jnp.exp(tot - plane_a), dec_rows], e16_ref[...])
    decx = decx[0:1]

    xf = xs_ref[...].astype(F32)
    xdt = xf * dtx
    xw = (xf * wx).astype(BF16)
    plane64 = lax.broadcasted_iota(I32, (CHUNK, 2 * SSD_HEAD_DIM), 1) < SSD_HEAD_DIM
    iq = lax.broadcasted_iota(I32, (CHUNK, CHUNK), 0)
    ik = lax.broadcasted_iota(I32, (CHUNK, CHUNK), 1)
    mask = (iq >= ik) if direction == 0 else (iq <= ik)
    gw = hpg * SSD_HEAD_DIM
    groups = range(SSD_GROUPS)
    bgs = [bc_ref[:, g * SSD_STATE:(g + 1) * SSD_STATE] for g in groups]
    cgs = [bc_ref[:, (SSD_GROUPS + g) * SSD_STATE:(SSD_GROUPS + g + 1) * SSD_STATE] for g in groups]
    sgs = [st_ref[:, g * gw:(g + 1) * gw] for g in groups]
    cbs = [_dot_nt(cgs[g], bgs[g]) for g in groups]
    y_inters = [_dot(cgs[g], sgs[g].astype(BF16)) * ecx[:, g * gw:(g + 1) * gw] for g in groups]
    m_hs = []
    for h in range(n_heads):
        cum_q = jnp.broadcast_to(plane_a[:, lane0 + h:lane0 + h + 1], (CHUNK, CHUNK))
        seg = cum_q - cum_t[lane0 + h:lane0 + h + 1, :]
        m_hs.append((cbs[h // hpg] * jnp.exp(jnp.where(mask, seg, NEG_INF))).astype(BF16))
    for pair in range(n_heads // 2):
        g = (2 * pair) // hpg
        c0 = 2 * pair * SSD_HEAD_DIM
        xpair = xdt[:, c0:c0 + 2 * SSD_HEAD_DIM]
        acc = y_inters[g][:, c0 - g * gw:c0 - g * gw + 2 * SSD_HEAD_DIM]
        for par in range(2):
            keep = plane64 if par == 0 else jnp.logical_not(plane64)
            acc = acc + _dot(m_hs[2 * pair + par], jnp.where(keep, xpair, 0.0).astype(BF16))
        y_ref[:, c0:c0 + 2 * SSD_HEAD_DIM] = acc.astype(y_ref.dtype)
    for g in groups:
        st_ref[:, g * gw:(g + 1) * gw] = (decx[:, g * gw:(g + 1) * gw] * sgs[g]
                                          + _dot_tn(bgs[g], xw[:, g * gw:(g + 1) * gw]))


def _selectors(direction, d):
    n_sh = d // SSD_HEAD_DIM
    n_mh = d // MLSTM_DV
    e16 = np.zeros((LANES, d), np.float32)
    em = np.zeros((LANES, d), np.float32)
    for h in range(n_sh):
        e16[direction * n_sh + h, h * SSD_HEAD_DIM:(h + 1) * SSD_HEAD_DIM] = 1.0
    for h in range(n_mh):
        em[2 * n_sh + direction * n_mh + h, h * MLSTM_DV:(h + 1) * MLSTM_DV] = 1.0
    return jnp.asarray(e16, BF16), jnp.asarray(em, BF16)


def _mlstm_kernel(qk_ref, v_ref, small_ref, smallt_ref, em_ref, h_ref, cn_ref, m_ref, *, direction, n_heads, dk,
                  first):
    if first:
        @pl.when(pl.program_id(1) == 0)
        def _():
            cn_ref[...] = jnp.zeros_like(cn_ref)
            m_ref[...] = jnp.zeros_like(m_ref)

    n_dt = 2 * (n_heads * MLSTM_DV // SSD_HEAD_DIM)
    lane0 = n_dt + direction * n_heads
    last = CHUNK - 1 if direction == 0 else 0
    lane = lax.broadcasted_iota(I32, (CHUNK, LANES), 1)
    lm = jnp.logical_and(lane >= lane0, lane < lane0 + n_heads)
    b_q = jnp.where(lm, small_ref[:, 0:LANES], 0.0)
    r_k = jnp.where(lm, small_ref[:, LANES:2 * LANES], 0.0)
    cmr = jnp.where(lm, small_ref[:, 2 * LANES:3 * LANES], 0.0)
    r_t = smallt_ref[0, 1]
    m_all = m_ref[...]
    m_row = m_all[0:1]
    mm = jnp.maximum(cmr, m_row)
    w_state = jnp.exp(m_row - mm)
    e_mq = jnp.exp(-(b_q + mm))
    m_base8 = jnp.maximum(m_all, _rows16(cmr[last:last + 1]))
    m_base = m_base8[0:1]
    w_k = jnp.where(lm, jnp.exp(r_k - m_base), 0.0)
    lane16 = lax.broadcasted_iota(I32, (PACK_ROWS, LANES), 1)
    lm16 = jnp.logical_and(lane16 >= lane0, lane16 < lane0 + n_heads)
    dec_rows = jnp.where(lm16, jnp.exp(m_all - m_base8), 0.0)
    wsx, emqx, wkx, decx = _spread(
        [jnp.where(lm, w_state, 0.0), jnp.where(lm, e_mq, 0.0), w_k, dec_rows], em_ref[...])
    decx = decx[0:1]
    m_ref[...] = jnp.where(lm16, _rows16(b_q[last:last + 1]) + m_base8, 0.0)

    iq = lax.broadcasted_iota(I32, (CHUNK, CHUNK), 0)
    ik = lax.broadcasted_iota(I32, (CHUNK, CHUNK), 1)
    mask = (iq >= ik) if direction == 0 else (iq <= ik)
    ones = jnp.ones((CHUNK, MLSTM_DV), BF16)
    d_qk = n_heads * dk
    heads = range(n_heads)
    hsl = [slice(h * MLSTM_DV, (h + 1) * MLSTM_DV) for h in heads]
    qs = [qk_ref[:, h * dk:(h + 1) * dk] for h in heads]
    ks = [qk_ref[:, d_qk + h * dk:d_qk + (h + 1) * dk] for h in heads]
    vs = [v_ref[:, hsl[h]] for h in heads]
    cns = [cn_ref[h] for h in heads]
    scores = [_dot_nt(qs[h], ks[h]) for h in heads]
    inter = [_dot(qs[h], cns[h].astype(BF16)) for h in heads]
    smats = []
    for h in heads:
        mm_q = jnp.broadcast_to(mm[:, lane0 + h:lane0 + h + 1], (CHUNK, CHUNK))
        dmat = jnp.exp(jnp.where(mask, r_t[lane0 + h:lane0 + h + 1, :] - mm_q, NEG_INF))
        smats.append((scores[h] * dmat).astype(BF16))
    tots = [_dot(smats[h], jnp.concatenate([vs[h], ones], axis=1)) for h in heads]
    for h in heads:
        hs = hsl[h]
        kh, vh, cn = ks[h], vs[h], cns[h]
        wsh = wsx[:, hs]
        tot = tots[h] + jnp.concatenate([wsh, wsh], axis=1) * inter[h]
        num = tot[:, :MLSTM_DV]
        den = tot[:, MLSTM_DV:]
        h_ref[:, hs] = (num / jnp.maximum(jnp.abs(den), emqx[:, hs])).astype(h_ref.dtype)
        wkh = wkx[:, hs]
        rhs = jnp.concatenate([(vh.astype(F32) * wkh).astype(BF16), wkh.astype(BF16)], axis=1)
        dech = decx[:, hs]
        cn_ref[h] = jnp.concatenate([dech, dech], axis=1) * cn + _dot_tn(kh, rhs)


def _scans_kernel(*refs, n_sh, n_mh, dk):
    (xs0, bc0, qk0, v0, sm0, smt0, xs1, bc1, qk1, v1, sm1, smt1, e16_0, em_0, e16_1, em_1,
     y0, y1, h0, h1, st0, st1, cn0, cn1, m0, m1) = refs
    for t in range(SCAN_CHUNKS):
        j0, j1 = t, SCAN_CHUNKS - 1 - t
        r0, r1 = pl.ds(j0 * CHUNK, CHUNK), pl.ds(j1 * CHUNK, CHUNK)
        c0, c1 = pl.ds(j0, 1), pl.ds(j1, 1)
        first = t == 0
        _ssd_kernel(xs0.at[r0], bc0.at[r0], sm0.at[r0], smt0.at[c0], e16_0, y0.at[r0], st0,
                    direction=0, n_heads=n_sh, first=first)
        _ssd_kernel(xs1.at[r1], bc1.at[r1], sm1.at[r1], smt1.at[c1], e16_1, y1.at[r1], st1,
                    direction=1, n_heads=n_sh, first=first)
        _mlstm_kernel(qk0.at[r0], v0.at[r0], sm0.at[r0], smt0.at[c0], em_0, h0.at[r0], cn0, m0,
                      direction=0, n_heads=n_mh, dk=dk, first=first)
        _mlstm_kernel(qk1.at[r1], v1.at[r1], sm1.at[r1], smt1.at[c1], em_1, h1.at[r1], cn1, m1,
                      direction=1, n_heads=n_mh, dk=dk, first=first)


def _scans(xs, bc, qk, v, small, smallt, *, batch, n_ctx_chunks, n_lat_chunks, dk):
    t_all, d = xs.shape
    n_sh = d // SSD_HEAD_DIM
    n_mh = d // MLSTM_DV
    assert n_ctx_chunks % SCAN_CHUNKS == 0 and n_lat_chunks % SCAN_CHUNKS == 0
    n_ctx_chunks //= SCAN_CHUNKS
    n_lat_chunks //= SCAN_CHUNKS
    rb = SCAN_CHUNKS * CHUNK
    n_steps = n_ctx_chunks + n_lat_chunks
    in_specs, args = [], []
    for direction in range(2):
        idx = _chunk_block_map(direction, batch, n_ctx_chunks, n_lat_chunks)
        rows = lambda b, s, idx=idx: (idx(b, s), 0)
        in_specs += [pl.BlockSpec((rb, d), rows), pl.BlockSpec((rb, bc.shape[1]), rows),
                     pl.BlockSpec((rb, qk.shape[1]), rows), pl.BlockSpec((rb, d), rows),
                     pl.BlockSpec((rb, 3 * LANES), rows),
                     pl.BlockSpec((SCAN_CHUNKS, 2, LANES, CHUNK), lambda b, s, idx=idx: (idx(b, s), 0, 0, 0))]
        args += [xs, bc, qk, v, small, smallt]
    out_specs = []
    for direction in (0, 1, 0, 1):
        idx = _chunk_block_map(direction, batch, n_ctx_chunks, n_lat_chunks)
        out_specs.append(pl.BlockSpec((rb, d), lambda b, s, idx=idx: (idx(b, s), 0)))
    for direction in range(2):
        sel = _selectors(direction, d)
        in_specs += [pl.BlockSpec(a.shape, lambda b, s: (0, 0)) for a in sel]
        args += list(sel)
    state = [pltpu.VMEM((SSD_STATE, d), F32)] * 2 + [pltpu.VMEM((n_mh, dk, 2 * MLSTM_DV), F32)] * 2 \
        + [pltpu.VMEM((PACK_ROWS, LANES), F32)] * 2
    return pl.pallas_call(
        functools.partial(_scans_kernel, n_sh=n_sh, n_mh=n_mh, dk=dk),
        out_shape=[jax.ShapeDtypeStruct((t_all, d), BF16)] * 4,
        grid=(batch, n_steps),
        in_specs=in_specs, out_specs=out_specs, scratch_shapes=state,
        compiler_params=pltpu.CompilerParams(dimension_semantics=("arbitrary", "arbitrary"),
                                             vmem_limit_bytes=VMEM_LIMIT),
        name="scans",
    )(*args)


def _merge_kernel(y0_ref, y1_ref, xs_ref, z_ref, h0_ref, h1_ref, og_ref, mg_ref, x_ref, mod_ref, dexp_ref, gs_ref,
                  gm_ref, gf_ref, wso_ref, wmo_ref, wo_ref, x1_ref, u2_ref, *, d_model):
    d = d_model
    m = mod_ref[0]
    y = y0_ref[...].astype(F32) + y1_ref[...].astype(F32) + dexp_ref[...] * xs_ref[...].astype(F32)
    zf = z_ref[...].astype(F32)
    y = y * (zf * _sigmoid(zf))
    y = y * lax.rsqrt(jnp.mean(y * y, axis=-1, keepdims=True) + EPS) * gs_ref[...]
    a = _dot(y.astype(BF16), wso_ref[...])
    hm = h0_ref[...].astype(F32) + h1_ref[...].astype(F32)
    blocks = []
    for h in range(d // MLSTM_DV):
        blk = hm[:, h * MLSTM_DV:(h + 1) * MLSTM_DV]
        blocks.append(blk * lax.rsqrt(jnp.mean(blk * blk, axis=-1, keepdims=True) + EPS))
    hn = jnp.concatenate(blocks, axis=1) * gm_ref[...]
    bm = _dot((og_ref[...].astype(F32) * hn).astype(BF16), wmo_ref[...])
    merged = mg_ref[:, :d].astype(F32) * a + mg_ref[:, d:].astype(F32) * bm
    r = _dot(merged.astype(BF16), wo_ref[...])
    x1 = x_ref[...] + m[2:3] * r
    x1_ref[...] = x1
    u2 = x1 * lax.rsqrt(jnp.mean(x1 * x1, axis=-1, keepdims=True) + EPS) * gf_ref[...] * (1.0 + m[4:5]) + m[3:4]
    u2_ref[...] = _pack_halves(u2)


def _merge(y0, y1, xs, z, h0, h1, og, mg, x2, mods3, dexp, gs, gm, gf, wso, wmo, wo, *, batch, seq, n_ctx_tok):
    t, d = x2.shape
    tm = TOKEN_TILE
    off = n_ctx_tok // tm
    tiles_per_batch = seq // tm
    lat = lambda i: (i + off, 0)
    row = lambda i: (i, 0)
    const = lambda i: (0, 0)
    kern = functools.partial(_merge_kernel, d_model=d)
    wspec = pl.BlockSpec((d, d), const, pipeline_mode=pl.Buffered(1))
    return pl.pallas_call(
        kern,
        out_shape=[jax.ShapeDtypeStruct((t, d), F32), jax.ShapeDtypeStruct((t, d // 2), U32)],
        grid=(t // tm,),
        in_specs=[pl.BlockSpec((tm, d), lat), pl.BlockSpec((tm, d), lat), pl.BlockSpec((tm, d), lat),
                  pl.BlockSpec((tm, d), lat), pl.BlockSpec((tm, d), lat), pl.BlockSpec((tm, d), lat),
                  pl.BlockSpec((tm, d), lat), pl.BlockSpec((tm, 2 * d), lat), pl.BlockSpec((tm, d), row),
                  pl.BlockSpec((1, 6, d), lambda i: (i // tiles_per_batch, 0, 0)),
                  pl.BlockSpec((1, d), const), pl.BlockSpec((1, d), const), pl.BlockSpec((1, d), const),
                  pl.BlockSpec((1, d), const), wspec, wspec, wspec],
        out_specs=[pl.BlockSpec((tm, d), row), pl.BlockSpec((tm, d // 2), row)],
        compiler_params=pltpu.CompilerParams(dimension_semantics=("arbitrary",), vmem_limit_bytes=VMEM_LIMIT),
        name="merge",
    )(y0, y1, xs, z, h0, h1, og, mg, x2, mods3, dexp, gs, gm, gf, wso, wmo, wo)


def _first_index_of_max(vals, row_iota, n_rows):
    mx = jnp.max(vals, axis=0, keepdims=True)
    idx = jnp.min(jnp.where(vals == mx, row_iota, n_rows), axis=0, keepdims=True)
    return mx, idx


def _router_kernel(u_ref, wt_ref, bias_ref, su_ref, idx_ref, pos_ref, wts_ref, cnt_ref, run_ref, *, n_experts, tr):
    i = pl.program_id(0)

    @pl.when(i == 0)
    def _():
        run_ref[...] = jnp.zeros_like(run_ref)

    ua, ub = _unpack_halves(u_ref[...])
    urow = jnp.concatenate([ua.astype(BF16), ub.astype(BF16)], axis=1)
    scores = _sigmoid(_dot_nt(wt_ref[...], urow))
    biased = scores + bias_ref[...]
    gsz = n_experts // N_EXPERT_GROUPS
    gi = lax.broadcasted_iota(I32, (gsz, tr), 0).astype(F32)
    gscores = []
    for g in range(N_EXPERT_GROUPS):
        blk = biased[g * gsz:(g + 1) * gsz]
        m1, i1 = _first_index_of_max(blk, gi, gsz)
        m2 = jnp.max(jnp.where(gi == i1, NEG_INF, blk), axis=0, keepdims=True)
        gscores.append(m1 + m2)
    gs = jnp.concatenate(gscores, axis=0)
    g8 = lax.broadcasted_iota(I32, (N_EXPERT_GROUPS, tr), 0).astype(F32)
    gsel = jnp.zeros((N_EXPERT_GROUPS, tr), F32)
    for _ in range(TOPK_GROUPS):
        _, gidx = _first_index_of_max(gs, g8, N_EXPERT_GROUPS)
        hit = g8 == gidx
        gsel = jnp.where(hit, 1.0, gsel)
        gs = jnp.where(hit, NEG_INF, gs)
    cand = jnp.concatenate(
        [jnp.where(jnp.broadcast_to(gsel[g:g + 1], (gsz, tr)) > 0.5, biased[g * gsz:(g + 1) * gsz], NEG_INF)
         for g in range(N_EXPERT_GROUPS)], axis=0)
    ei = lax.broadcasted_iota(I32, (n_experts, tr), 0).astype(F32)
    sel = jnp.zeros((n_experts, tr), F32)
    idxs, ws = [], []
    for _ in range(TOP_K):
        _, eidx = _first_index_of_max(cand, ei, n_experts)
        hit = ei == eidx
        ws.append(jnp.sum(jnp.where(hit, scores, 0.0), axis=0, keepdims=True))
        idxs.append(eidx)
        sel = jnp.where(hit, 1.0, sel)
        cand = jnp.where(hit, NEG_INF, cand)
    wk = jnp.concatenate(ws, axis=0)
    wts_ref[...] = ROUTED_SCALE * wk / jnp.sum(wk, axis=0, keepdims=True)
    idx_ref[...] = jnp.concatenate(idxs, axis=0).astype(I32)
    selb = sel.astype(BF16)
    posmat = _dot(selb, su_ref[...]) + run_ref[:, 0:1]
    pos_ref[...] = jnp.concatenate(
        [jnp.sum(jnp.where(ei == idxs[k], posmat, 0.0), axis=0, keepdims=True) for k in range(TOP_K)],
        axis=0).astype(I32)
    run = run_ref[...] + _dot(selb, jnp.ones((tr, LANES), BF16))
    run_ref[...] = run
    cnt_ref[...] = run


def _router(u2p, router_wt, bias_col):
    t = u2p.shape[0]
    d = router_wt.shape[1]
    n_experts = router_wt.shape[0]
    tr = TOKEN_TILE
    su = jnp.asarray(np.triu(np.ones((tr, tr), np.float32), 1), BF16)
    kern = functools.partial(_router_kernel, n_experts=n_experts, tr=tr)
    col = lambda i: (0, i)
    const = lambda i: (0, 0)
    return pl.pallas_call(
        kern,
        out_shape=[jax.ShapeDtypeStruct((TOP_K, t), I32), jax.ShapeDtypeStruct((TOP_K, t), I32),
                   jax.ShapeDtypeStruct((TOP_K, t), F32), jax.ShapeDtypeStruct((n_experts, LANES), F32)],
        grid=(t // tr,),
        in_specs=[pl.BlockSpec((tr, d // 2), lambda i: (i, 0)), pl.BlockSpec((n_experts, d), const),
                  pl.BlockSpec((n_experts, 1), const), pl.BlockSpec((tr, tr), const)],
        out_specs=[pl.BlockSpec((TOP_K, tr), col), pl.BlockSpec((TOP_K, tr), col), pl.BlockSpec((TOP_K, tr), col),
                   pl.BlockSpec((n_experts, LANES), const)],
        scratch_shapes=[pltpu.VMEM((n_experts, LANES), F32)],
        compiler_params=pltpu.CompilerParams(dimension_semantics=("arbitrary",), vmem_limit_bytes=VMEM_LIMIT),
        name="router",
    )(u2p, router_wt, bias_col, su)


def _slots_kernel(idx_ref, pos_ref, pstart_ref, dest_ref, *, n_experts, tr):
    ei = lax.broadcasted_iota(I32, (n_experts, tr), 0).astype(F32)
    pstart = pstart_ref[...]
    idx = idx_ref[...].astype(F32)
    rows = [jnp.sum(jnp.where(ei == idx[k:k + 1], pstart, 0.0), axis=0, keepdims=True) for k in range(TOP_K)]
    dest_ref[...] = jnp.concatenate(rows, axis=0).astype(I32) + pos_ref[...]


def _slots(idx, pos, pstart_col):
    t = idx.shape[1]
    n_experts = pstart_col.shape[0]
    tr = TOKEN_TILE
    col = lambda i: (0, i)
    return pl.pallas_call(
        functools.partial(_slots_kernel, n_experts=n_experts, tr=tr),
        out_shape=jax.ShapeDtypeStruct((TOP_K, t), I32),
        grid=(t // tr,),
        in_specs=[pl.BlockSpec((TOP_K, tr), col), pl.BlockSpec((TOP_K, tr), col),
                  pl.BlockSpec((n_experts, 1), lambda i: (0, 0))],
        out_specs=pl.BlockSpec((TOP_K, tr), col),
        compiler_params=pltpu.CompilerParams(dimension_semantics=("arbitrary",), vmem_limit_bytes=VMEM_LIMIT),
        name="slots",
    )(idx, pos, pstart_col)


PAD_BITS = tuple(1 << b for b in reversed(range((MOE_BLOCK - 1).bit_length())))


def _dispatch_kernel(pstart_ref, cnt_ref, dest_ref, u_ref, xs_ref, ubuf, zbuf, usem, sem, psem, *, td, n_steps,
                     n_experts):
    i = pl.program_id(0)
    slot = i % 2

    def u_copies(step, s):
        rows = pl.ds(pl.multiple_of(step * td, td), td)
        return [pltpu.make_async_copy(u_ref.at[rows, pl.ds(j * LANES, LANES)], ubuf.at[s, :, j, :], usem.at[s])
                for j in range(ROW_SUBLANES)]

    @pl.when(i == 0)
    def _():
        for cp in u_copies(i, 0):
            cp.start()
        zbuf[...] = jnp.zeros_like(zbuf)

        def pads(e, wait):
            cnt = cnt_ref[e]
            n_pad = (MOE_BLOCK - (cnt & (MOE_BLOCK - 1))) & (MOE_BLOCK - 1)
            base = pstart_ref[e] + cnt
            for bit in PAD_BITS:
                cp = pltpu.make_async_copy(zbuf.at[pl.ds(0, bit)], xs_ref.at[pl.ds(base, bit)], psem)
                has = (n_pad & bit) != 0

                @pl.when(has)
                def _():
                    if wait:
                        cp.wait()
                    else:
                        cp.start()

                base = base + jnp.where(has, bit, 0)

        def start_pads(e, carry):
            pads(e, False)
            return carry

        def wait_pads(e, carry):
            pads(e, True)
            return carry

        lax.fori_loop(0, n_experts, start_pads, 0)
        lax.fori_loop(0, n_experts, wait_pads, 0)

    @pl.when(i + 1 < n_steps)
    def _():
        for cp in u_copies(i + 1, 1 - slot):
            cp.start()

    for cp in u_copies(i, slot):
        cp.wait()

    def start(t, carry):
        for k in range(TOP_K):
            pltpu.make_async_copy(ubuf.at[slot, t], xs_ref.at[dest_ref[k, t]], sem).start(priority=k % 2)
        return carry

    def wait(t, carry):
        for k in range(TOP_K):
            pltpu.make_async_copy(ubuf.at[slot, 0], xs_ref.at[0], sem).wait()
        return carry

    lax.fori_loop(0, td, start, 0, unroll=4)
    lax.fori_loop(0, td, wait, 0, unroll=8)


def _dispatch(pad_start, counts, dest, u2p, n_slots):
    t, dh = u2p.shape
    td = DISPATCH_TILE
    n_experts = pad_start.shape[0]
    assert MOE_BLOCK & (MOE_BLOCK - 1) == 0 and dh == ROW_SUBLANES * LANES
    kern = functools.partial(_dispatch_kernel, td=td, n_steps=t // td, n_experts=n_experts)
    return pl.pallas_call(
        kern,
        out_shape=jax.ShapeDtypeStruct((n_slots, ROW_SUBLANES, LANES), U32),
        grid_spec=pltpu.PrefetchScalarGridSpec(
            num_scalar_prefetch=2, grid=(t // td,),
            in_specs=[pl.BlockSpec((TOP_K, td), lambda i, ps, cn: (0, i), memory_space=pltpu.SMEM),
                      pl.BlockSpec(memory_space=pl.ANY)],
            out_specs=pl.BlockSpec(memory_space=pl.ANY),
            scratch_shapes=[pltpu.VMEM((2, td, ROW_SUBLANES, LANES), U32),
                            pltpu.VMEM((MOE_BLOCK // 2, ROW_SUBLANES, LANES), U32),
                            pltpu.SemaphoreType.DMA((2,)), pltpu.SemaphoreType.DMA, pltpu.SemaphoreType.DMA]),
        compiler_params=pltpu.CompilerParams(dimension_semantics=("arbitrary",), vmem_limit_bytes=VMEM_LIMIT),
        name="dispatch",
    )(pad_start, counts, dest, u2p)


def _experts_kernel(bs_ref, nblk_ref, nu_ref, xs_ref, wg_ref, wu_ref, wd_ref, ys_ref, wgub, wdb, xbuf, ybuf,
                    semx, semy, *, n_experts):
    e = pl.program_id(0)
    n_used = nu_ref[0]

    def x_copies(g):
        rows = pl.ds(pl.multiple_of(g * MOE_BLOCK, MOE_BLOCK), MOE_BLOCK)
        return [pltpu.make_async_copy(xs_ref.at[rows, j, :], xbuf.at[g % X_BUFFERS, :, pl.ds(j * LANES, LANES)],
                                      semx.at[g % X_BUFFERS]) for j in range(ROW_SUBLANES)]

    def x_start(g):
        for cp in x_copies(g):
            cp.start()

    def x_wait(g):
        for cp in x_copies(g):
            cp.wait()

    def y_copies(g, slot):
        rows = pl.ds(pl.multiple_of(g * MOE_BLOCK, MOE_BLOCK), MOE_BLOCK)
        return [pltpu.make_async_copy(ybuf.at[slot, :, pl.ds(j * LANES, LANES)], ys_ref.at[rows, j, :], semy.at[slot])
                for j in range(ROW_SUBLANES)]

    def y_start(g, slot):
        for cp in y_copies(g, slot):
            cp.start()

    def y_wait(g, slot):
        for cp in y_copies(g, slot):
            cp.wait()

    @pl.when(e == 0)
    def _():
        for g in range(X_BUFFERS - 1):
            @pl.when(g < n_used)
            def _():
                x_start(jnp.int32(g))

    g0 = bs_ref[e]
    nb = nblk_ref[e]

    de = wg_ref.shape[2]

    @pl.when(nb > 0)
    def _():
        wgub[:, :de] = wg_ref[0].astype(BF16)
        wgub[:, de:] = wu_ref[0].astype(BF16)
        wdb[...] = wd_ref[0].astype(BF16)

    def block(g, carry):
        slot = g % 2

        @pl.when(g + X_BUFFERS - 1 < n_used)
        def _():
            x_start(g + X_BUFFERS - 1)

        x_wait(g)

        @pl.when(g >= 2)
        def _():
            y_wait(g - 2, slot)

        xa, xb = _unpack_halves(xbuf[g % X_BUFFERS])
        xrow = jnp.concatenate([xa.astype(BF16), xb.astype(BF16)], axis=1)
        hgu = _dot(xrow, wgub[...])
        hg = hgu[:, :de]
        hb = (hg * _sigmoid(hg) * hgu[:, de:]).astype(BF16)
        ybuf[slot] = _pack_halves(_dot(hb, wdb[...]))
        y_start(g, slot)
        return carry

    lax.fori_loop(g0, g0 + nb, block, 0)

    @pl.when(e == n_experts - 1)
    def _():
        @pl.when(n_used >= 2)
        def _():
            y_wait(n_used - 2, n_used % 2)

        @pl.when(n_used >= 1)
        def _():
            y_wait(n_used - 1, (n_used - 1) % 2)


def _experts(block_start, n_blocks, n_used, xsorted, wg, wu, wd):
    n_slots = xsorted.shape[0]
    dh = ROW_SUBLANES * LANES
    n_experts, d, de = wg.shape
    wmap = lambda e, bs, nb, nu: (e, 0, 0)
    assert xsorted.shape[1:] == (ROW_SUBLANES, LANES) and d == 2 * dh
    return pl.pallas_call(
        functools.partial(_experts_kernel, n_experts=n_experts),
        out_shape=jax.ShapeDtypeStruct((n_slots, ROW_SUBLANES, LANES), U32),
        grid_spec=pltpu.PrefetchScalarGridSpec(
            num_scalar_prefetch=3, grid=(n_experts,),
            in_specs=[pl.BlockSpec(memory_space=pl.ANY), pl.BlockSpec((1, d, de), wmap),
                      pl.BlockSpec((1, d, de), wmap), pl.BlockSpec((1, de, d), wmap)],
            out_specs=pl.BlockSpec(memory_space=pl.ANY),
            scratch_shapes=[pltpu.VMEM((d, 2 * de), BF16), pltpu.VMEM((de, d), BF16),
                            pltpu.VMEM((X_BUFFERS, MOE_BLOCK, dh), U32),
                            pltpu.VMEM((2, MOE_BLOCK, dh), U32),
                            pltpu.SemaphoreType.DMA((X_BUFFERS,)), pltpu.SemaphoreType.DMA((2,))]),
        compiler_params=pltpu.CompilerParams(dimension_semantics=("arbitrary",), vmem_limit_bytes=VMEM_LIMIT),
        name="experts",
    )(block_start, n_blocks, n_used, xsorted, wg, wu, wd)


def _combine_kernel(dest_ref, dnext_ref, ys_ref, wt_ref, x1_ref, u_ref, mod_ref, wsgu_ref, wsd_ref,
                    gfin_ref, o_ref, gbuf, accbuf, sem, *, tc, n_steps):
    i = pl.program_id(0)
    slot = i % 2

    def issue(dref, s):
        def body(t, carry):
            for k in range(TOP_K):
                pltpu.make_async_copy(ys_ref.at[dref[k, t]], gbuf.at[s, k, t], sem.at[s]).start(priority=k % 2)
            return carry
        lax.fori_loop(0, tc, body, 0, unroll=4)

    @pl.when(i == 0)
    def _():
        issue(dest_ref, 0)

    def wait(t, carry):
        for k in range(TOP_K):
            pltpu.make_async_copy(ys_ref.at[0], gbuf.at[slot, 0, 0], sem.at[slot]).wait()
        return carry

    lax.fori_loop(0, tc, wait, 0, unroll=8)

    def sum_and_issue(with_issue):
        zeros = jnp.zeros((GATHER_CHUNK, ROW_SUBLANES, LANES), F32)

        def body(c, carry):
            prev_l, prev_r = carry
            t0 = c * GATHER_CHUNK
            prev_rows = pl.ds(jnp.maximum(t0 - GATHER_CHUNK, 0), GATHER_CHUNK)
            accbuf[0, prev_rows] = prev_l
            accbuf[1, prev_rows] = prev_r
            rows = pl.ds(t0, GATHER_CHUNK)
            wts = wt_ref[rows]
            gathered = [gbuf[slot, k, rows] for k in range(TOP_K)]
            if with_issue:
                for tt in range(GATHER_CHUNK):
                    for k in range(TOP_K):
                        pltpu.make_async_copy(ys_ref.at[dnext_ref[k, t0 + tt]], gbuf.at[1 - slot, k, t0 + tt],
                                              sem.at[1 - slot]).start(priority=k % 2)
            sum_l, sum_r = zeros, zeros
            for k in range(TOP_K):
                ga, gb = _unpack_halves(gathered[k])
                wk = jnp.broadcast_to(wts[:, k:k + 1, :], (GATHER_CHUNK, ROW_SUBLANES, LANES))
                sum_l = sum_l + ga * wk
                sum_r = sum_r + gb * wk
            return sum_l, sum_r

        n_chunks = tc // GATHER_CHUNK
        last_l, last_r = lax.fori_loop(0, n_chunks, body, (zeros, zeros))
        last_rows = pl.ds((n_chunks - 1) * GATHER_CHUNK, GATHER_CHUNK)
        accbuf[0, last_rows] = last_l
        accbuf[1, last_rows] = last_r

    @pl.when(i + 1 < n_steps)
    def _():
        sum_and_issue(True)

    @pl.when(i + 1 >= n_steps)
    def _():
        sum_and_issue(False)

    ua, ub = _unpack_halves(u_ref[...])
    half = ua.shape[1]
    hgu = _dot(jnp.concatenate([ua.astype(BF16), ub.astype(BF16)], axis=1), wsgu_ref[...])
    dsh = wsd_ref.shape[0]
    hg = hgu[:, :dsh]
    shared = _dot((hg * _sigmoid(hg) * hgu[:, dsh:]).astype(BF16), wsd_ref[...])
    acc_l = shared[:, :half]
    acc_r = shared[:, half:]
    acc_l = jnp.concatenate([acc_l[:, j * LANES:(j + 1) * LANES] + accbuf[0, :, j, :] for j in range(ROW_SUBLANES)],
                            axis=1)
    acc_r = jnp.concatenate([acc_r[:, j * LANES:(j + 1) * LANES] + accbuf[1, :, j, :] for j in range(ROW_SUBLANES)],
                            axis=1)
    m = mod_ref[0]
    xo_l = x1_ref[:, :half] + m[5:6, :half] * acc_l
    xo_r = x1_ref[:, half:] + m[5:6, half:] * acc_r
    ms = (jnp.sum(xo_l * xo_l, axis=-1, keepdims=True) + jnp.sum(xo_r * xo_r, axis=-1, keepdims=True)) / (2 * half)
    inv = lax.rsqrt(ms + EPS)
    o_ref[:, :half] = xo_l * inv * gfin_ref[:, :half]
    o_ref[:, half:] = xo_r * inv * gfin_ref[:, half:]


def _combine(dest, ysorted, wts_t, x1, u2p, mods3, wsgu, wsd, gfin, *, seq):
    t, d = x1.shape
    dh = d // 2
    tc = DISPATCH_TILE
    tiles_per_batch = seq // tc
    n_steps = t // tc
    kern = functools.partial(_combine_kernel, tc=tc, n_steps=n_steps)
    row = lambda i: (i, 0)
    const = lambda i: (0, 0)
    return pl.pallas_call(
        kern,
        out_shape=jax.ShapeDtypeStruct((t, d), F32),
        grid=(n_steps,),
        in_specs=[pl.BlockSpec((TOP_K, tc), lambda i: (0, i), memory_space=pltpu.SMEM),
                  pl.BlockSpec((TOP_K, tc), lambda i: (0, jnp.minimum(i + 1, n_steps - 1)), memory_space=pltpu.SMEM),
                  pl.BlockSpec(memory_space=pl.ANY), pl.BlockSpec((tc, TOP_K, LANES), lambda i: (i, 0, 0)),
                  pl.BlockSpec((tc, d), row), pl.BlockSpec((tc, dh), row),
                  pl.BlockSpec((1, 6, d), lambda i: (i // tiles_per_batch, 0, 0)),
                  pl.BlockSpec(wsgu.shape, const), pl.BlockSpec(wsd.shape, const), pl.BlockSpec((1, d), const)],
        out_specs=pl.BlockSpec((tc, d), row),
        scratch_shapes=[pltpu.VMEM((2, TOP_K, tc, ROW_SUBLANES, LANES), U32),
                        pltpu.VMEM((2, tc, ROW_SUBLANES, LANES), F32), pltpu.SemaphoreType.DMA((2,))],
        compiler_params=pltpu.CompilerParams(dimension_semantics=("arbitrary",), vmem_limit_bytes=VMEM_LIMIT),
        name="combine",
    )(dest, dest, ysorted, wts_t, x1, u2p, mods3, wsgu, wsd, gfin)


def kernel(x, c, ctx, c_ctx, ada_w, ada_b, norm_mix_g, norm_ffn_g, w_in, conv_xbc_w, conv_xbc_b, ssd_dt_bias, ssd_a_log, ssd_d, ssd_norm_g, conv_qk_w, conv_qk_b, mlstm_i_bias, mlstm_f_bias, mlstm_norm_g, w_ssd_out, w_mlstm_out, w_out, router_w, router_bias, moe_w_gate, moe_w_up, moe_w_down, shared_w_gate, shared_w_up, shared_w_down, norm_final_g):
    batch, seq, d = x.shape
    ctx_len = ctx.shape[1]
    depth = ada_w.shape[0]
    assert depth == 1, "only the single-layer configuration is implemented"
    assert seq % CHUNK == 0 and ctx_len % CHUNK == 0 and seq % GRID_W == 0
    l = 0
    n_sh = d // SSD_HEAD_DIM
    n_mh = d // MLSTM_DV
    dk = MLSTM_DV // 2
    d_xbc = d + 2 * SSD_GROUPS * SSD_STATE
    d_qk = 2 * n_mh * dk
    sizes = (d, d_xbc, 2 * n_sh, d_qk, d, 4 * n_mh, d, 2 * d)
    offs = np.concatenate([[0], np.cumsum(sizes)])
    assert offs[-1] == w_in.shape[2] and 2 * n_sh + 2 * n_mh <= LANES

    cond = jnp.concatenate([c, c_ctx[None], jnp.zeros((SUBLANES - (batch + 1) % SUBLANES, d), F32)], axis=0)
    mods = _adaln(cond, ada_w[l], ada_b[l])
    mods3 = mods.reshape(mods.shape[0], 6, d)

    w = w_in[l]
    seg = lambda k: w[:, offs[k]:offs[k + 1]]
    wbig = jnp.concatenate([seg(0), seg(1), seg(3), seg(4), seg(6), seg(7)], axis=1).astype(BF16)
    w_dt = seg(2)
    w_g = seg(5).reshape(d, 2, 2, n_mh)
    w_i = w_g[:, :, 0].reshape(d, 2 * n_mh)
    w_f = w_g[:, :, 1].reshape(d, 2 * n_mh)
    pad = jnp.zeros((d, LANES - 2 * n_sh - 2 * n_mh), F32)
    wsm = jnp.concatenate([w_dt, w_f, pad, w_dt, w_i, pad], axis=1)
    wsh = wsm.astype(BF16)
    wsl = (wsm - wsh.astype(F32)).astype(BF16)
    padb = jnp.zeros((LANES - 2 * n_sh - 2 * n_mh,), F32)
    dtb = ssd_dt_bias[l].reshape(-1).astype(F32)
    smb = jnp.concatenate([dtb, mlstm_f_bias[l].reshape(-1).astype(F32), padb,
                           dtb, mlstm_i_bias[l].reshape(-1).astype(F32), padb]).reshape(1, 2 * LANES)
    aneg = jnp.concatenate([-jnp.exp(ssd_a_log[l].astype(F32)).reshape(-1),
                            jnp.zeros((LANES - 2 * n_sh,), F32)]).reshape(1, LANES)

    x2 = x.reshape(batch * seq, d)
    ctx2 = ctx.reshape(batch * ctx_len, d)
    z, xs, bc, qk, v, og, mg, small, smallt = _inproj(
        x2, ctx2, mods3, norm_mix_g[l].reshape(1, d), wbig, wsh, wsl,
        conv_xbc_w[l], conv_xbc_b[l].reshape(1, d_xbc), conv_qk_w[l], conv_qk_b[l].reshape(1, d_qk), smb, aneg,
        batch=batch, seq=seq, ctx_len=ctx_len, dk=dk)

    ncc = ctx_len // CHUNK
    ncl = seq // CHUNK
    y0, y1, h0, h1 = _scans(xs, bc, qk, v, small, smallt, batch=batch, n_ctx_chunks=ncc, n_lat_chunks=ncl, dk=dk)

    dexp = jnp.repeat(ssd_d[l].astype(F32), SSD_HEAD_DIM).reshape(1, d)
    x1, u2 = _merge(y0, y1, xs, z, h0, h1, og, mg, x2, mods3, dexp, ssd_norm_g[l].reshape(1, d),
                    mlstm_norm_g[l].reshape(1, d), norm_ffn_g[l].reshape(1, d),
                    w_ssd_out[l].astype(BF16), w_mlstm_out[l].astype(BF16), w_out[l].astype(BF16),
                    batch=batch, seq=seq, n_ctx_tok=batch * ctx_len)

    n_experts = router_w.shape[2]
    idx, pos, wts, cnt = _router(u2, router_w[l].T.astype(BF16), router_bias[l].astype(F32).reshape(n_experts, 1))
    counts = cnt[:, 0].astype(I32)
    padded = (counts + MOE_BLOCK - 1) // MOE_BLOCK * MOE_BLOCK
    pad_end = jnp.cumsum(padded)
    pad_start = (pad_end - padded).astype(I32)
    t = batch * seq
    nb = t * TOP_K // MOE_BLOCK + n_experts
    n_used = (pad_end[-1] // MOE_BLOCK).astype(I32).reshape(1)

    dest = _slots(idx, pos, pad_start.astype(F32).reshape(n_experts, 1))
    xsorted = _dispatch(pad_start, counts, dest, u2, nb * MOE_BLOCK)
    ysorted = _experts(pad_start // MOE_BLOCK, (padded // MOE_BLOCK).astype(I32), n_used, xsorted,
                       moe_w_gate[l], moe_w_up[l], moe_w_down[l])
    wts_lanes = jnp.broadcast_to(wts.T[:, :, None], (t, TOP_K, LANES))
    out = _combine(dest, ysorted, wts_lanes, x1, u2, mods3,
                   jnp.concatenate([shared_w_gate[l], shared_w_up[l]], axis=1).astype(BF16), shared_w_down[l].astype(BF16),
                   norm_final_g.reshape(1, d), seq=seq)
    return out.reshape(batch, seq, d)
```
